```python
import math
import jax
import jax.numpy as jnp
from jax import lax
import numpy as np

D_MODEL = 1024
BATCH = 4
SEQ = 8192
DEPTH = 4

CHUNK = 64
EPS = 1e-6
NEG_BIG = -1e30
HG_HEADS = 8
HG_DK = 128
HG_DV = 128
HG_QK = HG_HEADS * HG_DK
HG_W = HG_HEADS * HG_DV
ML_HEADS = 4
ML_DQK = 128
ML_DV = 256
ML_CONV = 4
ML_QK = ML_HEADS * ML_DQK
ML_W = ML_HEADS * ML_DV
MB_HEADS = 16
MB_P = 64
MB_GROUPS = 4
MB_N = 128
MB_CONV = 4
MB_W = MB_HEADS * MB_P
MB_CONV_DIM = MB_W + 2 * MB_GROUPS * MB_N
D_FF = 2816
FFN_CONV = 3
N_BRANCH = 3
IN_SPLITS = (HG_QK, HG_QK, HG_W, HG_W,
             2 * ML_QK, ML_W, 2 * ML_HEADS, ML_W,
             MB_W, MB_CONV_DIM, MB_HEADS,
             N_BRANCH * D_MODEL)
N_IN = sum(IN_SPLITS)

kernel_name = "hybrid_hgrn2_mlstm_ssd_convffn_trunk"


def rmsnorm(x, g):
    xf = x.astype(jnp.float32)
    y = xf * lax.rsqrt(jnp.mean(xf * xf, axis=-1, keepdims=True) + EPS)
    return (y * g.astype(jnp.float32)).astype(x.dtype)


def causal_dwconv(u, w, b):
    K = w.shape[0]
    S = u.shape[1]
    up = jnp.pad(u, ((0, 0), (K - 1, 0), (0, 0)))
    return sum(up[:, k:k + S] * w[k] for k in range(K)) + b


def to_chunks(t):
    B, S = t.shape[:2]
    t = t.astype(jnp.float32).reshape(B, S // CHUNK, CHUNK, *t.shape[2:])
    return jnp.moveaxis(t, 1, 0)


def from_chunks(t):
    t = jnp.moveaxis(t, 0, 1)
    return t.reshape(t.shape[0], -1, *t.shape[3:])


def causal_tri():
    return jnp.tril(jnp.ones((CHUNK, CHUNK), dtype=bool))


def hgrn2_scan(q, log_f, v):
    B, _, H, dk = q.shape
    dv = v.shape[-1]
    k = -jnp.expm1(log_f)
    tri = causal_tri()

    def step(state, inp):
        qc, lfc, kc, vc = inp
        b = jnp.cumsum(lfc, axis=1)
        diff = b[:, :, None] - b[:, None, :]
        decay = jnp.exp(jnp.where(tri[None, :, :, None, None], diff, NEG_BIG))
        attn = jnp.einsum('bthd,btshd->bhts', qc, decay * kc[:, None])
        o_intra = jnp.einsum('bhts,bshv->bthv', attn, vc)
        o_inter = jnp.einsum('bthd,bhdv->bthv', qc * jnp.exp(b), state)
        b_last = b[:, -1]
        k_dec = kc * jnp.exp(b_last[:, None] - b)
        state = jnp.exp(b_last)[..., None] * state + jnp.einsum('bshd,bshv->bhdv', k_dec, vc)
        return state, o_intra + o_inter

    s0 = jnp.zeros((B, H, dk, dv), jnp.float32)
    _, o = lax.scan(step, s0, (to_chunks(q), to_chunks(log_f), to_chunks(k), to_chunks(v)))
    return from_chunks(o)


def mlstm_scan(q, k, v, i_pre, log_f):
    B, _, H, dk = q.shape
    dv = v.shape[-1]
    tri = causal_tri()

    def step(carry, inp):
        C, n, m = carry
        qc, kc, vc, ic, fc = inp
        bh = jnp.moveaxis(jnp.cumsum(fc, axis=1), 1, 2)
        ih = jnp.moveaxis(ic, 1, 2)
        logD = jnp.where(tri, bh[..., :, None] - bh[..., None, :] + ih[..., None, :], NEG_BIG)
        log_inter = bh + m[..., None]
        m_t = jnp.maximum(jnp.max(logD, axis=-1), log_inter)
        s = jnp.einsum('bthd,bshd->bhts', qc, kc) * jnp.exp(logD - m_t[..., None])
        inter_w = jnp.exp(log_inter - m_t)
        inter_wt = jnp.swapaxes(inter_w, 1, 2)[..., None]
        num = jnp.einsum('bhts,bshv->bthv', s, vc) + jnp.einsum('bthd,bhdv->bthv', qc, C) * inter_wt
        den = jnp.sum(s, axis=-1) + jnp.einsum('bthd,bhd->bht', qc, n) * inter_w
        denom = jnp.maximum(jnp.abs(den), jnp.exp(-m_t))
        h = num / jnp.swapaxes(denom, 1, 2)[..., None]
        b_last = bh[..., -1]
        log_w = b_last[..., None] - bh + ih
        m_new = jnp.maximum(b_last + m, jnp.max(log_w, axis=-1))
        w = jnp.swapaxes(jnp.exp(log_w - m_new[..., None]), 1, 2)[..., None]
        dstate = jnp.exp(b_last + m - m_new)
        C = dstate[..., None, None] * C + jnp.einsum('bshd,bshv->bhdv', kc * w, vc)
        n = dstate[..., None] * n + jnp.sum(kc * w, axis=1)
        return (C, n, m_new), h

    c0 = (jnp.zeros((B, H, dk, dv), jnp.float32), jnp.zeros((B, H, dk), jnp.float32),
          jnp.zeros((B, H), jnp.float32))
    _, h = lax.scan(step, c0, (to_chunks(q), to_chunks(k), to_chunks(v), to_chunks(i_pre), to_chunks(log_f)))
    return from_chunks(h)


def ssd_scan(x, dt, A, Bm, Cm):
    B, _, H, P = x.shape
    N = Bm.shape[-1]
    hpg = H // Bm.shape[2]
    tri = causal_tri()
    A = A.astype(jnp.float32)

    def step(state, inp):
        xc, dtc, Bc, Cc = inp
        cum = jnp.moveaxis(jnp.cumsum(dtc * A, axis=1), 1, 2)
        L = jnp.exp(jnp.where(tri, cum[..., :, None] - cum[..., None, :], NEG_BIG))
        CB = jnp.repeat(jnp.einsum('btgn,bsgn->bgts', Cc, Bc), hpg, axis=1)
        xdt = xc * dtc[..., None]
        y_intra = jnp.einsum('bhts,bshp->bthp', CB * L, xdt)
        Ch = jnp.repeat(Cc, hpg, axis=2)
        y_inter = jnp.einsum('bthn,bhpn->bthp', Ch, state) * jnp.swapaxes(jnp.exp(cum), 1, 2)[..., None]
        dec = jnp.swapaxes(jnp.exp(cum[..., -1:] - cum), 1, 2)[..., None]
        Bh = jnp.repeat(Bc, hpg, axis=2)
        state = jnp.exp(cum[..., -1])[..., None, None] * state + jnp.einsum('bshn,bshp->bhpn', Bh * dec, xdt)
        return state, y_intra + y_inter

    s0 = jnp.zeros((B, H, P, N), jnp.float32)
    _, y = lax.scan(step, s0, (to_chunks(x), to_chunks(dt), to_chunks(Bm), to_chunks(Cm)))
    return from_chunks(y)


def setup_inputs(seed: int = 0) -> dict:
    key = jax.random.key(seed)
    ks = jax.random.split(key, 32)
    f32 = jnp.float32

    def nrm(k, shape, scale):
        return jax.random.normal(k, shape, f32) * scale

    def gain(k, shape):
        return 1.0 + 0.05 * jax.random.normal(k, shape, f32)

    dt0 = jnp.exp(jax.random.uniform(ks[12], (DEPTH, MB_HEADS), f32, math.log(1e-3), math.log(1e-1)))
    ml_f_bias = jnp.linspace(3.0, 6.0, ML_HEADS, dtype=f32)[None] + nrm(ks[7], (DEPTH, ML_HEADS), 0.1)
    ml_i_bias = nrm(ks[8], (DEPTH, ML_HEADS), 0.1)
    return {
        "x": jax.random.normal(ks[0], (BATCH, SEQ, D_MODEL), f32),
        "norm1_g": gain(ks[1], (DEPTH, D_MODEL)),
        "w_in": nrm(ks[2], (DEPTH, D_MODEL, N_IN), D_MODEL ** -0.5),
        "hg_lb_logits": nrm(ks[3], (DEPTH, HG_QK), 0.1),
        "hg_norm_g": gain(ks[4], (DEPTH, HG_DV)),
        "ml_conv_w": nrm(ks[5], (DEPTH, ML_CONV, 2 * ML_QK), ML_CONV ** -0.5),
        "ml_conv_b": nrm(ks[6], (DEPTH, 2 * ML_QK), 0.02),
        "ml_gate_b": jnp.concatenate([ml_i_bias, ml_f_bias], axis=-1),
        "ml_norm_g": gain(ks[9], (DEPTH, ML_DV)),
        "mb_conv_w": nrm(ks[10], (DEPTH, MB_CONV, MB_CONV_DIM), MB_CONV ** -0.5),
        "mb_conv_b": nrm(ks[11], (DEPTH, MB_CONV_DIM), 0.02),
        "mb_dt_bias": dt0 + jnp.log(-jnp.expm1(-dt0)),
        "mb_a_log": jnp.log(jax.random.uniform(ks[13], (DEPTH, MB_HEADS), f32, 1.0, 16.0)),
        "mb_d": gain(ks[14], (DEPTH, MB_HEADS)),
        "mb_norm_g": gain(ks[15], (DEPTH, MB_W)),
        "w_br_hg": nrm(ks[16], (DEPTH, HG_W, D_MODEL), HG_W ** -0.5),
        "w_br_ml": nrm(ks[17], (DEPTH, ML_W, D_MODEL), ML_W ** -0.5),
        "w_br_mb": nrm(ks[18], (DEPTH, MB_W, D_MODEL), MB_W ** -0.5),
        "w_out": nrm(ks[19], (DEPTH, D_MODEL, D_MODEL), D_MODEL ** -0.5),
        "norm2_g": gain(ks[20], (DEPTH, D_MODEL)),
        "w_up": nrm(ks[21], (DEPTH, D_MODEL, 2 * D_FF), D_MODEL ** -0.5),
        "ffn_conv_w": nrm(ks[22], (DEPTH, FFN_CONV, 2 * D_FF), FFN_CONV ** -0.5),
        "ffn_conv_b": nrm(ks[23], (DEPTH, 2 * D_FF), 0.02),
        "w_down": nrm(ks[24], (DEPTH, D_FF, D_MODEL), D_FF ** -0.5),
        "final_g": gain(ks[25], (D_MODEL,)),
    }


def reference(x, norm1_g, w_in, hg_lb_logits, hg_norm_g, ml_conv_w, ml_conv_b, ml_gate_b, ml_norm_g,
              mb_conv_w, mb_conv_b, mb_dt_bias, mb_a_log, mb_d, mb_norm_g, w_br_hg, w_br_ml, w_br_mb,
              w_out, norm2_g, w_up, ffn_conv_w, ffn_conv_b, w_down, final_g):
    B, S, _ = x.shape
    f32 = jnp.float32
    lb_p = jax.nn.softmax(hg_lb_logits.astype(f32), axis=0)
    lbs = jnp.cumsum(lb_p, axis=0) - lb_p[0]
    split_at = [int(c) for c in np.cumsum(IN_SPLITS)[:-1]]

    for l in range(DEPTH):
        h = rmsnorm(x, norm1_g[l])
        (hg_q, hg_f, hg_i, hg_g, ml_qk, ml_v, ml_if, ml_o,
         mb_z, mb_xbc, mb_dt, gate_pre) = jnp.split(h @ w_in[l], split_at, axis=-1)

        lb = lbs[l]
        log_f = jnp.log(lb + (1.0 - lb) * jax.nn.sigmoid(hg_f.astype(f32)))
        o_hg = hgrn2_scan(jax.nn.silu(hg_q).reshape(B, S, HG_HEADS, HG_DK),
                          log_f.reshape(B, S, HG_HEADS, HG_DK),
                          hg_i.reshape(B, S, HG_HEADS, HG_DV))
        y_hg = (rmsnorm(o_hg, hg_norm_g[l]).reshape(B, S, HG_W)
                * jax.nn.silu(hg_g.astype(f32))).astype(x.dtype)

        qk = jax.nn.silu(causal_dwconv(ml_qk, ml_conv_w[l], ml_conv_b[l]))
        q_m, k_m = jnp.split(qk, 2, axis=-1)
        gb = ml_gate_b[l]
        i_pre = ml_if[..., :ML_HEADS] + gb[:ML_HEADS]
        log_f_m = jax.nn.log_sigmoid((ml_if[..., ML_HEADS:] + gb[ML_HEADS:]).astype(f32))
        h_ml = mlstm_scan(q_m.reshape(B, S, ML_HEADS, ML_DQK),
                          k_m.reshape(B, S, ML_HEADS, ML_DQK) * (ML_DQK ** -0.5),
                          ml_v.reshape(B, S, ML_HEADS, ML_DV), i_pre, log_f_m)
        y_ml = (rmsnorm(h_ml, ml_norm_g[l]).reshape(B, S, ML_W)
                * jax.nn.sigmoid(ml_o.astype(f32))).astype(x.dtype)

        xbc = jax.nn.silu(causal_dwconv(mb_xbc, mb_conv_w[l], mb_conv_b[l]))
        xs, Bm, Cm = jnp.split(xbc, [MB_W, MB_W + MB_GROUPS * MB_N], axis=-1)
        xs = xs.reshape(B, S, MB_HEADS, MB_P)
        dt = jax.nn.softplus((mb_dt + mb_dt_bias[l]).astype(f32))
        A = -jnp.exp(mb_a_log[l].astype(f32))
        y_ssd = ssd_scan(xs, dt, A, Bm.reshape(B, S, MB_GROUPS, MB_N), Cm.reshape(B, S, MB_GROUPS, MB_N))
        y_ssd = y_ssd + mb_d[l].astype(f32)[:, None] * xs
        yz = (y_ssd.reshape(B, S, MB_W) * jax.nn.silu(mb_z.astype(f32))).reshape(B, S, MB_GROUPS, MB_W // MB_GROUPS)
        y_mb = rmsnorm(yz, mb_norm_g[l].reshape(MB_GROUPS, MB_W // MB_GROUPS)).reshape(B, S, MB_W).astype(x.dtype)

        gates = jax.nn.sigmoid(gate_pre).reshape(B, S, N_BRANCH, D_MODEL)
        mixed = (gates[:, :, 0] * (y_hg @ w_br_hg[l])
                 + gates[:, :, 1] * (y_ml @ w_br_ml[l])
                 + gates[:, :, 2] * (y_mb @ w_br_mb[l]))
        x = x + mixed @ w_out[l]

        u = causal_dwconv(rmsnorm(x, norm2_g[l]) @ w_up[l], ffn_conv_w[l], ffn_conv_b[l])
        u_g, u_v = jnp.split(u, 2, axis=-1)
        x = x + (jax.nn.silu(u_g) * u_v) @ w_down[l]

    return rmsnorm(x, final_g)
```

```python
import functools

import jax
import jax.numpy as jnp
from jax import lax
from jax.experimental import pallas as pl
from jax.experimental.pallas import tpu as pltpu

F32 = jnp.float32
BF16 = jnp.bfloat16

D_MODEL = 1024
DEPTH = 4
CHUNK = 64
CHUNK_LOG2 = 6
SUB = 16
EPS = 1e-6
NEG_BIG = -1e30
HG_HEADS = 8
HG_D = 128
ML_HEADS = 4
ML_DQK = 128
ML_DV = 256
ML_CONV = 4
MB_HEADS = 16
MB_P = 64
MB_P_LOG2 = 6
MB_GROUPS = 4
MB_N = 128
MB_CONV = 4
MB_W = MB_HEADS * MB_P
MB_CONV_DIM = MB_W + 2 * MB_GROUPS * MB_N
D_FF = 2816
FFN_CONV = 3
FF_CHUNK = 1408
SMALL_W = 128
DT_COL = 8
CONV_PAD = 8

VMEM_LIMIT = 56 * 1024 * 1024

TM_PROJ = 1024
TN_PROJ = 1024
TS_MIX = 512
TM_MERGE = 512
TM_FFN = 512


def _silu(x):
    return x * jax.nn.sigmoid(x)


def _softplus(x):
    return jnp.maximum(x, 0.0) + jnp.log1p(jnp.exp(-jnp.abs(x)))


def _log_sigmoid(x):
    return jnp.minimum(x, 0.0) - jnp.log1p(jnp.exp(-jnp.abs(x)))


def _dot(a, b):
    return jnp.dot(a, b, preferred_element_type=F32)


def _dot_nt(a, b):
    return lax.dot_general(a, b, (((1,), (1,)), ((), ())), preferred_element_type=F32)


def _dot_tn(a, b):
    return lax.dot_general(a, b, (((0,), (0,)), ((), ())), preferred_element_type=F32)


def _split3(a):
    hi = a.astype(BF16)
    r = a - hi.astype(F32)
    mid = r.astype(BF16)
    lo = (r - mid.astype(F32)).astype(BF16)
    return hi, mid, lo


def _sel_left(sel, a):
    hi, mid, lo = _split3(a)
    return _dot(sel, hi) + (_dot(sel, mid) + _dot(sel, lo))


def _sel_right(a, sel):
    hi, mid, lo = _split3(a)
    return _dot(hi, sel) + (_dot(mid, sel) + _dot(lo, sel))


def _tri_lower(n):
    r = lax.broadcasted_iota(jnp.int32, (n, n), 0)
    c = lax.broadcasted_iota(jnp.int32, (n, n), 1)
    return r >= c


def _mask_bf16(m):
    return jnp.where(m, 1.0, 0.0).astype(BF16)


def _cparams(sem):
    return pltpu.CompilerParams(dimension_semantics=sem, vmem_limit_bytes=VMEM_LIMIT)


def _lbs_kernel(lg_ref, o_ref):
    lg = lg_ref[...]
    mx = jnp.max(lg, axis=0, keepdims=True)
    e = jnp.exp(lg - mx)
    p = e / jnp.sum(e, axis=0, keepdims=True)
    acc = jnp.zeros_like(p[0:1])
    rows = []
    for l in range(lg.shape[0]):
        acc = acc + p[l:l + 1]
        rows.append(acc - p[0:1])
    o_ref[...] = jnp.concatenate(rows, axis=0)


def _lbs(logits):
    return pl.pallas_call(
        _lbs_kernel,
        out_shape=jax.ShapeDtypeStruct(logits.shape, F32),
        name="hgrn2_lower_bounds",
    )(logits)


def _inproj_kernel(x_ref, g_ref, w_ref, ws_ref, o_ref, os_ref, h_ref):
    @pl.when(pl.program_id(1) == 0)
    def _():
        x = x_ref[...]
        ms = jnp.mean(x * x, axis=-1, keepdims=True)
        hb = (x * lax.rsqrt(ms + EPS) * g_ref[...]).astype(BF16)
        h_ref[...] = hb
        os_ref[...] = _dot(hb, ws_ref[...])

    o_ref[...] = _dot(h_ref[...], w_ref[...]).astype(o_ref.dtype)


def _inproj(x2, g, w_big, w_small):
    n = x2.shape[0]
    nb = w_big.shape[1]
    return pl.pallas_call(
        _inproj_kernel,
        grid=(n // TM_PROJ, nb // TN_PROJ),
        in_specs=[
            pl.BlockSpec((TM_PROJ, D_MODEL), lambda i, j: (i, 0)),
            pl.BlockSpec((1, D_MODEL), lambda i, j: (0, 0)),
            pl.BlockSpec((D_MODEL, TN_PROJ), lambda i, j: (0, j)),
            pl.BlockSpec((D_MODEL, SMALL_W), lambda i, j: (0, 0)),
        ],
        out_specs=[
            pl.BlockSpec((TM_PROJ, TN_PROJ), lambda i, j: (i, j)),
            pl.BlockSpec((TM_PROJ, SMALL_W), lambda i, j: (i, 0)),
        ],
        out_shape=[
            jax.ShapeDtypeStruct((n, nb), BF16),
            jax.ShapeDtypeStruct((n, SMALL_W), F32),
        ],
        scratch_shapes=[pltpu.VMEM((TM_PROJ, D_MODEL), BF16)],
        compiler_params=_cparams(("parallel", "arbitrary")),
        name="in_projection",
    )(x2, g, w_big, w_small)


def _conv_tile(raw_ref, ubuf, cw_ref, cb_ref, width):
    ts = raw_ref.shape[0]
    ubuf[CONV_PAD:CONV_PAD + ts, :] = raw_ref[...].astype(F32)
    y = cb_ref[...]
    for k in range(width):
        off = CONV_PAD - (width - 1) + k
        y = y + cw_ref[k:k + 1, :] * ubuf[off:off + ts, :]
    ubuf[0:CONV_PAD, :] = ubuf[ts:ts + CONV_PAD, :]
    return y


def _hgrn2_kernel(q_ref, f_ref, i_ref, g_ref, lb_ref, ng_ref, y_ref,
                  state_ref, qs, ks, bs, vs, os_):
    @pl.when(pl.program_id(1) == 0)
    def _():
        state_ref[...] = jnp.zeros_like(state_ref)

    ts = q_ref.shape[0]
    lb = lb_ref[...]
    tri = _mask_bf16(_tri_lower(CHUNK))
    row_sub = lax.broadcasted_iota(jnp.int32, (SUB, HG_D), 0)
    row_chunk = lax.broadcasted_iota(jnp.int32, (CHUNK, HG_D), 0)
    n_sub = CHUNK // SUB

    def head_body(h, carry):
        q = qs[h]
        k = ks[h]
        b = bs[h]
        v = vs[h]
        st = state_ref[h]
        o_inter = _dot_nt((q * jnp.exp(b)).astype(BF16), st.astype(BF16))
        b_last = b[CHUNK - 1:CHUNK, :]
        k_dec = k * jnp.exp(b_last - b)
        state_ref[h] = jnp.exp(b_last) * st + _dot_tn(v.astype(BF16), k_dec.astype(BF16))
        vb16 = v.astype(BF16)
        outs = []
        for i in range(n_sub):
            lo = i * SUB
            qb = q[lo:lo + SUB]
            kb = k[lo:lo + SUB]
            bb = b[lo:lo + SUB]
            vb = v[lo:lo + SUB]
            acc = o_inter[lo:lo + SUB]
            if i > 0:
                bref = b[lo - 1:lo, :]
                q_i = qb * jnp.exp(bb - bref)
                k_i = k * jnp.exp(jnp.where(row_chunk < lo, bref - b, NEG_BIG))
                s_i = _dot_nt(q_i.astype(BF16), k_i.astype(BF16))
                acc = acc + _dot(s_i.astype(BF16), vb16)
            for s in range(SUB):
                d = jnp.where(row_sub >= s, bb - bb[s:s + 1, :], NEG_BIG)
                z = qb * jnp.exp(d) * kb[s:s + 1, :]
                acc = acc + jnp.sum(z, axis=-1, keepdims=True) * vb[s:s + 1, :]
            outs.append(acc)
        os_[h] = jnp.concatenate(outs, axis=0)
        return carry

    def chunk_body(c, carry):
        rows = pl.ds(pl.multiple_of(c * CHUNK, CHUNK), CHUNK)
        sig = jax.nn.sigmoid(f_ref[rows, :].astype(F32))
        fgate = lb + (1.0 - lb) * sig
        b_all = _sel_left(tri, jnp.log(fgate))
        k_all = 1.0 - fgate
        q_all = _silu(q_ref[rows, :].astype(F32))
        v_all = i_ref[rows, :].astype(F32)
        for h in range(HG_HEADS):
            hs = slice(h * HG_D, (h + 1) * HG_D)
            qs[h] = q_all[:, hs]
            ks[h] = k_all[:, hs]
            bs[h] = b_all[:, hs]
            vs[h] = v_all[:, hs]
        lax.fori_loop(0, HG_HEADS, head_body, 0)
        parts = []
        for h in range(HG_HEADS):
            o = os_[h]
            parts.append(o * lax.rsqrt(jnp.mean(o * o, axis=-1, keepdims=True) + EPS))
        on = jnp.concatenate(parts, axis=-1) * ng_ref[...]
        y_ref[rows, :] = (on * _silu(g_ref[rows, :].astype(F32))).astype(y_ref.dtype)
        return carry

    lax.fori_loop(0, ts // CHUNK, chunk_body, 0)


def _hgrn2(proj, lb, ng, batch, seq):
    n = proj.shape[0]
    nt = seq // TS_MIX
    w = HG_HEADS * HG_D

    def col(cb):
        return pl.BlockSpec((TS_MIX, w), lambda b, j: (b * nt + j, cb))

    vec = pl.BlockSpec((1, w), lambda b, j: (0, 0))
    hbuf = pltpu.VMEM((HG_HEADS, CHUNK, HG_D), F32)
    return pl.pallas_call(
        _hgrn2_kernel,
        grid=(batch, nt),
        in_specs=[col(0), col(1), col(2), col(3), vec, vec],
        out_specs=pl.BlockSpec((TS_MIX, w), lambda b, j: (b * nt + j, 0)),
        out_shape=jax.ShapeDtypeStruct((n, w), BF16),
        scratch_shapes=[pltpu.VMEM((HG_HEADS, HG_D, HG_D), F32), hbuf, hbuf, hbuf, hbuf, hbuf],
        compiler_params=_cparams(("parallel", "arbitrary")),
        name="hgrn2_mixer",
    )(proj, proj, proj, proj, lb, ng)


def _mlstm_kernel(qk_ref, v_ref, og_ref, sm_ref, smt_ref, cw_ref, cb_ref, gbr_ref, gbc_ref, ng_ref,
                  y_ref, caug_ref, m_ref, ubuf, qk_s, h_s):
    ts = qk_ref.shape[0]
    half = ML_HEADS * ML_DQK

    @pl.when(pl.program_id(1) == 0)
    def _():
        caug_ref[...] = jnp.zeros_like(caug_ref)
        m_ref[...] = jnp.zeros_like(m_ref)
        ubuf[0:CONV_PAD, :] = jnp.zeros((CONV_PAD, ubuf.shape[1]), F32)

    qk = _silu(_conv_tile(qk_ref, ubuf, cw_ref, cb_ref, ML_CONV))
    qk_s[:, :half] = qk[:, :half]
    qk_s[:, half:] = qk[:, half:] * (ML_DQK ** -0.5)

    tri_b = _tri_lower(CHUNK)
    tri_l = _mask_bf16(tri_b)
    tri_u = _mask_bf16(lax.broadcasted_iota(jnp.int32, (CHUNK, CHUNK), 0)
                       <= lax.broadcasted_iota(jnp.int32, (CHUNK, CHUNK), 1))
    ones_col = jnp.where(lax.broadcasted_iota(jnp.int32, (CHUNK, 128), 1) == 0, 1.0, 0.0)

    def chunk_body(c, carry):
        rows = pl.ds(pl.multiple_of(c * CHUNK, CHUNK), CHUNK)
        pre_c = sm_ref[rows, :] + gbr_ref[...]
        pre_r = smt_ref[c] + gbc_ref[...]
        cum_c = _sel_left(tri_l, _log_sigmoid(pre_c))
        cum_r = _sel_right(_log_sigmoid(pre_r), tri_u)
        for h in range(ML_HEADS):
            q = qk_s[rows, h * ML_DQK:(h + 1) * ML_DQK]
            k = qk_s[rows, half + h * ML_DQK:half + (h + 1) * ML_DQK]
            v = v_ref[rows, h * ML_DV:(h + 1) * ML_DV].astype(F32)
            v_aug = jnp.concatenate([v, ones_col], axis=-1).astype(BF16)
            fh = ML_HEADS + h
            b_col = cum_c[:, fh:fh + 1]
            i_col = pre_c[:, h:h + 1]
            b_row = cum_r[fh:fh + 1, :]
            i_row = pre_r[h:h + 1, :]
            m = m_ref[h:h + 1, 0:1]
            c_aug = caug_ref[h]

            log_d = jnp.where(tri_b, b_col - b_row + i_row, NEG_BIG)
            log_inter = b_col + m
            m_t = jnp.maximum(jnp.max(log_d, axis=-1, keepdims=True), log_inter)
            qb = q.astype(BF16)
            s = _dot_nt(qb, k.astype(BF16)) * jnp.exp(log_d - m_t)
            inter_w = jnp.exp(log_inter - m_t)
            num = _dot(s.astype(BF16), v_aug) + _dot(qb, c_aug.astype(BF16)) * inter_w
            den = num[:, ML_DV:ML_DV + 1]
            denom = jnp.maximum(jnp.abs(den), jnp.exp(-m_t))
            h_s[:, h * ML_DV:(h + 1) * ML_DV] = num[:, :ML_DV] / denom

            b_last = b_col[CHUNK - 1:CHUNK, :]
            log_w = b_last - b_col + i_col
            m_new = jnp.maximum(b_last + m, jnp.max(log_w, axis=0, keepdims=True))
            kw = k * jnp.exp(log_w - m_new)
            caug_ref[h] = jnp.exp(b_last + m - m_new) * c_aug + _dot_tn(kw.astype(BF16), v_aug)
            m_ref[h:h + 1, :] = jnp.broadcast_to(m_new, (1, m_ref.shape[1]))

        parts = []
        for h in range(ML_HEADS):
            o = h_s[:, h * ML_DV:(h + 1) * ML_DV]
            parts.append(o * lax.rsqrt(jnp.mean(o * o, axis=-1, keepdims=True) + EPS))
        on = jnp.concatenate(parts, axis=-1) * ng_ref[...]
        y_ref[rows, :] = (on * jax.nn.sigmoid(og_ref[rows, :].astype(F32))).astype(y_ref.dtype)
        return carry

    lax.fori_loop(0, ts // CHUNK, chunk_body, 0)


def _mlstm(proj, small, small_t, cw, cb, gb_row, gb_col, ng, batch, seq):
    n = proj.shape[0]
    nt = seq // TS_MIX
    w = ML_HEADS * ML_DV
    wqk = 2 * ML_HEADS * ML_DQK

    def col(cb_):
        return pl.BlockSpec((TS_MIX, w), lambda b, j: (b * nt + j, cb_))

    def const(shape):
        return pl.BlockSpec(shape, lambda b, j: tuple(0 for _ in shape))

    return pl.pallas_call(
        _mlstm_kernel,
        grid=(batch, nt),
        in_specs=[
            col(4), col(5), col(6),
            pl.BlockSpec((TS_MIX, SMALL_W), lambda b, j: (b * nt + j, 0)),
            pl.BlockSpec((TS_MIX // CHUNK, 8, CHUNK), lambda b, j: (b * nt + j, 0, 0)),
            const((ML_CONV, wqk)), const((1, wqk)), const((1, SMALL_W)), const((8, 1)), const((1, w)),
        ],
        out_specs=pl.BlockSpec((TS_MIX, w), lambda b, j: (b * nt + j, 0)),
        out_shape=jax.ShapeDtypeStruct((n, w), BF16),
        scratch_shapes=[
            pltpu.VMEM((ML_HEADS, ML_DQK, ML_DV + 128), F32),
            pltpu.VMEM((8, 128), F32),
            pltpu.VMEM((TS_MIX + 2 * CONV_PAD, wqk), F32),
            pltpu.VMEM((TS_MIX, wqk), F32),
            pltpu.VMEM((CHUNK, w), F32),
        ],
        compiler_params=_cparams(("parallel", "arbitrary")),
        name="mlstm_mixer",
    )(proj, proj, proj, small, small_t, cw, cb, gb_row, gb_col, ng)


def _ssd_kernel(z_ref, xbc_ref, sm_ref, dtt_ref, cw_ref, cb_ref, dtb_c_ref, alog_c_ref,
                dtb_r_ref, alog_r_ref, d_ref, ng_ref, y_ref, state_ref, ubuf, xbc_s, y_s):
    ts = z_ref.shape[0]
    n_pair = MB_HEADS // 2
    pair_w = 2 * MB_P
    gw = MB_GROUPS * MB_N

    @pl.when(pl.program_id(1) == 0)
    def _():
        state_ref[...] = jnp.zeros_like(state_ref)
        ubuf[0:CONV_PAD, :] = jnp.zeros((CONV_PAD, ubuf.shape[1]), F32)

    xbc_s[...] = _silu(_conv_tile(xbc_ref, ubuf, cw_ref, cb_ref, MB_CONV))

    tri_l = _mask_bf16(_tri_lower(CHUNK))
    sel_x = _mask_bf16(lax.broadcasted_iota(jnp.int32, (SMALL_W, MB_W), 0) - DT_COL
                       == jnp.right_shift(lax.broadcasted_iota(jnp.int32, (SMALL_W, MB_W), 1), MB_P_LOG2))
    ur = lax.broadcasted_iota(jnp.int32, (pair_w, pair_w), 0)
    uc = lax.broadcasted_iota(jnp.int32, (pair_w, pair_w), 1)
    same_half = jnp.right_shift(ur, CHUNK_LOG2) == jnp.right_shift(uc, CHUNK_LOG2)
    tri_u2 = _mask_bf16(same_half & (ur <= uc))
    lane = lax.broadcasted_iota(jnp.int32, (CHUNK, pair_w), 1)
    row = lax.broadcasted_iota(jnp.int32, (CHUNK, pair_w), 0)
    causal2 = row >= jnp.bitwise_and(lane, CHUNK - 1)
    first_head = lane < MB_P
    a_c = -jnp.exp(alog_c_ref[...])
    a_r = -jnp.exp(alog_r_ref[...])

    def chunk_body(c, carry):
        rows = pl.ds(pl.multiple_of(c * CHUNK, CHUNK), CHUNK)
        dt_c = _softplus(sm_ref[rows, :] + dtb_c_ref[...])
        cum_c = _sel_left(tri_l, dt_c * a_c)
        dt_x = _sel_right(dt_c, sel_x)
        cum_x = _sel_right(cum_c, sel_x)
        dt_r = _softplus(dtt_ref[c] + dtb_r_ref[...])
        cum_r = _sel_right(dt_r * a_r, tri_u2)
        for p in range(n_pair):
            g = (2 * p) // (MB_HEADS // MB_GROUPS)
            ps = slice(p * pair_w, (p + 1) * pair_w)
            xs = xbc_s[rows, ps]
            bm = xbc_s[rows, MB_W + g * MB_N:MB_W + (g + 1) * MB_N].astype(BF16)
            cm = xbc_s[rows, MB_W + gw + g * MB_N:MB_W + gw + (g + 1) * MB_N].astype(BF16)
            cum_col = cum_x[:, ps]
            decay = jnp.exp(jnp.where(causal2, cum_col - cum_r[p:p + 1, :], NEG_BIG))
            cb2 = _dot_nt(cm, jnp.concatenate([bm, bm], axis=0))
            xdt = xs * dt_x[:, ps]
            x2 = jnp.concatenate([jnp.where(first_head, xdt, 0.0),
                                  jnp.where(first_head, 0.0, xdt)], axis=0).astype(BF16)
            st = state_ref[p]
            y = (_dot((cb2 * decay).astype(BF16), x2)
                 + _dot(cm, st.astype(BF16)) * jnp.exp(cum_col)
                 + d_ref[:, ps] * xs)
            y_s[:, ps] = y
            cum_last = cum_col[CHUNK - 1:CHUNK, :]
            x_dec = xdt * jnp.exp(cum_last - cum_col)
            state_ref[p] = jnp.exp(cum_last) * st + _dot_tn(bm, x_dec.astype(BF16))

        yz = y_s[...] * _silu(z_ref[rows, :].astype(F32))
        gsz = MB_W // MB_GROUPS
        parts = []
        for g in range(MB_GROUPS):
            o = yz[:, g * gsz:(g + 1) * gsz]
            parts.append(o * lax.rsqrt(jnp.mean(o * o, axis=-1, keepdims=True) + EPS))
        y_ref[rows, :] = (jnp.concatenate(parts, axis=-1) * ng_ref[...]).astype(y_ref.dtype)
        return carry

    lax.fori_loop(0, ts // CHUNK, chunk_body, 0)


def _ssd(proj, small, dt_t, cw, cb, dtb_c, alog_c, dtb_r, alog_r, d_x, ng, batch, seq):
    n = proj.shape[0]
    nt = seq // TS_MIX
    n_pair = MB_HEADS // 2

    def const(shape):
        return pl.BlockSpec(shape, lambda b, j: tuple(0 for _ in shape))

    return pl.pallas_call(
        _ssd_kernel,
        grid=(batch, nt),
        in_specs=[
            pl.BlockSpec((TS_MIX, MB_W), lambda b, j: (b * nt + j, 7)),
            pl.BlockSpec((TS_MIX, MB_CONV_DIM), lambda b, j: (b * nt + j, 4)),
            pl.BlockSpec((TS_MIX, SMALL_W), lambda b, j: (b * nt + j, 0)),
            pl.BlockSpec((TS_MIX // CHUNK, n_pair, 2 * CHUNK), lambda b, j: (b * nt + j, 0, 0)),
            const((MB_CONV, MB_CONV_DIM)), const((1, MB_CONV_DIM)),
            const((1, SMALL_W)), const((1, SMALL_W)),
            const((n_pair, 2 * CHUNK)), const((n_pair, 2 * CHUNK)),
            const((1, MB_W)), const((1, MB_W)),
        ],
        out_specs=pl.BlockSpec((TS_MIX, MB_W), lambda b, j: (b * nt + j, 0)),
        out_shape=jax.ShapeDtypeStruct((n, MB_W), BF16),
        scratch_shapes=[
            pltpu.VMEM((n_pair, MB_N, 2 * MB_P), F32),
            pltpu.VMEM((TS_MIX + 2 * CONV_PAD, MB_CONV_DIM), F32),
            pltpu.VMEM((TS_MIX, MB_CONV_DIM), F32),
            pltpu.VMEM((CHUNK, MB_W), F32),
        ],
        compiler_params=_cparams(("parallel", "arbitrary")),
        name="ssd_mixer",
    )(proj, proj, small, dt_t, cw, cb, dtb_c, alog_c, dtb_r, alog_r, d_x, ng)


def _merge_kernel(x_ref, yh_ref, ym_ref, yb_ref, g0_ref, g1_ref, g2_ref,
                  wh_ref, wm_ref, wb_ref, wo_ref, o_ref):
    mixed = jax.nn.sigmoid(g0_ref[...].astype(F32)) * _dot(yh_ref[...], wh_ref[...])
    mixed = mixed + jax.nn.sigmoid(g1_ref[...].astype(F32)) * _dot(ym_ref[...], wm_ref[...])
    mixed = mixed + jax.nn.sigmoid(g2_ref[...].astype(F32)) * _dot(yb_ref[...], wb_ref[...])
    o_ref[...] = x_ref[...] + _dot(mixed.astype(BF16), wo_ref[...])


def _merge(x2, y_hg, y_ml, y_mb, proj, w_hg, w_ml, w_mb, w_out):
    n = x2.shape[0]
    tile = lambda cb: pl.BlockSpec((TM_MERGE, D_MODEL), lambda i: (i, cb))
    wspec = pl.BlockSpec((D_MODEL, D_MODEL), lambda i: (0, 0))
    return pl.pallas_call(
        _merge_kernel,
        grid=(n // TM_MERGE,),
        in_specs=[tile(0), tile(0), tile(0), tile(0), tile(10), tile(11), tile(12),
                  wspec, wspec, wspec, wspec],
        out_specs=tile(0),
        out_shape=jax.ShapeDtypeStruct((n, D_MODEL), F32),
        compiler_params=_cparams(("parallel",)),
        name="branch_merge",
    )(x2, y_hg, y_ml, y_mb, proj, proj, proj, w_hg, w_ml, w_mb, w_out)


def _ffn_kernel(x_ref, g_ref, wug_ref, wuv_ref, cwg_ref, cwv_ref, cbg_ref, cbv_ref, wd_ref, fg_ref,
                o_ref, tail_g, tail_v, buf, *, final):
    tm = x_ref.shape[0]

    @pl.when(pl.program_id(1) == 0)
    def _():
        tail_g[...] = jnp.zeros_like(tail_g)
        tail_v[...] = jnp.zeros_like(tail_v)

    x = x_ref[...]
    hb = (x * lax.rsqrt(jnp.mean(x * x, axis=-1, keepdims=True) + EPS) * g_ref[...]).astype(BF16)

    def conv(w_ref, cw_ref, cb_ref, tail, cs):
        buf[0:CONV_PAD, :] = tail[:, cs]
        buf[CONV_PAD:CONV_PAD + tm, :] = _dot(hb, w_ref[:, cs])
        tail[:, cs] = buf[tm:tm + CONV_PAD, :]
        y = cb_ref[:, cs]
        for k in range(FFN_CONV):
            off = CONV_PAD - (FFN_CONV - 1) + k
            y = y + cw_ref[k:k + 1, cs] * buf[off:off + tm, :]
        return y

    acc = x
    for c in range(D_FF // FF_CHUNK):
        cs = slice(c * FF_CHUNK, (c + 1) * FF_CHUNK)
        u_g = conv(wug_ref, cwg_ref, cbg_ref, tail_g, cs)
        u_v = conv(wuv_ref, cwv_ref, cbv_ref, tail_v, cs)
        acc = acc + _dot((_silu(u_g) * u_v).astype(BF16), wd_ref[cs, :])
    if final:
        acc = acc * lax.rsqrt(jnp.mean(acc * acc, axis=-1, keepdims=True) + EPS) * fg_ref[...]
    o_ref[...] = acc


def _ffn(x2, g, w_ug, w_uv, cw_g, cw_v, cb_g, cb_v, w_down, final_g, batch, seq, final):
    n = x2.shape[0]
    nt = seq // TM_FFN

    def const(shape):
        return pl.BlockSpec(shape, lambda b, j: tuple(0 for _ in shape),
                            pipeline_mode=pl.Buffered(1))

    tile = pl.BlockSpec((TM_FFN, D_MODEL), lambda b, j: (b * nt + j, 0))
    return pl.pallas_call(
        functools.partial(_ffn_kernel, final=final),
        grid=(batch, nt),
        in_specs=[
            tile, const((1, D_MODEL)),
            const((D_MODEL, D_FF)), const((D_MODEL, D_FF)),
            const((FFN_CONV, D_FF)), const((FFN_CONV, D_FF)),
            const((1, D_FF)), const((1, D_FF)),
            const((D_FF, D_MODEL)), const((1, D_MODEL)),
        ],
        out_specs=tile,
        out_shape=jax.ShapeDtypeStruct((n, D_MODEL), F32),
        scratch_shapes=[
            pltpu.VMEM((CONV_PAD, D_FF), F32),
            pltpu.VMEM((CONV_PAD, D_FF), F32),
            pltpu.VMEM((TM_FFN + 2 * CONV_PAD, FF_CHUNK), F32),
        ],
        compiler_params=_cparams(("parallel", "arbitrary")),
        name="conv_gated_mlp",
    )(x2, g, w_ug, w_uv, cw_g, cw_v, cb_g, cb_v, w_down, final_g)


def kernel(x, norm1_g, w_in, hg_lb_logits, hg_norm_g, ml_conv_w, ml_conv_b, ml_gate_b, ml_norm_g,
           mb_conv_w, mb_conv_b, mb_dt_bias, mb_a_log, mb_d, mb_norm_g, w_br_hg, w_br_ml, w_br_mb,
           w_out, norm2_g, w_up, ffn_conv_w, ffn_conv_b, w_down, final_g):
    batch, seq, _ = x.shape
    n = batch * seq
    assert seq % TS_MIX == 0 and seq % TM_FFN == 0 and n % TM_PROJ == 0 and n % TM_MERGE == 0
    depth = w_in.shape[0]

    o_if = 4 * 1024 + 2 * ML_HEADS * ML_DQK + ML_HEADS * ML_DV
    o_mlo = o_if + 2 * ML_HEADS
    o_dt = o_mlo + ML_HEADS * ML_DV + MB_W + MB_CONV_DIM
    o_gate = o_dt + MB_HEADS
    w_big = jnp.concatenate(
        [w_in[:, :, :o_if], w_in[:, :, o_mlo:o_dt], w_in[:, :, o_gate:]], axis=-1).astype(BF16)
    pad = SMALL_W - 2 * ML_HEADS - MB_HEADS
    w_small = jnp.concatenate(
        [w_in[:, :, o_if:o_mlo], w_in[:, :, o_dt:o_gate],
         jnp.zeros((depth, D_MODEL, pad), w_in.dtype)], axis=-1).astype(BF16)

    lbs = _lbs(hg_lb_logits.astype(F32))
    hg_ng = jnp.tile(hg_norm_g, (1, HG_HEADS))
    ml_ng = jnp.tile(ml_norm_g, (1, ML_HEADS))
    gb_row = jnp.pad(ml_gate_b, ((0, 0), (0, SMALL_W - 2 * ML_HEADS)))
    dtb_c = jnp.pad(mb_dt_bias, ((0, 0), (DT_COL, SMALL_W - DT_COL - MB_HEADS)))
    alog_c = jnp.pad(mb_a_log, ((0, 0), (DT_COL, SMALL_W - DT_COL - MB_HEADS)))
    n_pair = MB_HEADS // 2
    dtb_r = jnp.repeat(mb_dt_bias, CHUNK, axis=-1).reshape(depth, n_pair, 2 * CHUNK)
    alog_r = jnp.repeat(mb_a_log, CHUNK, axis=-1).reshape(depth, n_pair, 2 * CHUNK)
    d_x = jnp.repeat(mb_d, MB_P, axis=-1)

    w_hg = w_br_hg.astype(BF16)
    w_ml = w_br_ml.astype(BF16)
    w_mb = w_br_mb.astype(BF16)
    w_o = w_out.astype(BF16)
    w_ug = w_up[:, :, :D_FF].astype(BF16)
    w_uv = w_up[:, :, D_FF:].astype(BF16)
    w_d = w_down.astype(BF16)

    x2 = x.reshape(n, D_MODEL)
    row = lambda a: a.reshape(1, -1)
    for l in range(depth):
        proj, small = _inproj(x2, row(norm1_g[l]), w_big[l], w_small[l])
        chunks = small.reshape(n // CHUNK, CHUNK, SMALL_W)
        small_t = jnp.swapaxes(chunks[:, :, :2 * ML_HEADS], 1, 2)
        dt_t = jnp.swapaxes(chunks[:, :, DT_COL:DT_COL + MB_HEADS], 1, 2).reshape(
            n // CHUNK, n_pair, 2 * CHUNK)

        y_hg = _hgrn2(proj, row(lbs[l]), row(hg_ng[l]), batch, seq)
        y_ml = _mlstm(proj, small, small_t, ml_conv_w[l], row(ml_conv_b[l]), row(gb_row[l]),
                      ml_gate_b[l].reshape(2 * ML_HEADS, 1), row(ml_ng[l]), batch, seq)
        y_mb = _ssd(proj, small, dt_t, mb_conv_w[l], row(mb_conv_b[l]), row(dtb_c[l]), row(alog_c[l]),
                    dtb_r[l], alog_r[l], row(d_x[l]), row(mb_norm_g[l]), batch, seq)
        x2 = _merge(x2, y_hg, y_ml, y_mb, proj, w_hg[l], w_ml[l], w_mb[l], w_o[l])
        x2 = _ffn(x2, row(norm2_g[l]), w_ug[l], w_uv[l], ffn_conv_w[l, :, :D_FF], ffn_conv_w[l, :, D_FF:],
                  row(ffn_conv_b[l, :D_FF]), row(ffn_conv_b[l, D_FF:]), w_d[l], row(final_g),
                  batch, seq, final=(l == depth - 1))
    return x2.reshape(batch, seq, D_MODEL)
```

```python
import functools
import math

import jax
import jax.numpy as jnp
from jax import lax
from jax.experimental import pallas as pl
from jax.experimental.pallas import tpu as pltpu

F32 = jnp.float32
BF16 = jnp.bfloat16

D_MODEL = 1024
DEPTH = 4
CHUNK = 64
CHUNK_LOG2 = 6
SUB = 16
SUBLANES = 8
LOG2E = math.log2(math.e)
EPS = 1e-6
NEG_BIG = -1e30
HG_HEADS = 8
HG_D = 128
ML_HEADS = 4
ML_DQK = 128
ML_DV = 256
ML_CONV = 4
MB_HEADS = 16
MB_P = 64
MB_P_LOG2 = 6
MB_GROUPS = 4
MB_N = 128
MB_CONV = 4
MB_W = MB_HEADS * MB_P
MB_CONV_DIM = MB_W + 2 * MB_GROUPS * MB_N
D_FF = 2816
FFN_CONV = 3
FF_CHUNK = 1408
SMALL_W = 128
DT_COL = 8
CONV_PAD = 8

VMEM_LIMIT = 56 * 1024 * 1024

TM_PROJ = 1024
TN_PROJ = 1024
TS_MIX = 512
CHUNK_UNROLL = 2
TM_MERGE = 512
TM_FFN = 512


def _silu(x):
    return x * jax.nn.sigmoid(x)


def _softplus(x):
    return jnp.maximum(x, 0.0) + jnp.log1p(jnp.exp(-jnp.abs(x)))


def _log_sigmoid(x):
    return jnp.minimum(x, 0.0) - jnp.log1p(jnp.exp(-jnp.abs(x)))


def _dot(a, b):
    return jnp.dot(a, b, preferred_element_type=F32)


def _dot_nt(a, b):
    return lax.dot_general(a, b, (((1,), (1,)), ((), ())), preferred_element_type=F32)


def _dot_tn(a, b):
    return lax.dot_general(a, b, (((0,), (0,)), ((), ())), preferred_element_type=F32)


def _split3(a):
    hi = a.astype(BF16)
    r = a - hi.astype(F32)
    mid = r.astype(BF16)
    lo = (r - mid.astype(F32)).astype(BF16)
    return hi, mid, lo


def _sel_left(sel, a):
    hi, mid, lo = _split3(a)
    return _dot(sel, hi) + (_dot(sel, mid) + _dot(sel, lo))


def _sel_right(a, sel):
    hi, mid, lo = _split3(a)
    return _dot(hi, sel) + (_dot(mid, sel) + _dot(lo, sel))


def _tri_lower(n):
    r = lax.broadcasted_iota(jnp.int32, (n, n), 0)
    c = lax.broadcasted_iota(jnp.int32, (n, n), 1)
    return r >= c


def _mask_bf16(m):
    return jnp.where(m, 1.0, 0.0).astype(BF16)


def _cparams(sem):
    return pltpu.CompilerParams(dimension_semantics=sem, vmem_limit_bytes=VMEM_LIMIT)


def _lbs_kernel(lg_ref, o_ref):
    lg = lg_ref[...]
    mx = jnp.max(lg, axis=0, keepdims=True)
    e = jnp.exp(lg - mx)
    p = e / jnp.sum(e, axis=0, keepdims=True)
    acc = jnp.zeros_like(p[0:1])
    rows = []
    for l in range(lg.shape[0]):
        acc = acc + p[l:l + 1]
        rows.append(acc - p[0:1])
    o_ref[...] = jnp.concatenate(rows, axis=0)


def _lbs(logits):
    return pl.pallas_call(
        _lbs_kernel,
        out_shape=jax.ShapeDtypeStruct(logits.shape, F32),
        name="hgrn2_lower_bounds",
    )(logits)


def _inproj_kernel(x_ref, g_ref, w_ref, ws_ref, o_ref, os_ref, h_ref):
    @pl.when(pl.program_id(1) == 0)
    def _():
        x = x_ref[...]
        ms = jnp.mean(x * x, axis=-1, keepdims=True)
        hb = (x * lax.rsqrt(ms + EPS) * g_ref[...]).astype(BF16)
        h_ref[...] = hb
        os_ref[...] = _dot(hb, ws_ref[...])

    o_ref[...] = _dot(h_ref[...], w_ref[...]).astype(o_ref.dtype)


def _inproj(x2, g, w_big, w_small):
    n = x2.shape[0]
    nb = w_big.shape[1]
    return pl.pallas_call(
        _inproj_kernel,
        grid=(n // TM_PROJ, nb // TN_PROJ),
        in_specs=[
            pl.BlockSpec((TM_PROJ, D_MODEL), lambda i, j: (i, 0)),
            pl.BlockSpec((1, D_MODEL), lambda i, j: (0, 0)),
            pl.BlockSpec((D_MODEL, TN_PROJ), lambda i, j: (0, j)),
            pl.BlockSpec((D_MODEL, SMALL_W), lambda i, j: (0, 0)),
        ],
        out_specs=[
            pl.BlockSpec((TM_PROJ, TN_PROJ), lambda i, j: (i, j)),
            pl.BlockSpec((TM_PROJ, SMALL_W), lambda i, j: (i, 0)),
        ],
        out_shape=[
            jax.ShapeDtypeStruct((n, nb), BF16),
            jax.ShapeDtypeStruct((n, SMALL_W), F32),
        ],
        scratch_shapes=[pltpu.VMEM((TM_PROJ, D_MODEL), BF16)],
        compiler_params=_cparams(("parallel", "arbitrary")),
        name="in_projection",
    )(x2, g, w_big, w_small)


def _conv_tile(raw_ref, ubuf, cw_ref, cb_ref, width):
    ts = raw_ref.shape[0]
    ubuf[CONV_PAD:CONV_PAD + ts, :] = raw_ref[...].astype(F32)
    y = cb_ref[...]
    for k in range(width):
        off = CONV_PAD - (width - 1) + k
        y = y + cw_ref[k:k + 1, :] * ubuf[off:off + ts, :]
    ubuf[0:CONV_PAD, :] = ubuf[ts:ts + CONV_PAD, :]
    return y


def _hgrn2_kernel(q_ref, f_ref, i_ref, g_ref, lb_ref, ng_ref, y_ref,
                  state_ref, c_s):
    @pl.when(pl.program_id(1) == 0)
    def _():
        state_ref[...] = jnp.zeros_like(state_ref)

    ts = q_ref.shape[0]
    lb = lb_ref[...]
    tri = _mask_bf16(_tri_lower(CHUNK))
    n_sub = CHUNK // SUB
    ones_rhs = jnp.ones((HG_D, HG_D), BF16)
    lane = lax.broadcasted_iota(jnp.int32, (SUBLANES, HG_D), 1)
    row = lax.broadcasted_iota(jnp.int32, (SUBLANES, HG_D), 0)

    heads = range(HG_HEADS)
    hsl = [slice(h * HG_D, (h + 1) * HG_D) for h in heads]
    zero_tail = jnp.zeros((HG_D - CHUNK, HG_D), BF16)

    def chunk_body(c, carry):
        rows = pl.ds(pl.multiple_of(c * CHUNK, CHUNK), CHUNK)
        sig = jax.nn.sigmoid(f_ref[rows, :].astype(F32))
        fgate = lb + (1.0 - lb) * sig
        b = _sel_left(tri, jnp.log(fgate) * LOG2E)
        cc = b - jnp.log(jnp.maximum(1.0 - fgate, 0.0)) * LOG2E
        c_s[...] = cc
        q = _silu(q_ref[rows, :].astype(F32))
        v16 = i_ref[rows, :]

        st = [state_ref[h] for h in heads]
        q_dec = (q * jnp.exp2(b)).astype(BF16)
        b_last = b[CHUNK - 1:CHUNK, :]
        k_dec = jnp.exp2(b_last - cc).astype(BF16)
        st_decay = jnp.exp2(b_last)
        o_inter = [_dot_nt(q_dec[:, hsl[h]], st[h].astype(BF16)) for h in heads]
        for h in heads:
            state_ref[h] = st_decay[:, hsl[h]] * st[h] + _dot_tn(v16[:, hsl[h]], k_dec[:, hsl[h]])

        zs = []
        for i in range(n_sub):
            lo = i * SUB
            for s in range(SUB):
                r0 = lo + (s // SUBLANES) * SUBLANES
                zs.append(q[r0:lo + SUB] * jnp.exp2(b[r0:lo + SUB] - c_s[pl.ds(lo + s, 1), :]))
        z_rows = sum(z.shape[0] for z in zs)
        z_all = jnp.concatenate([z[:, hsl[h]] for h in heads for z in zs], axis=0)
        r = _dot(z_all.astype(BF16), ones_rhs)

        a_off = []
        for i in range(1, n_sub):
            lo = i * SUB
            bref = b[lo - 1:lo, :]
            q_i = (q[lo:lo + SUB] * jnp.exp2(b[lo:lo + SUB] - bref)).astype(BF16)
            k_i = jnp.exp2(bref - cc[:lo]).astype(BF16)
            zero_rows = jnp.zeros((HG_D - lo, HG_D), BF16)
            a_off.append([_dot_nt(q_i[:, hsl[h]], jnp.concatenate([k_i[:, hsl[h]], zero_rows], axis=0))
                          for h in heads])

        outs = []
        for h in heads:
            a_rows = []
            off = h * z_rows
            for i in range(n_sub):
                lo = i * SUB
                a_i = a_off[i - 1][h] if i > 0 else jnp.zeros((SUB, HG_D), F32)
                tiles = [a_i[j * SUBLANES:(j + 1) * SUBLANES] for j in range(SUB // SUBLANES)]
                for s in range(SUB):
                    for j in range(s // SUBLANES, SUB // SUBLANES):
                        tiles[j] = jnp.where(lane == lo + s, r[off:off + SUBLANES], tiles[j])
                        off += SUBLANES
                for j in range(SUB // SUBLANES):
                    a_rows.append(jnp.where(lane - lo <= row + j * SUBLANES, tiles[j], 0.0))
            a_full = jnp.concatenate(a_rows, axis=0).astype(BF16)
            v_pad = jnp.concatenate([v16[:, hsl[h]], zero_tail], axis=0)
            outs.append(o_inter[h] + _dot(a_full, v_pad))
        parts = []
        for o in outs:
            parts.append(o * lax.rsqrt(jnp.mean(o * o, axis=-1, keepdims=True) + EPS))
        on = jnp.concatenate(parts, axis=-1) * ng_ref[...]
        y_ref[rows, :] = (on * _silu(g_ref[rows, :].astype(F32))).astype(y_ref.dtype)
        return carry

    lax.fori_loop(0, ts // CHUNK, chunk_body, 0)


def _hgrn2(proj, lb, ng, batch, seq):
    n = proj.shape[0]
    nt = seq // TS_MIX
    w = HG_HEADS * HG_D

    def col(cb):
        return pl.BlockSpec((TS_MIX, w), lambda b, j: (b * nt + j, cb))

    vec = pl.BlockSpec((1, w), lambda b, j: (0, 0))
    return pl.pallas_call(
        _hgrn2_kernel,
        grid=(batch, nt),
        in_specs=[col(0), col(1), col(2), col(3), vec, vec],
        out_specs=pl.BlockSpec((TS_MIX, w), lambda b, j: (b * nt + j, 0)),
        out_shape=jax.ShapeDtypeStruct((n, w), BF16),
        scratch_shapes=[pltpu.VMEM((HG_HEADS, HG_D, HG_D), F32), pltpu.VMEM((CHUNK, w), F32)],
        compiler_params=_cparams(("parallel", "arbitrary")),
        name="hgrn2_mixer",
    )(proj, proj, proj, proj, lb, ng)


def _mlstm_kernel(qk_ref, v_ref, og_ref, sm_ref, smt_ref, cw_ref, cb_ref, gbr_ref, gbc_ref, ng_ref,
                  y_ref, caug_ref, m_ref, ubuf, qk_s):
    ts = qk_ref.shape[0]
    half = ML_HEADS * ML_DQK

    @pl.when(pl.program_id(1) == 0)
    def _():
        caug_ref[...] = jnp.zeros_like(caug_ref)
        m_ref[...] = jnp.zeros_like(m_ref)
        ubuf[0:CONV_PAD, :] = jnp.zeros((CONV_PAD, ubuf.shape[1]), F32)

    qk = _silu(_conv_tile(qk_ref, ubuf, cw_ref, cb_ref, ML_CONV))
    qk_s[:, :half] = qk[:, :half]
    qk_s[:, half:] = qk[:, half:] * (ML_DQK ** -0.5)

    tri_b = _tri_lower(CHUNK)
    tri_l = _mask_bf16(tri_b)
    tri_u = _mask_bf16(lax.broadcasted_iota(jnp.int32, (CHUNK, CHUNK), 0)
                       <= lax.broadcasted_iota(jnp.int32, (CHUNK, CHUNK), 1))
    ones_col = jnp.where(lax.broadcasted_iota(jnp.int32, (CHUNK, 128), 1) == 0, 1.0, 0.0)

    def chunk_body(c, carry):
        rows = pl.ds(pl.multiple_of(c * CHUNK, CHUNK), CHUNK)
        pre_c = sm_ref[rows, :] + gbr_ref[...]
        pre_r = smt_ref[c] + gbc_ref[...]
        cum_c = _sel_left(tri_l, _log_sigmoid(pre_c))
        cum_r = _sel_right(_log_sigmoid(pre_r), tri_u)
        heads = range(ML_HEADS)
        m_all = m_ref[...]
        m_old = [m_all[h:h + 1, 0:1] for h in heads]
        b_col = [cum_c[:, ML_HEADS + h:ML_HEADS + h + 1] for h in heads]
        i_col = [pre_c[:, h:h + 1] for h in heads]
        log_d = [jnp.where(tri_b, b_col[h] - cum_r[ML_HEADS + h:ML_HEADS + h + 1, :] + pre_r[h:h + 1, :],
                           NEG_BIG) for h in heads]
        log_inter = [b_col[h] + m_old[h] for h in heads]
        m_t = [jnp.maximum(jnp.max(log_d[h], axis=-1, keepdims=True), log_inter[h]) for h in heads]
        b_last = [b_col[h][CHUNK - 1:CHUNK, :] for h in heads]
        log_w = [b_last[h] - b_col[h] + i_col[h] for h in heads]
        m_new = [jnp.maximum(b_last[h] + m_old[h], jnp.max(log_w[h], axis=0, keepdims=True)) for h in heads]
        m_ref[...] = jnp.concatenate(
            [jnp.broadcast_to(m_new[h], (1, m_ref.shape[1])) for h in heads] + [m_all[ML_HEADS:]], axis=0)

        qb = [qk_s[rows, h * ML_DQK:(h + 1) * ML_DQK].astype(BF16) for h in heads]
        k = [qk_s[rows, half + h * ML_DQK:half + (h + 1) * ML_DQK] for h in heads]
        v_aug = [jnp.concatenate([v_ref[rows, h * ML_DV:(h + 1) * ML_DV].astype(F32), ones_col],
                                 axis=-1).astype(BF16) for h in heads]
        qk = [_dot_nt(qb[h], k[h].astype(BF16)) for h in heads]
        c_aug = [caug_ref[h] for h in heads]
        inter = [_dot(qb[h], c_aug[h].astype(BF16)) * jnp.exp(log_inter[h] - m_t[h]) for h in heads]
        s = [(qk[h] * jnp.exp(log_d[h] - m_t[h])).astype(BF16) for h in heads]
        num = [_dot(s[h], v_aug[h]) + inter[h] for h in heads]
        kw = [(k[h] * jnp.exp(log_w[h] - m_new[h])).astype(BF16) for h in heads]
        for h in heads:
            caug_ref[h] = jnp.exp(b_last[h] + m_old[h] - m_new[h]) * c_aug[h] + _dot_tn(kw[h], v_aug[h])

        parts = []
        for h in heads:
            denom = jnp.maximum(jnp.abs(num[h][:, ML_DV:ML_DV + 1]), jnp.exp(-m_t[h]))
            o = num[h][:, :ML_DV] / denom
            parts.append(o * lax.rsqrt(jnp.mean(o * o, axis=-1, keepdims=True) + EPS))
        on = jnp.concatenate(parts, axis=-1) * ng_ref[...]
        y_ref[rows, :] = (on * jax.nn.sigmoid(og_ref[rows, :].astype(F32))).astype(y_ref.dtype)
        return carry

    lax.fori_loop(0, ts // CHUNK, chunk_body, 0, unroll=CHUNK_UNROLL)


def _mlstm(proj, small, small_t, cw, cb, gb_row, gb_col, ng, batch, seq):
    n = proj.shape[0]
    nt = seq // TS_MIX
    w = ML_HEADS * ML_DV
    wqk = 2 * ML_HEADS * ML_DQK

    def col(cb_):
        return pl.BlockSpec((TS_MIX, w), lambda b, j: (b * nt + j, cb_))

    def const(shape):
        return pl.BlockSpec(shape, lambda b, j: tuple(0 for _ in shape))

    return pl.pallas_call(
        _mlstm_kernel,
        grid=(batch, nt),
        in_specs=[
            col(4), col(5), col(6),
            pl.BlockSpec((TS_MIX, SMALL_W), lambda b, j: (b * nt + j, 0)),
            pl.BlockSpec((TS_MIX // CHUNK, 8, CHUNK), lambda b, j: (b * nt + j, 0, 0)),
            const((ML_CONV, wqk)), const((1, wqk)), const((1, SMALL_W)), const((8, 1)), const((1, w)),
        ],
        out_specs=pl.BlockSpec((TS_MIX, w), lambda b, j: (b * nt + j, 0)),
        out_shape=jax.ShapeDtypeStruct((n, w), BF16),
        scratch_shapes=[
            pltpu.VMEM((ML_HEADS, ML_DQK, ML_DV + 128), F32),
            pltpu.VMEM((8, 128), F32),
            pltpu.VMEM((TS_MIX + 2 * CONV_PAD, wqk), F32),
            pltpu.VMEM((TS_MIX, wqk), F32),
        ],
        compiler_params=_cparams(("parallel", "arbitrary")),
        name="mlstm_mixer",
    )(proj, proj, proj, small, small_t, cw, cb, gb_row, gb_col, ng)


def _ssd_kernel(z_ref, xbc_ref, sm_ref, dtt_ref, cw_ref, cb_ref, dtb_c_ref, alog_c_ref,
                dtb_r_ref, alog_r_ref, d_ref, ng_ref, y_ref, state_ref, ubuf, xbc_s):
    ts = z_ref.shape[0]
    n_pair = MB_HEADS // 2
    pair_w = 2 * MB_P
    gw = MB_GROUPS * MB_N

    @pl.when(pl.program_id(1) == 0)
    def _():
        state_ref[...] = jnp.zeros_like(state_ref)
        ubuf[0:CONV_PAD, :] = jnp.zeros((CONV_PAD, ubuf.shape[1]), F32)

    xbc_s[...] = _silu(_conv_tile(xbc_ref, ubuf, cw_ref, cb_ref, MB_CONV))

    tri_l = _mask_bf16(_tri_lower(CHUNK))
    sel_x = _mask_bf16(lax.broadcasted_iota(jnp.int32, (SMALL_W, MB_W), 0) - DT_COL
                       == jnp.right_shift(lax.broadcasted_iota(jnp.int32, (SMALL_W, MB_W), 1), MB_P_LOG2))
    ur = lax.broadcasted_iota(jnp.int32, (pair_w, pair_w), 0)
    uc = lax.broadcasted_iota(jnp.int32, (pair_w, pair_w), 1)
    same_half = jnp.right_shift(ur, CHUNK_LOG2) == jnp.right_shift(uc, CHUNK_LOG2)
    tri_u2 = _mask_bf16(same_half & (ur <= uc))
    causal = (lax.broadcasted_iota(jnp.int32, (CHUNK, MB_W), 0)
              >= jnp.bitwise_and(lax.broadcasted_iota(jnp.int32, (CHUNK, MB_W), 1), CHUNK - 1))
    first_head = lax.broadcasted_iota(jnp.int32, (CHUNK, pair_w), 1) < MB_P
    a_c = -jnp.exp(alog_c_ref[...])
    a_r = -jnp.exp(alog_r_ref[...])

    def chunk_body(c, carry):
        rows = pl.ds(pl.multiple_of(c * CHUNK, CHUNK), CHUNK)
        dt_c = _softplus(sm_ref[rows, :] + dtb_c_ref[...])
        cum_c = _sel_left(tri_l, dt_c * a_c)
        dt_x = _sel_right(dt_c, sel_x)
        cum_x = _sel_right(cum_c, sel_x)
        dt_r = _softplus(dtt_ref[c] + dtb_r_ref[...])
        cum_r = _sel_right(dt_r * a_r, tri_u2)
        pairs = range(n_pair)
        grp = [(2 * p) // (MB_HEADS // MB_GROUPS) for p in pairs]
        lanes = [slice(p * pair_w, (p + 1) * pair_w) for p in pairs]
        xs = xbc_s[rows, :MB_W]
        xdt = xs * dt_x
        cum_row = jnp.concatenate([cum_r[p:p + 1, :] for p in pairs], axis=-1)
        decay = jnp.exp(jnp.where(causal, cum_x - cum_row, NEG_BIG))
        cum_last = cum_x[CHUNK - 1:CHUNK, :]
        x_dec = (xdt * jnp.exp(cum_last - cum_x)).astype(BF16)
        st_decay = jnp.exp(cum_last)
        carry_w = jnp.exp(cum_x)
        bm = [xbc_s[rows, MB_W + g * MB_N:MB_W + (g + 1) * MB_N].astype(BF16) for g in range(MB_GROUPS)]
        cm = [xbc_s[rows, MB_W + gw + g * MB_N:MB_W + gw + (g + 1) * MB_N].astype(BF16)
              for g in range(MB_GROUPS)]
        cb2 = [_dot_nt(cm[g], jnp.concatenate([bm[g], bm[g]], axis=0)) for g in range(MB_GROUPS)]
        x2 = [jnp.concatenate([jnp.where(first_head, xdt[:, lanes[p]], 0.0),
                               jnp.where(first_head, 0.0, xdt[:, lanes[p]])], axis=0).astype(BF16)
              for p in pairs]
        st = [state_ref[p] for p in pairs]
        y_inter = [_dot(cm[grp[p]], st[p].astype(BF16)) for p in pairs]
        y_intra = [_dot((cb2[grp[p]] * decay[:, lanes[p]]).astype(BF16), x2[p]) for p in pairs]
        for p in pairs:
            state_ref[p] = st_decay[:, lanes[p]] * st[p] + _dot_tn(bm[grp[p]], x_dec[:, lanes[p]])
        y = (jnp.concatenate(y_intra, axis=-1) + jnp.concatenate(y_inter, axis=-1) * carry_w
             + d_ref[...] * xs)

        yz = y * _silu(z_ref[rows, :].astype(F32))
        gsz = MB_W // MB_GROUPS
        parts = []
        for g in range(MB_GROUPS):
            o = yz[:, g * gsz:(g + 1) * gsz]
            parts.append(o * lax.rsqrt(jnp.mean(o * o, axis=-1, keepdims=True) + EPS))
        y_ref[rows, :] = (jnp.concatenate(parts, axis=-1) * ng_ref[...]).astype(y_ref.dtype)
        return carry

    lax.fori_loop(0, ts // CHUNK, chunk_body, 0, unroll=CHUNK_UNROLL)


def _ssd(proj, small, dt_t, cw, cb, dtb_c, alog_c, dtb_r, alog_r, d_x, ng, batch, seq):
    n = proj.shape[0]
    nt = seq // TS_MIX
    n_pair = MB_HEADS // 2

    def const(shape):
        return pl.BlockSpec(shape, lambda b, j: tuple(0 for _ in shape))

    return pl.pallas_call(
        _ssd_kernel,
        grid=(batch, nt),
        in_specs=[
            pl.BlockSpec((TS_MIX, MB_W), lambda b, j: (b * nt + j, 7)),
            pl.BlockSpec((TS_MIX, MB_CONV_DIM), lambda b, j: (b * nt + j, 4)),
            pl.BlockSpec((TS_MIX, SMALL_W), lambda b, j: (b * nt + j, 0)),
            pl.BlockSpec((TS_MIX // CHUNK, n_pair, 2 * CHUNK), lambda b, j: (b * nt + j, 0, 0)),
            const((MB_CONV, MB_CONV_DIM)), const((1, MB_CONV_DIM)),
            const((1, SMALL_W)), const((1, SMALL_W)),
            const((n_pair, 2 * CHUNK)), const((n_pair, 2 * CHUNK)),
            const((1, MB_W)), const((1, MB_W)),
        ],
        out_specs=pl.BlockSpec((TS_MIX, MB_W), lambda b, j: (b * nt + j, 0)),
        out_shape=jax.ShapeDtypeStruct((n, MB_W), BF16),
        scratch_shapes=[
            pltpu.VMEM((n_pair, MB_N, 2 * MB_P), F32),
            pltpu.VMEM((TS_MIX + 2 * CONV_PAD, MB_CONV_DIM), F32),
            pltpu.VMEM((TS_MIX, MB_CONV_DIM), F32),
        ],
        compiler_params=_cparams(("parallel", "arbitrary")),
        name="ssd_mixer",
    )(proj, proj, small, dt_t, cw, cb, dtb_c, alog_c, dtb_r, alog_r, d_x, ng)


def _merge_kernel(x_ref, yh_ref, ym_ref, yb_ref, g0_ref, g1_ref, g2_ref,
                  wh_ref, wm_ref, wb_ref, wo_ref, o_ref):
    mixed = jax.nn.sigmoid(g0_ref[...].astype(F32)) * _dot(yh_ref[...], wh_ref[...])
    mixed = mixed + jax.nn.sigmoid(g1_ref[...].astype(F32)) * _dot(ym_ref[...], wm_ref[...])
    mixed = mixed + jax.nn.sigmoid(g2_ref[...].astype(F32)) * _dot(yb_ref[...], wb_ref[...])
    o_ref[...] = x_ref[...] + _dot(mixed.astype(BF16), wo_ref[...])


def _merge(x2, y_hg, y_ml, y_mb, proj, w_hg, w_ml, w_mb, w_out):
    n = x2.shape[0]
    tile = lambda cb: pl.BlockSpec((TM_MERGE, D_MODEL), lambda i: (i, cb))
    wspec = pl.BlockSpec((D_MODEL, D_MODEL), lambda i: (0, 0))
    return pl.pallas_call(
        _merge_kernel,
        grid=(n // TM_MERGE,),
        in_specs=[tile(0), tile(0), tile(0), tile(0), tile(10), tile(11), tile(12),
                  wspec, wspec, wspec, wspec],
        out_specs=tile(0),
        out_shape=jax.ShapeDtypeStruct((n, D_MODEL), F32),
        compiler_params=_cparams(("parallel",)),
        name="branch_merge",
    )(x2, y_hg, y_ml, y_mb, proj, proj, proj, w_hg, w_ml, w_mb, w_out)


def _ffn_kernel(x_ref, g_ref, wug_ref, wuv_ref, cwg_ref, cwv_ref, cbg_ref, cbv_ref, wd_ref, fg_ref,
                o_ref, tail_g, tail_v, buf, *, final):
    tm = x_ref.shape[0]

    @pl.when(pl.program_id(1) == 0)
    def _():
        tail_g[...] = jnp.zeros_like(tail_g)
        tail_v[...] = jnp.zeros_like(tail_v)

    x = x_ref[...]
    hb = (x * lax.rsqrt(jnp.mean(x * x, axis=-1, keepdims=True) + EPS) * g_ref[...]).astype(BF16)

    def conv(w_ref, cw_ref, cb_ref, tail, cs):
        buf[0:CONV_PAD, :] = tail[:, cs]
        buf[CONV_PAD:CONV_PAD + tm, :] = _dot(hb, w_ref[:, cs])
        tail[:, cs] = buf[tm:tm + CONV_PAD, :]
        y = cb_ref[:, cs]
        for k in range(FFN_CONV):
            off = CONV_PAD - (FFN_CONV - 1) + k
            y = y + cw_ref[k:k + 1, cs] * buf[off:off + tm, :]
        return y

    acc = x
    for c in range(D_FF // FF_CHUNK):
        cs = slice(c * FF_CHUNK, (c + 1) * FF_CHUNK)
        u_g = conv(wug_ref, cwg_ref, cbg_ref, tail_g, cs)
        u_v = conv(wuv_ref, cwv_ref, cbv_ref, tail_v, cs)
        acc = acc + _dot((_silu(u_g) * u_v).astype(BF16), wd_ref[cs, :])
    if final:
        acc = acc * lax.rsqrt(jnp.mean(acc * acc, axis=-1, keepdims=True) + EPS) * fg_ref[...]
    o_ref[...] = acc


def _ffn(x2, g, w_ug, w_uv, cw_g, cw_v, cb_g, cb_v, w_down, final_g, batch, seq, final):
    n = x2.shape[0]
    nt = seq // TM_FFN

    def const(shape):
        return pl.BlockSpec(shape, lambda b, j: tuple(0 for _ in shape),
                            pipeline_mode=pl.Buffered(1))

    tile = pl.BlockSpec((TM_FFN, D_MODEL), lambda b, j: (b * nt + j, 0))
    return pl.pallas_call(
        functools.partial(_ffn_kernel, final=final),
        grid=(batch, nt),
        in_specs=[
            tile, const((1, D_MODEL)),
            const((D_MODEL, D_FF)), const((D_MODEL, D_FF)),
            const((FFN_CONV, D_FF)), const((FFN_CONV, D_FF)),
            const((1, D_FF)), const((1, D_FF)),
            const((D_FF, D_MODEL)), const((1, D_MODEL)),
        ],
        out_specs=tile,
        out_shape=jax.ShapeDtypeStruct((n, D_MODEL), F32),
        scratch_shapes=[
            pltpu.VMEM((CONV_PAD, D_FF), F32),
            pltpu.VMEM((CONV_PAD, D_FF), F32),
            pltpu.VMEM((TM_FFN + 2 * CONV_PAD, FF_CHUNK), F32),
        ],
        compiler_params=_cparams(("parallel", "arbitrary")),
        name="conv_gated_mlp",
    )(x2, g, w_ug, w_uv, cw_g, cw_v, cb_g, cb_v, w_down, final_g)


def kernel(x, norm1_g, w_in, hg_lb_logits, hg_norm_g, ml_conv_w, ml_conv_b, ml_gate_b, ml_norm_g,
           mb_conv_w, mb_conv_b, mb_dt_bias, mb_a_log, mb_d, mb_norm_g, w_br_hg, w_br_ml, w_br_mb,
           w_out, norm2_g, w_up, ffn_conv_w, ffn_conv_b, w_down, final_g):
    batch, seq, _ = x.shape
    n = batch * seq
    assert seq % TS_MIX == 0 and seq % TM_FFN == 0 and n % TM_PROJ == 0 and n % TM_MERGE == 0
    depth = w_in.shape[0]

    o_if = 4 * 1024 + 2 * ML_HEADS * ML_DQK + ML_HEADS * ML_DV
    o_mlo = o_if + 2 * ML_HEADS
    o_dt = o_mlo + ML_HEADS * ML_DV + MB_W + MB_CONV_DIM
    o_gate = o_dt + MB_HEADS
    w_big = jnp.concatenate(
        [w_in[:, :, :o_if], w_in[:, :, o_mlo:o_dt], w_in[:, :, o_gate:]], axis=-1).astype(BF16)
    pad = SMALL_W - 2 * ML_HEADS - MB_HEADS
    w_small = jnp.concatenate(
        [w_in[:, :, o_if:o_mlo], w_in[:, :, o_dt:o_gate],
         jnp.zeros((depth, D_MODEL, pad), w_in.dtype)], axis=-1).astype(BF16)

    lbs = _lbs(hg_lb_logits.astype(F32))
    hg_ng = jnp.tile(hg_norm_g, (1, HG_HEADS))
    ml_ng = jnp.tile(ml_norm_g, (1, ML_HEADS))
    gb_row = jnp.pad(ml_gate_b, ((0, 0), (0, SMALL_W - 2 * ML_HEADS)))
    dtb_c = jnp.pad(mb_dt_bias, ((0, 0), (DT_COL, SMALL_W - DT_COL - MB_HEADS)))
    alog_c = jnp.pad(mb_a_log, ((0, 0), (DT_COL, SMALL_W - DT_COL - MB_HEADS)))
    n_pair = MB_HEADS // 2
    dtb_r = jnp.repeat(mb_dt_bias, CHUNK, axis=-1).reshape(depth, n_pair, 2 * CHUNK)
    alog_r = jnp.repeat(mb_a_log, CHUNK, axis=-1).reshape(depth, n_pair, 2 * CHUNK)
    d_x = jnp.repeat(mb_d, MB_P, axis=-1)

    w_hg = w_br_hg.astype(BF16)
    w_ml = w_br_ml.astype(BF16)
    w_mb = w_br_mb.astype(BF16)
    w_o = w_out.astype(BF16)
    w_ug = w_up[:, :, :D_FF].astype(BF16)
    w_uv = w_up[:, :, D_FF:].astype(BF16)
    w_d = w_down.astype(BF16)

    x2 = x.reshape(n, D_MODEL)
    row = lambda a: a.reshape(1, -1)
    for l in range(depth):
        proj, small = _inproj(x2, row(norm1_g[l]), w_big[l], w_small[l])
        chunks = small.reshape(n // CHUNK, CHUNK, SMALL_W)
        small_t = jnp.swapaxes(chunks[:, :, :2 * ML_HEADS], 1, 2)
        dt_t = jnp.swapaxes(chunks[:, :, DT_COL:DT_COL + MB_HEADS], 1, 2).reshape(
            n // CHUNK, n_pair, 2 * CHUNK)

        y_hg = _hgrn2(proj, row(lbs[l]), row(hg_ng[l]), batch, seq)
        y_ml = _mlstm(proj, small, small_t, ml_conv_w[l], row(ml_conv_b[l]), row(gb_row[l]),
                      ml_gate_b[l].reshape(2 * ML_HEADS, 1), row(ml_ng[l]), batch, seq)
        y_mb = _ssd(proj, small, dt_t, mb_conv_w[l], row(mb_conv_b[l]), row(dtb_c[l]), row(alog_c[l]),
                    dtb_r[l], alog_r[l], row(d_x[l]), row(mb_norm_g[l]), batch, seq)
        x2 = _merge(x2, y_hg, y_ml, y_mb, proj, w_hg[l], w_ml[l], w_mb[l], w_o[l])
        x2 = _ffn(x2, row(norm2_g[l]), w_ug[l], w_uv[l], ffn_conv_w[l, :, :D_FF], ffn_conv_w[l, :, D_FF:],
                  row(ffn_conv_b[l, :D_FF]), row(ffn_conv_b[l, D_FF:]), w_d[l], row(final_g),
                  batch, seq, final=(l == depth - 1))
    return x2.reshape(batch, seq, D_MODEL)
```

```python
import functools
import math

import jax
import jax.numpy as jnp
from jax import lax
from jax.experimental import pallas as pl
from jax.experimental.pallas import tpu as pltpu

F32 = jnp.float32
BF16 = jnp.bfloat16

D_MODEL = 1024
DEPTH = 4
CHUNK = 64
CHUNK_LOG2 = 6
SUB = 8
SUBLANES = 8
LANES = 128
LOG2E = math.log2(math.e)
EPS = 1e-6
NEG_BIG = -1e30
HG_HEADS = 8
HG_D = 128
ML_HEADS = 4
ML_DQK = 128
ML_DV = 256
ML_CONV = 4
MB_HEADS = 16
MB_P = 64
MB_P_LOG2 = 6
MB_GROUPS = 4
MB_N = 128
MB_CONV = 4
MB_W = MB_HEADS * MB_P
MB_CONV_DIM = MB_W + 2 * MB_GROUPS * MB_N
D_FF = 2816
FFN_CONV = 3
FF_CHUNK = 1408
SMALL_W = 128
DT_COL = 8
CONV_PAD = 8

VMEM_LIMIT = 56 * 1024 * 1024

TM_PROJ = 2048
TN_PROJ = 1024
TS_MIX = 512
CHUNK_UNROLL = 2
TM_MERGE = 512
TM_FFN = 512


def _silu(x):
    return x * jax.nn.sigmoid(x)


def _softplus(x):
    return jnp.maximum(x, 0.0) + jnp.log1p(jnp.exp(-jnp.abs(x)))


def _log_sigmoid(x):
    return jnp.minimum(x, 0.0) - jnp.log1p(jnp.exp(-jnp.abs(x)))


def _dot(a, b):
    return jnp.dot(a, b, preferred_element_type=F32)


def _dot_nt(a, b):
    return lax.dot_general(a, b, (((1,), (1,)), ((), ())), preferred_element_type=F32)


def _dot_tn(a, b):
    return lax.dot_general(a, b, (((0,), (0,)), ((), ())), preferred_element_type=F32)


def _split3(a):
    hi = a.astype(BF16)
    r = a - hi.astype(F32)
    mid = r.astype(BF16)
    lo = (r - mid.astype(F32)).astype(BF16)
    return hi, mid, lo


def _sel_left(sel, a):
    hi, mid, lo = _split3(a)
    return _dot(sel, hi) + (_dot(sel, mid) + _dot(sel, lo))


def _sel_right(a, sel):
    hi, mid, lo = _split3(a)
    return _dot(hi, sel) + (_dot(mid, sel) + _dot(lo, sel))


def _tri_lower(n):
    r = lax.broadcasted_iota(jnp.int32, (n, n), 0)
    c = lax.broadcasted_iota(jnp.int32, (n, n), 1)
    return r >= c


def _bcast_row(ref, h, r):
    return ref[h, pl.ds(r, SUBLANES, stride=0), :]


def _mask_bf16(m):
    return jnp.where(m, 1.0, 0.0).astype(BF16)


def _cparams(sem):
    return pltpu.CompilerParams(dimension_semantics=sem, vmem_limit_bytes=VMEM_LIMIT)


def _lbs_kernel(lg_ref, o_ref):
    lg = lg_ref[...]
    mx = jnp.max(lg, axis=0, keepdims=True)
    e = jnp.exp(lg - mx)
    p = e / jnp.sum(e, axis=0, keepdims=True)
    acc = jnp.zeros_like(p[0:1])
    rows = []
    for l in range(lg.shape[0]):
        acc = acc + p[l:l + 1]
        rows.append(acc - p[0:1])
    o_ref[...] = jnp.concatenate(rows, axis=0)


def _lbs(logits):
    return pl.pallas_call(
        _lbs_kernel,
        out_shape=jax.ShapeDtypeStruct(logits.shape, F32),
        name="hgrn2_lower_bounds",
    )(logits)


def _inproj_kernel(x_ref, g_ref, w_ref, ws_ref, o_ref, os_ref, h_ref):
    @pl.when(pl.program_id(1) == 0)
    def _():
        x = x_ref[...]
        ms = jnp.mean(x * x, axis=-1, keepdims=True)
        hb = (x * lax.rsqrt(ms + EPS) * g_ref[...]).astype(BF16)
        h_ref[...] = hb
        os_ref[...] = _dot(hb, ws_ref[...])

    o_ref[...] = _dot(h_ref[...], w_ref[...]).astype(o_ref.dtype)


def _inproj(x2, g, w_big, w_small):
    n = x2.shape[0]
    nb = w_big.shape[1]
    return pl.pallas_call(
        _inproj_kernel,
        grid=(n // TM_PROJ, nb // TN_PROJ),
        in_specs=[
            pl.BlockSpec((TM_PROJ, D_MODEL), lambda i, j: (i, 0)),
            pl.BlockSpec((1, D_MODEL), lambda i, j: (0, 0)),
            pl.BlockSpec((D_MODEL, TN_PROJ), lambda i, j: (0, j)),
            pl.BlockSpec((D_MODEL, SMALL_W), lambda i, j: (0, 0)),
        ],
        out_specs=[
            pl.BlockSpec((TM_PROJ, TN_PROJ), lambda i, j: (i, j)),
            pl.BlockSpec((TM_PROJ, SMALL_W), lambda i, j: (i, 0)),
        ],
        out_shape=[
            jax.ShapeDtypeStruct((n, nb), BF16),
            jax.ShapeDtypeStruct((n, SMALL_W), F32),
        ],
        scratch_shapes=[pltpu.VMEM((TM_PROJ, D_MODEL), BF16)],
        compiler_params=_cparams(("parallel", "arbitrary")),
        name="in_projection",
    )(x2, g, w_big, w_small)


def _conv_tile(raw_ref, ubuf, cw_ref, cb_ref, width):
    ts = raw_ref.shape[0]
    u = raw_ref[...].astype(F32)
    y = cb_ref[...] + cw_ref[width - 1:width, :] * u
    for k in range(width - 1):
        y = y + cw_ref[k:k + 1, :] * pltpu.roll(u, width - 1 - k, axis=0)
    ubuf[CONV_PAD:2 * CONV_PAD, :] = u[:CONV_PAD]
    head = cb_ref[...]
    for k in range(width):
        off = CONV_PAD - (width - 1) + k
        head = head + cw_ref[k:k + 1, :] * ubuf[off:off + CONV_PAD, :]
    ubuf[0:CONV_PAD, :] = u[ts - CONV_PAD:]
    return jnp.concatenate([head, y[CONV_PAD:]], axis=0)


def _hgrn2_kernel(q_ref, f_ref, i_ref, g_ref, lb_ref, ng_ref, y_ref,
                  state_ref, c_s):
    @pl.when(pl.program_id(1) == 0)
    def _():
        state_ref[...] = jnp.zeros_like(state_ref)

    ts = q_ref.shape[0]
    lb = lb_ref[...]
    tri = _mask_bf16(_tri_lower(CHUNK))
    n_sub = CHUNK // SUB
    ones_rhs = jnp.ones((HG_D, HG_D), BF16)
    lane = lax.broadcasted_iota(jnp.int32, (SUBLANES, HG_D), 1)
    row = lax.broadcasted_iota(jnp.int32, (SUBLANES, HG_D), 0)

    heads = range(HG_HEADS)
    hsl = [slice(h * HG_D, (h + 1) * HG_D) for h in heads]
    zero_tail = jnp.zeros((HG_D - CHUNK, HG_D), BF16)

    def chunk_body(c, carry):
        rows = pl.ds(pl.multiple_of(c * CHUNK, CHUNK), CHUNK)
        sig = jax.nn.sigmoid(f_ref[rows, :].astype(F32))
        fgate = lb + (1.0 - lb) * sig
        b = _sel_left(tri, jnp.log(fgate) * LOG2E)
        cc = b - jnp.log(jnp.maximum(1.0 - fgate, 0.0)) * LOG2E
        for h in heads:
            c_s[h] = cc[:, hsl[h]]
        q = _silu(q_ref[rows, :].astype(F32))
        v16 = i_ref[rows, :]

        st = [state_ref[h] for h in heads]
        q_dec = (q * jnp.exp2(b)).astype(BF16)
        b_last = b[CHUNK - 1:CHUNK, :]
        k_dec = jnp.exp2(b_last - cc).astype(BF16)
        st_decay = jnp.exp2(b_last)
        o_inter = [_dot_nt(q_dec[:, hsl[h]], st[h].astype(BF16)) for h in heads]
        for h in heads:
            state_ref[h] = st_decay[:, hsl[h]] * st[h] + _dot_tn(v16[:, hsl[h]], k_dec[:, hsl[h]])

        zs = []
        for i in range(n_sub):
            lo = i * SUB
            for s in range(SUB):
                r0 = lo + (s // SUBLANES) * SUBLANES
                c_row = jnp.concatenate([_bcast_row(c_s, h, lo + s) for h in heads], axis=-1)
                c_row = jnp.concatenate([c_row] * ((lo + SUB - r0) // SUBLANES), axis=0)
                zs.append(q[r0:lo + SUB] * jnp.exp2(b[r0:lo + SUB] - c_row))
        z_rows = sum(z.shape[0] for z in zs)
        z_all = jnp.concatenate([z[:, hsl[h]] for h in heads for z in zs], axis=0)
        r = _dot(z_all.astype(BF16), ones_rhs)

        a_off = []
        for i in range(1, n_sub):
            lo = i * SUB
            bref = b[lo - 1:lo, :]
            q_i = (q[lo:lo + SUB] * jnp.exp2(b[lo:lo + SUB] - bref)).astype(BF16)
            k_i = jnp.exp2(bref - cc[:lo]).astype(BF16)
            zero_rows = jnp.zeros((HG_D - lo, HG_D), BF16)
            a_off.append([_dot_nt(q_i[:, hsl[h]], jnp.concatenate([k_i[:, hsl[h]], zero_rows], axis=0))
                          for h in heads])

        outs = []
        for h in heads:
            a_rows = []
            off = h * z_rows
            for i in range(n_sub):
                lo = i * SUB
                a_i = a_off[i - 1][h] if i > 0 else jnp.zeros((SUB, HG_D), F32)
                tiles = [a_i[j * SUBLANES:(j + 1) * SUBLANES] for j in range(SUB // SUBLANES)]
                for s in range(SUB):
                    for j in range(s // SUBLANES, SUB // SUBLANES):
                        tiles[j] = jnp.where(lane == lo + s, r[off:off + SUBLANES], tiles[j])
                        off += SUBLANES
                for j in range(SUB // SUBLANES):
                    a_rows.append(jnp.where(lane - lo <= row + j * SUBLANES, tiles[j], 0.0))
            a_full = jnp.concatenate(a_rows, axis=0).astype(BF16)
            v_pad = jnp.concatenate([v16[:, hsl[h]], zero_tail], axis=0)
            outs.append(o_inter[h] + _dot(a_full, v_pad))
        parts = []
        for o in outs:
            parts.append(o * lax.rsqrt(jnp.mean(o * o, axis=-1, keepdims=True) + EPS))
        on = jnp.concatenate(parts, axis=-1) * ng_ref[...]
        y_ref[rows, :] = (on * _silu(g_ref[rows, :].astype(F32))).astype(y_ref.dtype)
        return carry

    lax.fori_loop(0, ts // CHUNK, chunk_body, 0)


def _hgrn2(proj, lb, ng, batch, seq):
    n = proj.shape[0]
    nt = seq // TS_MIX
    w = HG_HEADS * HG_D

    def col(cb):
        return pl.BlockSpec((TS_MIX, w), lambda b, j: (b * nt + j, cb))

    vec = pl.BlockSpec((1, w), lambda b, j: (0, 0))
    return pl.pallas_call(
        _hgrn2_kernel,
        grid=(batch, nt),
        in_specs=[col(0), col(1), col(2), col(3), vec, vec],
        out_specs=pl.BlockSpec((TS_MIX, w), lambda b, j: (b * nt + j, 0)),
        out_shape=jax.ShapeDtypeStruct((n, w), BF16),
        scratch_shapes=[pltpu.VMEM((HG_HEADS, HG_D, HG_D), F32), pltpu.VMEM((HG_HEADS, CHUNK, HG_D), F32)],
        compiler_params=_cparams(("parallel", "arbitrary")),
        name="hgrn2_mixer",
    )(proj, proj, proj, proj, lb, ng)


def _mlstm_kernel(qk_ref, v_ref, og_ref, sm_ref, smt_ref, cw_ref, cb_ref, gbr_ref, gbc_ref, ng_ref,
                  y_ref, caug_ref, m_ref, ubuf, qk_s):
    ts = qk_ref.shape[0]
    half = ML_HEADS * ML_DQK

    @pl.when(pl.program_id(1) == 0)
    def _():
        caug_ref[...] = jnp.zeros_like(caug_ref)
        m_ref[...] = jnp.zeros_like(m_ref)
        ubuf[0:CONV_PAD, :] = jnp.zeros((CONV_PAD, ubuf.shape[1]), F32)

    qk = _silu(_conv_tile(qk_ref, ubuf, cw_ref, cb_ref, ML_CONV))
    qk_s[:, :half] = qk[:, :half]
    qk_s[:, half:] = qk[:, half:] * (ML_DQK ** -0.5)

    tri_b = _tri_lower(CHUNK)
    tri_l = _mask_bf16(tri_b)
    tri_u = _mask_bf16(lax.broadcasted_iota(jnp.int32, (CHUNK, CHUNK), 0)
                       <= lax.broadcasted_iota(jnp.int32, (CHUNK, CHUNK), 1))
    ones_col = _mask_bf16(lax.broadcasted_iota(jnp.int32, (CHUNK, LANES), 1) == 0)

    def chunk_body(c, carry):
        rows = pl.ds(pl.multiple_of(c * CHUNK, CHUNK), CHUNK)
        pre_c = sm_ref[rows, :] + gbr_ref[...]
        pre_r = smt_ref[c] + gbc_ref[...]
        cum_c = _sel_left(tri_l, _log_sigmoid(pre_c))
        cum_r = _sel_right(_log_sigmoid(pre_r), tri_u)
        heads = range(ML_HEADS)
        m_all = m_ref[...]
        m_old = [m_all[h:h + 1, 0:1] for h in heads]
        b_col = [cum_c[:, ML_HEADS + h:ML_HEADS + h + 1] for h in heads]
        i_col = [pre_c[:, h:h + 1] for h in heads]
        log_d = [jnp.where(tri_b, b_col[h] - cum_r[ML_HEADS + h:ML_HEADS + h + 1, :] + pre_r[h:h + 1, :],
                           NEG_BIG) for h in heads]
        log_inter = [b_col[h] + m_old[h] for h in heads]
        m_t = [jnp.maximum(jnp.max(log_d[h], axis=-1, keepdims=True), log_inter[h]) for h in heads]
        b_last = [b_col[h][CHUNK - 1:CHUNK, :] for h in heads]
        log_w = [b_last[h] - b_col[h] + i_col[h] for h in heads]
        m_new = [jnp.maximum(b_last[h] + m_old[h], jnp.max(log_w[h], axis=0, keepdims=True)) for h in heads]
        m_ref[...] = jnp.concatenate(
            [jnp.broadcast_to(m_new[h], (1, m_ref.shape[1])) for h in heads] + [m_all[ML_HEADS:]], axis=0)

        qb = [qk_s[rows, h * ML_DQK:(h + 1) * ML_DQK].astype(BF16) for h in heads]
        k = [qk_s[rows, half + h * ML_DQK:half + (h + 1) * ML_DQK] for h in heads]
        v_aug = [jnp.concatenate([v_ref[rows, h * ML_DV:(h + 1) * ML_DV], ones_col], axis=-1)
                 for h in heads]
        qk = [_dot_nt(qb[h], k[h].astype(BF16)) for h in heads]
        c_aug = [caug_ref[h] for h in heads]
        inter = [_dot(qb[h], c_aug[h].astype(BF16)) * jnp.exp(log_inter[h] - m_t[h]) for h in heads]
        s = [(qk[h] * jnp.exp(log_d[h] - m_t[h])).astype(BF16) for h in heads]
        num = [_dot(s[h], v_aug[h]) + inter[h] for h in heads]
        kw = [(k[h] * jnp.exp(log_w[h] - m_new[h])).astype(BF16) for h in heads]
        for h in heads:
            caug_ref[h] = jnp.exp(b_last[h] + m_old[h] - m_new[h]) * c_aug[h] + _dot_tn(kw[h], v_aug[h])

        parts = []
        for h in heads:
            denom = jnp.maximum(jnp.abs(num[h][:, ML_DV:ML_DV + 1]), jnp.exp(-m_t[h]))
            o = num[h][:, :ML_DV] / denom
            parts.append(o * lax.rsqrt(jnp.mean(o * o, axis=-1, keepdims=True) + EPS))
        on = jnp.concatenate(parts, axis=-1) * ng_ref[...]
        y_ref[rows, :] = (on * jax.nn.sigmoid(og_ref[rows, :].astype(F32))).astype(y_ref.dtype)
        return carry

    lax.fori_loop(0, ts // CHUNK, chunk_body, 0, unroll=CHUNK_UNROLL)


def _mlstm(proj, small, small_t, cw, cb, gb_row, gb_col, ng, batch, seq):
    n = proj.shape[0]
    nt = seq // TS_MIX
    w = ML_HEADS * ML_DV
    wqk = 2 * ML_HEADS * ML_DQK

    def col(cb_):
        return pl.BlockSpec((TS_MIX, w), lambda b, j: (b * nt + j, cb_))

    def const(shape):
        return pl.BlockSpec(shape, lambda b, j: tuple(0 for _ in shape))

    return pl.pallas_call(
        _mlstm_kernel,
        grid=(batch, nt),
        in_specs=[
            col(4), col(5), col(6),
            pl.BlockSpec((TS_MIX, SMALL_W), lambda b, j: (b * nt + j, 0)),
            pl.BlockSpec((TS_MIX // CHUNK, 8, CHUNK), lambda b, j: (b * nt + j, 0, 0)),
            const((ML_CONV, wqk)), const((1, wqk)), const((1, SMALL_W)), const((8, 1)), const((1, w)),
        ],
        out_specs=pl.BlockSpec((TS_MIX, w), lambda b, j: (b * nt + j, 0)),
        out_shape=jax.ShapeDtypeStruct((n, w), BF16),
        scratch_shapes=[
            pltpu.VMEM((ML_HEADS, ML_DQK, ML_DV + LANES), F32),
            pltpu.VMEM((SUBLANES, LANES), F32),
            pltpu.VMEM((TS_MIX + 2 * CONV_PAD, wqk), F32),
            pltpu.VMEM((TS_MIX, wqk), F32),
        ],
        compiler_params=_cparams(("parallel", "arbitrary")),
        name="mlstm_mixer",
    )(proj, proj, proj, small, small_t, cw, cb, gb_row, gb_col, ng)


def _ssd_kernel(z_ref, xbc_ref, sm_ref, dtt_ref, cw_ref, cb_ref, dtb_c_ref, alog_c_ref,
                dtb_r_ref, alog_r_ref, d_ref, ng_ref, y_ref, state_ref, ubuf, xbc_s):
    ts = z_ref.shape[0]
    n_pair = MB_HEADS // 2
    pair_w = 2 * MB_P
    gw = MB_GROUPS * MB_N

    @pl.when(pl.program_id(1) == 0)
    def _():
        state_ref[...] = jnp.zeros_like(state_ref)
        ubuf[0:CONV_PAD, :] = jnp.zeros((CONV_PAD, ubuf.shape[1]), F32)

    xbc_s[...] = _silu(_conv_tile(xbc_ref, ubuf, cw_ref, cb_ref, MB_CONV))

    tri_l = _mask_bf16(_tri_lower(CHUNK))
    sel_x = _mask_bf16(lax.broadcasted_iota(jnp.int32, (SMALL_W, MB_W), 0) - DT_COL
                       == jnp.right_shift(lax.broadcasted_iota(jnp.int32, (SMALL_W, MB_W), 1), MB_P_LOG2))
    ur = lax.broadcasted_iota(jnp.int32, (pair_w, pair_w), 0)
    uc = lax.broadcasted_iota(jnp.int32, (pair_w, pair_w), 1)
    same_half = jnp.right_shift(ur, CHUNK_LOG2) == jnp.right_shift(uc, CHUNK_LOG2)
    tri_u2 = _mask_bf16(same_half & (ur <= uc))
    causal = (lax.broadcasted_iota(jnp.int32, (CHUNK, MB_W), 0)
              >= jnp.bitwise_and(lax.broadcasted_iota(jnp.int32, (CHUNK, MB_W), 1), CHUNK - 1))
    first_head = lax.broadcasted_iota(jnp.int32, (CHUNK, pair_w), 1) < MB_P
    a_c = -jnp.exp(alog_c_ref[...])
    a_r = -jnp.exp(alog_r_ref[...])

    def chunk_body(c, carry):
        rows = pl.ds(pl.multiple_of(c * CHUNK, CHUNK), CHUNK)
        dt_c = _softplus(sm_ref[rows, :] + dtb_c_ref[...])
        cum_c = _sel_left(tri_l, dt_c * a_c)
        dt_x = _sel_right(dt_c, sel_x)
        cum_x = _sel_right(cum_c, sel_x)
        dt_r = _softplus(dtt_ref[c] + dtb_r_ref[...])
        cum_r = _sel_right(dt_r * a_r, tri_u2)
        pairs = range(n_pair)
        grp = [(2 * p) // (MB_HEADS // MB_GROUPS) for p in pairs]
        lanes = [slice(p * pair_w, (p + 1) * pair_w) for p in pairs]
        xs = xbc_s[rows, :MB_W]
        xdt = xs * dt_x
        cum_row = jnp.concatenate([cum_r[p:p + 1, :] for p in pairs], axis=-1)
        decay = jnp.exp(jnp.where(causal, cum_x - cum_row, NEG_BIG))
        cum_last = cum_x[CHUNK - 1:CHUNK, :]
        x_dec = (xdt * jnp.exp(cum_last - cum_x)).astype(BF16)
        st_decay = jnp.exp(cum_last)
        carry_w = jnp.exp(cum_x)
        bm = [xbc_s[rows, MB_W + g * MB_N:MB_W + (g + 1) * MB_N].astype(BF16) for g in range(MB_GROUPS)]
        cm = [xbc_s[rows, MB_W + gw + g * MB_N:MB_W + gw + (g + 1) * MB_N].astype(BF16)
              for g in range(MB_GROUPS)]
        cb2 = [_dot_nt(cm[g], jnp.concatenate([bm[g], bm[g]], axis=0)) for g in range(MB_GROUPS)]
        x2 = [jnp.concatenate([jnp.where(first_head, xdt[:, lanes[p]], 0.0),
                               jnp.where(first_head, 0.0, xdt[:, lanes[p]])], axis=0).astype(BF16)
              for p in pairs]
        st = [state_ref[p] for p in pairs]
        y_inter = [_dot(cm[grp[p]], st[p].astype(BF16)) for p in pairs]
        y_intra = [_dot((cb2[grp[p]] * decay[:, lanes[p]]).astype(BF16), x2[p]) for p in pairs]
        for p in pairs:
            state_ref[p] = st_decay[:, lanes[p]] * st[p] + _dot_tn(bm[grp[p]], x_dec[:, lanes[p]])
        y = (jnp.concatenate(y_intra, axis=-1) + jnp.concatenate(y_inter, axis=-1) * carry_w
             + d_ref[...] * xs)

        yz = y * _silu(z_ref[rows, :].astype(F32))
        gsz = MB_W // MB_GROUPS
        parts = []
        for g in range(MB_GROUPS):
            o = yz[:, g * gsz:(g + 1) * gsz]
            parts.append(o * lax.rsqrt(jnp.mean(o * o, axis=-1, keepdims=True) + EPS))
        y_ref[rows, :] = (jnp.concatenate(parts, axis=-1) * ng_ref[...]).astype(y_ref.dtype)
        return carry

    lax.fori_loop(0, ts // CHUNK, chunk_body, 0, unroll=CHUNK_UNROLL)


def _ssd(proj, small, dt_t, cw, cb, dtb_c, alog_c, dtb_r, alog_r, d_x, ng, batch, seq):
    n = proj.shape[0]
    nt = seq // TS_MIX
    n_pair = MB_HEADS // 2

    def const(shape):
        return pl.BlockSpec(shape, lambda b, j: tuple(0 for _ in shape))

    return pl.pallas_call(
        _ssd_kernel,
        grid=(batch, nt),
        in_specs=[
            pl.BlockSpec((TS_MIX, MB_W), lambda b, j: (b * nt + j, 7)),
            pl.BlockSpec((TS_MIX, MB_CONV_DIM), lambda b, j: (b * nt + j, 4)),
            pl.BlockSpec((TS_MIX, SMALL_W), lambda b, j: (b * nt + j, 0)),
            pl.BlockSpec((TS_MIX // CHUNK, n_pair, 2 * CHUNK), lambda b, j: (b * nt + j, 0, 0)),
            const((MB_CONV, MB_CONV_DIM)), const((1, MB_CONV_DIM)),
            const((1, SMALL_W)), const((1, SMALL_W)),
            const((n_pair, 2 * CHUNK)), const((n_pair, 2 * CHUNK)),
            const((1, MB_W)), const((1, MB_W)),
        ],
        out_specs=pl.BlockSpec((TS_MIX, MB_W), lambda b, j: (b * nt + j, 0)),
        out_shape=jax.ShapeDtypeStruct((n, MB_W), BF16),
        scratch_shapes=[
            pltpu.VMEM((n_pair, MB_N, 2 * MB_P), F32),
            pltpu.VMEM((TS_MIX + 2 * CONV_PAD, MB_CONV_DIM), F32),
            pltpu.VMEM((TS_MIX, MB_CONV_DIM), F32),
        ],
        compiler_params=_cparams(("parallel", "arbitrary")),
        name="ssd_mixer",
    )(proj, proj, small, dt_t, cw, cb, dtb_c, alog_c, dtb_r, alog_r, d_x, ng)


def _merge_kernel(x_ref, yh_ref, ym_ref, yb_ref, g0_ref, g1_ref, g2_ref,
                  wh_ref, wm_ref, wb_ref, wo_ref, o_ref):
    mixed = jax.nn.sigmoid(g0_ref[...].astype(F32)) * _dot(yh_ref[...], wh_ref[...])
    mixed = mixed + jax.nn.sigmoid(g1_ref[...].astype(F32)) * _dot(ym_ref[...], wm_ref[...])
    mixed = mixed + jax.nn.sigmoid(g2_ref[...].astype(F32)) * _dot(yb_ref[...], wb_ref[...])
    o_ref[...] = x_ref[...] + _dot(mixed.astype(BF16), wo_ref[...])


def _merge(x2, y_hg, y_ml, y_mb, proj, w_hg, w_ml, w_mb, w_out):
    n = x2.shape[0]
    tile = lambda cb: pl.BlockSpec((TM_MERGE, D_MODEL), lambda i: (i, cb))
    wspec = pl.BlockSpec((D_MODEL, D_MODEL), lambda i: (0, 0))
    return pl.pallas_call(
        _merge_kernel,
        grid=(n // TM_MERGE,),
        in_specs=[tile(0), tile(0), tile(0), tile(0), tile(10), tile(11), tile(12),
                  wspec, wspec, wspec, wspec],
        out_specs=tile(0),
        out_shape=jax.ShapeDtypeStruct((n, D_MODEL), F32),
        compiler_params=_cparams(("parallel",)),
        name="branch_merge",
    )(x2, y_hg, y_ml, y_mb, proj, proj, proj, w_hg, w_ml, w_mb, w_out)


def _ffn_kernel(x_ref, g_ref, wug_ref, wuv_ref, cwg_ref, cwv_ref, cbg_ref, cbv_ref, wd_ref, fg_ref,
                o_ref, tail_g, tail_v, buf, *, final):
    tm = x_ref.shape[0]

    @pl.when(pl.program_id(1) == 0)
    def _():
        tail_g[...] = jnp.zeros_like(tail_g)
        tail_v[...] = jnp.zeros_like(tail_v)

    x = x_ref[...]
    hb = (x * lax.rsqrt(jnp.mean(x * x, axis=-1, keepdims=True) + EPS) * g_ref[...]).astype(BF16)

    def conv(w_ref, cw_ref, cb_ref, tail, cs):
        buf[0:CONV_PAD, :] = tail[:, cs]
        buf[CONV_PAD:CONV_PAD + tm, :] = _dot(hb, w_ref[:, cs])
        tail[:, cs] = buf[tm:tm + CONV_PAD, :]
        y = cb_ref[:, cs]
        for k in range(FFN_CONV):
            off = CONV_PAD - (FFN_CONV - 1) + k
            y = y + cw_ref[k:k + 1, cs] * buf[off:off + tm, :]
        return y

    acc = x
    for c in range(D_FF // FF_CHUNK):
        cs = slice(c * FF_CHUNK, (c + 1) * FF_CHUNK)
        u_g = conv(wug_ref, cwg_ref, cbg_ref, tail_g, cs)
        u_v = conv(wuv_ref, cwv_ref, cbv_ref, tail_v, cs)
        acc = acc + _dot((_silu(u_g) * u_v).astype(BF16), wd_ref[cs, :])
    if final:
        acc = acc * lax.rsqrt(jnp.mean(acc * acc, axis=-1, keepdims=True) + EPS) * fg_ref[...]
    o_ref[...] = acc


def _ffn(x2, g, w_ug, w_uv, cw_g, cw_v, cb_g, cb_v, w_down, final_g, batch, seq, final):
    n = x2.shape[0]
    nt = seq // TM_FFN

    def const(shape):
        return pl.BlockSpec(shape, lambda b, j: tuple(0 for _ in shape),
                            pipeline_mode=pl.Buffered(1))

    tile = pl.BlockSpec((TM_FFN, D_MODEL), lambda b, j: (b * nt + j, 0))
    return pl.pallas_call(
        functools.partial(_ffn_kernel, final=final),
        grid=(batch, nt),
        in_specs=[
            tile, const((1, D_MODEL)),
            const((D_MODEL, D_FF)), const((D_MODEL, D_FF)),
            const((FFN_CONV, D_FF)), const((FFN_CONV, D_FF)),
            const((1, D_FF)), const((1, D_FF)),
            const((D_FF, D_MODEL)), const((1, D_MODEL)),
        ],
        out_specs=tile,
        out_shape=jax.ShapeDtypeStruct((n, D_MODEL), F32),
        scratch_shapes=[
            pltpu.VMEM((CONV_PAD, D_FF), F32),
            pltpu.VMEM((CONV_PAD, D_FF), F32),
            pltpu.VMEM((TM_FFN + 2 * CONV_PAD, FF_CHUNK), F32),
        ],
        compiler_params=_cparams(("parallel", "arbitrary")),
        name="conv_gated_mlp",
    )(x2, g, w_ug, w_uv, cw_g, cw_v, cb_g, cb_v, w_down, final_g)


def kernel(x, norm1_g, w_in, hg_lb_logits, hg_norm_g, ml_conv_w, ml_conv_b, ml_gate_b, ml_norm_g,
           mb_conv_w, mb_conv_b, mb_dt_bias, mb_a_log, mb_d, mb_norm_g, w_br_hg, w_br_ml, w_br_mb,
           w_out, norm2_g, w_up, ffn_conv_w, ffn_conv_b, w_down, final_g):
    batch, seq, _ = x.shape
    n = batch * seq
    assert seq % TS_MIX == 0 and seq % TM_FFN == 0 and n % TM_PROJ == 0 and n % TM_MERGE == 0
    depth = w_in.shape[0]

    o_if = 4 * 1024 + 2 * ML_HEADS * ML_DQK + ML_HEADS * ML_DV
    o_mlo = o_if + 2 * ML_HEADS
    o_dt = o_mlo + ML_HEADS * ML_DV + MB_W + MB_CONV_DIM
    o_gate = o_dt + MB_HEADS
    w_big = jnp.concatenate(
        [w_in[:, :, :o_if], w_in[:, :, o_mlo:o_dt], w_in[:, :, o_gate:]], axis=-1).astype(BF16)
    pad = SMALL_W - 2 * ML_HEADS - MB_HEADS
    w_small = jnp.concatenate(
        [w_in[:, :, o_if:o_mlo], w_in[:, :, o_dt:o_gate],
         jnp.zeros((depth, D_MODEL, pad), w_in.dtype)], axis=-1).astype(BF16)

    lbs = _lbs(hg_lb_logits.astype(F32))
    hg_ng = jnp.tile(hg_norm_g, (1, HG_HEADS))
    ml_ng = jnp.tile(ml_norm_g, (1, ML_HEADS))
    gb_row = jnp.pad(ml_gate_b, ((0, 0), (0, SMALL_W - 2 * ML_HEADS)))
    dtb_c = jnp.pad(mb_dt_bias, ((0, 0), (DT_COL, SMALL_W - DT_COL - MB_HEADS)))
    alog_c = jnp.pad(mb_a_log, ((0, 0), (DT_COL, SMALL_W - DT_COL - MB_HEADS)))
    n_pair = MB_HEADS // 2
    dtb_r = jnp.repeat(mb_dt_bias, CHUNK, axis=-1).reshape(depth, n_pair, 2 * CHUNK)
    alog_r = jnp.repeat(mb_a_log, CHUNK, axis=-1).reshape(depth, n_pair, 2 * CHUNK)
    d_x = jnp.repeat(mb_d, MB_P, axis=-1)

    w_hg = w_br_hg.astype(BF16)
    w_ml = w_br_ml.astype(BF16)
    w_mb = w_br_mb.astype(BF16)
    w_o = w_out.astype(BF16)
    w_ug = w_up[:, :, :D_FF].astype(BF16)
    w_uv = w_up[:, :, D_FF:].astype(BF16)
    w_d = w_down.astype(BF16)

    x2 = x.reshape(n, D_MODEL)
    row = lambda a: a.reshape(1, -1)
    for l in range(depth):
        proj, small = _inproj(x2, row(norm1_g[l]), w_big[l], w_small[l])
        chunks = small.reshape(n // CHUNK, CHUNK, SMALL_W)
        small_t = jnp.swapaxes(chunks[:, :, :2 * ML_HEADS], 1, 2)
        dt_t = jnp.swapaxes(chunks[:, :, DT_COL:DT_COL + MB_HEADS], 1, 2).reshape(
            n // CHUNK, n_pair, 2 * CHUNK)

        y_hg = _hgrn2(proj, row(lbs[l]), row(hg_ng[l]), batch, seq)
        y_ml = _mlstm(proj, small, small_t, ml_conv_w[l], row(ml_conv_b[l]), row(gb_row[l]),
                      ml_gate_b[l].reshape(2 * ML_HEADS, 1), row(ml_ng[l]), batch, seq)
        y_mb = _ssd(proj, small, dt_t, mb_conv_w[l], row(mb_conv_b[l]), row(dtb_c[l]), row(alog_c[l]),
                    dtb_r[l], alog_r[l], row(d_x[l]), row(mb_norm_g[l]), batch, seq)
        x2 = _merge(x2, y_hg, y_ml, y_mb, proj, w_hg[l], w_ml[l], w_mb[l], w_o[l])
        x2 = _ffn(x2, row(norm2_g[l]), w_ug[l], w_uv[l], ffn_conv_w[l, :, :D_FF], ffn_conv_w[l, :, D_FF:],
                  row(ffn_conv_b[l, :D_FF]), row(ffn_conv_b[l, D_FF:]), w_d[l], row(final_g),
                  batch, seq, final=(l == depth - 1))
    return x2.reshape(batch, seq, D_MODEL)
```

```python
import functools
import math

import jax
import jax.numpy as jnp
from jax import lax
from jax.experimental import pallas as pl
from jax.experimental.pallas import tpu as pltpu

F32 = jnp.float32
BF16 = jnp.bfloat16

D_MODEL = 1024
DEPTH = 4
CHUNK = 64
CHUNK_LOG2 = 6
SUB = 8
SUBLANES = 8
LANES = 128
LOG2E = math.log2(math.e)
EPS = 1e-6
NEG_BIG = -1e30
HG_HEADS = 8
HG_D = 128
ML_HEADS = 4
ML_DQK = 128
ML_DV = 256
ML_CONV = 4
MB_HEADS = 16
MB_P = 64
MB_P_LOG2 = 6
MB_GROUPS = 4
MB_N = 128
MB_CONV = 4
MB_W = MB_HEADS * MB_P
MB_CONV_DIM = MB_W + 2 * MB_GROUPS * MB_N
D_FF = 2816
FFN_CONV = 3
FF_CHUNK = 1408
SMALL_W = 128
DT_COL = 8
CONV_PAD = 8

VMEM_LIMIT = 56 * 1024 * 1024

TM_PROJ = 2048
TN_PROJ = 1024
EPI_ROWS = 256
TILES_SILU = (0, 3, 7)
TILES_SIGMOID = (6, 10, 11, 12)
TILES_RAW = (1, 2, 5)
TILES_CONV = (4, 8, 9)
CONV_W = 4
TS_MIX = 512
CHUNK_UNROLL = 2
TM_MERGE = 512
TM_FFN = 512


def _silu(x):
    return x * jax.nn.sigmoid(x)


def _softplus(x):
    return jnp.maximum(x, 0.0) + jnp.log1p(jnp.exp(-jnp.abs(x)))


def _log_sigmoid(x):
    return jnp.minimum(x, 0.0) - jnp.log1p(jnp.exp(-jnp.abs(x)))


def _dot(a, b):
    return jnp.dot(a, b, preferred_element_type=F32)


def _dot_nt(a, b):
    return lax.dot_general(a, b, (((1,), (1,)), ((), ())), preferred_element_type=F32)


def _dot_tn(a, b):
    return lax.dot_general(a, b, (((0,), (0,)), ((), ())), preferred_element_type=F32)


def _split3(a):
    hi = a.astype(BF16)
    r = a - hi.astype(F32)
    mid = r.astype(BF16)
    lo = (r - mid.astype(F32)).astype(BF16)
    return hi, mid, lo


def _sel_left(sel, a):
    hi, mid, lo = _split3(a)
    return _dot(sel, hi) + (_dot(sel, mid) + _dot(sel, lo))


def _sel_right(a, sel):
    hi, mid, lo = _split3(a)
    return _dot(hi, sel) + (_dot(mid, sel) + _dot(lo, sel))


def _tri_lower(n):
    r = lax.broadcasted_iota(jnp.int32, (n, n), 0)
    c = lax.broadcasted_iota(jnp.int32, (n, n), 1)
    return r >= c


def _bcast_row(ref, h, r):
    return ref[h, pl.ds(r, SUBLANES, stride=0), :]


def _mask_bf16(m):
    return jnp.where(m, 1.0, 0.0).astype(BF16)


def _cparams(sem):
    return pltpu.CompilerParams(dimension_semantics=sem, vmem_limit_bytes=VMEM_LIMIT)


def _lbs_kernel(lg_ref, o_ref):
    lg = lg_ref[...]
    mx = jnp.max(lg, axis=0, keepdims=True)
    e = jnp.exp(lg - mx)
    p = e / jnp.sum(e, axis=0, keepdims=True)
    acc = jnp.zeros_like(p[0:1])
    rows = []
    for l in range(lg.shape[0]):
        acc = acc + p[l:l + 1]
        rows.append(acc - p[0:1])
    o_ref[...] = jnp.concatenate(rows, axis=0)


def _lbs(logits):
    return pl.pallas_call(
        _lbs_kernel,
        out_shape=jax.ShapeDtypeStruct(logits.shape, F32),
        name="hgrn2_lower_bounds",
    )(logits)


def _any_tile(j, tiles):
    hit = j == tiles[0]
    for t in tiles[1:]:
        hit = jnp.logical_or(hit, j == t)
    return hit


def _inproj_kernel(x_ref, g_ref, w_ref, ws_ref, cw_ref, cb_ref, o_ref, os_ref, h_ref, tail_ref, hbuf,
                   *, tiles_per_seq):
    i = pl.program_id(0)
    j = pl.program_id(1)
    n_blk = TM_PROJ // EPI_ROWS

    @pl.when(j == 0)
    def _():
        x = x_ref[...]
        ms = jnp.mean(x * x, axis=-1, keepdims=True)
        hb = (x * lax.rsqrt(ms + EPS) * g_ref[...]).astype(BF16)
        h_ref[...] = hb
        os_ref[...] = _dot(hb, ws_ref[...])

    def pointwise(fn):
        ys = [_dot(h_ref[r * EPI_ROWS:(r + 1) * EPI_ROWS, :], w_ref[...]) for r in range(n_blk)]
        for r in range(n_blk):
            o_ref[r * EPI_ROWS:(r + 1) * EPI_ROWS, :] = fn(ys[r]).astype(o_ref.dtype)

    @pl.when(_any_tile(j, TILES_SILU))
    def _():
        pointwise(_silu)

    @pl.when(_any_tile(j, TILES_SIGMOID))
    def _():
        pointwise(jax.nn.sigmoid)

    @pl.when(_any_tile(j, TILES_RAW))
    def _():
        pointwise(lambda y: y)

    @pl.when(_any_tile(j, TILES_CONV))
    def _():
        slot = jnp.where(j == TILES_CONV[0], 0, j - TILES_CONV[1] + 1)
        prev = jnp.where(i % tiles_per_seq == 0, 0.0, tail_ref[slot])
        cw = cw_ref[0]
        cb = cb_ref[0]
        ys = [_dot(h_ref[r * EPI_ROWS:(r + 1) * EPI_ROWS, :], w_ref[...]) for r in range(n_blk)]
        for r in range(n_blk):
            yb = ys[r]
            out = cb + cw[CONV_W - 1:CONV_W] * yb
            for k in range(CONV_W - 1):
                out = out + cw[k:k + 1] * pltpu.roll(yb, CONV_W - 1 - k, axis=0)
            hbuf[0:CONV_PAD, :] = prev
            hbuf[CONV_PAD:2 * CONV_PAD, :] = yb[:CONV_PAD]
            head = cb
            for k in range(CONV_W):
                off = CONV_PAD - (CONV_W - 1) + k
                head = head + cw[k:k + 1] * hbuf[off:off + CONV_PAD, :]
            prev = yb[EPI_ROWS - CONV_PAD:]
            o_ref[r * EPI_ROWS:(r + 1) * EPI_ROWS, :] = _silu(
                jnp.concatenate([head, out[CONV_PAD:]], axis=0)).astype(o_ref.dtype)
        tail_ref[slot] = prev


def _inproj(x2, g, w_big, w_small, cw_all, cb_all, seq):
    n = x2.shape[0]
    nb = w_big.shape[1]
    n_tiles = nb // TN_PROJ
    assert n_tiles == len(TILES_SILU + TILES_SIGMOID + TILES_RAW + TILES_CONV)
    return pl.pallas_call(
        functools.partial(_inproj_kernel, tiles_per_seq=seq // TM_PROJ),
        grid=(n // TM_PROJ, n_tiles),
        in_specs=[
            pl.BlockSpec((TM_PROJ, D_MODEL), lambda i, j: (i, 0)),
            pl.BlockSpec((1, D_MODEL), lambda i, j: (0, 0)),
            pl.BlockSpec((D_MODEL, TN_PROJ), lambda i, j: (0, j)),
            pl.BlockSpec((D_MODEL, SMALL_W), lambda i, j: (0, 0)),
            pl.BlockSpec((1, CONV_W, TN_PROJ), lambda i, j: (j, 0, 0)),
            pl.BlockSpec((1, 1, TN_PROJ), lambda i, j: (j, 0, 0)),
        ],
        out_specs=[
            pl.BlockSpec((TM_PROJ, TN_PROJ), lambda i, j: (i, j)),
            pl.BlockSpec((TM_PROJ, SMALL_W), lambda i, j: (i, 0)),
        ],
        out_shape=[
            jax.ShapeDtypeStruct((n, nb), BF16),
            jax.ShapeDtypeStruct((n, SMALL_W), F32),
        ],
        scratch_shapes=[
            pltpu.VMEM((TM_PROJ, D_MODEL), BF16),
            pltpu.VMEM((len(TILES_CONV), CONV_PAD, TN_PROJ), F32),
            pltpu.VMEM((2 * CONV_PAD, TN_PROJ), F32),
        ],
        compiler_params=_cparams(("arbitrary", "arbitrary")),
        name="in_projection",
    )(x2, g, w_big, w_small, cw_all, cb_all)


def _hgrn2_kernel(q_ref, f_ref, i_ref, g_ref, lb_ref, ng_ref, y_ref,
                  state_ref, c_s):
    @pl.when(pl.program_id(1) == 0)
    def _():
        state_ref[...] = jnp.zeros_like(state_ref)

    ts = q_ref.shape[0]
    lb = lb_ref[...]
    tri = _mask_bf16(_tri_lower(CHUNK))
    n_sub = CHUNK // SUB
    ones_rhs = jnp.ones((HG_D, HG_D), BF16)
    lane = lax.broadcasted_iota(jnp.int32, (SUBLANES, HG_D), 1)
    row = lax.broadcasted_iota(jnp.int32, (SUBLANES, HG_D), 0)

    heads = range(HG_HEADS)
    hsl = [slice(h * HG_D, (h + 1) * HG_D) for h in heads]
    zero_tail = jnp.zeros((HG_D - CHUNK, HG_D), BF16)

    def chunk_body(c, carry):
        rows = pl.ds(pl.multiple_of(c * CHUNK, CHUNK), CHUNK)
        sig = jax.nn.sigmoid(f_ref[rows, :].astype(F32))
        fgate = lb + (1.0 - lb) * sig
        b = _sel_left(tri, jnp.log(fgate) * LOG2E)
        cc = b - jnp.log(jnp.maximum(1.0 - fgate, 0.0)) * LOG2E
        for h in heads:
            c_s[h] = cc[:, hsl[h]]
        q = q_ref[rows, :].astype(F32)
        v16 = i_ref[rows, :]

        st = [state_ref[h] for h in heads]
        q_dec = (q * jnp.exp2(b)).astype(BF16)
        b_last = b[CHUNK - 1:CHUNK, :]
        k_dec = jnp.exp2(b_last - cc).astype(BF16)
        st_decay = jnp.exp2(b_last)
        o_inter = [_dot_nt(q_dec[:, hsl[h]], st[h].astype(BF16)) for h in heads]
        for h in heads:
            state_ref[h] = st_decay[:, hsl[h]] * st[h] + _dot_tn(v16[:, hsl[h]], k_dec[:, hsl[h]])

        zs = []
        for i in range(n_sub):
            lo = i * SUB
            for s in range(SUB):
                r0 = lo + (s // SUBLANES) * SUBLANES
                c_row = jnp.concatenate([_bcast_row(c_s, h, lo + s) for h in heads], axis=-1)
                c_row = jnp.concatenate([c_row] * ((lo + SUB - r0) // SUBLANES), axis=0)
                zs.append(q[r0:lo + SUB] * jnp.exp2(b[r0:lo + SUB] - c_row))
        z_rows = sum(z.shape[0] for z in zs)
        z_all = jnp.concatenate([z[:, hsl[h]] for h in heads for z in zs], axis=0)
        r = _dot(z_all.astype(BF16), ones_rhs)

        a_off = []
        for i in range(1, n_sub):
            lo = i * SUB
            bref = b[lo - 1:lo, :]
            q_i = (q[lo:lo + SUB] * jnp.exp2(b[lo:lo + SUB] - bref)).astype(BF16)
            k_i = jnp.exp2(bref - cc[:lo]).astype(BF16)
            zero_rows = jnp.zeros((HG_D - lo, HG_D), BF16)
            a_off.append([_dot_nt(q_i[:, hsl[h]], jnp.concatenate([k_i[:, hsl[h]], zero_rows], axis=0))
                          for h in heads])

        outs = []
        for h in heads:
            a_rows = []
            off = h * z_rows
            for i in range(n_sub):
                lo = i * SUB
                a_i = a_off[i - 1][h] if i > 0 else jnp.zeros((SUB, HG_D), F32)
                tiles = [a_i[j * SUBLANES:(j + 1) * SUBLANES] for j in range(SUB // SUBLANES)]
                for s in range(SUB):
                    for j in range(s // SUBLANES, SUB // SUBLANES):
                        tiles[j] = jnp.where(lane == lo + s, r[off:off + SUBLANES], tiles[j])
                        off += SUBLANES
                for j in range(SUB // SUBLANES):
                    a_rows.append(jnp.where(lane - lo <= row + j * SUBLANES, tiles[j], 0.0))
            a_full = jnp.concatenate(a_rows, axis=0).astype(BF16)
            v_pad = jnp.concatenate([v16[:, hsl[h]], zero_tail], axis=0)
            outs.append(o_inter[h] + _dot(a_full, v_pad))
        parts = []
        for o in outs:
            parts.append(o * lax.rsqrt(jnp.mean(o * o, axis=-1, keepdims=True) + EPS))
        on = jnp.concatenate(parts, axis=-1) * ng_ref[...]
        y_ref[rows, :] = (on * g_ref[rows, :].astype(F32)).astype(y_ref.dtype)
        return carry

    lax.fori_loop(0, ts // CHUNK, chunk_body, 0)


def _hgrn2(proj, lb, ng, batch, seq):
    n = proj.shape[0]
    nt = seq // TS_MIX
    w = HG_HEADS * HG_D

    def col(cb):
        return pl.BlockSpec((TS_MIX, w), lambda b, j: (b * nt + j, cb))

    vec = pl.BlockSpec((1, w), lambda b, j: (0, 0))
    return pl.pallas_call(
        _hgrn2_kernel,
        grid=(batch, nt),
        in_specs=[col(0), col(1), col(2), col(3), vec, vec],
        out_specs=pl.BlockSpec((TS_MIX, w), lambda b, j: (b * nt + j, 0)),
        out_shape=jax.ShapeDtypeStruct((n, w), BF16),
        scratch_shapes=[pltpu.VMEM((HG_HEADS, HG_D, HG_D), F32), pltpu.VMEM((HG_HEADS, CHUNK, HG_D), F32)],
        compiler_params=_cparams(("parallel", "arbitrary")),
        name="hgrn2_mixer",
    )(proj, proj, proj, proj, lb, ng)


def _mlstm_kernel(qk_ref, v_ref, og_ref, sm_ref, smt_ref, gbr_ref, gbc_ref, ng_ref,
                  y_ref, caug_ref, m_ref):
    ts = qk_ref.shape[0]
    half = ML_HEADS * ML_DQK
    k_scale = ML_DQK ** -0.5

    @pl.when(pl.program_id(1) == 0)
    def _():
        caug_ref[...] = jnp.zeros_like(caug_ref)
        m_ref[...] = jnp.zeros_like(m_ref)

    tri_b = _tri_lower(CHUNK)
    tri_l = _mask_bf16(tri_b)
    tri_u = _mask_bf16(lax.broadcasted_iota(jnp.int32, (CHUNK, CHUNK), 0)
                       <= lax.broadcasted_iota(jnp.int32, (CHUNK, CHUNK), 1))
    ones_col = _mask_bf16(lax.broadcasted_iota(jnp.int32, (CHUNK, LANES), 1) == 0)

    def chunk_body(c, carry):
        rows = pl.ds(pl.multiple_of(c * CHUNK, CHUNK), CHUNK)
        pre_c = sm_ref[rows, :] + gbr_ref[...]
        pre_r = smt_ref[c] + gbc_ref[...]
        cum_c = _sel_left(tri_l, _log_sigmoid(pre_c))
        cum_r = _sel_right(_log_sigmoid(pre_r), tri_u)
        heads = range(ML_HEADS)
        m_all = m_ref[...]
        m_old = [m_all[h:h + 1, 0:1] for h in heads]
        b_col = [cum_c[:, ML_HEADS + h:ML_HEADS + h + 1] for h in heads]
        i_col = [pre_c[:, h:h + 1] for h in heads]
        log_d = [jnp.where(tri_b, b_col[h] - cum_r[ML_HEADS + h:ML_HEADS + h + 1, :] + pre_r[h:h + 1, :],
                           NEG_BIG) for h in heads]
        log_inter = [b_col[h] + m_old[h] for h in heads]
        m_t = [jnp.maximum(jnp.max(log_d[h], axis=-1, keepdims=True), log_inter[h]) for h in heads]
        b_last = [b_col[h][CHUNK - 1:CHUNK, :] for h in heads]
        log_w = [b_last[h] - b_col[h] + i_col[h] for h in heads]
        m_new = [jnp.maximum(b_last[h] + m_old[h], jnp.max(log_w[h], axis=0, keepdims=True)) for h in heads]
        m_ref[...] = jnp.concatenate(
            [jnp.broadcast_to(m_new[h], (1, m_ref.shape[1])) for h in heads] + [m_all[ML_HEADS:]], axis=0)

        qb = [qk_ref[rows, h * ML_DQK:(h + 1) * ML_DQK] for h in heads]
        kb = [qk_ref[rows, half + h * ML_DQK:half + (h + 1) * ML_DQK] for h in heads]
        v_aug = [jnp.concatenate([v_ref[rows, h * ML_DV:(h + 1) * ML_DV], ones_col], axis=-1)
                 for h in heads]
        qk = [_dot_nt(qb[h], kb[h]) for h in heads]
        c_aug = [caug_ref[h] for h in heads]
        inter = [_dot(qb[h], c_aug[h].astype(BF16)) * jnp.exp(log_inter[h] - m_t[h]) for h in heads]
        s = [(qk[h] * (jnp.exp(log_d[h] - m_t[h]) * k_scale)).astype(BF16) for h in heads]
        num = [_dot(s[h], v_aug[h]) + inter[h] for h in heads]
        kw = [(kb[h].astype(F32) * (jnp.exp(log_w[h] - m_new[h]) * k_scale)).astype(BF16) for h in heads]
        for h in heads:
            caug_ref[h] = jnp.exp(b_last[h] + m_old[h] - m_new[h]) * c_aug[h] + _dot_tn(kw[h], v_aug[h])

        parts = []
        for h in heads:
            denom = jnp.maximum(jnp.abs(num[h][:, ML_DV:ML_DV + 1]), jnp.exp(-m_t[h]))
            o = num[h][:, :ML_DV] / denom
            parts.append(o * lax.rsqrt(jnp.mean(o * o, axis=-1, keepdims=True) + EPS))
        on = jnp.concatenate(parts, axis=-1) * ng_ref[...]
        y_ref[rows, :] = (on * og_ref[rows, :].astype(F32)).astype(y_ref.dtype)
        return carry

    lax.fori_loop(0, ts // CHUNK, chunk_body, 0, unroll=CHUNK_UNROLL)


def _mlstm(proj, small, small_t, gb_row, gb_col, ng, batch, seq):
    n = proj.shape[0]
    nt = seq // TS_MIX
    w = ML_HEADS * ML_DV

    def col(cb_):
        return pl.BlockSpec((TS_MIX, w), lambda b, j: (b * nt + j, cb_))

    def const(shape):
        return pl.BlockSpec(shape, lambda b, j: tuple(0 for _ in shape))

    return pl.pallas_call(
        _mlstm_kernel,
        grid=(batch, nt),
        in_specs=[
            col(4), col(5), col(6),
            pl.BlockSpec((TS_MIX, SMALL_W), lambda b, j: (b * nt + j, 0)),
            pl.BlockSpec((TS_MIX // CHUNK, 8, CHUNK), lambda b, j: (b * nt + j, 0, 0)),
            const((1, SMALL_W)), const((8, 1)), const((1, w)),
        ],
        out_specs=pl.BlockSpec((TS_MIX, w), lambda b, j: (b * nt + j, 0)),
        out_shape=jax.ShapeDtypeStruct((n, w), BF16),
        scratch_shapes=[
            pltpu.VMEM((ML_HEADS, ML_DQK, ML_DV + LANES), F32),
            pltpu.VMEM((SUBLANES, LANES), F32),
        ],
        compiler_params=_cparams(("parallel", "arbitrary")),
        name="mlstm_mixer",
    )(proj, proj, proj, small, small_t, gb_row, gb_col, ng)


def _ssd_kernel(z_ref, xbc_ref, sm_ref, dtt_ref, dtb_c_ref, alog_c_ref,
                dtb_r_ref, alog_r_ref, d_ref, ng_ref, y_ref, state_ref):
    ts = z_ref.shape[0]
    n_pair = MB_HEADS // 2
    pair_w = 2 * MB_P
    gw = MB_GROUPS * MB_N

    @pl.when(pl.program_id(1) == 0)
    def _():
        state_ref[...] = jnp.zeros_like(state_ref)

    tri_l = _mask_bf16(_tri_lower(CHUNK))
    sel_x = _mask_bf16(lax.broadcasted_iota(jnp.int32, (SMALL_W, MB_W), 0) - DT_COL
                       == jnp.right_shift(lax.broadcasted_iota(jnp.int32, (SMALL_W, MB_W), 1), MB_P_LOG2))
    ur = lax.broadcasted_iota(jnp.int32, (pair_w, pair_w), 0)
    uc = lax.broadcasted_iota(jnp.int32, (pair_w, pair_w), 1)
    same_half = jnp.right_shift(ur, CHUNK_LOG2) == jnp.right_shift(uc, CHUNK_LOG2)
    tri_u2 = _mask_bf16(same_half & (ur <= uc))
    causal = (lax.broadcasted_iota(jnp.int32, (CHUNK, MB_W), 0)
              >= jnp.bitwise_and(lax.broadcasted_iota(jnp.int32, (CHUNK, MB_W), 1), CHUNK - 1))
    first_head = lax.broadcasted_iota(jnp.int32, (CHUNK, pair_w), 1) < MB_P
    a_c = -jnp.exp(alog_c_ref[...])
    a_r = -jnp.exp(alog_r_ref[...])

    def chunk_body(c, carry):
        rows = pl.ds(pl.multiple_of(c * CHUNK, CHUNK), CHUNK)
        dt_c = _softplus(sm_ref[rows, :] + dtb_c_ref[...])
        cum_c = _sel_left(tri_l, dt_c * a_c)
        dt_x = _sel_right(dt_c, sel_x)
        cum_x = _sel_right(cum_c, sel_x)
        dt_r = _softplus(dtt_ref[c] + dtb_r_ref[...])
        cum_r = _sel_right(dt_r * a_r, tri_u2)
        pairs = range(n_pair)
        grp = [(2 * p) // (MB_HEADS // MB_GROUPS) for p in pairs]
        lanes = [slice(p * pair_w, (p + 1) * pair_w) for p in pairs]
        xs = xbc_ref[rows, :MB_W].astype(F32)
        xdt = xs * dt_x
        cum_row = jnp.concatenate([cum_r[p:p + 1, :] for p in pairs], axis=-1)
        decay = jnp.exp(jnp.where(causal, cum_x - cum_row, NEG_BIG))
        cum_last = cum_x[CHUNK - 1:CHUNK, :]
        x_dec = (xdt * jnp.exp(cum_last - cum_x)).astype(BF16)
        st_decay = jnp.exp(cum_last)
        carry_w = jnp.exp(cum_x)
        bm = [xbc_ref[rows, MB_W + g * MB_N:MB_W + (g + 1) * MB_N] for g in range(MB_GROUPS)]
        cm = [xbc_ref[rows, MB_W + gw + g * MB_N:MB_W + gw + (g + 1) * MB_N] for g in range(MB_GROUPS)]
        cb2 = [_dot_nt(cm[g], jnp.concatenate([bm[g], bm[g]], axis=0)) for g in range(MB_GROUPS)]
        x2 = [jnp.concatenate([jnp.where(first_head, xdt[:, lanes[p]], 0.0),
                               jnp.where(first_head, 0.0, xdt[:, lanes[p]])], axis=0).astype(BF16)
              for p in pairs]
        st = [state_ref[p] for p in pairs]
        y_inter = [_dot(cm[grp[p]], st[p].astype(BF16)) for p in pairs]
        y_intra = [_dot((cb2[grp[p]] * decay[:, lanes[p]]).astype(BF16), x2[p]) for p in pairs]
        for p in pairs:
            state_ref[p] = st_decay[:, lanes[p]] * st[p] + _dot_tn(bm[grp[p]], x_dec[:, lanes[p]])
        y = (jnp.concatenate(y_intra, axis=-1) + jnp.concatenate(y_inter, axis=-1) * carry_w
             + d_ref[...] * xs)

        yz = y * z_ref[rows, :].astype(F32)
        gsz = MB_W // MB_GROUPS
        parts = []
        for g in range(MB_GROUPS):
            o = yz[:, g * gsz:(g + 1) * gsz]
            parts.append(o * lax.rsqrt(jnp.mean(o * o, axis=-1, keepdims=True) + EPS))
        y_ref[rows, :] = (jnp.concatenate(parts, axis=-1) * ng_ref[...]).astype(y_ref.dtype)
        return carry

    lax.fori_loop(0, ts // CHUNK, chunk_body, 0, unroll=CHUNK_UNROLL)


def _ssd(proj, small, dt_t, dtb_c, alog_c, dtb_r, alog_r, d_x, ng, batch, seq):
    n = proj.shape[0]
    nt = seq // TS_MIX
    n_pair = MB_HEADS // 2

    def const(shape):
        return pl.BlockSpec(shape, lambda b, j: tuple(0 for _ in shape))

    return pl.pallas_call(
        _ssd_kernel,
        grid=(batch, nt),
        in_specs=[
            pl.BlockSpec((TS_MIX, MB_W), lambda b, j: (b * nt + j, 7)),
            pl.BlockSpec((TS_MIX, MB_CONV_DIM), lambda b, j: (b * nt + j, 4)),
            pl.BlockSpec((TS_MIX, SMALL_W), lambda b, j: (b * nt + j, 0)),
            pl.BlockSpec((TS_MIX // CHUNK, n_pair, 2 * CHUNK), lambda b, j: (b * nt + j, 0, 0)),
            const((1, SMALL_W)), const((1, SMALL_W)),
            const((n_pair, 2 * CHUNK)), const((n_pair, 2 * CHUNK)),
            const((1, MB_W)), const((1, MB_W)),
        ],
        out_specs=pl.BlockSpec((TS_MIX, MB_W), lambda b, j: (b * nt + j, 0)),
        out_shape=jax.ShapeDtypeStruct((n, MB_W), BF16),
        scratch_shapes=[pltpu.VMEM((n_pair, MB_N, 2 * MB_P), F32)],
        compiler_params=_cparams(("parallel", "arbitrary")),
        name="ssd_mixer",
    )(proj, proj, small, dt_t, dtb_c, alog_c, dtb_r, alog_r, d_x, ng)


def _merge_kernel(x_ref, yh_ref, ym_ref, yb_ref, g0_ref, g1_ref, g2_ref,
                  wh_ref, wm_ref, wb_ref, wo_ref, o_ref):
    mixed = g0_ref[...].astype(F32) * _dot(yh_ref[...], wh_ref[...])
    mixed = mixed + g1_ref[...].astype(F32) * _dot(ym_ref[...], wm_ref[...])
    mixed = mixed + g2_ref[...].astype(F32) * _dot(yb_ref[...], wb_ref[...])
    o_ref[...] = x_ref[...] + _dot(mixed.astype(BF16), wo_ref[...])


def _merge(x2, y_hg, y_ml, y_mb, proj, w_hg, w_ml, w_mb, w_out):
    n = x2.shape[0]
    tile = lambda cb: pl.BlockSpec((TM_MERGE, D_MODEL), lambda i: (i, cb))
    wspec = pl.BlockSpec((D_MODEL, D_MODEL), lambda i: (0, 0))
    return pl.pallas_call(
        _merge_kernel,
        grid=(n // TM_MERGE,),
        in_specs=[tile(0), tile(0), tile(0), tile(0), tile(10), tile(11), tile(12),
                  wspec, wspec, wspec, wspec],
        out_specs=tile(0),
        out_shape=jax.ShapeDtypeStruct((n, D_MODEL), F32),
        compiler_params=_cparams(("parallel",)),
        name="branch_merge",
    )(x2, y_hg, y_ml, y_mb, proj, proj, proj, w_hg, w_ml, w_mb, w_out)


def _ffn_kernel(x_ref, g_ref, wug_ref, wuv_ref, cwg_ref, cwv_ref, cbg_ref, cbv_ref, wd_ref, fg_ref,
                o_ref, tail_g, tail_v, buf, *, final):
    tm = x_ref.shape[0]

    @pl.when(pl.program_id(1) == 0)
    def _():
        tail_g[...] = jnp.zeros_like(tail_g)
        tail_v[...] = jnp.zeros_like(tail_v)

    x = x_ref[...]
    hb = (x * lax.rsqrt(jnp.mean(x * x, axis=-1, keepdims=True) + EPS) * g_ref[...]).astype(BF16)

    def conv(w_ref, cw_ref, cb_ref, tail, cs):
        buf[0:CONV_PAD, :] = tail[:, cs]
        buf[CONV_PAD:CONV_PAD + tm, :] = _dot(hb, w_ref[:, cs])
        tail[:, cs] = buf[tm:tm + CONV_PAD, :]
        y = cb_ref[:, cs]
        for k in range(FFN_CONV):
            off = CONV_PAD - (FFN_CONV - 1) + k
            y = y + cw_ref[k:k + 1, cs] * buf[off:off + tm, :]
        return y

    acc = x
    for c in range(D_FF // FF_CHUNK):
        cs = slice(c * FF_CHUNK, (c + 1) * FF_CHUNK)
        u_g = conv(wug_ref, cwg_ref, cbg_ref, tail_g, cs)
        u_v = conv(wuv_ref, cwv_ref, cbv_ref, tail_v, cs)
        acc = acc + _dot((_silu(u_g) * u_v).astype(BF16), wd_ref[cs, :])
    if final:
        acc = acc * lax.rsqrt(jnp.mean(acc * acc, axis=-1, keepdims=True) + EPS) * fg_ref[...]
    o_ref[...] = acc


def _ffn(x2, g, w_ug, w_uv, cw_g, cw_v, cb_g, cb_v, w_down, final_g, batch, seq, final):
    n = x2.shape[0]
    nt = seq // TM_FFN

    def const(shape):
        return pl.BlockSpec(shape, lambda b, j: tuple(0 for _ in shape),
                            pipeline_mode=pl.Buffered(1))

    tile = pl.BlockSpec((TM_FFN, D_MODEL), lambda b, j: (b * nt + j, 0))
    return pl.pallas_call(
        functools.partial(_ffn_kernel, final=final),
        grid=(batch, nt),
        in_specs=[
            tile, const((1, D_MODEL)),
            const((D_MODEL, D_FF)), const((D_MODEL, D_FF)),
            const((FFN_CONV, D_FF)), const((FFN_CONV, D_FF)),
            const((1, D_FF)), const((1, D_FF)),
            const((D_FF, D_MODEL)), const((1, D_MODEL)),
        ],
        out_specs=tile,
        out_shape=jax.ShapeDtypeStruct((n, D_MODEL), F32),
        scratch_shapes=[
            pltpu.VMEM((CONV_PAD, D_FF), F32),
            pltpu.VMEM((CONV_PAD, D_FF), F32),
            pltpu.VMEM((TM_FFN + 2 * CONV_PAD, FF_CHUNK), F32),
        ],
        compiler_params=_cparams(("parallel", "arbitrary")),
        name="conv_gated_mlp",
    )(x2, g, w_ug, w_uv, cw_g, cw_v, cb_g, cb_v, w_down, final_g)


def kernel(x, norm1_g, w_in, hg_lb_logits, hg_norm_g, ml_conv_w, ml_conv_b, ml_gate_b, ml_norm_g,
           mb_conv_w, mb_conv_b, mb_dt_bias, mb_a_log, mb_d, mb_norm_g, w_br_hg, w_br_ml, w_br_mb,
           w_out, norm2_g, w_up, ffn_conv_w, ffn_conv_b, w_down, final_g):
    batch, seq, _ = x.shape
    n = batch * seq
    assert seq % TS_MIX == 0 and seq % TM_FFN == 0 and seq % TM_PROJ == 0 and n % TM_MERGE == 0
    depth = w_in.shape[0]

    o_if = 4 * 1024 + 2 * ML_HEADS * ML_DQK + ML_HEADS * ML_DV
    o_mlo = o_if + 2 * ML_HEADS
    o_dt = o_mlo + ML_HEADS * ML_DV + MB_W + MB_CONV_DIM
    o_gate = o_dt + MB_HEADS
    w_big = jnp.concatenate(
        [w_in[:, :, :o_if], w_in[:, :, o_mlo:o_dt], w_in[:, :, o_gate:]], axis=-1).astype(BF16)
    pad = SMALL_W - 2 * ML_HEADS - MB_HEADS
    w_small = jnp.concatenate(
        [w_in[:, :, o_if:o_mlo], w_in[:, :, o_dt:o_gate],
         jnp.zeros((depth, D_MODEL, pad), w_in.dtype)], axis=-1).astype(BF16)

    lbs = _lbs(hg_lb_logits.astype(F32))
    hg_ng = jnp.tile(hg_norm_g, (1, HG_HEADS))
    ml_ng = jnp.tile(ml_norm_g, (1, ML_HEADS))
    gb_row = jnp.pad(ml_gate_b, ((0, 0), (0, SMALL_W - 2 * ML_HEADS)))
    dtb_c = jnp.pad(mb_dt_bias, ((0, 0), (DT_COL, SMALL_W - DT_COL - MB_HEADS)))
    alog_c = jnp.pad(mb_a_log, ((0, 0), (DT_COL, SMALL_W - DT_COL - MB_HEADS)))
    n_pair = MB_HEADS // 2
    dtb_r = jnp.repeat(mb_dt_bias, CHUNK, axis=-1).reshape(depth, n_pair, 2 * CHUNK)
    alog_r = jnp.repeat(mb_a_log, CHUNK, axis=-1).reshape(depth, n_pair, 2 * CHUNK)
    d_x = jnp.repeat(mb_d, MB_P, axis=-1)

    n_tiles = w_big.shape[-1] // TN_PROJ
    conv_w = jnp.concatenate([ml_conv_w, mb_conv_w], axis=-1).reshape(depth, CONV_W, len(TILES_CONV), TN_PROJ)
    conv_b = jnp.concatenate([ml_conv_b, mb_conv_b], axis=-1).reshape(depth, 1, len(TILES_CONV), TN_PROJ)
    cw_all = jnp.zeros((depth, n_tiles, CONV_W, TN_PROJ), F32).at[:, jnp.array(TILES_CONV)].set(
        jnp.swapaxes(conv_w, 1, 2))
    cb_all = jnp.zeros((depth, n_tiles, 1, TN_PROJ), F32).at[:, jnp.array(TILES_CONV)].set(
        jnp.swapaxes(conv_b, 1, 2))

    w_hg = w_br_hg.astype(BF16)
    w_ml = w_br_ml.astype(BF16)
    w_mb = w_br_mb.astype(BF16)
    w_o = w_out.astype(BF16)
    w_ug = w_up[:, :, :D_FF].astype(BF16)
    w_uv = w_up[:, :, D_FF:].astype(BF16)
    w_d = w_down.astype(BF16)

    x2 = x.reshape(n, D_MODEL)
    row = lambda a: a.reshape(1, -1)
    for l in range(depth):
        proj, small = _inproj(x2, row(norm1_g[l]), w_big[l], w_small[l], cw_all[l], cb_all[l], seq)
        chunks = small.reshape(n // CHUNK, CHUNK, SMALL_W)
        small_t = jnp.swapaxes(chunks[:, :, :2 * ML_HEADS], 1, 2)
        dt_t = jnp.swapaxes(chunks[:, :, DT_COL:DT_COL + MB_HEADS], 1, 2).reshape(
            n // CHUNK, n_pair, 2 * CHUNK)

        y_hg = _hgrn2(proj, row(lbs[l]), row(hg_ng[l]), batch, seq)
        y_ml = _mlstm(proj, small, small_t, row(gb_row[l]),
                      ml_gate_b[l].reshape(2 * ML_HEADS, 1), row(ml_ng[l]), batch, seq)
        y_mb = _ssd(proj, small, dt_t, row(dtb_c[l]), row(alog_c[l]),
                    dtb_r[l], alog_r[l], row(d_x[l]), row(mb_norm_g[l]), batch, seq)
        x2 = _merge(x2, y_hg, y_ml, y_mb, proj, w_hg[l], w_ml[l], w_mb[l], w_o[l])
        x2 = _ffn(x2, row(norm2_g[l]), w_ug[l], w_uv[l], ffn_conv_w[l, :, :D_FF], ffn_conv_w[l, :, D_FF:],
                  row(ffn_conv_b[l, :D_FF]), row(ffn_conv_b[l, D_FF:]), w_d[l], row(final_g),
                  batch, seq, final=(l == depth - 1))
    return x2.reshape(batch, seq, D_MODEL)
```

```python
import functools
import math

import jax
import jax.numpy as jnp
from jax import lax
from jax.experimental import pallas as pl
from jax.experimental.pallas import tpu as pltpu

F32 = jnp.float32
BF16 = jnp.bfloat16

D_MODEL = 1024
DEPTH = 4
CHUNK = 64
CHUNK_LOG2 = 6
SUB = 8
SUBLANES = 8
LANES = 128
LOG2E = math.log2(math.e)
EPS = 1e-6
NEG_BIG = -1e30
HG_HEADS = 8
HG_D = 128
ML_HEADS = 4
ML_DQK = 128
ML_DV = 256
ML_CONV = 4
MB_HEADS = 16
MB_P = 64
MB_P_LOG2 = 6
MB_GROUPS = 4
MB_N = 128
MB_CONV = 4
MB_W = MB_HEADS * MB_P
MB_CONV_DIM = MB_W + 2 * MB_GROUPS * MB_N
D_FF = 2816
FFN_CONV = 3
FF_CHUNK = 1408
SMALL_W = 128
DT_COL = 8
CONV_PAD = 8

VMEM_LIMIT = 56 * 1024 * 1024

TM_PROJ = 2048
TN_PROJ = 1024
EPI_ROWS = 256
TILES_SILU = (0, 3, 7)
TILES_SIGMOID = (6, 10, 11, 12)
TILES_RAW = (1, 2, 5)
TILES_CONV = (4, 8, 9)
CONV_W = 4
TS_MIX = 512
CHUNK_UNROLL = 2
TM_MERGE = 512
TM_FFN = 512


def _silu(x):
    return x * jax.nn.sigmoid(x)


def _softplus(x):
    return jnp.maximum(x, 0.0) + jnp.log1p(jnp.exp(-jnp.abs(x)))


def _log_sigmoid(x):
    return jnp.minimum(x, 0.0) - jnp.log1p(jnp.exp(-jnp.abs(x)))


def _dot(a, b):
    return jnp.dot(a, b, preferred_element_type=F32)


def _dot_nt(a, b):
    return lax.dot_general(a, b, (((1,), (1,)), ((), ())), preferred_element_type=F32)


def _dot_tn(a, b):
    return lax.dot_general(a, b, (((0,), (0,)), ((), ())), preferred_element_type=F32)


def _split3(a):
    hi = a.astype(BF16)
    r = a - hi.astype(F32)
    mid = r.astype(BF16)
    lo = (r - mid.astype(F32)).astype(BF16)
    return hi, mid, lo


def _sel_left(sel, a):
    hi, mid, lo = _split3(a)
    return _dot(sel, hi) + (_dot(sel, mid) + _dot(sel, lo))


def _sel_right(a, sel):
    hi, mid, lo = _split3(a)
    return _dot(hi, sel) + (_dot(mid, sel) + _dot(lo, sel))


def _tri_lower(n):
    r = lax.broadcasted_iota(jnp.int32, (n, n), 0)
    c = lax.broadcasted_iota(jnp.int32, (n, n), 1)
    return r >= c


def _bcast_row(ref, h, r):
    return ref[h, pl.ds(r, SUBLANES, stride=0), :]


def _mask_bf16(m):
    return jnp.where(m, 1.0, 0.0).astype(BF16)


def _cparams(sem):
    return pltpu.CompilerParams(dimension_semantics=sem, vmem_limit_bytes=VMEM_LIMIT)


def _lbs_kernel(lg_ref, o_ref):
    lg = lg_ref[...]
    mx = jnp.max(lg, axis=0, keepdims=True)
    e = jnp.exp(lg - mx)
    p = e / jnp.sum(e, axis=0, keepdims=True)
    acc = jnp.zeros_like(p[0:1])
    rows = []
    for l in range(lg.shape[0]):
        acc = acc + p[l:l + 1]
        rows.append(acc - p[0:1])
    o_ref[...] = jnp.concatenate(rows, axis=0)


def _lbs(logits):
    return pl.pallas_call(
        _lbs_kernel,
        out_shape=jax.ShapeDtypeStruct(logits.shape, F32),
        name="hgrn2_lower_bounds",
    )(logits)


def _any_tile(j, tiles):
    hit = j == tiles[0]
    for t in tiles[1:]:
        hit = jnp.logical_or(hit, j == t)
    return hit


def _inproj_kernel(x_ref, g_ref, w_ref, ws_ref, cw_ref, cb_ref, o_ref, os_ref, h_ref, tail_ref, hbuf,
                   *, tiles_per_seq):
    i = pl.program_id(0)
    j = pl.program_id(1)
    n_blk = TM_PROJ // EPI_ROWS

    @pl.when(j == 0)
    def _():
        x = x_ref[...]
        ms = jnp.mean(x * x, axis=-1, keepdims=True)
        hb = (x * lax.rsqrt(ms + EPS) * g_ref[...]).astype(BF16)
        h_ref[...] = hb
        os_ref[...] = _dot(hb, ws_ref[...])

    def pointwise(fn):
        ys = [_dot(h_ref[r * EPI_ROWS:(r + 1) * EPI_ROWS, :], w_ref[...]) for r in range(n_blk)]
        for r in range(n_blk):
            o_ref[r * EPI_ROWS:(r + 1) * EPI_ROWS, :] = fn(ys[r]).astype(o_ref.dtype)

    @pl.when(_any_tile(j, TILES_SILU))
    def _():
        pointwise(_silu)

    @pl.when(_any_tile(j, TILES_SIGMOID))
    def _():
        pointwise(jax.nn.sigmoid)

    @pl.when(_any_tile(j, TILES_RAW))
    def _():
        pointwise(lambda y: y)

    @pl.when(_any_tile(j, TILES_CONV))
    def _():
        slot = jnp.where(j == TILES_CONV[0], 0, j - TILES_CONV[1] + 1)
        prev = jnp.where(i % tiles_per_seq == 0, 0.0, tail_ref[slot])
        cw = cw_ref[0]
        cb = cb_ref[0]
        ys = [_dot(h_ref[r * EPI_ROWS:(r + 1) * EPI_ROWS, :], w_ref[...]) for r in range(n_blk)]
        for r in range(n_blk):
            yb = ys[r]
            out = cb + cw[CONV_W - 1:CONV_W] * yb
            for k in range(CONV_W - 1):
                out = out + cw[k:k + 1] * pltpu.roll(yb, CONV_W - 1 - k, axis=0)
            hbuf[0:CONV_PAD, :] = prev
            hbuf[CONV_PAD:2 * CONV_PAD, :] = yb[:CONV_PAD]
            head = cb
            for k in range(CONV_W):
                off = CONV_PAD - (CONV_W - 1) + k
                head = head + cw[k:k + 1] * hbuf[off:off + CONV_PAD, :]
            prev = yb[EPI_ROWS - CONV_PAD:]
            o_ref[r * EPI_ROWS:(r + 1) * EPI_ROWS, :] = _silu(
                jnp.concatenate([head, out[CONV_PAD:]], axis=0)).astype(o_ref.dtype)
        tail_ref[slot] = prev


def _inproj(x2, g, w_big, w_small, cw_all, cb_all, seq):
    n = x2.shape[0]
    nb = w_big.shape[1]
    n_tiles = nb // TN_PROJ
    assert n_tiles == len(TILES_SILU + TILES_SIGMOID + TILES_RAW + TILES_CONV)
    return pl.pallas_call(
        functools.partial(_inproj_kernel, tiles_per_seq=seq // TM_PROJ),
        grid=(n // TM_PROJ, n_tiles),
        in_specs=[
            pl.BlockSpec((TM_PROJ, D_MODEL), lambda i, j: (i, 0)),
            pl.BlockSpec((1, D_MODEL), lambda i, j: (0, 0)),
            pl.BlockSpec((D_MODEL, TN_PROJ), lambda i, j: (0, j)),
            pl.BlockSpec((D_MODEL, SMALL_W), lambda i, j: (0, 0)),
            pl.BlockSpec((1, CONV_W, TN_PROJ), lambda i, j: (j, 0, 0)),
            pl.BlockSpec((1, 1, TN_PROJ), lambda i, j: (j, 0, 0)),
        ],
        out_specs=[
            pl.BlockSpec((TM_PROJ, TN_PROJ), lambda i, j: (i, j)),
            pl.BlockSpec((TM_PROJ, SMALL_W), lambda i, j: (i, 0)),
        ],
        out_shape=[
            jax.ShapeDtypeStruct((n, nb), BF16),
            jax.ShapeDtypeStruct((n, SMALL_W), F32),
        ],
        scratch_shapes=[
            pltpu.VMEM((TM_PROJ, D_MODEL), BF16),
            pltpu.VMEM((len(TILES_CONV), CONV_PAD, TN_PROJ), F32),
            pltpu.VMEM((2 * CONV_PAD, TN_PROJ), F32),
        ],
        compiler_params=_cparams(("arbitrary", "arbitrary")),
        name="in_projection",
    )(x2, g, w_big, w_small, cw_all, cb_all)


def _hgrn2_kernel(q_ref, f_ref, i_ref, g_ref, lb_ref, ng_ref, y_ref,
                  state_ref, c_s):
    @pl.when(pl.program_id(1) == 0)
    def _():
        state_ref[...] = jnp.zeros_like(state_ref)

    ts = q_ref.shape[0]
    lb = lb_ref[...]
    tri = _mask_bf16(_tri_lower(CHUNK))
    n_sub = CHUNK // SUB
    ones_rhs = jnp.ones((HG_D, HG_D), BF16)
    lane = lax.broadcasted_iota(jnp.int32, (SUBLANES, HG_D), 1)
    row = lax.broadcasted_iota(jnp.int32, (SUBLANES, HG_D), 0)

    heads = range(HG_HEADS)
    hsl = [slice(h * HG_D, (h + 1) * HG_D) for h in heads]
    zero_tail = jnp.zeros((HG_D - CHUNK, HG_D), BF16)

    def chunk_body(c, carry):
        rows = pl.ds(pl.multiple_of(c * CHUNK, CHUNK), CHUNK)
        sig = jax.nn.sigmoid(f_ref[rows, :].astype(F32))
        fgate = lb + (1.0 - lb) * sig
        b = _sel_left(tri, jnp.log(fgate) * LOG2E)
        cc = b - jnp.log(jnp.maximum(1.0 - fgate, 0.0)) * LOG2E
        for h in heads:
            c_s[h] = cc[:, hsl[h]]
        q = q_ref[rows, :].astype(F32)
        v16 = i_ref[rows, :]

        st = [state_ref[h] for h in heads]
        q_dec = (q * jnp.exp2(b)).astype(BF16)
        b_last = b[CHUNK - 1:CHUNK, :]
        k_dec = jnp.exp2(b_last - cc).astype(BF16)
        st_decay = jnp.exp2(b_last)
        o_inter = [_dot_nt(q_dec[:, hsl[h]], st[h].astype(BF16)) for h in heads]
        for h in heads:
            state_ref[h] = st_decay[:, hsl[h]] * st[h] + _dot_tn(v16[:, hsl[h]], k_dec[:, hsl[h]])

        zs = []
        for i in range(n_sub):
            lo = i * SUB
            for s in range(SUB):
                r0 = lo + (s // SUBLANES) * SUBLANES
                c_row = jnp.concatenate([_bcast_row(c_s, h, lo + s) for h in heads], axis=-1)
                c_row = jnp.concatenate([c_row] * ((lo + SUB - r0) // SUBLANES), axis=0)
                zs.append(q[r0:lo + SUB] * jnp.exp2(b[r0:lo + SUB] - c_row))
        z_rows = sum(z.shape[0] for z in zs)
        z_all = jnp.concatenate([z[:, hsl[h]] for h in heads for z in zs], axis=0)
        r = _dot(z_all.astype(BF16), ones_rhs)

        a_off = []
        for i in range(1, n_sub):
            lo = i * SUB
            bref = b[lo - 1:lo, :]
            q_i = (q[lo:lo + SUB] * jnp.exp2(b[lo:lo + SUB] - bref)).astype(BF16)
            k_i = jnp.exp2(bref - cc[:lo]).astype(BF16)
            zero_rows = jnp.zeros((HG_D - lo, HG_D), BF16)
            a_off.append([_dot_nt(q_i[:, hsl[h]], jnp.concatenate([k_i[:, hsl[h]], zero_rows], axis=0))
                          for h in heads])

        outs = []
        for h in heads:
            a_rows = []
            off = h * z_rows
            for i in range(n_sub):
                lo = i * SUB
                a_i = a_off[i - 1][h] if i > 0 else jnp.zeros((SUB, HG_D), F32)
                tiles = [a_i[j * SUBLANES:(j + 1) * SUBLANES] for j in range(SUB // SUBLANES)]
                for s in range(SUB):
                    for j in range(s // SUBLANES, SUB // SUBLANES):
                        tiles[j] = jnp.where(lane == lo + s, r[off:off + SUBLANES], tiles[j])
                        off += SUBLANES
                for j in range(SUB // SUBLANES):
                    a_rows.append(jnp.where(lane - lo <= row + j * SUBLANES, tiles[j], 0.0))
            a_full = jnp.concatenate(a_rows, axis=0).astype(BF16)
            v_pad = jnp.concatenate([v16[:, hsl[h]], zero_tail], axis=0)
            outs.append(o_inter[h] + _dot(a_full, v_pad))
        parts = []
        for o in outs:
            parts.append(o * lax.rsqrt(jnp.mean(o * o, axis=-1, keepdims=True) + EPS))
        on = jnp.concatenate(parts, axis=-1) * ng_ref[...]
        y_ref[rows, :] = (on * g_ref[rows, :].astype(F32)).astype(y_ref.dtype)
        return carry

    lax.fori_loop(0, ts // CHUNK, chunk_body, 0, unroll=CHUNK_UNROLL)


def _hgrn2(proj, lb, ng, batch, seq):
    n = proj.shape[0]
    nt = seq // TS_MIX
    w = HG_HEADS * HG_D

    def col(cb):
        return pl.BlockSpec((TS_MIX, w), lambda b, j: (b * nt + j, cb))

    vec = pl.BlockSpec((1, w), lambda b, j: (0, 0))
    return pl.pallas_call(
        _hgrn2_kernel,
        grid=(batch, nt),
        in_specs=[col(0), col(1), col(2), col(3), vec, vec],
        out_specs=pl.BlockSpec((TS_MIX, w), lambda b, j: (b * nt + j, 0)),
        out_shape=jax.ShapeDtypeStruct((n, w), BF16),
        scratch_shapes=[pltpu.VMEM((HG_HEADS, HG_D, HG_D), F32), pltpu.VMEM((HG_HEADS, CHUNK, HG_D), F32)],
        compiler_params=_cparams(("parallel", "arbitrary")),
        name="hgrn2_mixer",
    )(proj, proj, proj, proj, lb, ng)


def _mlstm_kernel(qk_ref, v_ref, og_ref, sm_ref, smt_ref, gbr_ref, gbc_ref, ng_ref,
                  y_ref, caug_ref, m_ref):
    ts = qk_ref.shape[0]
    half = ML_HEADS * ML_DQK
    k_scale = ML_DQK ** -0.5

    @pl.when(pl.program_id(1) == 0)
    def _():
        caug_ref[...] = jnp.zeros_like(caug_ref)
        m_ref[...] = jnp.zeros_like(m_ref)

    tri_b = _tri_lower(CHUNK)
    tri_l = _mask_bf16(tri_b)
    tri_u = _mask_bf16(lax.broadcasted_iota(jnp.int32, (CHUNK, CHUNK), 0)
                       <= lax.broadcasted_iota(jnp.int32, (CHUNK, CHUNK), 1))
    ones_col = _mask_bf16(lax.broadcasted_iota(jnp.int32, (CHUNK, LANES), 1) == 0)

    def chunk_body(c, carry):
        rows = pl.ds(pl.multiple_of(c * CHUNK, CHUNK), CHUNK)
        pre_c = sm_ref[rows, :] + gbr_ref[...]
        pre_r = smt_ref[c] + gbc_ref[...]
        cum_c = _sel_left(tri_l, _log_sigmoid(pre_c))
        cum_r = _sel_right(_log_sigmoid(pre_r), tri_u)
        heads = range(ML_HEADS)
        m_all = m_ref[...]
        m_old = [m_all[h:h + 1, 0:1] for h in heads]
        b_col = [cum_c[:, ML_HEADS + h:ML_HEADS + h + 1] for h in heads]
        i_col = [pre_c[:, h:h + 1] for h in heads]
        log_d = [jnp.where(tri_b, b_col[h] - cum_r[ML_HEADS + h:ML_HEADS + h + 1, :] + pre_r[h:h + 1, :],
                           NEG_BIG) for h in heads]
        log_inter = [b_col[h] + m_old[h] for h in heads]
        m_t = [jnp.maximum(jnp.max(log_d[h], axis=-1, keepdims=True), log_inter[h]) for h in heads]
        b_last = [b_col[h][CHUNK - 1:CHUNK, :] for h in heads]
        log_w = [b_last[h] - b_col[h] + i_col[h] for h in heads]
        m_new = [jnp.maximum(b_last[h] + m_old[h], jnp.max(log_w[h], axis=0, keepdims=True)) for h in heads]
        m_ref[...] = jnp.concatenate(
            [jnp.broadcast_to(m_new[h], (1, m_ref.shape[1])) for h in heads] + [m_all[ML_HEADS:]], axis=0)

        qb = [qk_ref[rows, h * ML_DQK:(h + 1) * ML_DQK] for h in heads]
        kb = [qk_ref[rows, half + h * ML_DQK:half + (h + 1) * ML_DQK] for h in heads]
        v_aug = [jnp.concatenate([v_ref[rows, h * ML_DV:(h + 1) * ML_DV], ones_col], axis=-1)
                 for h in heads]
        qk = [_dot_nt(qb[h], kb[h]) for h in heads]
        c_aug = [caug_ref[h] for h in heads]
        inter = [_dot(qb[h], c_aug[h].astype(BF16)) * jnp.exp(log_inter[h] - m_t[h]) for h in heads]
        s = [(qk[h] * (jnp.exp(log_d[h] - m_t[h]) * k_scale)).astype(BF16) for h in heads]
        num = [_dot(s[h], v_aug[h]) + inter[h] for h in heads]
        kw = [(kb[h].astype(F32) * (jnp.exp(log_w[h] - m_new[h]) * k_scale)).astype(BF16) for h in heads]
        for h in heads:
            caug_ref[h] = jnp.exp(b_last[h] + m_old[h] - m_new[h]) * c_aug[h] + _dot_tn(kw[h], v_aug[h])

        parts = []
        for h in heads:
            denom = jnp.maximum(jnp.abs(num[h][:, ML_DV:ML_DV + 1]), jnp.exp(-m_t[h]))
            o = num[h][:, :ML_DV] / denom
            parts.append(o * lax.rsqrt(jnp.mean(o * o, axis=-1, keepdims=True) + EPS))
        on = jnp.concatenate(parts, axis=-1) * ng_ref[...]
        y_ref[rows, :] = (on * og_ref[rows, :].astype(F32)).astype(y_ref.dtype)
        return carry

    lax.fori_loop(0, ts // CHUNK, chunk_body, 0, unroll=CHUNK_UNROLL)


def _mlstm(proj, small, small_t, gb_row, gb_col, ng, batch, seq):
    n = proj.shape[0]
    nt = seq // TS_MIX
    w = ML_HEADS * ML_DV

    def col(cb_):
        return pl.BlockSpec((TS_MIX, w), lambda b, j: (b * nt + j, cb_))

    def const(shape):
        return pl.BlockSpec(shape, lambda b, j: tuple(0 for _ in shape))

    return pl.pallas_call(
        _mlstm_kernel,
        grid=(batch, nt),
        in_specs=[
            col(4), col(5), col(6),
            pl.BlockSpec((TS_MIX, SMALL_W), lambda b, j: (b * nt + j, 0)),
            pl.BlockSpec((TS_MIX // CHUNK, 8, CHUNK), lambda b, j: (b * nt + j, 0, 0)),
            const((1, SMALL_W)), const((8, 1)), const((1, w)),
        ],
        out_specs=pl.BlockSpec((TS_MIX, w), lambda b, j: (b * nt + j, 0)),
        out_shape=jax.ShapeDtypeStruct((n, w), BF16),
        scratch_shapes=[
            pltpu.VMEM((ML_HEADS, ML_DQK, ML_DV + LANES), F32),
            pltpu.VMEM((SUBLANES, LANES), F32),
        ],
        compiler_params=_cparams(("parallel", "arbitrary")),
        name="mlstm_mixer",
    )(proj, proj, proj, small, small_t, gb_row, gb_col, ng)


def _ssd_kernel(z_ref, xbc_ref, sm_ref, dtt_ref, dtb_c_ref, alog_c_ref,
                dtb_r_ref, alog_r_ref, d_ref, ng_ref, y_ref, state_ref):
    ts = z_ref.shape[0]
    n_pair = MB_HEADS // 2
    pair_w = 2 * MB_P
    gw = MB_GROUPS * MB_N

    @pl.when(pl.program_id(1) == 0)
    def _():
        state_ref[...] = jnp.zeros_like(state_ref)

    tri_l = _mask_bf16(_tri_lower(CHUNK))
    sel_x = _mask_bf16(lax.broadcasted_iota(jnp.int32, (SMALL_W, MB_W), 0) - DT_COL
                       == jnp.right_shift(lax.broadcasted_iota(jnp.int32, (SMALL_W, MB_W), 1), MB_P_LOG2))
    ur = lax.broadcasted_iota(jnp.int32, (pair_w, pair_w), 0)
    uc = lax.broadcasted_iota(jnp.int32, (pair_w, pair_w), 1)
    same_half = jnp.right_shift(ur, CHUNK_LOG2) == jnp.right_shift(uc, CHUNK_LOG2)
    tri_u2 = _mask_bf16(same_half & (ur <= uc))
    causal = (lax.broadcasted_iota(jnp.int32, (CHUNK, MB_W), 0)
              >= jnp.bitwise_and(lax.broadcasted_iota(jnp.int32, (CHUNK, MB_W), 1), CHUNK - 1))
    first_head = lax.broadcasted_iota(jnp.int32, (CHUNK, pair_w), 1) < MB_P
    a_c = -jnp.exp(alog_c_ref[...])
    a_r = -jnp.exp(alog_r_ref[...])

    def chunk_body(c, carry):
        rows = pl.ds(pl.multiple_of(c * CHUNK, CHUNK), CHUNK)
        dt_c = _softplus(sm_ref[rows, :] + dtb_c_ref[...])
        cum_c = _sel_left(tri_l, dt_c * a_c)
        dt_x = _sel_right(dt_c, sel_x)
        cum_x = _sel_right(cum_c, sel_x)
        dt_r = _softplus(dtt_ref[c] + dtb_r_ref[...])
        cum_r = _sel_right(dt_r * a_r, tri_u2)
        pairs = range(n_pair)
        grp = [(2 * p) // (MB_HEADS // MB_GROUPS) for p in pairs]
        lanes = [slice(p * pair_w, (p + 1) * pair_w) for p in pairs]
        xs = xbc_ref[rows, :MB_W].astype(F32)
        xdt = xs * dt_x
        cum_row = jnp.concatenate([cum_r[p:p + 1, :] for p in pairs], axis=-1)
        decay = jnp.exp(jnp.where(causal, cum_x - cum_row, NEG_BIG))
        cum_last = cum_x[CHUNK - 1:CHUNK, :]
        x_dec = (xdt * jnp.exp(cum_last - cum_x)).astype(BF16)
        st_decay = jnp.exp(cum_last)
        carry_w = jnp.exp(cum_x)
        bm = [xbc_ref[rows, MB_W + g * MB_N:MB_W + (g + 1) * MB_N] for g in range(MB_GROUPS)]
        cm = [xbc_ref[rows, MB_W + gw + g * MB_N:MB_W + gw + (g + 1) * MB_N] for g in range(MB_GROUPS)]
        cb2 = [_dot_nt(cm[g], jnp.concatenate([bm[g], bm[g]], axis=0)) for g in range(MB_GROUPS)]
        x2 = [jnp.concatenate([jnp.where(first_head, xdt[:, lanes[p]], 0.0),
                               jnp.where(first_head, 0.0, xdt[:, lanes[p]])], axis=0).astype(BF16)
              for p in pairs]
        st = [state_ref[p] for p in pairs]
        y_inter = [_dot(cm[grp[p]], st[p].astype(BF16)) for p in pairs]
        y_intra = [_dot((cb2[grp[p]] * decay[:, lanes[p]]).astype(BF16), x2[p]) for p in pairs]
        for p in pairs:
            state_ref[p] = st_decay[:, lanes[p]] * st[p] + _dot_tn(bm[grp[p]], x_dec[:, lanes[p]])
        y = (jnp.concatenate(y_intra, axis=-1) + jnp.concatenate(y_inter, axis=-1) * carry_w
             + d_ref[...] * xs)

        yz = y * z_ref[rows, :].astype(F32)
        gsz = MB_W // MB_GROUPS
        parts = []
        for g in range(MB_GROUPS):
            o = yz[:, g * gsz:(g + 1) * gsz]
            parts.append(o * lax.rsqrt(jnp.mean(o * o, axis=-1, keepdims=True) + EPS))
        y_ref[rows, :] = (jnp.concatenate(parts, axis=-1) * ng_ref[...]).astype(y_ref.dtype)
        return carry

    lax.fori_loop(0, ts // CHUNK, chunk_body, 0, unroll=CHUNK_UNROLL)


def _ssd(proj, small, dt_t, dtb_c, alog_c, dtb_r, alog_r, d_x, ng, batch, seq):
    n = proj.shape[0]
    nt = seq // TS_MIX
    n_pair = MB_HEADS // 2

    def const(shape):
        return pl.BlockSpec(shape, lambda b, j: tuple(0 for _ in shape))

    return pl.pallas_call(
        _ssd_kernel,
        grid=(batch, nt),
        in_specs=[
            pl.BlockSpec((TS_MIX, MB_W), lambda b, j: (b * nt + j, 7)),
            pl.BlockSpec((TS_MIX, MB_CONV_DIM), lambda b, j: (b * nt + j, 4)),
            pl.BlockSpec((TS_MIX, SMALL_W), lambda b, j: (b * nt + j, 0)),
            pl.BlockSpec((TS_MIX // CHUNK, n_pair, 2 * CHUNK), lambda b, j: (b * nt + j, 0, 0)),
            const((1, SMALL_W)), const((1, SMALL_W)),
            const((n_pair, 2 * CHUNK)), const((n_pair, 2 * CHUNK)),
            const((1, MB_W)), const((1, MB_W)),
        ],
        out_specs=pl.BlockSpec((TS_MIX, MB_W), lambda b, j: (b * nt + j, 0)),
        out_shape=jax.ShapeDtypeStruct((n, MB_W), BF16),
        scratch_shapes=[pltpu.VMEM((n_pair, MB_N, 2 * MB_P), F32)],
        compiler_params=_cparams(("parallel", "arbitrary")),
        name="ssd_mixer",
    )(proj, proj, small, dt_t, dtb_c, alog_c, dtb_r, alog_r, d_x, ng)


def _merge_kernel(x_ref, yh_ref, ym_ref, yb_ref, g0_ref, g1_ref, g2_ref,
                  wh_ref, wm_ref, wb_ref, wo_ref, o_ref):
    mixed = g0_ref[...].astype(F32) * _dot(yh_ref[...], wh_ref[...])
    mixed = mixed + g1_ref[...].astype(F32) * _dot(ym_ref[...], wm_ref[...])
    mixed = mixed + g2_ref[...].astype(F32) * _dot(yb_ref[...], wb_ref[...])
    o_ref[...] = x_ref[...] + _dot(mixed.astype(BF16), wo_ref[...])


def _merge(x2, y_hg, y_ml, y_mb, proj, w_hg, w_ml, w_mb, w_out):
    n = x2.shape[0]
    tile = lambda cb: pl.BlockSpec((TM_MERGE, D_MODEL), lambda i: (i, cb))
    wspec = pl.BlockSpec((D_MODEL, D_MODEL), lambda i: (0, 0))
    return pl.pallas_call(
        _merge_kernel,
        grid=(n // TM_MERGE,),
        in_specs=[tile(0), tile(0), tile(0), tile(0), tile(10), tile(11), tile(12),
                  wspec, wspec, wspec, wspec],
        out_specs=tile(0),
        out_shape=jax.ShapeDtypeStruct((n, D_MODEL), F32),
        compiler_params=_cparams(("parallel",)),
        name="branch_merge",
    )(x2, y_hg, y_ml, y_mb, proj, proj, proj, w_hg, w_ml, w_mb, w_out)


def _ffn_kernel(x_ref, g_ref, wu_ref, cw_ref, cb_ref, wd_ref, fg_ref,
                o_ref, tail, hbuf_g, hbuf_v, *, final):
    tm = x_ref.shape[0]
    n_ck = D_FF // FF_CHUNK

    @pl.when(pl.program_id(1) == 0)
    def _():
        tail[...] = jnp.zeros_like(tail)

    x = x_ref[...]
    hb = (x * lax.rsqrt(jnp.mean(x * x, axis=-1, keepdims=True) + EPS) * g_ref[...]).astype(BF16)

    def cols(c, half):
        return slice(half * D_FF + c * FF_CHUNK, half * D_FF + (c + 1) * FF_CHUNK)

    def up(c):
        return _dot(hb, wu_ref[:, cols(c, 0)]), _dot(hb, wu_ref[:, cols(c, 1)])

    def conv(u, hbuf, cs):
        out = cb_ref[:, cs] + cw_ref[FFN_CONV - 1:FFN_CONV, cs] * u
        for k in range(FFN_CONV - 1):
            out = out + cw_ref[k:k + 1, cs] * pltpu.roll(u, FFN_CONV - 1 - k, axis=0)
        hbuf[0:CONV_PAD, :] = tail[:, cs]
        hbuf[CONV_PAD:2 * CONV_PAD, :] = u[:CONV_PAD]
        tail[:, cs] = u[tm - CONV_PAD:]
        head = cb_ref[:, cs]
        for k in range(FFN_CONV):
            off = CONV_PAD - (FFN_CONV - 1) + k
            head = head + cw_ref[k:k + 1, cs] * hbuf[off:off + CONV_PAD, :]
        return jnp.concatenate([head, out[CONV_PAD:]], axis=0)

    acc = x
    u_next = up(0)
    for c in range(n_ck):
        u_g, u_v = u_next
        if c + 1 < n_ck:
            u_next = up(c + 1)
        a_g = conv(u_g, hbuf_g, cols(c, 0))
        a_v = conv(u_v, hbuf_v, cols(c, 1))
        acc = acc + _dot((_silu(a_g) * a_v).astype(BF16), wd_ref[cols(c, 0), :])
    if final:
        acc = acc * lax.rsqrt(jnp.mean(acc * acc, axis=-1, keepdims=True) + EPS) * fg_ref[...]
    o_ref[...] = acc


def _ffn(x2, g, w_up, cw, cb, w_down, final_g, batch, seq, final):
    n = x2.shape[0]
    nt = seq // TM_FFN

    def const(shape):
        return pl.BlockSpec(shape, lambda b, j: tuple(0 for _ in shape),
                            pipeline_mode=pl.Buffered(1))

    tile = pl.BlockSpec((TM_FFN, D_MODEL), lambda b, j: (b * nt + j, 0))
    return pl.pallas_call(
        functools.partial(_ffn_kernel, final=final),
        grid=(batch, nt),
        in_specs=[
            tile, const((1, D_MODEL)),
            const((D_MODEL, 2 * D_FF)), const((FFN_CONV, 2 * D_FF)), const((1, 2 * D_FF)),
            const((D_FF, D_MODEL)), const((1, D_MODEL)),
        ],
        out_specs=tile,
        out_shape=jax.ShapeDtypeStruct((n, D_MODEL), F32),
        scratch_shapes=[
            pltpu.VMEM((CONV_PAD, 2 * D_FF), F32),
            pltpu.VMEM((2 * CONV_PAD, FF_CHUNK), F32),
            pltpu.VMEM((2 * CONV_PAD, FF_CHUNK), F32),
        ],
        compiler_params=_cparams(("parallel", "arbitrary")),
        name="conv_gated_mlp",
    )(x2, g, w_up, cw, cb, w_down, final_g)


def kernel(x, norm1_g, w_in, hg_lb_logits, hg_norm_g, ml_conv_w, ml_conv_b, ml_gate_b, ml_norm_g,
           mb_conv_w, mb_conv_b, mb_dt_bias, mb_a_log, mb_d, mb_norm_g, w_br_hg, w_br_ml, w_br_mb,
           w_out, norm2_g, w_up, ffn_conv_w, ffn_conv_b, w_down, final_g):
    batch, seq, _ = x.shape
    n = batch * seq
    assert seq % TS_MIX == 0 and seq % TM_FFN == 0 and seq % TM_PROJ == 0 and n % TM_MERGE == 0
    depth = w_in.shape[0]

    o_if = 4 * 1024 + 2 * ML_HEADS * ML_DQK + ML_HEADS * ML_DV
    o_mlo = o_if + 2 * ML_HEADS
    o_dt = o_mlo + ML_HEADS * ML_DV + MB_W + MB_CONV_DIM
    o_gate = o_dt + MB_HEADS
    w_big = jnp.concatenate(
        [w_in[:, :, :o_if], w_in[:, :, o_mlo:o_dt], w_in[:, :, o_gate:]], axis=-1).astype(BF16)
    pad = SMALL_W - 2 * ML_HEADS - MB_HEADS
    w_small = jnp.concatenate(
        [w_in[:, :, o_if:o_mlo], w_in[:, :, o_dt:o_gate],
         jnp.zeros((depth, D_MODEL, pad), w_in.dtype)], axis=-1).astype(BF16)

    lbs = _lbs(hg_lb_logits.astype(F32))
    hg_ng = jnp.tile(hg_norm_g, (1, HG_HEADS))
    ml_ng = jnp.tile(ml_norm_g, (1, ML_HEADS))
    gb_row = jnp.pad(ml_gate_b, ((0, 0), (0, SMALL_W - 2 * ML_HEADS)))
    dtb_c = jnp.pad(mb_dt_bias, ((0, 0), (DT_COL, SMALL_W - DT_COL - MB_HEADS)))
    alog_c = jnp.pad(mb_a_log, ((0, 0), (DT_COL, SMALL_W - DT_COL - MB_HEADS)))
    n_pair = MB_HEADS // 2
    dtb_r = jnp.repeat(mb_dt_bias, CHUNK, axis=-1).reshape(depth, n_pair, 2 * CHUNK)
    alog_r = jnp.repeat(mb_a_log, CHUNK, axis=-1).reshape(depth, n_pair, 2 * CHUNK)
    d_x = jnp.repeat(mb_d, MB_P, axis=-1)

    n_tiles = w_big.shape[-1] // TN_PROJ
    conv_w = jnp.concatenate([ml_conv_w, mb_conv_w], axis=-1).reshape(depth, CONV_W, len(TILES_CONV), TN_PROJ)
    conv_b = jnp.concatenate([ml_conv_b, mb_conv_b], axis=-1).reshape(depth, 1, len(TILES_CONV), TN_PROJ)
    cw_all = jnp.zeros((depth, n_tiles, CONV_W, TN_PROJ), F32).at[:, jnp.array(TILES_CONV)].set(
        jnp.swapaxes(conv_w, 1, 2))
    cb_all = jnp.zeros((depth, n_tiles, 1, TN_PROJ), F32).at[:, jnp.array(TILES_CONV)].set(
        jnp.swapaxes(conv_b, 1, 2))

    w_hg = w_br_hg.astype(BF16)
    w_ml = w_br_ml.astype(BF16)
    w_mb = w_br_mb.astype(BF16)
    w_o = w_out.astype(BF16)
    w_u = w_up.astype(BF16)
    w_d = w_down.astype(BF16)

    x2 = x.reshape(n, D_MODEL)
    row = lambda a: a.reshape(1, -1)
    for l in range(depth):
        proj, small = _inproj(x2, row(norm1_g[l]), w_big[l], w_small[l], cw_all[l], cb_all[l], seq)
        chunks = small.reshape(n // CHUNK, CHUNK, SMALL_W)
        small_t = jnp.swapaxes(chunks[:, :, :2 * ML_HEADS], 1, 2)
        dt_t = jnp.swapaxes(chunks[:, :, DT_COL:DT_COL + MB_HEADS], 1, 2).reshape(
            n // CHUNK, n_pair, 2 * CHUNK)

        y_hg = _hgrn2(proj, row(lbs[l]), row(hg_ng[l]), batch, seq)
        y_ml = _mlstm(proj, small, small_t, row(gb_row[l]),
                      ml_gate_b[l].reshape(2 * ML_HEADS, 1), row(ml_ng[l]), batch, seq)
        y_mb = _ssd(proj, small, dt_t, row(dtb_c[l]), row(alog_c[l]),
                    dtb_r[l], alog_r[l], row(d_x[l]), row(mb_norm_g[l]), batch, seq)
        x2 = _merge(x2, y_hg, y_ml, y_mb, proj, w_hg[l], w_ml[l], w_mb[l], w_o[l])
        x2 = _ffn(x2, row(norm2_g[l]), w_u[l], ffn_conv_w[l], row(ffn_conv_b[l]), w_d[l], row(final_g),
                  batch, seq, final=(l == depth - 1))
    return x2.reshape(batch, seq, D_MODEL)
```

```python
import functools
import math

import jax
import jax.numpy as jnp
from jax import lax
from jax.experimental import pallas as pl
from jax.experimental.pallas import tpu as pltpu

F32 = jnp.float32
BF16 = jnp.bfloat16

D_MODEL = 1024
DEPTH = 4
CHUNK = 64
CHUNK_LOG2 = 6
SUB = 8
SUBLANES = 8
LANES = 128
LOG2E = math.log2(math.e)
EPS = 1e-6
NEG_BIG = -1e30
HG_HEADS = 8
HG_D = 128
ML_HEADS = 4
ML_DQK = 128
ML_DV = 256
ML_CONV = 4
MB_HEADS = 16
MB_P = 64
MB_P_LOG2 = 6
MB_GROUPS = 4
MB_N = 128
MB_CONV = 4
MB_W = MB_HEADS * MB_P
MB_CONV_DIM = MB_W + 2 * MB_GROUPS * MB_N
D_FF = 2816
FFN_CONV = 3
FF_CHUNK = 1408
SMALL_W = 128
DT_COL = 8
CONV_PAD = 8

VMEM_LIMIT = 56 * 1024 * 1024

TM_PROJ = 2048
TN_PROJ = 1024
EPI_ROWS = 256
TILES_SILU = (0, 3, 7)
TILES_SIGMOID = (6, 10, 11, 12)
TILES_RAW = (1, 2, 5)
TILES_CONV = (4, 8, 9)
CONV_W = 4
TS_MIX = 512
CHUNK_UNROLL = 2
TM_MERGE = 512
TM_FFN = 512


def _silu(x):
    return x * jax.nn.sigmoid(x)


def _softplus(x):
    return jnp.maximum(x, 0.0) + jnp.log1p(jnp.exp(-jnp.abs(x)))


def _log_sigmoid(x):
    return jnp.minimum(x, 0.0) - jnp.log1p(jnp.exp(-jnp.abs(x)))


def _dot(a, b):
    return jnp.dot(a, b, preferred_element_type=F32)


def _dot_nt(a, b):
    return lax.dot_general(a, b, (((1,), (1,)), ((), ())), preferred_element_type=F32)


def _dot_tn(a, b):
    return lax.dot_general(a, b, (((0,), (0,)), ((), ())), preferred_element_type=F32)


def _split3(a):
    hi = a.astype(BF16)
    r = a - hi.astype(F32)
    mid = r.astype(BF16)
    lo = (r - mid.astype(F32)).astype(BF16)
    return hi, mid, lo


def _sel_left(sel, a):
    hi, mid, lo = _split3(a)
    return _dot(sel, hi) + (_dot(sel, mid) + _dot(sel, lo))


def _sel_right(a, sel):
    hi, mid, lo = _split3(a)
    return _dot(hi, sel) + (_dot(mid, sel) + _dot(lo, sel))


def _tri_lower(n):
    r = lax.broadcasted_iota(jnp.int32, (n, n), 0)
    c = lax.broadcasted_iota(jnp.int32, (n, n), 1)
    return r >= c


def _bcast_row(ref, h, r):
    return ref[h, pl.ds(r, SUBLANES, stride=0), :]


def _mask_bf16(m):
    return jnp.where(m, 1.0, 0.0).astype(BF16)


def _cparams(sem):
    return pltpu.CompilerParams(dimension_semantics=sem, vmem_limit_bytes=VMEM_LIMIT)


def _lbs_kernel(lg_ref, o_ref):
    lg = lg_ref[...]
    mx = jnp.max(lg, axis=0, keepdims=True)
    e = jnp.exp(lg - mx)
    p = e / jnp.sum(e, axis=0, keepdims=True)
    acc = jnp.zeros_like(p[0:1])
    rows = []
    for l in range(lg.shape[0]):
        acc = acc + p[l:l + 1]
        rows.append(acc - p[0:1])
    o_ref[...] = jnp.concatenate(rows, axis=0)


def _lbs(logits):
    return pl.pallas_call(
        _lbs_kernel,
        out_shape=jax.ShapeDtypeStruct(logits.shape, F32),
        name="hgrn2_lower_bounds",
    )(logits)


def _any_tile(j, tiles):
    hit = j == tiles[0]
    for t in tiles[1:]:
        hit = jnp.logical_or(hit, j == t)
    return hit


def _inproj_kernel(x_ref, g_ref, w_ref, ws_ref, cw_ref, cb_ref, o_ref, os_ref, h_ref, tail_ref, hbuf,
                   *, tiles_per_seq):
    i = pl.program_id(0)
    j = pl.program_id(1)
    n_blk = TM_PROJ // EPI_ROWS

    @pl.when(j == 0)
    def _():
        x = x_ref[...]
        ms = jnp.mean(x * x, axis=-1, keepdims=True)
        hb = (x * lax.rsqrt(ms + EPS) * g_ref[...]).astype(BF16)
        h_ref[...] = hb
        os_ref[...] = _dot(hb, ws_ref[...])

    def pointwise(fn):
        ys = [_dot(h_ref[r * EPI_ROWS:(r + 1) * EPI_ROWS, :], w_ref[...]) for r in range(n_blk)]
        for r in range(n_blk):
            o_ref[r * EPI_ROWS:(r + 1) * EPI_ROWS, :] = fn(ys[r]).astype(o_ref.dtype)

    @pl.when(_any_tile(j, TILES_SILU))
    def _():
        pointwise(_silu)

    @pl.when(_any_tile(j, TILES_SIGMOID))
    def _():
        pointwise(jax.nn.sigmoid)

    @pl.when(_any_tile(j, TILES_RAW))
    def _():
        pointwise(lambda y: y)

    @pl.when(_any_tile(j, TILES_CONV))
    def _():
        slot = jnp.where(j == TILES_CONV[0], 0, j - TILES_CONV[1] + 1)
        prev = jnp.where(i % tiles_per_seq == 0, 0.0, tail_ref[slot])
        cw = cw_ref[0]
        cb = cb_ref[0]
        ys = [_dot(h_ref[r * EPI_ROWS:(r + 1) * EPI_ROWS, :], w_ref[...]) for r in range(n_blk)]
        for r in range(n_blk):
            yb = ys[r]
            out = cb + cw[CONV_W - 1:CONV_W] * yb
            for k in range(CONV_W - 1):
                out = out + cw[k:k + 1] * pltpu.roll(yb, CONV_W - 1 - k, axis=0)
            hbuf[0:CONV_PAD, :] = prev
            hbuf[CONV_PAD:2 * CONV_PAD, :] = yb[:CONV_PAD]
            head = cb
            for k in range(CONV_W):
                off = CONV_PAD - (CONV_W - 1) + k
                head = head + cw[k:k + 1] * hbuf[off:off + CONV_PAD, :]
            prev = yb[EPI_ROWS - CONV_PAD:]
            o_ref[r * EPI_ROWS:(r + 1) * EPI_ROWS, :] = _silu(
                jnp.concatenate([head, out[CONV_PAD:]], axis=0)).astype(o_ref.dtype)
        tail_ref[slot] = prev


def _inproj(x2, g, w_big, w_small, cw_all, cb_all, seq):
    n = x2.shape[0]
    nb = w_big.shape[1]
    n_tiles = nb // TN_PROJ
    assert n_tiles == len(TILES_SILU + TILES_SIGMOID + TILES_RAW + TILES_CONV)
    return pl.pallas_call(
        functools.partial(_inproj_kernel, tiles_per_seq=seq // TM_PROJ),
        grid=(n // TM_PROJ, n_tiles),
        in_specs=[
            pl.BlockSpec((TM_PROJ, D_MODEL), lambda i, j: (i, 0)),
            pl.BlockSpec((1, D_MODEL), lambda i, j: (0, 0)),
            pl.BlockSpec((D_MODEL, TN_PROJ), lambda i, j: (0, j)),
            pl.BlockSpec((D_MODEL, SMALL_W), lambda i, j: (0, 0)),
            pl.BlockSpec((1, CONV_W, TN_PROJ), lambda i, j: (j, 0, 0)),
            pl.BlockSpec((1, 1, TN_PROJ), lambda i, j: (j, 0, 0)),
        ],
        out_specs=[
            pl.BlockSpec((TM_PROJ, TN_PROJ), lambda i, j: (i, j)),
            pl.BlockSpec((TM_PROJ, SMALL_W), lambda i, j: (i, 0)),
        ],
        out_shape=[
            jax.ShapeDtypeStruct((n, nb), BF16),
            jax.ShapeDtypeStruct((n, SMALL_W), F32),
        ],
        scratch_shapes=[
            pltpu.VMEM((TM_PROJ, D_MODEL), BF16),
            pltpu.VMEM((len(TILES_CONV), CONV_PAD, TN_PROJ), F32),
            pltpu.VMEM((2 * CONV_PAD, TN_PROJ), F32),
        ],
        compiler_params=_cparams(("arbitrary", "arbitrary")),
        name="in_projection",
    )(x2, g, w_big, w_small, cw_all, cb_all)


def _hgrn2_stages(q_ref, f_ref, i_ref, g_ref, lb_ref, ng_ref, y_ref, state_ref, c_s):
    @pl.when(pl.program_id(1) == 0)
    def _():
        state_ref[...] = jnp.zeros_like(state_ref)

    lb = lb_ref[...]
    tri = _mask_bf16(_tri_lower(CHUNK))
    n_sub = CHUNK // SUB
    ones_rhs = jnp.ones((HG_D, HG_D), BF16)
    lane = lax.broadcasted_iota(jnp.int32, (SUBLANES, HG_D), 1)
    row = lax.broadcasted_iota(jnp.int32, (SUBLANES, HG_D), 0)

    heads = range(HG_HEADS)
    hsl = [slice(h * HG_D, (h + 1) * HG_D) for h in heads]
    zero_tail = jnp.zeros((HG_D - CHUNK, HG_D), BF16)

    def chunk(c):
        rows = pl.ds(pl.multiple_of(c * CHUNK, CHUNK), CHUNK)
        sig = jax.nn.sigmoid(f_ref[rows, :].astype(F32))
        fgate = lb + (1.0 - lb) * sig
        b = _sel_left(tri, jnp.log(fgate) * LOG2E)
        cc = b - jnp.log(jnp.maximum(1.0 - fgate, 0.0)) * LOG2E
        for h in heads:
            c_s[h] = cc[:, hsl[h]]
        q = q_ref[rows, :].astype(F32)
        v16 = i_ref[rows, :]
        yield

        st = [state_ref[h] for h in heads]
        q_dec = (q * jnp.exp2(b)).astype(BF16)
        b_last = b[CHUNK - 1:CHUNK, :]
        k_dec = jnp.exp2(b_last - cc).astype(BF16)
        st_decay = jnp.exp2(b_last)
        o_inter = [_dot_nt(q_dec[:, hsl[h]], st[h].astype(BF16)) for h in heads]
        for h in heads:
            state_ref[h] = st_decay[:, hsl[h]] * st[h] + _dot_tn(v16[:, hsl[h]], k_dec[:, hsl[h]])
        yield

        zs = []
        for i in range(n_sub):
            lo = i * SUB
            for s in range(SUB):
                r0 = lo + (s // SUBLANES) * SUBLANES
                c_row = jnp.concatenate([_bcast_row(c_s, h, lo + s) for h in heads], axis=-1)
                c_row = jnp.concatenate([c_row] * ((lo + SUB - r0) // SUBLANES), axis=0)
                zs.append(q[r0:lo + SUB] * jnp.exp2(b[r0:lo + SUB] - c_row))
            yield
        z_rows = sum(z.shape[0] for z in zs)
        z_all = jnp.concatenate([z[:, hsl[h]] for h in heads for z in zs], axis=0)
        r = _dot(z_all.astype(BF16), ones_rhs)
        yield

        a_off = []
        for i in range(1, n_sub):
            lo = i * SUB
            bref = b[lo - 1:lo, :]
            q_i = (q[lo:lo + SUB] * jnp.exp2(b[lo:lo + SUB] - bref)).astype(BF16)
            k_i = jnp.exp2(bref - cc[:lo]).astype(BF16)
            zero_rows = jnp.zeros((HG_D - lo, HG_D), BF16)
            a_off.append([_dot_nt(q_i[:, hsl[h]], jnp.concatenate([k_i[:, hsl[h]], zero_rows], axis=0))
                          for h in heads])
        yield

        outs = []
        for h in heads:
            a_rows = []
            off = h * z_rows
            for i in range(n_sub):
                lo = i * SUB
                a_i = a_off[i - 1][h] if i > 0 else jnp.zeros((SUB, HG_D), F32)
                tiles = [a_i[j * SUBLANES:(j + 1) * SUBLANES] for j in range(SUB // SUBLANES)]
                for s in range(SUB):
                    for j in range(s // SUBLANES, SUB // SUBLANES):
                        tiles[j] = jnp.where(lane == lo + s, r[off:off + SUBLANES], tiles[j])
                        off += SUBLANES
                for j in range(SUB // SUBLANES):
                    a_rows.append(jnp.where(lane - lo <= row + j * SUBLANES, tiles[j], 0.0))
            a_full = jnp.concatenate(a_rows, axis=0).astype(BF16)
            v_pad = jnp.concatenate([v16[:, hsl[h]], zero_tail], axis=0)
            outs.append(o_inter[h] + _dot(a_full, v_pad))
            if h % 2 == 1:
                yield
        parts = []
        for o in outs:
            parts.append(o * lax.rsqrt(jnp.mean(o * o, axis=-1, keepdims=True) + EPS))
        on = jnp.concatenate(parts, axis=-1) * ng_ref[...]
        y_ref[rows, :] = (on * g_ref[rows, :].astype(F32)).astype(y_ref.dtype)

    return chunk


def _mlstm_stages(qk_ref, v_ref, og_ref, sm_ref, smt_ref, gbr_ref, gbc_ref, ng_ref,
                  y_ref, caug_ref, m_ref):
    half = ML_HEADS * ML_DQK
    k_scale = ML_DQK ** -0.5

    @pl.when(pl.program_id(1) == 0)
    def _():
        caug_ref[...] = jnp.zeros_like(caug_ref)
        m_ref[...] = jnp.zeros_like(m_ref)

    tri_b = _tri_lower(CHUNK)
    tri_l = _mask_bf16(tri_b)
    tri_u = _mask_bf16(lax.broadcasted_iota(jnp.int32, (CHUNK, CHUNK), 0)
                       <= lax.broadcasted_iota(jnp.int32, (CHUNK, CHUNK), 1))
    ones_col = _mask_bf16(lax.broadcasted_iota(jnp.int32, (CHUNK, LANES), 1) == 0)

    def chunk(c):
        rows = pl.ds(pl.multiple_of(c * CHUNK, CHUNK), CHUNK)
        pre_c = sm_ref[rows, :] + gbr_ref[...]
        pre_r = smt_ref[c] + gbc_ref[...]
        cum_c = _sel_left(tri_l, _log_sigmoid(pre_c))
        cum_r = _sel_right(_log_sigmoid(pre_r), tri_u)
        yield
        heads = range(ML_HEADS)
        m_all = m_ref[...]
        m_old = [m_all[h:h + 1, 0:1] for h in heads]
        b_col = [cum_c[:, ML_HEADS + h:ML_HEADS + h + 1] for h in heads]
        i_col = [pre_c[:, h:h + 1] for h in heads]
        log_d = [jnp.where(tri_b, b_col[h] - cum_r[ML_HEADS + h:ML_HEADS + h + 1, :] + pre_r[h:h + 1, :],
                           NEG_BIG) for h in heads]
        log_inter = [b_col[h] + m_old[h] for h in heads]
        m_t = [jnp.maximum(jnp.max(log_d[h], axis=-1, keepdims=True), log_inter[h]) for h in heads]
        b_last = [b_col[h][CHUNK - 1:CHUNK, :] for h in heads]
        log_w = [b_last[h] - b_col[h] + i_col[h] for h in heads]
        m_new = [jnp.maximum(b_last[h] + m_old[h], jnp.max(log_w[h], axis=0, keepdims=True)) for h in heads]
        m_ref[...] = jnp.concatenate(
            [jnp.broadcast_to(m_new[h], (1, m_ref.shape[1])) for h in heads] + [m_all[ML_HEADS:]], axis=0)
        yield

        qb =[qk_ref[rows, h * ML_DQK:(h + 1) * ML_DQK] for h in heads]
        kb = [qk_ref[rows, half + h * ML_DQK:half + (h + 1) * ML_DQK] for h in heads]
        v_aug = [jnp.concatenate([v_ref[rows, h * ML_DV:(h + 1) * ML_DV], ones_col], axis=-1)
                 for h in heads]
        qk = [_dot_nt(qb[h], kb[h]) for h in heads]
        c_aug = [caug_ref[h] for h in heads]
        inter = [_dot(qb[h], c_aug[h].astype(BF16)) * jnp.exp(log_inter[h] - m_t[h]) for h in heads]
        yield
        s = [(qk[h] * (jnp.exp(log_d[h] - m_t[h]) * k_scale)).astype(BF16) for h in heads]
        num = [_dot(s[h], v_aug[h]) + inter[h] for h in heads]
        yield
        kw = [(kb[h].astype(F32) * (jnp.exp(log_w[h] - m_new[h]) * k_scale)).astype(BF16) for h in heads]
        for h in heads:
            caug_ref[h] = jnp.exp(b_last[h] + m_old[h] - m_new[h]) * c_aug[h] + _dot_tn(kw[h], v_aug[h])
        yield

        parts = []
        for h in heads:
            denom = jnp.maximum(jnp.abs(num[h][:, ML_DV:ML_DV + 1]), jnp.exp(-m_t[h]))
            o = num[h][:, :ML_DV] / denom
            parts.append(o * lax.rsqrt(jnp.mean(o * o, axis=-1, keepdims=True) + EPS))
        on = jnp.concatenate(parts, axis=-1) * ng_ref[...]
        y_ref[rows, :] = (on * og_ref[rows, :].astype(F32)).astype(y_ref.dtype)

    return chunk


def _ssd_stages(z_ref, xbc_ref, sm_ref, dtt_ref, dtb_c_ref, alog_c_ref,
                dtb_r_ref, alog_r_ref, d_ref, ng_ref, y_ref, state_ref):
    n_pair = MB_HEADS // 2
    pair_w = 2 * MB_P
    gw = MB_GROUPS * MB_N

    @pl.when(pl.program_id(1) == 0)
    def _():
        state_ref[...] = jnp.zeros_like(state_ref)

    tri_l = _mask_bf16(_tri_lower(CHUNK))
    sel_x = _mask_bf16(lax.broadcasted_iota(jnp.int32, (SMALL_W, MB_W), 0) - DT_COL
                       == jnp.right_shift(lax.broadcasted_iota(jnp.int32, (SMALL_W, MB_W), 1), MB_P_LOG2))
    ur = lax.broadcasted_iota(jnp.int32, (pair_w, pair_w), 0)
    uc = lax.broadcasted_iota(jnp.int32, (pair_w, pair_w), 1)
    same_half = jnp.right_shift(ur, CHUNK_LOG2) == jnp.right_shift(uc, CHUNK_LOG2)
    tri_u2 = _mask_bf16(same_half & (ur <= uc))
    causal = (lax.broadcasted_iota(jnp.int32, (CHUNK, MB_W), 0)
              >= jnp.bitwise_and(lax.broadcasted_iota(jnp.int32, (CHUNK, MB_W), 1), CHUNK - 1))
    first_head = lax.broadcasted_iota(jnp.int32, (CHUNK, pair_w), 1) < MB_P
    a_c = -jnp.exp(alog_c_ref[...])
    a_r = -jnp.exp(alog_r_ref[...])

    def chunk(c):
        rows = pl.ds(pl.multiple_of(c * CHUNK, CHUNK), CHUNK)
        dt_c = _softplus(sm_ref[rows, :] + dtb_c_ref[...])
        cum_c = _sel_left(tri_l, dt_c * a_c)
        dt_x = _sel_right(dt_c, sel_x)
        cum_x = _sel_right(cum_c, sel_x)
        dt_r = _softplus(dtt_ref[c] + dtb_r_ref[...])
        cum_r = _sel_right(dt_r * a_r, tri_u2)
        yield
        pairs = range(n_pair)
        grp = [(2 * p) // (MB_HEADS // MB_GROUPS) for p in pairs]
        lanes = [slice(p * pair_w, (p + 1) * pair_w) for p in pairs]
        xs = xbc_ref[rows, :MB_W].astype(F32)
        xdt = xs * dt_x
        cum_row = jnp.concatenate([cum_r[p:p + 1, :] for p in pairs], axis=-1)
        decay = jnp.exp(jnp.where(causal, cum_x - cum_row, NEG_BIG))
        cum_last = cum_x[CHUNK - 1:CHUNK, :]
        x_dec = (xdt * jnp.exp(cum_last - cum_x)).astype(BF16)
        st_decay = jnp.exp(cum_last)
        carry_w = jnp.exp(cum_x)
        yield
        bm =[xbc_ref[rows, MB_W + g * MB_N:MB_W + (g + 1) * MB_N] for g in range(MB_GROUPS)]
        cm = [xbc_ref[rows, MB_W + gw + g * MB_N:MB_W + gw + (g + 1) * MB_N] for g in range(MB_GROUPS)]
        cb2 = [_dot_nt(cm[g], jnp.concatenate([bm[g], bm[g]], axis=0)) for g in range(MB_GROUPS)]
        x2 = [jnp.concatenate([jnp.where(first_head, xdt[:, lanes[p]], 0.0),
                               jnp.where(first_head, 0.0, xdt[:, lanes[p]])], axis=0).astype(BF16)
              for p in pairs]
        yield
        st = [state_ref[p] for p in pairs]
        y_inter = [_dot(cm[grp[p]], st[p].astype(BF16)) for p in pairs]
        y_intra = [_dot((cb2[grp[p]] * decay[:, lanes[p]]).astype(BF16), x2[p]) for p in pairs]
        yield
        for p in pairs:
            state_ref[p] = st_decay[:, lanes[p]] * st[p] + _dot_tn(bm[grp[p]], x_dec[:, lanes[p]])
        yield
        y = (jnp.concatenate(y_intra, axis=-1) + jnp.concatenate(y_inter, axis=-1) * carry_w
             + d_ref[...] * xs)

        yz = y * z_ref[rows, :].astype(F32)
        gsz = MB_W // MB_GROUPS
        parts = []
        for g in range(MB_GROUPS):
            o = yz[:, g * gsz:(g + 1) * gsz]
            parts.append(o * lax.rsqrt(jnp.mean(o * o, axis=-1, keepdims=True) + EPS))
        y_ref[rows, :] = (jnp.concatenate(parts, axis=-1) * ng_ref[...]).astype(y_ref.dtype)

    return chunk


N_HG_IN, N_ML_IN, N_MB_IN = 6, 8, 10
_DONE = object()


def _mixers_kernel(*refs):
    hg_in, refs = refs[:N_HG_IN], refs[N_HG_IN:]
    ml_in, refs = refs[:N_ML_IN], refs[N_ML_IN:]
    mb_in, refs = refs[:N_MB_IN], refs[N_MB_IN:]
    y_hg, y_ml, y_mb, hg_state, hg_c, ml_caug, ml_m, mb_state = refs
    ts = y_hg.shape[0]
    mixers = [
        _hgrn2_stages(*hg_in, y_hg, hg_state, hg_c),
        _mlstm_stages(*ml_in, y_ml, ml_caug, ml_m),
        _ssd_stages(*mb_in, y_mb, mb_state),
    ]

    def chunk_body(c, carry):
        live = [m(c) for m in mixers]
        while live:
            for g in list(live):
                if next(g, _DONE) is _DONE:
                    live.remove(g)
        return carry

    lax.fori_loop(0, ts // CHUNK, chunk_body, 0, unroll=CHUNK_UNROLL)


def _mixers(proj, small, small_t, dt_t, lb, hg_ng, gb_row, gb_col, ml_ng,
            dtb_c, alog_c, dtb_r, alog_r, d_x, mb_ng, batch, seq):
    n = proj.shape[0]
    nt = seq // TS_MIX
    n_pair = MB_HEADS // 2
    w = D_MODEL

    def col(cb, width=w):
        return pl.BlockSpec((TS_MIX, width), lambda b, j: (b * nt + j, cb))

    def const(shape):
        return pl.BlockSpec(shape, lambda b, j: tuple(0 for _ in shape))

    sm_spec = pl.BlockSpec((TS_MIX, SMALL_W), lambda b, j: (b * nt + j, 0))
    hg_specs = [col(0), col(1), col(2), col(3), const((1, w)), const((1, w))]
    ml_specs = [col(4), col(5), col(6), sm_spec,
                pl.BlockSpec((TS_MIX // CHUNK, 2 * ML_HEADS, CHUNK), lambda b, j: (b * nt + j, 0, 0)),
                const((1, SMALL_W)), const((2 * ML_HEADS, 1)), const((1, w))]
    mb_specs = [col(7), col(4, MB_CONV_DIM), sm_spec,
                pl.BlockSpec((TS_MIX // CHUNK, n_pair, 2 * CHUNK), lambda b, j: (b * nt + j, 0, 0)),
                const((1, SMALL_W)), const((1, SMALL_W)),
                const((n_pair, 2 * CHUNK)), const((n_pair, 2 * CHUNK)),
                const((1, w)), const((1, w))]
    assert (len(hg_specs), len(ml_specs), len(mb_specs)) == (N_HG_IN, N_ML_IN, N_MB_IN)
    out = pl.BlockSpec((TS_MIX, w), lambda b, j: (b * nt + j, 0))
    return pl.pallas_call(
        _mixers_kernel,
        grid=(batch, nt),
        in_specs=hg_specs + ml_specs + mb_specs,
        out_specs=[out, out, out],
        out_shape=[jax.ShapeDtypeStruct((n, w), BF16)] * 3,
        scratch_shapes=[
            pltpu.VMEM((HG_HEADS, HG_D, HG_D), F32),
            pltpu.VMEM((HG_HEADS, CHUNK, HG_D), F32),
            pltpu.VMEM((ML_HEADS, ML_DQK, ML_DV + LANES), F32),
            pltpu.VMEM((SUBLANES, LANES), F32),
            pltpu.VMEM((n_pair, MB_N, 2 * MB_P), F32),
        ],
        compiler_params=_cparams(("parallel", "arbitrary")),
        name="token_mixers",
    )(proj, proj, proj, proj, lb, hg_ng,
      proj, proj, proj, small, small_t, gb_row, gb_col, ml_ng,
      proj, proj, small, dt_t, dtb_c, alog_c, dtb_r, alog_r, d_x, mb_ng)


def _merge_kernel(x_ref, yh_ref, ym_ref, yb_ref, g0_ref, g1_ref, g2_ref,
                  wh_ref, wm_ref, wb_ref, wo_ref, o_ref):
    mixed = g0_ref[...].astype(F32) * _dot(yh_ref[...], wh_ref[...])
    mixed = mixed + g1_ref[...].astype(F32) * _dot(ym_ref[...], wm_ref[...])
    mixed = mixed + g2_ref[...].astype(F32) * _dot(yb_ref[...], wb_ref[...])
    o_ref[...] = x_ref[...] + _dot(mixed.astype(BF16), wo_ref[...])


def _merge(x2, y_hg, y_ml, y_mb, proj, w_hg, w_ml, w_mb, w_out):
    n = x2.shape[0]
    tile = lambda cb: pl.BlockSpec((TM_MERGE, D_MODEL), lambda i: (i, cb))
    wspec = pl.BlockSpec((D_MODEL, D_MODEL), lambda i: (0, 0))
    return pl.pallas_call(
        _merge_kernel,
        grid=(n // TM_MERGE,),
        in_specs=[tile(0), tile(0), tile(0), tile(0), tile(10), tile(11), tile(12),
                  wspec, wspec, wspec, wspec],
        out_specs=tile(0),
        out_shape=jax.ShapeDtypeStruct((n, D_MODEL), F32),
        compiler_params=_cparams(("parallel",)),
        name="branch_merge",
    )(x2, y_hg, y_ml, y_mb, proj, proj, proj, w_hg, w_ml, w_mb, w_out)


def _ffn_kernel(x_ref, g_ref, wu_ref, cw_ref, cb_ref, wd_ref, fg_ref,
                o_ref, tail, hbuf_g, hbuf_v, *, final):
    tm = x_ref.shape[0]
    n_ck = D_FF // FF_CHUNK

    @pl.when(pl.program_id(1) == 0)
    def _():
        tail[...] = jnp.zeros_like(tail)

    x = x_ref[...]
    hb = (x * lax.rsqrt(jnp.mean(x * x, axis=-1, keepdims=True) + EPS) * g_ref[...]).astype(BF16)

    def cols(c, half):
        return slice(half * D_FF + c * FF_CHUNK, half * D_FF + (c + 1) * FF_CHUNK)

    def up(c):
        return _dot(hb, wu_ref[:, cols(c, 0)]), _dot(hb, wu_ref[:, cols(c, 1)])

    def conv(u, hbuf, cs):
        out = cb_ref[:, cs] + cw_ref[FFN_CONV - 1:FFN_CONV, cs] * u
        for k in range(FFN_CONV - 1):
            out = out + cw_ref[k:k + 1, cs] * pltpu.roll(u, FFN_CONV - 1 - k, axis=0)
        hbuf[0:CONV_PAD, :] = tail[:, cs]
        hbuf[CONV_PAD:2 * CONV_PAD, :] = u[:CONV_PAD]
        tail[:, cs] = u[tm - CONV_PAD:]
        head = cb_ref[:, cs]
        for k in range(FFN_CONV):
            off = CONV_PAD - (FFN_CONV - 1) + k
            head = head + cw_ref[k:k + 1, cs] * hbuf[off:off + CONV_PAD, :]
        return jnp.concatenate([head, out[CONV_PAD:]], axis=0)

    acc = x
    u_next = up(0)
    for c in range(n_ck):
        u_g, u_v = u_next
        if c + 1 < n_ck:
            u_next = up(c + 1)
        a_g = conv(u_g, hbuf_g, cols(c, 0))
        a_v = conv(u_v, hbuf_v, cols(c, 1))
        acc = acc + _dot((_silu(a_g) * a_v).astype(BF16), wd_ref[cols(c, 0), :])
    if final:
        acc = acc * lax.rsqrt(jnp.mean(acc * acc, axis=-1, keepdims=True) + EPS) * fg_ref[...]
    o_ref[...] = acc


def _ffn(x2, g, w_up, cw, cb, w_down, final_g, batch, seq, final):
    n = x2.shape[0]
    nt = seq // TM_FFN

    def const(shape):
        return pl.BlockSpec(shape, lambda b, j: tuple(0 for _ in shape),
                            pipeline_mode=pl.Buffered(1))

    tile = pl.BlockSpec((TM_FFN, D_MODEL), lambda b, j: (b * nt + j, 0))
    return pl.pallas_call(
        functools.partial(_ffn_kernel, final=final),
        grid=(batch, nt),
        in_specs=[
            tile, const((1, D_MODEL)),
            const((D_MODEL, 2 * D_FF)), const((FFN_CONV, 2 * D_FF)), const((1, 2 * D_FF)),
            const((D_FF, D_MODEL)), const((1, D_MODEL)),
        ],
        out_specs=tile,
        out_shape=jax.ShapeDtypeStruct((n, D_MODEL), F32),
        scratch_shapes=[
            pltpu.VMEM((CONV_PAD, 2 * D_FF), F32),
            pltpu.VMEM((2 * CONV_PAD, FF_CHUNK), F32),
            pltpu.VMEM((2 * CONV_PAD, FF_CHUNK), F32),
        ],
        compiler_params=_cparams(("parallel", "arbitrary")),
        name="conv_gated_mlp",
    )(x2, g, w_up, cw, cb, w_down, final_g)


def kernel(x, norm1_g, w_in, hg_lb_logits, hg_norm_g, ml_conv_w, ml_conv_b, ml_gate_b, ml_norm_g,
           mb_conv_w, mb_conv_b, mb_dt_bias, mb_a_log, mb_d, mb_norm_g, w_br_hg, w_br_ml, w_br_mb,
           w_out, norm2_g, w_up, ffn_conv_w, ffn_conv_b, w_down, final_g):
    batch, seq, _ = x.shape
    n = batch * seq
    assert seq % TS_MIX == 0 and seq % TM_FFN == 0 and seq % TM_PROJ == 0 and n % TM_MERGE == 0
    depth = w_in.shape[0]

    o_if = 4 * 1024 + 2 * ML_HEADS * ML_DQK + ML_HEADS * ML_DV
    o_mlo = o_if + 2 * ML_HEADS
    o_dt = o_mlo + ML_HEADS * ML_DV + MB_W + MB_CONV_DIM
    o_gate = o_dt + MB_HEADS
    w_big = jnp.concatenate(
        [w_in[:, :, :o_if], w_in[:, :, o_mlo:o_dt], w_in[:, :, o_gate:]], axis=-1).astype(BF16)
    pad = SMALL_W - 2 * ML_HEADS - MB_HEADS
    w_small = jnp.concatenate(
        [w_in[:, :, o_if:o_mlo], w_in[:, :, o_dt:o_gate],
         jnp.zeros((depth, D_MODEL, pad), w_in.dtype)], axis=-1).astype(BF16)

    lbs = _lbs(hg_lb_logits.astype(F32))
    hg_ng = jnp.tile(hg_norm_g, (1, HG_HEADS))
    ml_ng = jnp.tile(ml_norm_g, (1, ML_HEADS))
    gb_row = jnp.pad(ml_gate_b, ((0, 0), (0, SMALL_W - 2 * ML_HEADS)))
    dtb_c = jnp.pad(mb_dt_bias, ((0, 0), (DT_COL, SMALL_W - DT_COL - MB_HEADS)))
    alog_c = jnp.pad(mb_a_log, ((0, 0), (DT_COL, SMALL_W - DT_COL - MB_HEADS)))
    n_pair = MB_HEADS // 2
    dtb_r = jnp.repeat(mb_dt_bias, CHUNK, axis=-1).reshape(depth, n_pair, 2 * CHUNK)
    alog_r = jnp.repeat(mb_a_log, CHUNK, axis=-1).reshape(depth, n_pair, 2 * CHUNK)
    d_x = jnp.repeat(mb_d, MB_P, axis=-1)

    n_tiles = w_big.shape[-1] // TN_PROJ
    conv_w = jnp.concatenate([ml_conv_w, mb_conv_w], axis=-1).reshape(depth, CONV_W, len(TILES_CONV), TN_PROJ)
    conv_b = jnp.concatenate([ml_conv_b, mb_conv_b], axis=-1).reshape(depth, 1, len(TILES_CONV), TN_PROJ)
    cw_all = jnp.zeros((depth, n_tiles, CONV_W, TN_PROJ), F32).at[:, jnp.array(TILES_CONV)].set(
        jnp.swapaxes(conv_w, 1, 2))
    cb_all = jnp.zeros((depth, n_tiles, 1, TN_PROJ), F32).at[:, jnp.array(TILES_CONV)].set(
        jnp.swapaxes(conv_b, 1, 2))

    w_hg = w_br_hg.astype(BF16)
    w_ml = w_br_ml.astype(BF16)
    w_mb = w_br_mb.astype(BF16)
    w_o = w_out.astype(BF16)
    w_u = w_up.astype(BF16)
    w_d = w_down.astype(BF16)

    x2 = x.reshape(n, D_MODEL)
    row = lambda a: a.reshape(1, -1)
    for l in range(depth):
        proj, small = _inproj(x2, row(norm1_g[l]), w_big[l], w_small[l], cw_all[l], cb_all[l], seq)
        chunks = small.reshape(n // CHUNK, CHUNK, SMALL_W)
        small_t = jnp.swapaxes(chunks[:, :, :2 * ML_HEADS], 1, 2)
        dt_t = jnp.swapaxes(chunks[:, :, DT_COL:DT_COL + MB_HEADS], 1, 2).reshape(
            n // CHUNK, n_pair, 2 * CHUNK)

        y_hg, y_ml, y_mb = _mixers(
            proj, small, small_t, dt_t, row(lbs[l]), row(hg_ng[l]),
            row(gb_row[l]), ml_gate_b[l].reshape(2 * ML_HEADS, 1), row(ml_ng[l]),
            row(dtb_c[l]), row(alog_c[l]), dtb_r[l], alog_r[l], row(d_x[l]), row(mb_norm_g[l]), batch, seq)
        x2 = _merge(x2, y_hg, y_ml, y_mb, proj, w_hg[l], w_ml[l], w_mb[l], w_o[l])
        x2 = _ffn(x2, row(norm2_g[l]), w_u[l], ffn_conv_w[l], row(ffn_conv_b[l]), w_d[l], row(final_g),
                  batch, seq, final=(l == depth - 1))
    return x2.reshape(batch, seq, D_MODEL)
```

```python
import functools
import math

import jax
import jax.numpy as jnp
from jax import lax
from jax.experimental import pallas as pl
from jax.experimental.pallas import tpu as pltpu

F32 = jnp.float32
BF16 = jnp.bfloat16

D_MODEL = 1024
DEPTH = 4
CHUNK = 64
CHUNK_LOG2 = 6
SUB = 8
SUBLANES = 8
LANES = 128
LOG2E = math.log2(math.e)
EPS = 1e-6
NEG_BIG = -1e30
HG_HEADS = 8
HG_D = 128
ML_HEADS = 4
ML_DQK = 128
ML_DV = 256
ML_CONV = 4
MB_HEADS = 16
MB_P = 64
MB_P_LOG2 = 6
MB_GROUPS = 4
MB_N = 128
MB_CONV = 4
MB_W = MB_HEADS * MB_P
MB_CONV_DIM = MB_W + 2 * MB_GROUPS * MB_N
D_FF = 2816
FFN_CONV = 3
FF_CHUNK = 1408
SMALL_W = 128
DT_COL = 8
CONV_PAD = 8

VMEM_LIMIT = 56 * 1024 * 1024

TM_PROJ = 2048
TN_PROJ = 1024
EPI_ROWS = 256
TILES_SILU = (0, 1, 2)
TILES_SIGMOID = (3, 4, 5, 6)
TILES_MIXED = (7, 8, 9, 10, 11, 12)
T_HG_Q, T_HG_G, T_MB_Z, T_ML_O, T_GATE0 = 0, 1, 2, 3, 4
N_GATES = 3
HALF_TILE = TN_PROJ // 2
CONV_W = 4
TS_MIX = 512
CHUNK_UNROLL = 2
TM_MERGE = 512
TM_FFN = 512


def _silu(x):
    return x * jax.nn.sigmoid(x)


def _softplus(x):
    return jnp.maximum(x, 0.0) + jnp.log1p(jnp.exp(-jnp.abs(x)))


def _log_sigmoid(x):
    return jnp.minimum(x, 0.0) - jnp.log1p(jnp.exp(-jnp.abs(x)))


def _dot(a, b):
    return jnp.dot(a, b, preferred_element_type=F32)


def _dot_nt(a, b):
    return lax.dot_general(a, b, (((1,), (1,)), ((), ())), preferred_element_type=F32)


def _dot_tn(a, b):
    return lax.dot_general(a, b, (((0,), (0,)), ((), ())), preferred_element_type=F32)


def _split3(a):
    hi = a.astype(BF16)
    r = a - hi.astype(F32)
    mid = r.astype(BF16)
    lo = (r - mid.astype(F32)).astype(BF16)
    return hi, mid, lo


def _sel_left(sel, a):
    hi, mid, lo = _split3(a)
    return _dot(sel, hi) + (_dot(sel, mid) + _dot(sel, lo))


def _sel_right(a, sel):
    hi, mid, lo = _split3(a)
    return _dot(hi, sel) + (_dot(mid, sel) + _dot(lo, sel))


def _tri_lower(n):
    r = lax.broadcasted_iota(jnp.int32, (n, n), 0)
    c = lax.broadcasted_iota(jnp.int32, (n, n), 1)
    return r >= c


def _bcast_row(ref, h, r):
    return ref[h, pl.ds(r, SUBLANES, stride=0), :]


def _mask_bf16(m):
    return jnp.where(m, 1.0, 0.0).astype(BF16)


def _cparams(sem):
    return pltpu.CompilerParams(dimension_semantics=sem, vmem_limit_bytes=VMEM_LIMIT)


def _lbs_kernel(lg_ref, o_ref):
    lg = lg_ref[...]
    mx = jnp.max(lg, axis=0, keepdims=True)
    e = jnp.exp(lg - mx)
    p = e / jnp.sum(e, axis=0, keepdims=True)
    acc = jnp.zeros_like(p[0:1])
    rows = []
    for l in range(lg.shape[0]):
        acc = acc + p[l:l + 1]
        rows.append(acc - p[0:1])
    o_ref[...] = jnp.concatenate(rows, axis=0)


def _lbs(logits):
    return pl.pallas_call(
        _lbs_kernel,
        out_shape=jax.ShapeDtypeStruct(logits.shape, F32),
        name="hgrn2_lower_bounds",
    )(logits)


def _any_tile(j, tiles):
    hit = j == tiles[0]
    for t in tiles[1:]:
        hit = jnp.logical_or(hit, j == t)
    return hit


def _inproj_kernel(x_ref, g_ref, w_ref, ws_ref, cw_ref, cb_ref, o_ref, os_ref, h_ref, tail_ref, hbuf,
                   *, tiles_per_seq):
    i = pl.program_id(0)
    j = pl.program_id(1)
    n_blk = TM_PROJ // EPI_ROWS

    @pl.when(j == 0)
    def _():
        x = x_ref[...]
        ms = jnp.mean(x * x, axis=-1, keepdims=True)
        hb = (x * lax.rsqrt(ms + EPS) * g_ref[...]).astype(BF16)
        h_ref[...] = hb
        os_ref[...] = _dot(hb, ws_ref[...])

    def pointwise(fn):
        ys = [_dot(h_ref[r * EPI_ROWS:(r + 1) * EPI_ROWS, :], w_ref[...]) for r in range(n_blk)]
        for r in range(n_blk):
            o_ref[r * EPI_ROWS:(r + 1) * EPI_ROWS, :] = fn(ys[r]).astype(o_ref.dtype)

    @pl.when(_any_tile(j, TILES_SILU))
    def _():
        pointwise(_silu)

    @pl.when(_any_tile(j, TILES_SIGMOID))
    def _():
        pointwise(jax.nn.sigmoid)

    @pl.when(j >= TILES_MIXED[0])
    def _():
        slot = j - TILES_MIXED[0]
        prev = jnp.where(i % tiles_per_seq == 0, 0.0, tail_ref[slot])
        cw = cw_ref[0]
        cb = cb_ref[0]
        ys = [_dot(h_ref[r * EPI_ROWS:(r + 1) * EPI_ROWS, :], w_ref[...]) for r in range(n_blk)]
        for r in range(n_blk):
            yb = ys[r][:, :HALF_TILE]
            out = cb + cw[CONV_W - 1:CONV_W] * yb
            for k in range(CONV_W - 1):
                out = out + cw[k:k + 1] * pltpu.roll(yb, CONV_W - 1 - k, axis=0)
            hbuf[0:CONV_PAD, :] = prev
            hbuf[CONV_PAD:2 * CONV_PAD, :] = yb[:CONV_PAD]
            head = cb
            for k in range(CONV_W):
                off = CONV_PAD - (CONV_W - 1) + k
                head = head + cw[k:k + 1] * hbuf[off:off + CONV_PAD, :]
            prev = yb[EPI_ROWS - CONV_PAD:]
            act = _silu(jnp.concatenate([head, out[CONV_PAD:]], axis=0))
            o_ref[r * EPI_ROWS:(r + 1) * EPI_ROWS, :] = jnp.concatenate(
                [act, ys[r][:, HALF_TILE:]], axis=-1).astype(o_ref.dtype)
        tail_ref[slot] = prev


def _inproj(x2, g, w_big, w_small, cw_all, cb_all, layer, seq):
    n = x2.shape[0]
    nb = w_big.shape[-1]
    n_tiles = nb // TN_PROJ
    assert n_tiles == len(TILES_SILU + TILES_SIGMOID + TILES_MIXED)
    return pl.pallas_call(
        functools.partial(_inproj_kernel, tiles_per_seq=seq // TM_PROJ),
        grid=(n // TM_PROJ, n_tiles),
        in_specs=[
            pl.BlockSpec((TM_PROJ, D_MODEL), lambda i, j: (i, 0)),
            pl.BlockSpec((1, D_MODEL), lambda i, j: (0, 0)),
            pl.BlockSpec((None, D_MODEL, TN_PROJ), lambda i, j: (layer, 0, j)),
            pl.BlockSpec((None, D_MODEL, SMALL_W), lambda i, j: (layer, 0, 0)),
            pl.BlockSpec((None, 1, CONV_W, HALF_TILE), lambda i, j: (layer, j, 0, 0)),
            pl.BlockSpec((None, 1, 1, HALF_TILE), lambda i, j: (layer, j, 0, 0)),
        ],
        out_specs=[
            pl.BlockSpec((TM_PROJ, TN_PROJ), lambda i, j: (i, j)),
            pl.BlockSpec((TM_PROJ, SMALL_W), lambda i, j: (i, 0)),
        ],
        out_shape=[
            jax.ShapeDtypeStruct((n, nb), BF16),
            jax.ShapeDtypeStruct((n, SMALL_W), F32),
        ],
        scratch_shapes=[
            pltpu.VMEM((TM_PROJ, D_MODEL), BF16),
            pltpu.VMEM((len(TILES_MIXED), CONV_PAD, HALF_TILE), F32),
            pltpu.VMEM((2 * CONV_PAD, HALF_TILE), F32),
        ],
        compiler_params=_cparams(("arbitrary", "arbitrary")),
        name="in_projection",
    )(x2, g, w_big, w_small, cw_all, cb_all)


def _hgrn2_stages(q_ref, f_ref, i_ref, g_ref, lb_ref, ng_ref, y_ref, state_ref, c_s):
    @pl.when(pl.program_id(1) == 0)
    def _():
        state_ref[...] = jnp.zeros_like(state_ref)

    lb = lb_ref[...]
    tri = _mask_bf16(_tri_lower(CHUNK))
    n_sub = CHUNK // SUB
    ones_rhs = jnp.ones((HG_D, HG_D), BF16)
    lane = lax.broadcasted_iota(jnp.int32, (SUBLANES, HG_D), 1)
    row = lax.broadcasted_iota(jnp.int32, (SUBLANES, HG_D), 0)

    heads = range(HG_HEADS)
    hsl = [slice(h * HG_D, (h + 1) * HG_D) for h in heads]
    zero_tail = jnp.zeros((HG_D - CHUNK, HG_D), BF16)

    def chunk(c):
        rows = pl.ds(pl.multiple_of(c * CHUNK, CHUNK), CHUNK)
        sig = jax.nn.sigmoid(f_ref[rows, :].astype(F32))
        fgate = lb + (1.0 - lb) * sig
        b = _sel_left(tri, jnp.log(fgate) * LOG2E)
        cc = b - jnp.log(jnp.maximum(1.0 - fgate, 0.0)) * LOG2E
        for h in heads:
            c_s[h] = cc[:, hsl[h]]
        q = q_ref[rows, :].astype(F32)
        v16 = i_ref[rows, :]
        yield

        st = [state_ref[h] for h in heads]
        q_dec = (q * jnp.exp2(b)).astype(BF16)
        b_last = b[CHUNK - 1:CHUNK, :]
        k_dec = jnp.exp2(b_last - cc).astype(BF16)
        st_decay = jnp.exp2(b_last)
        o_inter = [_dot_nt(q_dec[:, hsl[h]], st[h].astype(BF16)) for h in heads]
        for h in heads:
            state_ref[h] = st_decay[:, hsl[h]] * st[h] + _dot_tn(v16[:, hsl[h]], k_dec[:, hsl[h]])
        yield

        zs = []
        for i in range(n_sub):
            lo = i * SUB
            for s in range(SUB):
                r0 = lo + (s // SUBLANES) * SUBLANES
                c_row = jnp.concatenate([_bcast_row(c_s, h, lo + s) for h in heads], axis=-1)
                c_row = jnp.concatenate([c_row] * ((lo + SUB - r0) // SUBLANES), axis=0)
                zs.append(q[r0:lo + SUB] * jnp.exp2(b[r0:lo + SUB] - c_row))
            yield
        z_rows = sum(z.shape[0] for z in zs)
        z_all = jnp.concatenate([z[:, hsl[h]] for h in heads for z in zs], axis=0)
        r = _dot(z_all.astype(BF16), ones_rhs)
        yield

        a_off = []
        for i in range(1, n_sub):
            lo = i * SUB
            bref = b[lo - 1:lo, :]
            q_i = (q[lo:lo + SUB] * jnp.exp2(b[lo:lo + SUB] - bref)).astype(BF16)
            k_i = jnp.exp2(bref - cc[:lo]).astype(BF16)
            zero_rows = jnp.zeros((HG_D - lo, HG_D), BF16)
            a_off.append([_dot_nt(q_i[:, hsl[h]], jnp.concatenate([k_i[:, hsl[h]], zero_rows], axis=0))
                          for h in heads])
        yield

        outs = []
        for h in heads:
            a_rows = []
            off = h * z_rows
            for i in range(n_sub):
                lo = i * SUB
                a_i = a_off[i - 1][h] if i > 0 else jnp.zeros((SUB, HG_D), F32)
                tiles = [a_i[j * SUBLANES:(j + 1) * SUBLANES] for j in range(SUB // SUBLANES)]
                for s in range(SUB):
                    for j in range(s // SUBLANES, SUB // SUBLANES):
                        tiles[j] = jnp.where(lane == lo + s, r[off:off + SUBLANES], tiles[j])
                        off += SUBLANES
                for j in range(SUB // SUBLANES):
                    a_rows.append(jnp.where(lane - lo <= row + j * SUBLANES, tiles[j], 0.0))
            a_full = jnp.concatenate(a_rows, axis=0).astype(BF16)
            v_pad = jnp.concatenate([v16[:, hsl[h]], zero_tail], axis=0)
            outs.append(o_inter[h] + _dot(a_full, v_pad))
            if h % 2 == 1:
                yield
        parts = []
        for o in outs:
            parts.append(o * lax.rsqrt(jnp.mean(o * o, axis=-1, keepdims=True) + EPS))
        on = jnp.concatenate(parts, axis=-1) * ng_ref[...]
        y_ref[rows, :] = (on * g_ref[rows, :].astype(F32)).astype(y_ref.dtype)

    return chunk


def _mlstm_stages(qk_ref, v_ref, og_ref, sm_ref, smt_ref, gbr_ref, gbc_ref, ng_ref,
                  y_ref, caug_ref, m_ref):
    half = ML_HEADS * ML_DQK
    k_scale = ML_DQK ** -0.5

    @pl.when(pl.program_id(1) == 0)
    def _():
        caug_ref[...] = jnp.zeros_like(caug_ref)
        m_ref[...] = jnp.zeros_like(m_ref)

    tri_b = _tri_lower(CHUNK)
    tri_l = _mask_bf16(tri_b)
    tri_u = _mask_bf16(lax.broadcasted_iota(jnp.int32, (CHUNK, CHUNK), 0)
                       <= lax.broadcasted_iota(jnp.int32, (CHUNK, CHUNK), 1))
    ones_col = _mask_bf16(lax.broadcasted_iota(jnp.int32, (CHUNK, LANES), 1) == 0)

    def chunk(c):
        rows = pl.ds(pl.multiple_of(c * CHUNK, CHUNK), CHUNK)
        pre_c = sm_ref[rows, :] + gbr_ref[...]
        pre_r = smt_ref[c] + gbc_ref[...]
        cum_c = _sel_left(tri_l, _log_sigmoid(pre_c))
        cum_r = _sel_right(_log_sigmoid(pre_r), tri_u)
        yield
        heads = range(ML_HEADS)
        m_all = m_ref[...]
        m_old = [m_all[h:h + 1, 0:1] for h in heads]
        b_col = [cum_c[:, ML_HEADS + h:ML_HEADS + h + 1] for h in heads]
        i_col = [pre_c[:, h:h + 1] for h in heads]
        log_d = [jnp.where(tri_b, b_col[h] - cum_r[ML_HEADS + h:ML_HEADS + h + 1, :] + pre_r[h:h + 1, :],
                           NEG_BIG) for h in heads]
        log_inter = [b_col[h] + m_old[h] for h in heads]
        m_t = [jnp.maximum(jnp.max(log_d[h], axis=-1, keepdims=True), log_inter[h]) for h in heads]
        b_last = [b_col[h][CHUNK - 1:CHUNK, :] for h in heads]
        log_w = [b_last[h] - b_col[h] + i_col[h] for h in heads]
        m_new = [jnp.maximum(b_last[h] + m_old[h], jnp.max(log_w[h], axis=0, keepdims=True)) for h in heads]
        m_ref[...] = jnp.concatenate(
            [jnp.broadcast_to(m_new[h], (1, m_ref.shape[1])) for h in heads] + [m_all[ML_HEADS:]], axis=0)
        yield

        qb =[qk_ref[rows, h * ML_DQK:(h + 1) * ML_DQK] for h in heads]
        kb = [qk_ref[rows, half + h * ML_DQK:half + (h + 1) * ML_DQK] for h in heads]
        v_aug = [jnp.concatenate([v_ref[rows, h * ML_DV:(h + 1) * ML_DV], ones_col], axis=-1)
                 for h in heads]
        qk = [_dot_nt(qb[h], kb[h]) for h in heads]
        c_aug = [caug_ref[h] for h in heads]
        inter = [_dot(qb[h], c_aug[h].astype(BF16)) * jnp.exp(log_inter[h] - m_t[h]) for h in heads]
        yield
        s = [(qk[h] * (jnp.exp(log_d[h] - m_t[h]) * k_scale)).astype(BF16) for h in heads]
        num = [_dot(s[h], v_aug[h]) + inter[h] for h in heads]
        yield
        kw = [(kb[h].astype(F32) * (jnp.exp(log_w[h] - m_new[h]) * k_scale)).astype(BF16) for h in heads]
        for h in heads:
            caug_ref[h] = jnp.exp(b_last[h] + m_old[h] - m_new[h]) * c_aug[h] + _dot_tn(kw[h], v_aug[h])
        yield

        parts = []
        for h in heads:
            denom = jnp.maximum(jnp.abs(num[h][:, ML_DV:ML_DV + 1]), jnp.exp(-m_t[h]))
            o = num[h][:, :ML_DV] / denom
            parts.append(o * lax.rsqrt(jnp.mean(o * o, axis=-1, keepdims=True) + EPS))
        on = jnp.concatenate(parts, axis=-1) * ng_ref[...]
        y_ref[rows, :] = (on * og_ref[rows, :].astype(F32)).astype(y_ref.dtype)

    return chunk


def _ssd_stages(z_ref, xbc_ref, sm_ref, dtt_ref, dtb_c_ref, alog_c_ref,
                dtb_r_ref, alog_r_ref, d_ref, ng_ref, y_ref, state_ref):
    n_pair = MB_HEADS // 2
    pair_w = 2 * MB_P
    gw = MB_GROUPS * MB_N

    @pl.when(pl.program_id(1) == 0)
    def _():
        state_ref[...] = jnp.zeros_like(state_ref)

    tri_l = _mask_bf16(_tri_lower(CHUNK))
    sel_x = _mask_bf16(lax.broadcasted_iota(jnp.int32, (SMALL_W, MB_W), 0) - DT_COL
                       == jnp.right_shift(lax.broadcasted_iota(jnp.int32, (SMALL_W, MB_W), 1), MB_P_LOG2))
    ur = lax.broadcasted_iota(jnp.int32, (pair_w, pair_w), 0)
    uc = lax.broadcasted_iota(jnp.int32, (pair_w, pair_w), 1)
    same_half = jnp.right_shift(ur, CHUNK_LOG2) == jnp.right_shift(uc, CHUNK_LOG2)
    tri_u2 = _mask_bf16(same_half & (ur <= uc))
    causal = (lax.broadcasted_iota(jnp.int32, (CHUNK, MB_W), 0)
              >= jnp.bitwise_and(lax.broadcasted_iota(jnp.int32, (CHUNK, MB_W), 1), CHUNK - 1))
    first_head = lax.broadcasted_iota(jnp.int32, (CHUNK, pair_w), 1) < MB_P
    a_c = -jnp.exp(alog_c_ref[...])
    a_r = -jnp.exp(alog_r_ref[...])

    def chunk(c):
        rows = pl.ds(pl.multiple_of(c * CHUNK, CHUNK), CHUNK)
        dt_c = _softplus(sm_ref[rows, :] + dtb_c_ref[...])
        cum_c = _sel_left(tri_l, dt_c * a_c)
        dt_x = _sel_right(dt_c, sel_x)
        cum_x = _sel_right(cum_c, sel_x)
        dt_r = _softplus(dtt_ref[c] + dtb_r_ref[...])
        cum_r = _sel_right(dt_r * a_r, tri_u2)
        yield
        pairs = range(n_pair)
        grp = [(2 * p) // (MB_HEADS // MB_GROUPS) for p in pairs]
        lanes = [slice(p * pair_w, (p + 1) * pair_w) for p in pairs]
        xs = xbc_ref[rows, :MB_W].astype(F32)
        xdt = xs * dt_x
        cum_row = jnp.concatenate([cum_r[p:p + 1, :] for p in pairs], axis=-1)
        decay = jnp.exp(jnp.where(causal, cum_x - cum_row, NEG_BIG))
        cum_last = cum_x[CHUNK - 1:CHUNK, :]
        x_dec = (xdt * jnp.exp(cum_last - cum_x)).astype(BF16)
        st_decay = jnp.exp(cum_last)
        carry_w = jnp.exp(cum_x)
        yield
        bm =[xbc_ref[rows, MB_W + g * MB_N:MB_W + (g + 1) * MB_N] for g in range(MB_GROUPS)]
        cm = [xbc_ref[rows, MB_W + gw + g * MB_N:MB_W + gw + (g + 1) * MB_N] for g in range(MB_GROUPS)]
        cb2 = [_dot_nt(cm[g], jnp.concatenate([bm[g], bm[g]], axis=0)) for g in range(MB_GROUPS)]
        x2 = [jnp.concatenate([jnp.where(first_head, xdt[:, lanes[p]], 0.0),
                               jnp.where(first_head, 0.0, xdt[:, lanes[p]])], axis=0).astype(BF16)
              for p in pairs]
        yield
        st = [state_ref[p] for p in pairs]
        y_inter = [_dot(cm[grp[p]], st[p].astype(BF16)) for p in pairs]
        y_intra = [_dot((cb2[grp[p]] * decay[:, lanes[p]]).astype(BF16), x2[p]) for p in pairs]
        yield
        for p in pairs:
            state_ref[p] = st_decay[:, lanes[p]] * st[p] + _dot_tn(bm[grp[p]], x_dec[:, lanes[p]])
        yield
        y = (jnp.concatenate(y_intra, axis=-1) + jnp.concatenate(y_inter, axis=-1) * carry_w
             + d_ref[...] * xs)

        yz = y * z_ref[rows, :].astype(F32)
        gsz = MB_W // MB_GROUPS
        parts = []
        for g in range(MB_GROUPS):
            o = yz[:, g * gsz:(g + 1) * gsz]
            parts.append(o * lax.rsqrt(jnp.mean(o * o, axis=-1, keepdims=True) + EPS))
        y_ref[rows, :] = (jnp.concatenate(parts, axis=-1) * ng_ref[...]).astype(y_ref.dtype)

    return chunk


_DONE = object()


class _ColumnView:
    def __init__(self, pieces):
        self.pieces = pieces
        self.width = sum(w for _, _, w in pieces)

    def __getitem__(self, idx):
        rows, cols = idx
        lo, hi, _ = cols.indices(self.width)
        out, base = [], 0
        for ref, start, w in self.pieces:
            a, b = max(lo, base), min(hi, base + w)
            if a < b:
                out.append(ref[rows, start + a - base:start + b - base])
            base += w
        return out[0] if len(out) == 1 else jnp.concatenate(out, axis=-1)


def _mixers_kernel(q_ref, g_ref, z_ref, og_ref, m0, m1, m2, m3, m4, m5, sm_ref, smt_ref, dtt_ref,
                   lb_ref, hg_ng_ref, gbr_ref, gbc_ref, ml_ng_ref,
                   dtb_c_ref, alog_c_ref, dtb_r_ref, alog_r_ref, d_ref, mb_ng_ref,
                   y_hg, y_ml, y_mb, hg_state, hg_c, ml_caug, ml_m, mb_state):
    ts = y_hg.shape[0]
    mixed = (m0, m1, m2, m3, m4, m5)
    conv = [(m, 0, HALF_TILE) for m in mixed]
    raw = [(m, HALF_TILE, HALF_TILE) for m in mixed]
    qk_view = _ColumnView(conv[0:2])
    xbc_view = _ColumnView(conv[2:6])
    f_view = _ColumnView(raw[0:2])
    i_view = _ColumnView(raw[2:4])
    v_view = _ColumnView(raw[4:6])
    mixers = [
        _hgrn2_stages(q_ref, f_view, i_view, g_ref, lb_ref, hg_ng_ref, y_hg, hg_state, hg_c),
        _mlstm_stages(qk_view, v_view, og_ref, sm_ref, smt_ref, gbr_ref, gbc_ref, ml_ng_ref,
                      y_ml, ml_caug, ml_m),
        _ssd_stages(z_ref, xbc_view, sm_ref, dtt_ref, dtb_c_ref, alog_c_ref, dtb_r_ref, alog_r_ref,
                    d_ref, mb_ng_ref, y_mb, mb_state),
    ]

    def chunk_body(c, carry):
        live = [m(c) for m in mixers]
        while live:
            for g in list(live):
                if next(g, _DONE) is _DONE:
                    live.remove(g)
        return carry

    lax.fori_loop(0, ts // CHUNK, chunk_body, 0, unroll=CHUNK_UNROLL)


def _mixers(proj, small, small_t, dt_t, lb, hg_ng, gb_row, gb_col, ml_ng,
            dtb_c, alog_c, dtb_r, alog_r, d_x, mb_ng, batch, seq):
    n = proj.shape[0]
    nt = seq // TS_MIX
    n_pair = MB_HEADS // 2
    w = D_MODEL

    def col(cb):
        return pl.BlockSpec((TS_MIX, w), lambda b, j: (b * nt + j, cb))

    def const(shape):
        return pl.BlockSpec(shape, lambda b, j: tuple(0 for _ in shape))

    tile_specs = [col(T_HG_Q), col(T_HG_G), col(T_MB_Z), col(T_ML_O)] + [col(t) for t in TILES_MIXED]
    gate_specs = [
        pl.BlockSpec((TS_MIX, SMALL_W), lambda b, j: (b * nt + j, 0)),
        pl.BlockSpec((TS_MIX // CHUNK, 2 * ML_HEADS, CHUNK), lambda b, j: (b * nt + j, 0, 0)),
        pl.BlockSpec((TS_MIX // CHUNK, n_pair, 2 * CHUNK), lambda b, j: (b * nt + j, 0, 0)),
    ]
    param_specs = [const((1, w)), const((1, w)),
                   const((1, SMALL_W)), const((2 * ML_HEADS, 1)), const((1, w)),
                   const((1, SMALL_W)), const((1, SMALL_W)),
                   const((n_pair, 2 * CHUNK)), const((n_pair, 2 * CHUNK)),
                   const((1, w)), const((1, w))]
    out = pl.BlockSpec((TS_MIX, w), lambda b, j: (b * nt + j, 0))
    return pl.pallas_call(
        _mixers_kernel,
        grid=(batch, nt),
        in_specs=tile_specs + gate_specs + param_specs,
        out_specs=[out, out, out],
        out_shape=[jax.ShapeDtypeStruct((n, w), BF16)] * 3,
        scratch_shapes=[
            pltpu.VMEM((HG_HEADS, HG_D, HG_D), F32),
            pltpu.VMEM((HG_HEADS, CHUNK, HG_D), F32),
            pltpu.VMEM((ML_HEADS, ML_DQK, ML_DV + LANES), F32),
            pltpu.VMEM((SUBLANES, LANES), F32),
            pltpu.VMEM((n_pair, MB_N, 2 * MB_P), F32),
        ],
        compiler_params=_cparams(("parallel", "arbitrary")),
        name="token_mixers",
    )(*([proj] * (4 + len(TILES_MIXED))), small, small_t, dt_t,
      lb, hg_ng, gb_row, gb_col, ml_ng, dtb_c, alog_c, dtb_r, alog_r, d_x, mb_ng)


def _merge_kernel(x_ref, yh_ref, ym_ref, yb_ref, g0_ref, g1_ref, g2_ref,
                  wh_ref, wm_ref, wb_ref, wo_ref, o_ref):
    mixed = g0_ref[...].astype(F32) * _dot(yh_ref[...], wh_ref[...])
    mixed = mixed + g1_ref[...].astype(F32) * _dot(ym_ref[...], wm_ref[...])
    mixed = mixed + g2_ref[...].astype(F32) * _dot(yb_ref[...], wb_ref[...])
    o_ref[...] = x_ref[...] + _dot(mixed.astype(BF16), wo_ref[...])


def _merge(x2, y_hg, y_ml, y_mb, proj, w_hg, w_ml, w_mb, w_out, layer):
    n = x2.shape[0]
    tile = lambda cb: pl.BlockSpec((TM_MERGE, D_MODEL), lambda i: (i, cb))
    wspec = pl.BlockSpec((None, D_MODEL, D_MODEL), lambda i: (layer, 0, 0))
    return pl.pallas_call(
        _merge_kernel,
        grid=(n // TM_MERGE,),
        in_specs=[tile(0), tile(0), tile(0), tile(0), tile(T_GATE0), tile(T_GATE0 + 1), tile(T_GATE0 + 2),
                  wspec, wspec, wspec, wspec],
        out_specs=tile(0),
        out_shape=jax.ShapeDtypeStruct((n, D_MODEL), F32),
        compiler_params=_cparams(("parallel",)),
        name="branch_merge",
    )(x2, y_hg, y_ml, y_mb, proj, proj, proj, w_hg, w_ml, w_mb, w_out)


def _ffn_kernel(x_ref, g_ref, wu_ref, cw_ref, cb_ref, wd_ref, fg_ref,
                o_ref, tail, hbuf_g, hbuf_v, *, final):
    tm = x_ref.shape[0]
    n_ck = D_FF // FF_CHUNK

    @pl.when(pl.program_id(1) == 0)
    def _():
        tail[...] = jnp.zeros_like(tail)

    x = x_ref[...]
    hb = (x * lax.rsqrt(jnp.mean(x * x, axis=-1, keepdims=True) + EPS) * g_ref[...]).astype(BF16)

    def cols(c, half):
        return slice(half * D_FF + c * FF_CHUNK, half * D_FF + (c + 1) * FF_CHUNK)

    def up(c):
        return _dot(hb, wu_ref[:, cols(c, 0)]), _dot(hb, wu_ref[:, cols(c, 1)])

    def conv(u, hbuf, cs):
        out = cb_ref[:, cs] + cw_ref[FFN_CONV - 1:FFN_CONV, cs] * u
        for k in range(FFN_CONV - 1):
            out = out + cw_ref[k:k + 1, cs] * pltpu.roll(u, FFN_CONV - 1 - k, axis=0)
        hbuf[0:CONV_PAD, :] = tail[:, cs]
        hbuf[CONV_PAD:2 * CONV_PAD, :] = u[:CONV_PAD]
        tail[:, cs] = u[tm - CONV_PAD:]
        head = cb_ref[:, cs]
        for k in range(FFN_CONV):
            off = CONV_PAD - (FFN_CONV - 1) + k
            head = head + cw_ref[k:k + 1, cs] * hbuf[off:off + CONV_PAD, :]
        return jnp.concatenate([head, out[CONV_PAD:]], axis=0)

    acc = x
    u_next = up(0)
    for c in range(n_ck):
        u_g, u_v = u_next
        if c + 1 < n_ck:
            u_next = up(c + 1)
        a_g = conv(u_g, hbuf_g, cols(c, 0))
        a_v = conv(u_v, hbuf_v, cols(c, 1))
        acc = acc + _dot((_silu(a_g) * a_v).astype(BF16), wd_ref[cols(c, 0), :])
    if final:
        acc = acc * lax.rsqrt(jnp.mean(acc * acc, axis=-1, keepdims=True) + EPS) * fg_ref[...]
    o_ref[...] = acc


def _ffn(x2, g, w_up, cw, cb, w_down, final_g, layer, batch, seq, final):
    n = x2.shape[0]
    nt = seq // TM_FFN

    def const(shape):
        return pl.BlockSpec(shape, lambda b, j: tuple(0 for _ in shape),
                            pipeline_mode=pl.Buffered(1))

    def layer_weight(shape):
        return pl.BlockSpec((None,) + shape, lambda b, j: (layer,) + tuple(0 for _ in shape),
                            pipeline_mode=pl.Buffered(1))

    tile = pl.BlockSpec((TM_FFN, D_MODEL), lambda b, j: (b * nt + j, 0))
    return pl.pallas_call(
        functools.partial(_ffn_kernel, final=final),
        grid=(batch, nt),
        in_specs=[
            tile, const((1, D_MODEL)),
            layer_weight((D_MODEL, 2 * D_FF)), const((FFN_CONV, 2 * D_FF)), const((1, 2 * D_FF)),
            layer_weight((D_FF, D_MODEL)), const((1, D_MODEL)),
        ],
        out_specs=tile,
        out_shape=jax.ShapeDtypeStruct((n, D_MODEL), F32),
        scratch_shapes=[
            pltpu.VMEM((CONV_PAD, 2 * D_FF), F32),
            pltpu.VMEM((2 * CONV_PAD, FF_CHUNK), F32),
            pltpu.VMEM((2 * CONV_PAD, FF_CHUNK), F32),
        ],
        compiler_params=_cparams(("parallel", "arbitrary")),
        name="conv_gated_mlp",
    )(x2, g, w_up, cw, cb, w_down, final_g)


def kernel(x, norm1_g, w_in, hg_lb_logits, hg_norm_g, ml_conv_w, ml_conv_b, ml_gate_b, ml_norm_g,
           mb_conv_w, mb_conv_b, mb_dt_bias, mb_a_log, mb_d, mb_norm_g, w_br_hg, w_br_ml, w_br_mb,
           w_out, norm2_g, w_up, ffn_conv_w, ffn_conv_b, w_down, final_g):
    batch, seq, _ = x.shape
    n = batch * seq
    assert seq % TS_MIX == 0 and seq % TM_FFN == 0 and seq % TM_PROJ == 0 and n % TM_MERGE == 0
    depth = w_in.shape[0]

    w = D_MODEL
    o_hg_q, o_hg_f, o_hg_i, o_hg_g = 0, w, 2 * w, 3 * w
    o_ml_qk = 4 * w
    o_ml_v = o_ml_qk + 2 * ML_HEADS * ML_DQK
    o_if = o_ml_v + ML_HEADS * ML_DV
    o_ml_o = o_if + 2 * ML_HEADS
    o_mb_z = o_ml_o + ML_HEADS * ML_DV
    o_mb_xbc = o_mb_z + MB_W
    o_dt = o_mb_xbc + MB_CONV_DIM
    o_gate = o_dt + MB_HEADS
    hw = HALF_TILE
    conv_halves = [o_ml_qk + k * hw for k in range(2)] + [o_mb_xbc + k * hw for k in range(4)]
    raw_halves = [o_hg_f, o_hg_f + hw, o_hg_i, o_hg_i + hw, o_ml_v, o_ml_v + hw]
    cols = [(o_hg_q, w), (o_hg_g, w), (o_mb_z, w), (o_ml_o, w), (o_gate, N_GATES * w)]
    for c0, r0 in zip(conv_halves, raw_halves):
        cols += [(c0, hw), (r0, hw)]
    w_big = jnp.concatenate([w_in[:, :, a:a + n_] for a, n_ in cols], axis=-1).astype(BF16)
    pad = SMALL_W - 2 * ML_HEADS - MB_HEADS
    w_small = jnp.concatenate(
        [w_in[:, :, o_if:o_ml_o], w_in[:, :, o_dt:o_gate],
         jnp.zeros((depth, D_MODEL, pad), w_in.dtype)], axis=-1).astype(BF16)

    lbs = _lbs(hg_lb_logits.astype(F32))
    hg_ng = jnp.tile(hg_norm_g, (1, HG_HEADS))
    ml_ng = jnp.tile(ml_norm_g, (1, ML_HEADS))
    gb_row = jnp.pad(ml_gate_b, ((0, 0), (0, SMALL_W - 2 * ML_HEADS)))
    dtb_c = jnp.pad(mb_dt_bias, ((0, 0), (DT_COL, SMALL_W - DT_COL - MB_HEADS)))
    alog_c = jnp.pad(mb_a_log, ((0, 0), (DT_COL, SMALL_W - DT_COL - MB_HEADS)))
    n_pair = MB_HEADS // 2
    dtb_r = jnp.repeat(mb_dt_bias, CHUNK, axis=-1).reshape(depth, n_pair, 2 * CHUNK)
    alog_r = jnp.repeat(mb_a_log, CHUNK, axis=-1).reshape(depth, n_pair, 2 * CHUNK)
    d_x = jnp.repeat(mb_d, MB_P, axis=-1)

    n_tiles = w_big.shape[-1] // TN_PROJ
    n_mixed = len(TILES_MIXED)
    conv_w = jnp.concatenate([ml_conv_w, mb_conv_w], axis=-1).reshape(depth, CONV_W, n_mixed, hw)
    conv_b = jnp.concatenate([ml_conv_b, mb_conv_b], axis=-1).reshape(depth, 1, n_mixed, hw)
    front = n_tiles - n_mixed
    cw_all = jnp.pad(jnp.swapaxes(conv_w, 1, 2), ((0, 0), (front, 0), (0, 0), (0, 0)))
    cb_all = jnp.pad(jnp.swapaxes(conv_b, 1, 2), ((0, 0), (front, 0), (0, 0), (0, 0)))

    w_hg = w_br_hg.astype(BF16)
    w_ml = w_br_ml.astype(BF16)
    w_mb = w_br_mb.astype(BF16)
    w_o = w_out.astype(BF16)
    w_u = w_up.astype(BF16)
    w_d = w_down.astype(BF16)

    x2 = x.reshape(n, D_MODEL)
    row = lambda a: a.reshape(1, -1)
    for l in range(depth):
        proj, small = _inproj(x2, row(norm1_g[l]), w_big, w_small, cw_all, cb_all, l, seq)
        chunks = small.reshape(n // CHUNK, CHUNK, SMALL_W)
        small_t = jnp.swapaxes(chunks[:, :, :2 * ML_HEADS], 1, 2)
        dt_t = jnp.swapaxes(chunks[:, :, DT_COL:DT_COL + MB_HEADS], 1, 2).reshape(
            n // CHUNK, n_pair, 2 * CHUNK)

        y_hg, y_ml, y_mb = _mixers(
            proj, small, small_t, dt_t, row(lbs[l]), row(hg_ng[l]),
            row(gb_row[l]), ml_gate_b[l].reshape(2 * ML_HEADS, 1), row(ml_ng[l]),
            row(dtb_c[l]), row(alog_c[l]), dtb_r[l], alog_r[l], row(d_x[l]), row(mb_norm_g[l]), batch, seq)
        x2 = _merge(x2, y_hg, y_ml, y_mb, proj, w_hg, w_ml, w_mb, w_o, l)
        x2 = _ffn(x2, row(norm2_g[l]), w_u, ffn_conv_w[l], row(ffn_conv_b[l]), w_d, row(final_g),
                  l, batch, seq, final=(l == depth - 1))
    return x2.reshape(batch, seq, D_MODEL)
```

```python
import functools
import math

import jax
import jax.numpy as jnp
from jax import lax
from jax.experimental import pallas as pl
from jax.experimental.pallas import tpu as pltpu

F32 = jnp.float32
BF16 = jnp.bfloat16

D_MODEL = 1024
CHUNK = 64
CHUNK_LOG2 = 6
SUB = 8
SUBLANES = 8
LANES = 128
LOG2E = math.log2(math.e)
EPS = 1e-6
NEG_BIG = -1e30
HG_HEADS = 8
HG_D = 128
ML_HEADS = 4
ML_DQK = 128
ML_DV = 256
MB_HEADS = 16
MB_P = 64
MB_P_LOG2 = 6
MB_GROUPS = 4
MB_N = 128
MB_W = MB_HEADS * MB_P
MB_CONV_DIM = MB_W + 2 * MB_GROUPS * MB_N
D_FF = 2816
FFN_CONV = 3
FF_CHUNK = 1408
SMALL_W = 128
DT_COL = 8
CONV_PAD = 8

VMEM_LIMIT = 56 * 1024 * 1024

TM_PROJ = 2048
TN_PROJ = 1024
EPI_ROWS = 256
TILES_SILU = (0, 1, 2)
TILES_SIGMOID = (3, 4, 5, 6)
TILES_MIXED = (7, 8, 9, 10, 11, 12)
T_HG_Q, T_HG_G, T_MB_Z, T_ML_O, T_GATE0 = 0, 1, 2, 3, 4
N_GATES = 3
HALF_TILE = TN_PROJ // 2
CONV_W = 4
TS_MIX = 512
CHUNK_UNROLL = 2
TM_MERGE = 512
TM_FFN = 512


def _silu(x):
    return x * jax.nn.sigmoid(x)


def _softplus(x):
    return jnp.maximum(x, 0.0) + jnp.log1p(jnp.exp(-jnp.abs(x)))


def _log_sigmoid(x):
    return jnp.minimum(x, 0.0) - jnp.log1p(jnp.exp(-jnp.abs(x)))


def _dot(a, b):
    return jnp.dot(a, b, preferred_element_type=F32)


def _dot_nt(a, b):
    return lax.dot_general(a, b, (((1,), (1,)), ((), ())), preferred_element_type=F32)


def _dot_tn(a, b):
    return lax.dot_general(a, b, (((0,), (0,)), ((), ())), preferred_element_type=F32)


def _split3(a):
    hi = a.astype(BF16)
    r = a - hi.astype(F32)
    mid = r.astype(BF16)
    lo = (r - mid.astype(F32)).astype(BF16)
    return hi, mid, lo


def _sel_left(sel, a):
    hi, mid, lo = _split3(a)
    return _dot(sel, hi) + (_dot(sel, mid) + _dot(sel, lo))


def _sel_right(a, sel):
    hi, mid, lo = _split3(a)
    return _dot(hi, sel) + (_dot(mid, sel) + _dot(lo, sel))


def _tri_lower(n):
    r = lax.broadcasted_iota(jnp.int32, (n, n), 0)
    c = lax.broadcasted_iota(jnp.int32, (n, n), 1)
    return r >= c


def _bcast_row(ref, h, r):
    return ref[h, pl.ds(r, SUBLANES, stride=0), :]


def _mask_bf16(m):
    return jnp.where(m, 1.0, 0.0).astype(BF16)


def _cparams(sem):
    return pltpu.CompilerParams(dimension_semantics=sem, vmem_limit_bytes=VMEM_LIMIT)


def _lbs_kernel(lg_ref, o_ref):
    lg = lg_ref[...]
    mx = jnp.max(lg, axis=0, keepdims=True)
    e = jnp.exp(lg - mx)
    p = e / jnp.sum(e, axis=0, keepdims=True)
    acc = jnp.zeros_like(p[0:1])
    rows = []
    for l in range(lg.shape[0]):
        acc = acc + p[l:l + 1]
        rows.append(acc - p[0:1])
    o_ref[...] = jnp.concatenate(rows, axis=0)


def _lbs(logits):
    return pl.pallas_call(
        _lbs_kernel,
        out_shape=jax.ShapeDtypeStruct(logits.shape, F32),
        name="hgrn2_lower_bounds",
    )(logits)


def _any_tile(j, tiles):
    hit = j == tiles[0]
    for t in tiles[1:]:
        hit = jnp.logical_or(hit, j == t)
    return hit


def _inproj_kernel(x_ref, g_ref, w_ref, ws_ref, cw_ref, cb_ref, o_ref, os_ref, h_ref, tail_ref, hbuf,
                   *, tiles_per_seq):
    i = pl.program_id(0)
    j = pl.program_id(1)
    n_blk = TM_PROJ // EPI_ROWS

    @pl.when(jnp.logical_and(i == 0, j == 0))
    def _():
        tail_ref[...] = jnp.zeros_like(tail_ref)

    @pl.when(j == 0)
    def _():
        x = x_ref[...]
        ms = jnp.mean(x * x, axis=-1, keepdims=True)
        hb = (x * lax.rsqrt(ms + EPS) * g_ref[...]).astype(BF16)
        h_ref[...] = hb
        os_ref[...] = _dot(hb, ws_ref[...])

    def pointwise(fn):
        ys = [_dot(h_ref[r * EPI_ROWS:(r + 1) * EPI_ROWS, :], w_ref[...]) for r in range(n_blk)]
        for r in range(n_blk):
            o_ref[r * EPI_ROWS:(r + 1) * EPI_ROWS, :] = fn(ys[r]).astype(o_ref.dtype)

    @pl.when(_any_tile(j, TILES_SILU))
    def _():
        pointwise(_silu)

    @pl.when(_any_tile(j, TILES_SIGMOID))
    def _():
        pointwise(jax.nn.sigmoid)

    @pl.when(j >= TILES_MIXED[0])
    def _():
        slot = j - TILES_MIXED[0]
        prev = jnp.where(i % tiles_per_seq == 0, 0.0, tail_ref[slot])
        cw = cw_ref[0]
        cb = cb_ref[0]
        ys = [_dot(h_ref[r * EPI_ROWS:(r + 1) * EPI_ROWS, :], w_ref[...]) for r in range(n_blk)]
        for r in range(n_blk):
            yb = ys[r][:, :HALF_TILE]
            out = cb + cw[CONV_W - 1:CONV_W] * yb
            for k in range(CONV_W - 1):
                out = out + cw[k:k + 1] * pltpu.roll(yb, CONV_W - 1 - k, axis=0)
            hbuf[0:CONV_PAD, :] = prev
            hbuf[CONV_PAD:2 * CONV_PAD, :] = yb[:CONV_PAD]
            head = cb
            for k in range(CONV_W):
                off = CONV_PAD - (CONV_W - 1) + k
                head = head + cw[k:k + 1] * hbuf[off:off + CONV_PAD, :]
            prev = yb[EPI_ROWS - CONV_PAD:]
            act = _silu(jnp.concatenate([head, out[CONV_PAD:]], axis=0))
            o_ref[r * EPI_ROWS:(r + 1) * EPI_ROWS, :] = jnp.concatenate(
                [act, ys[r][:, HALF_TILE:]], axis=-1).astype(o_ref.dtype)
        tail_ref[slot] = prev


def _inproj(x2, g, w_big, w_small, cw_all, cb_all, layer, seq):
    n = x2.shape[0]
    nb = w_big.shape[-1]
    n_tiles = nb // TN_PROJ
    assert n_tiles == len(TILES_SILU + TILES_SIGMOID + TILES_MIXED)
    return pl.pallas_call(
        functools.partial(_inproj_kernel, tiles_per_seq=seq // TM_PROJ),
        grid=(n // TM_PROJ, n_tiles),
        in_specs=[
            pl.BlockSpec((TM_PROJ, D_MODEL), lambda i, j: (i, 0)),
            pl.BlockSpec((1, D_MODEL), lambda i, j: (0, 0)),
            pl.BlockSpec((None, D_MODEL, TN_PROJ), lambda i, j: (layer, 0, j)),
            pl.BlockSpec((None, D_MODEL, SMALL_W), lambda i, j: (layer, 0, 0)),
            pl.BlockSpec((None, 1, CONV_W, HALF_TILE), lambda i, j: (layer, j, 0, 0)),
            pl.BlockSpec((None, 1, 1, HALF_TILE), lambda i, j: (layer, j, 0, 0)),
        ],
        out_specs=[
            pl.BlockSpec((TM_PROJ, TN_PROJ), lambda i, j: (i, j)),
            pl.BlockSpec((TM_PROJ, SMALL_W), lambda i, j: (i, 0)),
        ],
        out_shape=[
            jax.ShapeDtypeStruct((n, nb), BF16),
            jax.ShapeDtypeStruct((n, SMALL_W), F32),
        ],
        scratch_shapes=[
            pltpu.VMEM((TM_PROJ, D_MODEL), BF16),
            pltpu.VMEM((len(TILES_MIXED), CONV_PAD, HALF_TILE), F32),
            pltpu.VMEM((2 * CONV_PAD, HALF_TILE), F32),
        ],
        compiler_params=_cparams(("arbitrary", "arbitrary")),
        name="in_projection",
    )(x2, g, w_big, w_small, cw_all, cb_all)


def _hgrn2_stages(q_ref, f_ref, i_ref, g_ref, lb_ref, ng_ref, y_ref, state_ref, c_s):
    @pl.when(pl.program_id(1) == 0)
    def _():
        state_ref[...] = jnp.zeros_like(state_ref)

    lb = lb_ref[...]
    tri = _mask_bf16(_tri_lower(CHUNK))
    n_sub = CHUNK // SUB
    ones_rhs = jnp.ones((HG_D, HG_D), BF16)
    lane = lax.broadcasted_iota(jnp.int32, (SUBLANES, HG_D), 1)
    row = lax.broadcasted_iota(jnp.int32, (SUBLANES, HG_D), 0)

    heads = range(HG_HEADS)
    hsl = [slice(h * HG_D, (h + 1) * HG_D) for h in heads]
    zero_tail = jnp.zeros((HG_D - CHUNK, HG_D), BF16)

    def chunk(c):
        rows = pl.ds(pl.multiple_of(c * CHUNK, CHUNK), CHUNK)
        sig = jax.nn.sigmoid(f_ref[rows, :].astype(F32))
        fgate = lb + (1.0 - lb) * sig
        b = _sel_left(tri, jnp.log(fgate) * LOG2E)
        cc = b - jnp.log(jnp.maximum(1.0 - fgate, 0.0)) * LOG2E
        for h in heads:
            c_s[h] = cc[:, hsl[h]]
        q = q_ref[rows, :].astype(F32)
        v16 = i_ref[rows, :]
        yield

        st = [state_ref[h] for h in heads]
        q_dec = (q * jnp.exp2(b)).astype(BF16)
        b_last = b[CHUNK - 1:CHUNK, :]
        k_dec = jnp.exp2(b_last - cc).astype(BF16)
        st_decay = jnp.exp2(b_last)
        o_inter = [_dot_nt(q_dec[:, hsl[h]], st[h].astype(BF16)) for h in heads]
        for h in heads:
            state_ref[h] = st_decay[:, hsl[h]] * st[h] + _dot_tn(v16[:, hsl[h]], k_dec[:, hsl[h]])
        yield

        zs = []
        for i in range(n_sub):
            lo = i * SUB
            for s in range(SUB):
                r0 = lo + (s // SUBLANES) * SUBLANES
                c_row = jnp.concatenate([_bcast_row(c_s, h, lo + s) for h in heads], axis=-1)
                c_row = jnp.concatenate([c_row] * ((lo + SUB - r0) // SUBLANES), axis=0)
                zs.append(q[r0:lo + SUB] * jnp.exp2(b[r0:lo + SUB] - c_row))
            yield
        z_rows = sum(z.shape[0] for z in zs)
        z_all = jnp.concatenate([z[:, hsl[h]] for h in heads for z in zs], axis=0)
        r = _dot(z_all.astype(BF16), ones_rhs)
        yield

        a_off = []
        for i in range(1, n_sub):
            lo = i * SUB
            bref = b[lo - 1:lo, :]
            q_i = (q[lo:lo + SUB] * jnp.exp2(b[lo:lo + SUB] - bref)).astype(BF16)
            k_i = jnp.exp2(bref - cc[:lo]).astype(BF16)
            zero_rows = jnp.zeros((HG_D - lo, HG_D), BF16)
            a_off.append([_dot_nt(q_i[:, hsl[h]], jnp.concatenate([k_i[:, hsl[h]], zero_rows], axis=0))
                          for h in heads])
        yield

        outs = []
        for h in heads:
            a_rows = []
            off = h * z_rows
            for i in range(n_sub):
                lo = i * SUB
                a_i = a_off[i - 1][h] if i > 0 else jnp.zeros((SUB, HG_D), F32)
                tiles = [a_i[j * SUBLANES:(j + 1) * SUBLANES] for j in range(SUB // SUBLANES)]
                for s in range(SUB):
                    for j in range(s // SUBLANES, SUB // SUBLANES):
                        tiles[j] = jnp.where(lane == lo + s, r[off:off + SUBLANES], tiles[j])
                        off += SUBLANES
                for j in range(SUB // SUBLANES):
                    a_rows.append(jnp.where(lane - lo <= row + j * SUBLANES, tiles[j], 0.0))
            a_full = jnp.concatenate(a_rows, axis=0).astype(BF16)
            v_pad = jnp.concatenate([v16[:, hsl[h]], zero_tail], axis=0)
            outs.append(o_inter[h] + _dot(a_full, v_pad))
            if h % 2 == 1:
                yield
        parts = []
        for o in outs:
            parts.append(o * lax.rsqrt(jnp.mean(o * o, axis=-1, keepdims=True) + EPS))
        on = jnp.concatenate(parts, axis=-1) * ng_ref[...]
        y_ref[rows, :] = (on * g_ref[rows, :].astype(F32)).astype(y_ref.dtype)

    return chunk


def _mlstm_stages(qk_ref, v_ref, og_ref, sm_ref, smt_ref, gbr_ref, gbc_ref, ng_ref,
                  y_ref, caug_ref, m_ref):
    half = ML_HEADS * ML_DQK
    k_scale = ML_DQK ** -0.5

    @pl.when(pl.program_id(1) == 0)
    def _():
        caug_ref[...] = jnp.zeros_like(caug_ref)
        m_ref[...] = jnp.zeros_like(m_ref)

    tri_b = _tri_lower(CHUNK)
    tri_l = _mask_bf16(tri_b)
    tri_u = _mask_bf16(lax.broadcasted_iota(jnp.int32, (CHUNK, CHUNK), 0)
                       <= lax.broadcasted_iota(jnp.int32, (CHUNK, CHUNK), 1))
    ones_col = _mask_bf16(lax.broadcasted_iota(jnp.int32, (CHUNK, LANES), 1) == 0)

    def chunk(c):
        rows = pl.ds(pl.multiple_of(c * CHUNK, CHUNK), CHUNK)
        pre_c = sm_ref[rows, :] + gbr_ref[...]
        pre_r = smt_ref[c] + gbc_ref[...]
        cum_c = _sel_left(tri_l, _log_sigmoid(pre_c))
        cum_r = _sel_right(_log_sigmoid(pre_r), tri_u)
        yield
        heads = range(ML_HEADS)
        m_all = m_ref[...]
        m_old = [m_all[h:h + 1, 0:1] for h in heads]
        b_col = [cum_c[:, ML_HEADS + h:ML_HEADS + h + 1] for h in heads]
        i_col = [pre_c[:, h:h + 1] for h in heads]
        log_d = [jnp.where(tri_b, b_col[h] - cum_r[ML_HEADS + h:ML_HEADS + h + 1, :] + pre_r[h:h + 1, :],
                           NEG_BIG) for h in heads]
        log_inter = [b_col[h] + m_old[h] for h in heads]
        m_t = [jnp.maximum(jnp.max(log_d[h], axis=-1, keepdims=True), log_inter[h]) for h in heads]
        b_last = [b_col[h][CHUNK - 1:CHUNK, :] for h in heads]
        log_w = [b_last[h] - b_col[h] + i_col[h] for h in heads]
        m_new = [jnp.maximum(b_last[h] + m_old[h], jnp.max(log_w[h], axis=0, keepdims=True)) for h in heads]
        m_ref[...] = jnp.concatenate(
            [jnp.broadcast_to(m_new[h], (1, m_ref.shape[1])) for h in heads] + [m_all[ML_HEADS:]], axis=0)
        yield

        qb = [qk_ref[rows, h * ML_DQK:(h + 1) * ML_DQK] for h in heads]
        kb = [qk_ref[rows, half + h * ML_DQK:half + (h + 1) * ML_DQK] for h in heads]
        v_aug = [jnp.concatenate([v_ref[rows, h * ML_DV:(h + 1) * ML_DV], ones_col], axis=-1)
                 for h in heads]
        qk = [_dot_nt(qb[h], kb[h]) for h in heads]
        c_aug = [caug_ref[h] for h in heads]
        inter = [_dot(qb[h], c_aug[h].astype(BF16)) * jnp.exp(log_inter[h] - m_t[h]) for h in heads]
        yield
        s = [(qk[h] * (jnp.exp(log_d[h] - m_t[h]) * k_scale)).astype(BF16) for h in heads]
        num = [_dot(s[h], v_aug[h]) + inter[h] for h in heads]
        yield
        kw = [(kb[h].astype(F32) * (jnp.exp(log_w[h] - m_new[h]) * k_scale)).astype(BF16) for h in heads]
        for h in heads:
            caug_ref[h] = jnp.exp(b_last[h] + m_old[h] - m_new[h]) * c_aug[h] + _dot_tn(kw[h], v_aug[h])
        yield

        parts = []
        for h in heads:
            denom = jnp.maximum(jnp.abs(num[h][:, ML_DV:ML_DV + 1]), jnp.exp(-m_t[h]))
            o = num[h][:, :ML_DV] / denom
            parts.append(o * lax.rsqrt(jnp.mean(o * o, axis=-1, keepdims=True) + EPS))
        on = jnp.concatenate(parts, axis=-1) * ng_ref[...]
        y_ref[rows, :] = (on * og_ref[rows, :].astype(F32)).astype(y_ref.dtype)

    return chunk


def _ssd_stages(z_ref, xbc_ref, sm_ref, dtt_ref, dtb_c_ref, alog_c_ref,
                dtb_r_ref, alog_r_ref, d_ref, ng_ref, y_ref, state_ref):
    n_pair = MB_HEADS // 2
    pair_w = 2 * MB_P
    gw = MB_GROUPS * MB_N

    @pl.when(pl.program_id(1) == 0)
    def _():
        state_ref[...] = jnp.zeros_like(state_ref)

    tri_l = _mask_bf16(_tri_lower(CHUNK))
    sel_x = _mask_bf16(lax.broadcasted_iota(jnp.int32, (SMALL_W, MB_W), 0) - DT_COL
                       == jnp.right_shift(lax.broadcasted_iota(jnp.int32, (SMALL_W, MB_W), 1), MB_P_LOG2))
    ur = lax.broadcasted_iota(jnp.int32, (pair_w, pair_w), 0)
    uc = lax.broadcasted_iota(jnp.int32, (pair_w, pair_w), 1)
    same_half = jnp.right_shift(ur, CHUNK_LOG2) == jnp.right_shift(uc, CHUNK_LOG2)
    tri_u2 = _mask_bf16(same_half & (ur <= uc))
    causal = (lax.broadcasted_iota(jnp.int32, (CHUNK, MB_W), 0)
              >= jnp.bitwise_and(lax.broadcasted_iota(jnp.int32, (CHUNK, MB_W), 1), CHUNK - 1))
    first_head = lax.broadcasted_iota(jnp.int32, (CHUNK, pair_w), 1) < MB_P
    a_c = -jnp.exp(alog_c_ref[...])
    a_r = -jnp.exp(alog_r_ref[...])

    def chunk(c):
        rows = pl.ds(pl.multiple_of(c * CHUNK, CHUNK), CHUNK)
        dt_c = _softplus(sm_ref[rows, :] + dtb_c_ref[...])
        cum_c = _sel_left(tri_l, dt_c * a_c)
        dt_x = _sel_right(dt_c, sel_x)
        cum_x = _sel_right(cum_c, sel_x)
        dt_r = _softplus(dtt_ref[c] + dtb_r_ref[...])
        cum_r = _sel_right(dt_r * a_r, tri_u2)
        yield
        pairs = range(n_pair)
        grp = [(2 * p) // (MB_HEADS // MB_GROUPS) for p in pairs]
        lanes = [slice(p * pair_w, (p + 1) * pair_w) for p in pairs]
        xs = xbc_ref[rows, :MB_W].astype(F32)
        xdt = xs * dt_x
        cum_row = jnp.concatenate([cum_r[p:p + 1, :] for p in pairs], axis=-1)
        decay = jnp.exp(jnp.where(causal, cum_x - cum_row, NEG_BIG))
        cum_last = cum_x[CHUNK - 1:CHUNK, :]
        x_dec = (xdt * jnp.exp(cum_last - cum_x)).astype(BF16)
        st_decay = jnp.exp(cum_last)
        carry_w = jnp.exp(cum_x)
        yield
        bm = [xbc_ref[rows, MB_W + g * MB_N:MB_W + (g + 1) * MB_N] for g in range(MB_GROUPS)]
        cm = [xbc_ref[rows, MB_W + gw + g * MB_N:MB_W + gw + (g + 1) * MB_N] for g in range(MB_GROUPS)]
        cb2 = [_dot_nt(cm[g], jnp.concatenate([bm[g], bm[g]], axis=0)) for g in range(MB_GROUPS)]
        x2 = [jnp.concatenate([jnp.where(first_head, xdt[:, lanes[p]], 0.0),
                               jnp.where(first_head, 0.0, xdt[:, lanes[p]])], axis=0).astype(BF16)
              for p in pairs]
        yield
        st = [state_ref[p] for p in pairs]
        y_inter = [_dot(cm[grp[p]], st[p].astype(BF16)) for p in pairs]
        y_intra = [_dot((cb2[grp[p]] * decay[:, lanes[p]]).astype(BF16), x2[p]) for p in pairs]
        yield
        for p in pairs:
            state_ref[p] = st_decay[:, lanes[p]] * st[p] + _dot_tn(bm[grp[p]], x_dec[:, lanes[p]])
        yield
        y = (jnp.concatenate(y_intra, axis=-1) + jnp.concatenate(y_inter, axis=-1) * carry_w
             + d_ref[...] * xs)

        yz = y * z_ref[rows, :].astype(F32)
        gsz = MB_W // MB_GROUPS
        parts = []
        for g in range(MB_GROUPS):
            o = yz[:, g * gsz:(g + 1) * gsz]
            parts.append(o * lax.rsqrt(jnp.mean(o * o, axis=-1, keepdims=True) + EPS))
        y_ref[rows, :] = (jnp.concatenate(parts, axis=-1) * ng_ref[...]).astype(y_ref.dtype)

    return chunk


_DONE = object()


class _ColumnView:
    def __init__(self, pieces):
        self.pieces = pieces
        self.width = sum(w for _, _, w in pieces)

    def __getitem__(self, idx):
        rows, cols = idx
        lo, hi, _ = cols.indices(self.width)
        out, base = [], 0
        for ref, start, w in self.pieces:
            a, b = max(lo, base), min(hi, base + w)
            if a < b:
                out.append(ref[rows, start + a - base:start + b - base])
            base += w
        return out[0] if len(out) == 1 else jnp.concatenate(out, axis=-1)


def _mixers_kernel(q_ref, g_ref, z_ref, og_ref, m0, m1, m2, m3, m4, m5, sm_ref, smt_ref, dtt_ref,
                   lb_ref, hg_ng_ref, gbr_ref, gbc_ref, ml_ng_ref,
                   dtb_c_ref, alog_c_ref, dtb_r_ref, alog_r_ref, d_ref, mb_ng_ref,
                   y_hg, y_ml, y_mb, hg_state, hg_c, ml_caug, ml_m, mb_state):
    ts = y_hg.shape[0]
    mixed = (m0, m1, m2, m3, m4, m5)
    conv = [(m, 0, HALF_TILE) for m in mixed]
    raw = [(m, HALF_TILE, HALF_TILE) for m in mixed]
    qk_view = _ColumnView(conv[0:2])
    xbc_view = _ColumnView(conv[2:6])
    f_view = _ColumnView(raw[0:2])
    i_view = _ColumnView(raw[2:4])
    v_view = _ColumnView(raw[4:6])
    mixers = [
        _hgrn2_stages(q_ref, f_view, i_view, g_ref, lb_ref, hg_ng_ref, y_hg, hg_state, hg_c),
        _mlstm_stages(qk_view, v_view, og_ref, sm_ref, smt_ref, gbr_ref, gbc_ref, ml_ng_ref,
                      y_ml, ml_caug, ml_m),
        _ssd_stages(z_ref, xbc_view, sm_ref, dtt_ref, dtb_c_ref, alog_c_ref, dtb_r_ref, alog_r_ref,
                    d_ref, mb_ng_ref, y_mb, mb_state),
    ]

    def chunk_body(c, carry):
        live = [m(c) for m in mixers]
        while live:
            for g in list(live):
                if next(g, _DONE) is _DONE:
                    live.remove(g)
        return carry

    lax.fori_loop(0, ts // CHUNK, chunk_body, 0, unroll=CHUNK_UNROLL)


def _mixers(proj, small, small_t, dt_t, lb, hg_ng, gb_row, gb_col, ml_ng,
            dtb_c, alog_c, dtb_r, alog_r, d_x, mb_ng, batch, seq):
    n = proj.shape[0]
    nt = seq // TS_MIX
    n_pair = MB_HEADS // 2
    w = D_MODEL

    def col(cb):
        return pl.BlockSpec((TS_MIX, w), lambda b, j: (b * nt + j, cb))

    def const(shape):
        return pl.BlockSpec(shape, lambda b, j: tuple(0 for _ in shape))

    tile_specs = [col(T_HG_Q), col(T_HG_G), col(T_MB_Z), col(T_ML_O)] + [col(t) for t in TILES_MIXED]
    gate_specs = [
        pl.BlockSpec((TS_MIX, SMALL_W), lambda b, j: (b * nt + j, 0)),
        pl.BlockSpec((TS_MIX // CHUNK, 2 * ML_HEADS, CHUNK), lambda b, j: (b * nt + j, 0, 0)),
        pl.BlockSpec((TS_MIX // CHUNK, n_pair, 2 * CHUNK), lambda b, j: (b * nt + j, 0, 0)),
    ]
    param_specs = [const((1, w)), const((1, w)),
                   const((1, SMALL_W)), const((2 * ML_HEADS, 1)), const((1, w)),
                   const((1, SMALL_W)), const((1, SMALL_W)),
                   const((n_pair, 2 * CHUNK)), const((n_pair, 2 * CHUNK)),
                   const((1, w)), const((1, w))]
    out = pl.BlockSpec((TS_MIX, w), lambda b, j: (b * nt + j, 0))
    return pl.pallas_call(
        _mixers_kernel,
        grid=(batch, nt),
        in_specs=tile_specs + gate_specs + param_specs,
        out_specs=[out, out, out],
        out_shape=[jax.ShapeDtypeStruct((n, w), BF16)] * 3,
        scratch_shapes=[
            pltpu.VMEM((HG_HEADS, HG_D, HG_D), F32),
            pltpu.VMEM((HG_HEADS, CHUNK, HG_D), F32),
            pltpu.VMEM((ML_HEADS, ML_DQK, ML_DV + LANES), F32),
            pltpu.VMEM((SUBLANES, LANES), F32),
            pltpu.VMEM((n_pair, MB_N, 2 * MB_P), F32),
        ],
        compiler_params=_cparams(("parallel", "arbitrary")),
        name="token_mixers",
    )(*([proj] * (4 + len(TILES_MIXED))), small, small_t, dt_t,
      lb, hg_ng, gb_row, gb_col, ml_ng, dtb_c, alog_c, dtb_r, alog_r, d_x, mb_ng)


def _merge_kernel(x_ref, yh_ref, ym_ref, yb_ref, g0_ref, g1_ref, g2_ref,
                  wh_ref, wm_ref, wb_ref, wo_ref, o_ref):
    mixed = g0_ref[...].astype(F32) * _dot(yh_ref[...], wh_ref[...])
    mixed = mixed + g1_ref[...].astype(F32) * _dot(ym_ref[...], wm_ref[...])
    mixed = mixed + g2_ref[...].astype(F32) * _dot(yb_ref[...], wb_ref[...])
    o_ref[...] = x_ref[...] + _dot(mixed.astype(BF16), wo_ref[...])


def _merge(x2, y_hg, y_ml, y_mb, proj, w_hg, w_ml, w_mb, w_out, layer):
    n = x2.shape[0]
    tile = lambda cb: pl.BlockSpec((TM_MERGE, D_MODEL), lambda i: (i, cb))
    wspec = pl.BlockSpec((None, D_MODEL, D_MODEL), lambda i: (layer, 0, 0))
    return pl.pallas_call(
        _merge_kernel,
        grid=(n // TM_MERGE,),
        in_specs=[tile(0), tile(0), tile(0), tile(0), tile(T_GATE0), tile(T_GATE0 + 1), tile(T_GATE0 + 2),
                  wspec, wspec, wspec, wspec],
        out_specs=tile(0),
        out_shape=jax.ShapeDtypeStruct((n, D_MODEL), F32),
        compiler_params=_cparams(("parallel",)),
        name="branch_merge",
    )(x2, y_hg, y_ml, y_mb, proj, proj, proj, w_hg, w_ml, w_mb, w_out)


def _ffn_kernel(x_ref, g_ref, wu_ref, cw_ref, cb_ref, wd_ref, fg_ref,
                o_ref, tail, hbuf_g, hbuf_v, *, final):
    tm = x_ref.shape[0]
    n_ck = D_FF // FF_CHUNK

    @pl.when(pl.program_id(1) == 0)
    def _():
        tail[...] = jnp.zeros_like(tail)

    x = x_ref[...]
    hb = (x * lax.rsqrt(jnp.mean(x * x, axis=-1, keepdims=True) + EPS) * g_ref[...]).astype(BF16)

    def cols(c, half):
        return slice(half * D_FF + c * FF_CHUNK, half * D_FF + (c + 1) * FF_CHUNK)

    def up(c):
        return _dot(hb, wu_ref[:, cols(c, 0)]), _dot(hb, wu_ref[:, cols(c, 1)])

    def conv(u, hbuf, cs):
        out = cb_ref[:, cs] + cw_ref[FFN_CONV - 1:FFN_CONV, cs] * u
        for k in range(FFN_CONV - 1):
            out = out + cw_ref[k:k + 1, cs] * pltpu.roll(u, FFN_CONV - 1 - k, axis=0)
        hbuf[0:CONV_PAD, :] = tail[:, cs]
        hbuf[CONV_PAD:2 * CONV_PAD, :] = u[:CONV_PAD]
        tail[:, cs] = u[tm - CONV_PAD:]
        head = cb_ref[:, cs]
        for k in range(FFN_CONV):
            off = CONV_PAD - (FFN_CONV - 1) + k
            head = head + cw_ref[k:k + 1, cs] * hbuf[off:off + CONV_PAD, :]
        return jnp.concatenate([head, out[CONV_PAD:]], axis=0)

    acc = x
    u_next = up(0)
    for c in range(n_ck):
        u_g, u_v = u_next
        if c + 1 < n_ck:
            u_next = up(c + 1)
        a_g = conv(u_g, hbuf_g, cols(c, 0))
        a_v = conv(u_v, hbuf_v, cols(c, 1))
        acc = acc + _dot((_silu(a_g) * a_v).astype(BF16), wd_ref[cols(c, 0), :])
    if final:
        acc = acc * lax.rsqrt(jnp.mean(acc * acc, axis=-1, keepdims=True) + EPS) * fg_ref[...]
    o_ref[...] = acc


def _ffn(x2, g, w_up, cw, cb, w_down, final_g, layer, batch, seq, final):
    n = x2.shape[0]
    nt = seq // TM_FFN

    def const(shape):
        return pl.BlockSpec(shape, lambda b, j: tuple(0 for _ in shape),
                            pipeline_mode=pl.Buffered(1))

    def layer_weight(shape):
        return pl.BlockSpec((None,) + shape, lambda b, j: (layer,) + tuple(0 for _ in shape),
                            pipeline_mode=pl.Buffered(1))

    tile = pl.BlockSpec((TM_FFN, D_MODEL), lambda b, j: (b * nt + j, 0))
    return pl.pallas_call(
        functools.partial(_ffn_kernel, final=final),
        grid=(batch, nt),
        in_specs=[
            tile, const((1, D_MODEL)),
            layer_weight((D_MODEL, 2 * D_FF)), const((FFN_CONV, 2 * D_FF)), const((1, 2 * D_FF)),
            layer_weight((D_FF, D_MODEL)), const((1, D_MODEL)),
        ],
        out_specs=tile,
        out_shape=jax.ShapeDtypeStruct((n, D_MODEL), F32),
        scratch_shapes=[
            pltpu.VMEM((CONV_PAD, 2 * D_FF), F32),
            pltpu.VMEM((2 * CONV_PAD, FF_CHUNK), F32),
            pltpu.VMEM((2 * CONV_PAD, FF_CHUNK), F32),
        ],
        compiler_params=_cparams(("parallel", "arbitrary")),
        name="conv_gated_mlp",
    )(x2, g, w_up, cw, cb, w_down, final_g)


def kernel(x, norm1_g, w_in, hg_lb_logits, hg_norm_g, ml_conv_w, ml_conv_b, ml_gate_b, ml_norm_g,
           mb_conv_w, mb_conv_b, mb_dt_bias, mb_a_log, mb_d, mb_norm_g, w_br_hg, w_br_ml, w_br_mb,
           w_out, norm2_g, w_up, ffn_conv_w, ffn_conv_b, w_down, final_g):
    batch, seq, _ = x.shape
    n = batch * seq
    assert seq % TS_MIX == 0 and seq % TM_FFN == 0 and seq % TM_PROJ == 0 and n % TM_MERGE == 0
    depth = w_in.shape[0]

    w = D_MODEL
    o_hg_q, o_hg_f, o_hg_i, o_hg_g = 0, w, 2 * w, 3 * w
    o_ml_qk = 4 * w
    o_ml_v = o_ml_qk + 2 * ML_HEADS * ML_DQK
    o_if = o_ml_v + ML_HEADS * ML_DV
    o_ml_o = o_if + 2 * ML_HEADS
    o_mb_z = o_ml_o + ML_HEADS * ML_DV
    o_mb_xbc = o_mb_z + MB_W
    o_dt = o_mb_xbc + MB_CONV_DIM
    o_gate = o_dt + MB_HEADS
    hw = HALF_TILE
    conv_halves = [o_ml_qk + k * hw for k in range(2)] + [o_mb_xbc + k * hw for k in range(4)]
    raw_halves = [o_hg_f, o_hg_f + hw, o_hg_i, o_hg_i + hw, o_ml_v, o_ml_v + hw]
    cols = [(o_hg_q, w), (o_hg_g, w), (o_mb_z, w), (o_ml_o, w), (o_gate, N_GATES * w)]
    for c0, r0 in zip(conv_halves, raw_halves):
        cols += [(c0, hw), (r0, hw)]
    w_big = jnp.concatenate([w_in[:, :, a:a + n_] for a, n_ in cols], axis=-1).astype(BF16)
    pad = SMALL_W - 2 * ML_HEADS - MB_HEADS
    w_small = jnp.concatenate(
        [w_in[:, :, o_if:o_ml_o], w_in[:, :, o_dt:o_gate],
         jnp.zeros((depth, D_MODEL, pad), w_in.dtype)], axis=-1).astype(BF16)

    lbs = _lbs(hg_lb_logits.astype(F32))
    hg_ng = jnp.tile(hg_norm_g, (1, HG_HEADS))
    ml_ng = jnp.tile(ml_norm_g, (1, ML_HEADS))
    gb_row = jnp.pad(ml_gate_b, ((0, 0), (0, SMALL_W - 2 * ML_HEADS)))
    dtb_c = jnp.pad(mb_dt_bias, ((0, 0), (DT_COL, SMALL_W - DT_COL - MB_HEADS)))
    alog_c = jnp.pad(mb_a_log, ((0, 0), (DT_COL, SMALL_W - DT_COL - MB_HEADS)))
    n_pair = MB_HEADS // 2
    dtb_r = jnp.repeat(mb_dt_bias, CHUNK, axis=-1).reshape(depth, n_pair, 2 * CHUNK)
    alog_r = jnp.repeat(mb_a_log, CHUNK, axis=-1).reshape(depth, n_pair, 2 * CHUNK)
    d_x = jnp.repeat(mb_d, MB_P, axis=-1)

    n_tiles = w_big.shape[-1] // TN_PROJ
    n_mixed = len(TILES_MIXED)
    conv_w = jnp.concatenate([ml_conv_w, mb_conv_w], axis=-1).reshape(depth, CONV_W, n_mixed, hw)
    conv_b = jnp.concatenate([ml_conv_b, mb_conv_b], axis=-1).reshape(depth, 1, n_mixed, hw)
    front = n_tiles - n_mixed
    cw_all = jnp.pad(jnp.swapaxes(conv_w, 1, 2), ((0, 0), (front, 0), (0, 0), (0, 0)))
    cb_all = jnp.pad(jnp.swapaxes(conv_b, 1, 2), ((0, 0), (front, 0), (0, 0), (0, 0)))

    w_hg = w_br_hg.astype(BF16)
    w_ml = w_br_ml.astype(BF16)
    w_mb = w_br_mb.astype(BF16)
    w_o = w_out.astype(BF16)
    w_u = w_up.astype(BF16)
    w_d = w_down.astype(BF16)

    x2 = x.reshape(n, D_MODEL)
    row = lambda a: a.reshape(1, -1)
    for l in range(depth):
        proj, small = _inproj(x2, row(norm1_g[l]), w_big, w_small, cw_all, cb_all, l, seq)
        chunks = small.reshape(n // CHUNK, CHUNK, SMALL_W)
        small_t = jnp.swapaxes(chunks[:, :, :2 * ML_HEADS], 1, 2)
        dt_t = jnp.swapaxes(chunks[:, :, DT_COL:DT_COL + MB_HEADS], 1, 2).reshape(
            n // CHUNK, n_pair, 2 * CHUNK)

        y_hg, y_ml, y_mb = _mixers(
            proj, small, small_t, dt_t, row(lbs[l]), row(hg_ng[l]),
            row(gb_row[l]), ml_gate_b[l].reshape(2 * ML_HEADS, 1), row(ml_ng[l]),
            row(dtb_c[l]), row(alog_c[l]), dtb_r[l], alog_r[l], row(d_x[l]), row(mb_norm_g[l]), batch, seq)
        x2 = _merge(x2, y_hg, y_ml, y_mb, proj, w_hg, w_ml, w_mb, w_o, l)
        x2 = _ffn(x2, row(norm2_g[l]), w_u, ffn_conv_w[l], row(ffn_conv_b[l]), w_d, row(final_g),
                  l, batch, seq, final=(l == depth - 1))
    return x2.reshape(batch, seq, D_MODEL)
```

```python
import functools
import math

import jax
import jax.numpy as jnp
from jax import lax
from jax.experimental import pallas as pl
from jax.experimental.pallas import tpu as pltpu

F32 = jnp.float32
BF16 = jnp.bfloat16

D_MODEL = 1024
CHUNK = 64
CHUNK_LOG2 = 6
SUB = 8
SUBLANES = 8
LANES = 128
LOG2E = math.log2(math.e)
EPS = 1e-6
NEG_BIG = -1e30
HG_HEADS = 8
HG_D = 128
ML_HEADS = 4
ML_DQK = 128
ML_DV = 256
MB_HEADS = 16
MB_P = 64
MB_P_LOG2 = 6
MB_GROUPS = 4
MB_N = 128
MB_W = MB_HEADS * MB_P
MB_CONV_DIM = MB_W + 2 * MB_GROUPS * MB_N
D_FF = 2816
FFN_CONV = 3
FF_CHUNK = 1408
SMALL_W = 128
DT_COL = 8
CONV_PAD = 8

VMEM_LIMIT = 56 * 1024 * 1024

TM_PROJ = 2048
TN_PROJ = 1024
EPI_ROWS = 256
TILES_SILU = (0, 1, 2)
TILES_SIGMOID = (3, 4, 5, 6)
TILES_MIXED = (7, 8, 9, 10, 11, 12)
T_HG_Q, T_HG_G, T_MB_Z, T_ML_O, T_GATE0 = 0, 1, 2, 3, 4
N_GATES = 3
HALF_TILE = TN_PROJ // 2
CONV_W = 4
TS_MIX = 512
CHUNK_UNROLL = 2
TM_MERGE = 512
TM_FFN = 512


def _silu(x):
    return x * jax.nn.sigmoid(x)


def _softplus(x):
    return jnp.maximum(x, 0.0) + jnp.log1p(jnp.exp(-jnp.abs(x)))


def _log_sigmoid(x):
    return jnp.minimum(x, 0.0) - jnp.log1p(jnp.exp(-jnp.abs(x)))


def _dot(a, b):
    return jnp.dot(a, b, preferred_element_type=F32)


def _dot_nt(a, b):
    return lax.dot_general(a, b, (((1,), (1,)), ((), ())), preferred_element_type=F32)


def _dot_tn(a, b):
    return lax.dot_general(a, b, (((0,), (0,)), ((), ())), preferred_element_type=F32)


def _split3(a):
    hi = a.astype(BF16)
    r = a - hi.astype(F32)
    mid = r.astype(BF16)
    lo = (r - mid.astype(F32)).astype(BF16)
    return hi, mid, lo


def _sel_left(sel, a):
    hi, mid, lo = _split3(a)
    return _dot(sel, hi) + (_dot(sel, mid) + _dot(sel, lo))


def _sel_right(a, sel):
    hi, mid, lo = _split3(a)
    return _dot(hi, sel) + (_dot(mid, sel) + _dot(lo, sel))


def _tri_lower(n):
    r = lax.broadcasted_iota(jnp.int32, (n, n), 0)
    c = lax.broadcasted_iota(jnp.int32, (n, n), 1)
    return r >= c


def _bcast_row(ref, h, r):
    return ref[h, pl.ds(r, SUBLANES, stride=0), :]


def _mask_bf16(m):
    return jnp.where(m, 1.0, 0.0).astype(BF16)


def _cparams(sem):
    return pltpu.CompilerParams(dimension_semantics=sem, vmem_limit_bytes=VMEM_LIMIT)


def _lbs_kernel(lg_ref, o_ref):
    lg = lg_ref[...]
    mx = jnp.max(lg, axis=0, keepdims=True)
    e = jnp.exp(lg - mx)
    p = e / jnp.sum(e, axis=0, keepdims=True)
    acc = jnp.zeros_like(p[0:1])
    rows = []
    for l in range(lg.shape[0]):
        acc = acc + p[l:l + 1]
        rows.append(acc - p[0:1])
    o_ref[...] = jnp.concatenate(rows, axis=0)


def _lbs(logits):
    return pl.pallas_call(
        _lbs_kernel,
        out_shape=jax.ShapeDtypeStruct(logits.shape, F32),
        name="hgrn2_lower_bounds",
    )(logits)


def _any_tile(j, tiles):
    hit = j == tiles[0]
    for t in tiles[1:]:
        hit = jnp.logical_or(hit, j == t)
    return hit


def _inproj_kernel(x_ref, g_ref, w_ref, ws_ref, cw_ref, cb_ref, o_ref, os_ref, h_ref, tail_ref, hbuf,
                   *, tiles_per_seq):
    i = pl.program_id(0)
    j = pl.program_id(1)
    n_blk = TM_PROJ // EPI_ROWS

    @pl.when(jnp.logical_and(i == 0, j == 0))
    def _():
        tail_ref[...] = jnp.zeros_like(tail_ref)

    @pl.when(j == 0)
    def _():
        x = x_ref[...]
        ms = jnp.mean(x * x, axis=-1, keepdims=True)
        hb = (x * lax.rsqrt(ms + EPS) * g_ref[...]).astype(BF16)
        h_ref[...] = hb
        os_ref[...] = _dot(hb, ws_ref[...])

    def pointwise(fn):
        ys = [_dot(h_ref[r * EPI_ROWS:(r + 1) * EPI_ROWS, :], w_ref[...]) for r in range(n_blk)]
        for r in range(n_blk):
            o_ref[r * EPI_ROWS:(r + 1) * EPI_ROWS, :] = fn(ys[r]).astype(o_ref.dtype)

    @pl.when(_any_tile(j, TILES_SILU))
    def _():
        pointwise(_silu)

    @pl.when(_any_tile(j, TILES_SIGMOID))
    def _():
        pointwise(jax.nn.sigmoid)

    @pl.when(j >= TILES_MIXED[0])
    def _():
        slot = j - TILES_MIXED[0]
        prev = jnp.where(i % tiles_per_seq == 0, 0.0, tail_ref[slot])
        cw = cw_ref[0]
        cb = cb_ref[0]
        ys = [_dot(h_ref[r * EPI_ROWS:(r + 1) * EPI_ROWS, :], w_ref[...]) for r in range(n_blk)]
        for r in range(n_blk):
            yb = ys[r][:, :HALF_TILE]
            out = cb + cw[CONV_W - 1:CONV_W] * yb
            for k in range(CONV_W - 1):
                out = out + cw[k:k + 1] * pltpu.roll(yb, CONV_W - 1 - k, axis=0)
            hbuf[0:CONV_PAD, :] = prev
            hbuf[CONV_PAD:2 * CONV_PAD, :] = yb[:CONV_PAD]
            head = cb
            for k in range(CONV_W):
                off = CONV_PAD - (CONV_W - 1) + k
                head = head + cw[k:k + 1] * hbuf[off:off + CONV_PAD, :]
            prev = yb[EPI_ROWS - CONV_PAD:]
            act = _silu(jnp.concatenate([head, out[CONV_PAD:]], axis=0))
            o_ref[r * EPI_ROWS:(r + 1) * EPI_ROWS, :] = jnp.concatenate(
                [act, ys[r][:, HALF_TILE:]], axis=-1).astype(o_ref.dtype)
        tail_ref[slot] = prev


def _inproj(x2, g, w_big, w_small, cw_all, cb_all, layer, seq):
    n = x2.shape[0]
    nb = w_big.shape[-1]
    n_tiles = nb // TN_PROJ
    assert n_tiles == len(TILES_SILU + TILES_SIGMOID + TILES_MIXED)
    return pl.pallas_call(
        functools.partial(_inproj_kernel, tiles_per_seq=seq // TM_PROJ),
        grid=(n // TM_PROJ, n_tiles),
        in_specs=[
            pl.BlockSpec((TM_PROJ, D_MODEL), lambda i, j: (i, 0)),
            pl.BlockSpec((1, D_MODEL), lambda i, j: (0, 0)),
            pl.BlockSpec((None, D_MODEL, TN_PROJ), lambda i, j: (layer, 0, j)),
            pl.BlockSpec((None, D_MODEL, SMALL_W), lambda i, j: (layer, 0, 0)),
            pl.BlockSpec((None, 1, CONV_W, HALF_TILE), lambda i, j: (layer, j, 0, 0)),
            pl.BlockSpec((None, 1, 1, HALF_TILE), lambda i, j: (layer, j, 0, 0)),
        ],
        out_specs=[
            pl.BlockSpec((TM_PROJ, TN_PROJ), lambda i, j: (i, j)),
            pl.BlockSpec((TM_PROJ, SMALL_W), lambda i, j: (i, 0)),
        ],
        out_shape=[
            jax.ShapeDtypeStruct((n, nb), BF16),
            jax.ShapeDtypeStruct((n, SMALL_W), F32),
        ],
        scratch_shapes=[
            pltpu.VMEM((TM_PROJ, D_MODEL), BF16),
            pltpu.VMEM((len(TILES_MIXED), CONV_PAD, HALF_TILE), F32),
            pltpu.VMEM((2 * CONV_PAD, HALF_TILE), F32),
        ],
        compiler_params=_cparams(("arbitrary", "arbitrary")),
        name="in_projection",
    )(x2, g, w_big, w_small, cw_all, cb_all)


def _hgrn2_stages(q_ref, f_ref, i_ref, g_ref, lb_ref, ng_ref, y_ref, state_ref, c_s):
    @pl.when(pl.program_id(1) == 0)
    def _():
        state_ref[...] = jnp.zeros_like(state_ref)

    lb = lb_ref[...]
    tri = _mask_bf16(_tri_lower(CHUNK))
    n_sub = CHUNK // SUB
    ones_rhs = jnp.ones((HG_D, HG_D), BF16)
    lane = lax.broadcasted_iota(jnp.int32, (SUBLANES, HG_D), 1)
    row = lax.broadcasted_iota(jnp.int32, (SUBLANES, HG_D), 0)

    heads = range(HG_HEADS)
    hsl = [slice(h * HG_D, (h + 1) * HG_D) for h in heads]
    zero_tail = jnp.zeros((HG_D - CHUNK, HG_D), BF16)

    def chunk(c):
        rows = pl.ds(pl.multiple_of(c * CHUNK, CHUNK), CHUNK)
        sig = jax.nn.sigmoid(f_ref[rows, :].astype(F32))
        fgate = lb + (1.0 - lb) * sig
        b = _sel_left(tri, jnp.log(fgate) * LOG2E)
        cc = b - jnp.log(jnp.maximum(1.0 - fgate, 0.0)) * LOG2E
        for h in heads:
            c_s[h] = cc[:, hsl[h]]
        q = q_ref[rows, :].astype(F32)
        v16 = i_ref[rows, :]
        yield

        st = [state_ref[h] for h in heads]
        q_dec = (q * jnp.exp2(b)).astype(BF16)
        b_last = b[CHUNK - 1:CHUNK, :]
        k_dec = jnp.exp2(b_last - cc).astype(BF16)
        st_decay = jnp.exp2(b_last)
        o_inter = [_dot_nt(q_dec[:, hsl[h]], st[h].astype(BF16)) for h in heads]
        for h in heads:
            state_ref[h] = st_decay[:, hsl[h]] * st[h] + _dot_tn(v16[:, hsl[h]], k_dec[:, hsl[h]])
        yield

        zs = []
        for i in range(n_sub):
            lo = i * SUB
            for s in range(SUB):
                r0 = lo + (s // SUBLANES) * SUBLANES
                c_row = jnp.concatenate([_bcast_row(c_s, h, lo + s) for h in heads], axis=-1)
                c_row = jnp.concatenate([c_row] * ((lo + SUB - r0) // SUBLANES), axis=0)
                zs.append(q[r0:lo + SUB] * jnp.exp2(b[r0:lo + SUB] - c_row))
            yield
        z_rows = sum(z.shape[0] for z in zs)
        z_all = jnp.concatenate([z[:, hsl[h]] for h in heads for z in zs], axis=0)
        r = _dot(z_all.astype(BF16), ones_rhs)
        yield

        a_off = []
        for i in range(1, n_sub):
            lo = i * SUB
            bref = b[lo - 1:lo, :]
            q_i = (q[lo:lo + SUB] * jnp.exp2(b[lo:lo + SUB] - bref)).astype(BF16)
            k_i = jnp.exp2(bref - cc[:lo]).astype(BF16)
            zero_rows = jnp.zeros((HG_D - lo, HG_D), BF16)
            a_off.append([_dot_nt(q_i[:, hsl[h]], jnp.concatenate([k_i[:, hsl[h]], zero_rows], axis=0))
                          for h in heads])
        yield

        outs = []
        for h in heads:
            a_rows = []
            off = h * z_rows
            for i in range(n_sub):
                lo = i * SUB
                a_i = a_off[i - 1][h] if i > 0 else jnp.zeros((SUB, HG_D), F32)
                tiles = [a_i[j * SUBLANES:(j + 1) * SUBLANES] for j in range(SUB // SUBLANES)]
                for s in range(SUB):
                    for j in range(s // SUBLANES, SUB // SUBLANES):
                        tiles[j] = jnp.where(lane == lo + s, r[off:off + SUBLANES], tiles[j])
                        off += SUBLANES
                for j in range(SUB // SUBLANES):
                    a_rows.append(jnp.where(lane - lo <= row + j * SUBLANES, tiles[j], 0.0))
            a_full = jnp.concatenate(a_rows, axis=0).astype(BF16)
            v_pad = jnp.concatenate([v16[:, hsl[h]], zero_tail], axis=0)
            outs.append(o_inter[h] + _dot(a_full, v_pad))
            if h % 2 == 1:
                yield
        parts = []
        for o in outs:
            parts.append(o * lax.rsqrt(jnp.mean(o * o, axis=-1, keepdims=True) + EPS))
        on = jnp.concatenate(parts, axis=-1) * ng_ref[...]
        y_ref[rows, :] = (on * g_ref[rows, :].astype(F32)).astype(y_ref.dtype)

    return chunk


def _mlstm_stages(qk_ref, v_ref, og_ref, sm_ref, smt_ref, gbr_ref, gbc_ref, ng_ref,
                  y_ref, caug_ref, m_ref):
    half = ML_HEADS * ML_DQK
    k_scale = ML_DQK ** -0.5

    @pl.when(pl.program_id(1) == 0)
    def _():
        caug_ref[...] = jnp.zeros_like(caug_ref)
        m_ref[...] = jnp.zeros_like(m_ref)

    tri_b = _tri_lower(CHUNK)
    tri_l = _mask_bf16(tri_b)
    tri_u = _mask_bf16(lax.broadcasted_iota(jnp.int32, (CHUNK, CHUNK), 0)
                       <= lax.broadcasted_iota(jnp.int32, (CHUNK, CHUNK), 1))
    ones_col = _mask_bf16(lax.broadcasted_iota(jnp.int32, (CHUNK, LANES), 1) == 0)

    def chunk(c):
        rows = pl.ds(pl.multiple_of(c * CHUNK, CHUNK), CHUNK)
        pre_c = sm_ref[rows, :] + gbr_ref[...]
        pre_r = smt_ref[c] + gbc_ref[...]
        cum_c = _sel_left(tri_l, _log_sigmoid(pre_c))
        cum_r = _sel_right(_log_sigmoid(pre_r), tri_u)
        yield
        heads = range(ML_HEADS)
        m_all = m_ref[...]
        m_old = [m_all[h:h + 1, 0:1] for h in heads]
        b_col = [cum_c[:, ML_HEADS + h:ML_HEADS + h + 1] for h in heads]
        i_col = [pre_c[:, h:h + 1] for h in heads]
        log_d = [jnp.where(tri_b, b_col[h] - cum_r[ML_HEADS + h:ML_HEADS + h + 1, :] + pre_r[h:h + 1, :],
                           NEG_BIG) for h in heads]
        log_inter = [b_col[h] + m_old[h] for h in heads]
        m_t = [jnp.maximum(jnp.max(log_d[h], axis=-1, keepdims=True), log_inter[h]) for h in heads]
        b_last = [b_col[h][CHUNK - 1:CHUNK, :] for h in heads]
        log_w = [b_last[h] - b_col[h] + i_col[h] for h in heads]
        m_new = [jnp.maximum(b_last[h] + m_old[h], jnp.max(log_w[h], axis=0, keepdims=True)) for h in heads]
        m_ref[...] = jnp.concatenate(
            [jnp.broadcast_to(m_new[h], (1, m_ref.shape[1])) for h in heads] + [m_all[ML_HEADS:]], axis=0)
        yield

        qb = [qk_ref[rows, h * ML_DQK:(h + 1) * ML_DQK] for h in heads]
        kb = [qk_ref[rows, half + h * ML_DQK:half + (h + 1) * ML_DQK] for h in heads]
        v_aug = [jnp.concatenate([v_ref[rows, h * ML_DV:(h + 1) * ML_DV], ones_col], axis=-1)
                 for h in heads]
        qk = [_dot_nt(qb[h], kb[h]) for h in heads]
        c_aug = [caug_ref[h] for h in heads]
        inter = [_dot(qb[h], c_aug[h].astype(BF16)) * jnp.exp(log_inter[h] - m_t[h]) for h in heads]
        yield
        s = [(qk[h] * (jnp.exp(log_d[h] - m_t[h]) * k_scale)).astype(BF16) for h in heads]
        num = [_dot(s[h], v_aug[h]) + inter[h] for h in heads]
        yield
        kw = [(kb[h].astype(F32) * (jnp.exp(log_w[h] - m_new[h]) * k_scale)).astype(BF16) for h in heads]
        for h in heads:
            caug_ref[h] = jnp.exp(b_last[h] + m_old[h] - m_new[h]) * c_aug[h] + _dot_tn(kw[h], v_aug[h])
        yield

        parts = []
        for h in heads:
            denom = jnp.maximum(jnp.abs(num[h][:, ML_DV:ML_DV + 1]), jnp.exp(-m_t[h]))
            o = num[h][:, :ML_DV] / denom
            parts.append(o * lax.rsqrt(jnp.mean(o * o, axis=-1, keepdims=True) + EPS))
        on = jnp.concatenate(parts, axis=-1) * ng_ref[...]
        y_ref[rows, :] = (on * og_ref[rows, :].astype(F32)).astype(y_ref.dtype)

    return chunk


def _ssd_stages(z_ref, xbc_ref, sm_ref, dtt_ref, dtb_c_ref, alog_c_ref,
                dtb_r_ref, alog_r_ref, d_ref, ng_ref, y_ref, state_ref):
    n_pair = MB_HEADS // 2
    pair_w = 2 * MB_P
    gw = MB_GROUPS * MB_N

    @pl.when(pl.program_id(1) == 0)
    def _():
        state_ref[...] = jnp.zeros_like(state_ref)

    tri_l = _mask_bf16(_tri_lower(CHUNK))
    sel_x = _mask_bf16(lax.broadcasted_iota(jnp.int32, (SMALL_W, MB_W), 0) - DT_COL
                       == jnp.right_shift(lax.broadcasted_iota(jnp.int32, (SMALL_W, MB_W), 1), MB_P_LOG2))
    ur = lax.broadcasted_iota(jnp.int32, (pair_w, pair_w), 0)
    uc = lax.broadcasted_iota(jnp.int32, (pair_w, pair_w), 1)
    same_half = jnp.right_shift(ur, CHUNK_LOG2) == jnp.right_shift(uc, CHUNK_LOG2)
    tri_u2 = _mask_bf16(same_half & (ur <= uc))
    causal = (lax.broadcasted_iota(jnp.int32, (CHUNK, MB_W), 0)
              >= jnp.bitwise_and(lax.broadcasted_iota(jnp.int32, (CHUNK, MB_W), 1), CHUNK - 1))
    first_head = lax.broadcasted_iota(jnp.int32, (CHUNK, pair_w), 1) < MB_P
    a_c = -jnp.exp(alog_c_ref[...])
    a_r = -jnp.exp(alog_r_ref[...])

    def chunk(c):
        rows = pl.ds(pl.multiple_of(c * CHUNK, CHUNK), CHUNK)
        dt_c = _softplus(sm_ref[rows, :] + dtb_c_ref[...])
        cum_c = _sel_left(tri_l, dt_c * a_c)
        dt_x = _sel_right(dt_c, sel_x)
        cum_x = _sel_right(cum_c, sel_x)
        dt_r = _softplus(dtt_ref[c] + dtb_r_ref[...])
        cum_r = _sel_right(dt_r * a_r, tri_u2)
        yield
        pairs = range(n_pair)
        grp = [(2 * p) // (MB_HEADS // MB_GROUPS) for p in pairs]
        lanes = [slice(p * pair_w, (p + 1) * pair_w) for p in pairs]
        xs = xbc_ref[rows, :MB_W].astype(F32)
        xdt = xs * dt_x
        cum_row = jnp.concatenate([cum_r[p:p + 1, :] for p in pairs], axis=-1)
        decay = jnp.exp(jnp.where(causal, cum_x - cum_row, NEG_BIG))
        cum_last = cum_x[CHUNK - 1:CHUNK, :]
        x_dec = (xdt * jnp.exp(cum_last - cum_x)).astype(BF16)
        st_decay = jnp.exp(cum_last)
        carry_w = jnp.exp(cum_x)
        yield
        bm = [xbc_ref[rows, MB_W + g * MB_N:MB_W + (g + 1) * MB_N] for g in range(MB_GROUPS)]
        cm = [xbc_ref[rows, MB_W + gw + g * MB_N:MB_W + gw + (g + 1) * MB_N] for g in range(MB_GROUPS)]
        cb2 = [_dot_nt(cm[g], jnp.concatenate([bm[g], bm[g]], axis=0)) for g in range(MB_GROUPS)]
        x2 = [jnp.concatenate([jnp.where(first_head, xdt[:, lanes[p]], 0.0),
                               jnp.where(first_head, 0.0, xdt[:, lanes[p]])], axis=0).astype(BF16)
              for p in pairs]
        yield
        st = [state_ref[p] for p in pairs]
        y_inter = [_dot(cm[grp[p]], st[p].astype(BF16)) for p in pairs]
        y_intra = [_dot((cb2[grp[p]] * decay[:, lanes[p]]).astype(BF16), x2[p]) for p in pairs]
        yield
        for p in pairs:
            state_ref[p] = st_decay[:, lanes[p]] * st[p] + _dot_tn(bm[grp[p]], x_dec[:, lanes[p]])
        yield
        y = (jnp.concatenate(y_intra, axis=-1) + jnp.concatenate(y_inter, axis=-1) * carry_w
             + d_ref[...] * xs)

        yz = y * z_ref[rows, :].astype(F32)
        gsz = MB_W // MB_GROUPS
        parts = []
        for g in range(MB_GROUPS):
            o = yz[:, g * gsz:(g + 1) * gsz]
            parts.append(o * lax.rsqrt(jnp.mean(o * o, axis=-1, keepdims=True) + EPS))
        y_ref[rows, :] = (jnp.concatenate(parts, axis=-1) * ng_ref[...]).astype(y_ref.dtype)

    return chunk


_DONE = object()


class _ColumnView:
    def __init__(self, pieces):
        self.pieces = pieces
        self.width = sum(w for _, _, w in pieces)

    def __getitem__(self, idx):
        rows, cols = idx
        lo, hi, _ = cols.indices(self.width)
        out, base = [], 0
        for ref, start, w in self.pieces:
            a, b = max(lo, base), min(hi, base + w)
            if a < b:
                out.append(ref[rows, start + a - base:start + b - base])
            base += w
        return out[0] if len(out) == 1 else jnp.concatenate(out, axis=-1)


def _mixers_kernel(q_ref, g_ref, z_ref, og_ref, m0, m1, m2, m3, m4, m5, sm_ref, smt_ref, dtt_ref,
                   lb_ref, hg_ng_ref, gbr_ref, gbc_ref, ml_ng_ref,
                   dtb_c_ref, alog_c_ref, dtb_r_ref, alog_r_ref, d_ref, mb_ng_ref,
                   y_hg, y_ml, y_mb, hg_state, hg_c, ml_caug, ml_m, mb_state):
    ts = y_hg.shape[0]
    mixed = (m0, m1, m2, m3, m4, m5)
    conv = [(m, 0, HALF_TILE) for m in mixed]
    raw = [(m, HALF_TILE, HALF_TILE) for m in mixed]
    qk_view = _ColumnView(conv[0:2])
    xbc_view = _ColumnView(conv[2:6])
    f_view = _ColumnView(raw[0:2])
    i_view = _ColumnView(raw[2:4])
    v_view = _ColumnView(raw[4:6])
    mixers = [
        _hgrn2_stages(q_ref, f_view, i_view, g_ref, lb_ref, hg_ng_ref, y_hg, hg_state, hg_c),
        _mlstm_stages(qk_view, v_view, og_ref, sm_ref, smt_ref, gbr_ref, gbc_ref, ml_ng_ref,
                      y_ml, ml_caug, ml_m),
        _ssd_stages(z_ref, xbc_view, sm_ref, dtt_ref, dtb_c_ref, alog_c_ref, dtb_r_ref, alog_r_ref,
                    d_ref, mb_ng_ref, y_mb, mb_state),
    ]

    def chunk_body(c, carry):
        live = [m(c) for m in mixers]
        while live:
            for g in list(live):
                if next(g, _DONE) is _DONE:
                    live.remove(g)
        return carry

    lax.fori_loop(0, ts // CHUNK, chunk_body, 0, unroll=CHUNK_UNROLL)


def _mixers(proj, small, small_t, dt_t, lb, hg_ng, gb_row, gb_col, ml_ng,
            dtb_c, alog_c, dtb_r, alog_r, d_x, mb_ng, batch, seq):
    n = proj.shape[0]
    nt = seq // TS_MIX
    n_pair = MB_HEADS // 2
    w = D_MODEL

    def col(cb):
        return pl.BlockSpec((TS_MIX, w), lambda b, j: (b * nt + j, cb))

    def const(shape):
        return pl.BlockSpec(shape, lambda b, j: tuple(0 for _ in shape))

    tile_specs = [col(T_HG_Q), col(T_HG_G), col(T_MB_Z), col(T_ML_O)] + [col(t) for t in TILES_MIXED]
    gate_specs = [
        pl.BlockSpec((TS_MIX, SMALL_W), lambda b, j: (b * nt + j, 0)),
        pl.BlockSpec((TS_MIX // CHUNK, 2 * ML_HEADS, CHUNK), lambda b, j: (b * nt + j, 0, 0)),
        pl.BlockSpec((TS_MIX // CHUNK, n_pair, 2 * CHUNK), lambda b, j: (b * nt + j, 0, 0)),
    ]
    param_specs = [const((1, w)), const((1, w)),
                   const((1, SMALL_W)), const((2 * ML_HEADS, 1)), const((1, w)),
                   const((1, SMALL_W)), const((1, SMALL_W)),
                   const((n_pair, 2 * CHUNK)), const((n_pair, 2 * CHUNK)),
                   const((1, w)), const((1, w))]
    out = pl.BlockSpec((TS_MIX, w), lambda b, j: (b * nt + j, 0))
    return pl.pallas_call(
        _mixers_kernel,
        grid=(batch, nt),
        in_specs=tile_specs + gate_specs + param_specs,
        out_specs=[out, out, out],
        out_shape=[jax.ShapeDtypeStruct((n, w), BF16)] * 3,
        scratch_shapes=[
            pltpu.VMEM((HG_HEADS, HG_D, HG_D), F32),
            pltpu.VMEM((HG_HEADS, CHUNK, HG_D), F32),
            pltpu.VMEM((ML_HEADS, ML_DQK, ML_DV + LANES), F32),
            pltpu.VMEM((SUBLANES, LANES), F32),
            pltpu.VMEM((n_pair, MB_N, 2 * MB_P), F32),
        ],
        compiler_params=_cparams(("parallel", "arbitrary")),
        name="token_mixers",
    )(*([proj] * (4 + len(TILES_MIXED))), small, small_t, dt_t,
      lb, hg_ng, gb_row, gb_col, ml_ng, dtb_c, alog_c, dtb_r, alog_r, d_x, mb_ng)


def _merge_kernel(x_ref, yh_ref, ym_ref, yb_ref, g0_ref, g1_ref, g2_ref,
                  wh_ref, wm_ref, wb_ref, wo_ref, o_ref):
    mixed = g0_ref[...].astype(F32) * _dot(yh_ref[...], wh_ref[...])
    mixed = mixed + g1_ref[...].astype(F32) * _dot(ym_ref[...], wm_ref[...])
    mixed = mixed + g2_ref[...].astype(F32) * _dot(yb_ref[...], wb_ref[...])
    o_ref[...] = x_ref[...] + _dot(mixed.astype(BF16), wo_ref[...])


def _merge(x2, y_hg, y_ml, y_mb, proj, w_hg, w_ml, w_mb, w_out, layer):
    n = x2.shape[0]
    tile = lambda cb: pl.BlockSpec((TM_MERGE, D_MODEL), lambda i: (i, cb))
    wspec = pl.BlockSpec((None, D_MODEL, D_MODEL), lambda i: (layer, 0, 0))
    return pl.pallas_call(
        _merge_kernel,
        grid=(n // TM_MERGE,),
        in_specs=[tile(0), tile(0), tile(0), tile(0), tile(T_GATE0), tile(T_GATE0 + 1), tile(T_GATE0 + 2),
                  wspec, wspec, wspec, wspec],
        out_specs=tile(0),
        out_shape=jax.ShapeDtypeStruct((n, D_MODEL), F32),
        compiler_params=_cparams(("parallel",)),
        name="branch_merge",
    )(x2, y_hg, y_ml, y_mb, proj, proj, proj, w_hg, w_ml, w_mb, w_out)


def _ffn_kernel(x_ref, g_ref, wu_ref, cw_ref, cb_ref, wd_ref, fg_ref,
                o_ref, tail, hbuf_g, hbuf_v, *, final):
    tm = x_ref.shape[0]
    n_ck = D_FF // FF_CHUNK

    @pl.when(pl.program_id(1) == 0)
    def _():
        tail[...] = jnp.zeros_like(tail)

    x = x_ref[...]
    hb = (x * lax.rsqrt(jnp.mean(x * x, axis=-1, keepdims=True) + EPS) * g_ref[...]).astype(BF16)

    def cols(c, half):
        return slice(half * D_FF + c * FF_CHUNK, half * D_FF + (c + 1) * FF_CHUNK)

    def up(c):
        return _dot(hb, wu_ref[:, cols(c, 0)]), _dot(hb, wu_ref[:, cols(c, 1)])

    def conv(u, hbuf, cs):
        out = cb_ref[:, cs] + cw_ref[FFN_CONV - 1:FFN_CONV, cs] * u
        for k in range(FFN_CONV - 1):
            out = out + cw_ref[k:k + 1, cs] * pltpu.roll(u, FFN_CONV - 1 - k, axis=0)
        hbuf[0:CONV_PAD, :] = tail[:, cs]
        hbuf[CONV_PAD:2 * CONV_PAD, :] = u[:CONV_PAD]
        tail[:, cs] = u[tm - CONV_PAD:]
        head = cb_ref[:, cs]
        for k in range(FFN_CONV):
            off = CONV_PAD - (FFN_CONV - 1) + k
            head = head + cw_ref[k:k + 1, cs] * hbuf[off:off + CONV_PAD, :]
        return jnp.concatenate([head, out[CONV_PAD:]], axis=0)

    acc = x
    u_next = up(0)
    for c in range(n_ck):
        u_g, u_v = u_next
        if c + 1 < n_ck:
            u_next = up(c + 1)
        a_g = conv(u_g, hbuf_g, cols(c, 0))
        a_v = conv(u_v, hbuf_v, cols(c, 1))
        acc = acc + _dot((_silu(a_g) * a_v).astype(BF16), wd_ref[cols(c, 0), :])
    if final:
        acc = acc * lax.rsqrt(jnp.mean(acc * acc, axis=-1, keepdims=True) + EPS) * fg_ref[...]
    o_ref[...] = acc


def _ffn(x2, g, w_up, cw, cb, w_down, final_g, layer, batch, seq, final):
    n = x2.shape[0]
    nt = seq // TM_FFN

    def const(shape):
        return pl.BlockSpec(shape, lambda b, j: tuple(0 for _ in shape),
                            pipeline_mode=pl.Buffered(1))

    def layer_weight(shape):
        return pl.BlockSpec((None,) + shape, lambda b, j: (layer,) + tuple(0 for _ in shape),
                            pipeline_mode=pl.Buffered(1))

    tile = pl.BlockSpec((TM_FFN, D_MODEL), lambda b, j: (b * nt + j, 0))
    return pl.pallas_call(
        functools.partial(_ffn_kernel, final=final),
        grid=(batch, nt),
        in_specs=[
            tile, const((1, D_MODEL)),
            layer_weight((D_MODEL, 2 * D_FF)), const((FFN_CONV, 2 * D_FF)), const((1, 2 * D_FF)),
            layer_weight((D_FF, D_MODEL)), const((1, D_MODEL)),
        ],
        out_specs=tile,
        out_shape=jax.ShapeDtypeStruct((n, D_MODEL), F32),
        scratch_shapes=[
            pltpu.VMEM((CONV_PAD, 2 * D_FF), F32),
            pltpu.VMEM((2 * CONV_PAD, FF_CHUNK), F32),
            pltpu.VMEM((2 * CONV_PAD, FF_CHUNK), F32),
        ],
        compiler_params=_cparams(("parallel", "arbitrary")),
        name="conv_gated_mlp",
    )(x2, g, w_up, cw, cb, w_down, final_g)


def kernel(x, norm1_g, w_in, hg_lb_logits, hg_norm_g, ml_conv_w, ml_conv_b, ml_gate_b, ml_norm_g,
           mb_conv_w, mb_conv_b, mb_dt_bias, mb_a_log, mb_d, mb_norm_g, w_br_hg, w_br_ml, w_br_mb,
           w_out, norm2_g, w_up, ffn_conv_w, ffn_conv_b, w_down, final_g):
    batch, seq, _ = x.shape
    n = batch * seq
    assert seq % TS_MIX == 0 and seq % TM_FFN == 0 and seq % TM_PROJ == 0 and n % TM_MERGE == 0
    depth = w_in.shape[0]

    w = D_MODEL
    o_hg_q, o_hg_f, o_hg_i, o_hg_g = 0, w, 2 * w, 3 * w
    o_ml_qk = 4 * w
    o_ml_v = o_ml_qk + 2 * ML_HEADS * ML_DQK
    o_if = o_ml_v + ML_HEADS * ML_DV
    o_ml_o = o_if + 2 * ML_HEADS
    o_mb_z = o_ml_o + ML_HEADS * ML_DV
    o_mb_xbc = o_mb_z + MB_W
    o_dt = o_mb_xbc + MB_CONV_DIM
    o_gate = o_dt + MB_HEADS
    hw = HALF_TILE
    conv_halves = [o_ml_qk + k * hw for k in range(2)] + [o_mb_xbc + k * hw for k in range(4)]
    raw_halves = [o_hg_f, o_hg_f + hw, o_hg_i, o_hg_i + hw, o_ml_v, o_ml_v + hw]
    cols = [(o_hg_q, w), (o_hg_g, w), (o_mb_z, w), (o_ml_o, w), (o_gate, N_GATES * w)]
    for c0, r0 in zip(conv_halves, raw_halves):
        cols += [(c0, hw), (r0, hw)]
    w_in16 = w_in.astype(BF16)
    w_big = jnp.concatenate([w_in16[:, :, a:a + n_] for a, n_ in cols], axis=-1)
    pad = SMALL_W - 2 * ML_HEADS - MB_HEADS
    w_small = jnp.concatenate(
        [w_in16[:, :, o_if:o_ml_o], w_in16[:, :, o_dt:o_gate],
         jnp.zeros((depth, D_MODEL, pad), BF16)], axis=-1)

    lbs = _lbs(hg_lb_logits.astype(F32))
    hg_ng = jnp.tile(hg_norm_g, (1, HG_HEADS))
    ml_ng = jnp.tile(ml_norm_g, (1, ML_HEADS))
    gb_row = jnp.pad(ml_gate_b, ((0, 0), (0, SMALL_W - 2 * ML_HEADS)))
    dtb_c = jnp.pad(mb_dt_bias, ((0, 0), (DT_COL, SMALL_W - DT_COL - MB_HEADS)))
    alog_c = jnp.pad(mb_a_log, ((0, 0), (DT_COL, SMALL_W - DT_COL - MB_HEADS)))
    n_pair = MB_HEADS // 2
    dtb_r = jnp.repeat(mb_dt_bias, CHUNK, axis=-1).reshape(depth, n_pair, 2 * CHUNK)
    alog_r = jnp.repeat(mb_a_log, CHUNK, axis=-1).reshape(depth, n_pair, 2 * CHUNK)
    d_x = jnp.repeat(mb_d, MB_P, axis=-1)

    n_tiles = w_big.shape[-1] // TN_PROJ
    n_mixed = len(TILES_MIXED)
    conv_w = jnp.concatenate([ml_conv_w, mb_conv_w], axis=-1).reshape(depth, CONV_W, n_mixed, hw)
    conv_b = jnp.concatenate([ml_conv_b, mb_conv_b], axis=-1).reshape(depth, 1, n_mixed, hw)
    front = n_tiles - n_mixed
    cw_all = jnp.pad(jnp.swapaxes(conv_w, 1, 2), ((0, 0), (front, 0), (0, 0), (0, 0)))
    cb_all = jnp.pad(jnp.swapaxes(conv_b, 1, 2), ((0, 0), (front, 0), (0, 0), (0, 0)))

    w_hg = w_br_hg.astype(BF16)
    w_ml = w_br_ml.astype(BF16)
    w_mb = w_br_mb.astype(BF16)
    w_o = w_out.astype(BF16)
    w_u = w_up.astype(BF16)
    w_d = w_down.astype(BF16)

    x2 = x.reshape(n, D_MODEL)
    row = lambda a: a.reshape(1, -1)
    for l in range(depth):
        proj, small = _inproj(x2, row(norm1_g[l]), w_big, w_small, cw_all, cb_all, l, seq)
        chunks = small.reshape(n // CHUNK, CHUNK, SMALL_W)
        small_t = jnp.swapaxes(chunks[:, :, :2 * ML_HEADS], 1, 2)
        dt_t = jnp.swapaxes(chunks[:, :, DT_COL:DT_COL + MB_HEADS], 1, 2).reshape(
            n // CHUNK, n_pair, 2 * CHUNK)

        y_hg, y_ml, y_mb = _mixers(
            proj, small, small_t, dt_t, row(lbs[l]), row(hg_ng[l]),
            row(gb_row[l]), ml_gate_b[l].reshape(2 * ML_HEADS, 1), row(ml_ng[l]),
            row(dtb_c[l]), row(alog_c[l]), dtb_r[l], alog_r[l], row(d_x[l]), row(mb_norm_g[l]), batch, seq)
        x2 = _merge(x2, y_hg, y_ml, y_mb, proj, w_hg, w_ml, w_mb, w_o, l)
        x2 = _ffn(x2, row(norm2_g[l]), w_u, ffn_conv_w[l], row(ffn_conv_b[l]), w_d, row(final_g),
                  l, batch, seq, final=(l == depth - 1))
    return x2.reshape(batch, seq, D_MODEL)
```

```python
import functools
import math

import jax
import jax.numpy as jnp
from jax import lax
from jax.experimental import pallas as pl
from jax.experimental.pallas import tpu as pltpu

F32 = jnp.float32
BF16 = jnp.bfloat16

D_MODEL = 1024
CHUNK = 64
CHUNK_LOG2 = 6
SUB = 8
FACTORED_DECAY_LIMIT = 112.0
SUBLANES = 8
LANES = 128
LOG2E = math.log2(math.e)
EPS = 1e-6
NEG_BIG = -1e30
HG_HEADS = 8
HG_D = 128
ML_HEADS = 4
ML_DQK = 128
ML_DV = 256
MB_HEADS = 16
MB_P = 64
MB_P_LOG2 = 6
MB_GROUPS = 4
MB_N = 128
MB_W = MB_HEADS * MB_P
MB_CONV_DIM = MB_W + 2 * MB_GROUPS * MB_N
D_FF = 2816
FFN_CONV = 3
FF_CHUNK = 1408
SMALL_W = 128
DT_COL = 8
CONV_PAD = 8

VMEM_LIMIT = 56 * 1024 * 1024

TM_PROJ = 2048
TN_PROJ = 1024
EPI_ROWS = 256
TILES_SILU = (0, 1, 2)
TILES_SIGMOID = (3, 4, 5, 6)
TILES_MIXED = (7, 8, 9, 10, 11, 12)
T_HG_Q, T_HG_G, T_MB_Z, T_ML_O, T_GATE0 = 0, 1, 2, 3, 4
N_GATES = 3
HALF_TILE = TN_PROJ // 2
CONV_W = 4
TS_MIX = 512
CHUNK_UNROLL = 2
TM_MERGE = 512
TM_FFN = 512


def _silu(x):
    return x * jax.nn.sigmoid(x)


def _softplus(x):
    return jnp.maximum(x, 0.0) + jnp.log1p(jnp.exp(-jnp.abs(x)))


def _log_sigmoid(x):
    return jnp.minimum(x, 0.0) - jnp.log1p(jnp.exp(-jnp.abs(x)))


def _dot(a, b):
    return jnp.dot(a, b, preferred_element_type=F32)


def _dot_nt(a, b):
    return lax.dot_general(a, b, (((1,), (1,)), ((), ())), preferred_element_type=F32)


def _dot_tn(a, b):
    return lax.dot_general(a, b, (((0,), (0,)), ((), ())), preferred_element_type=F32)


def _split3(a):
    hi = a.astype(BF16)
    r = a - hi.astype(F32)
    mid = r.astype(BF16)
    lo = (r - mid.astype(F32)).astype(BF16)
    return hi, mid, lo


def _sel_left(sel, a):
    hi, mid, lo = _split3(a)
    return _dot(sel, hi) + (_dot(sel, mid) + _dot(sel, lo))


def _sel_right(a, sel):
    hi, mid, lo = _split3(a)
    return _dot(hi, sel) + (_dot(mid, sel) + _dot(lo, sel))


def _tri_lower(n):
    r = lax.broadcasted_iota(jnp.int32, (n, n), 0)
    c = lax.broadcasted_iota(jnp.int32, (n, n), 1)
    return r >= c


def _bcast_row(ref, h, r):
    return ref[h, pl.ds(r, SUBLANES, stride=0), :]


def _mask_bf16(m):
    return jnp.where(m, 1.0, 0.0).astype(BF16)


def _cparams(sem):
    return pltpu.CompilerParams(dimension_semantics=sem, vmem_limit_bytes=VMEM_LIMIT)


def _lbs_kernel(lg_ref, o_ref):
    lg = lg_ref[...]
    mx = jnp.max(lg, axis=0, keepdims=True)
    e = jnp.exp(lg - mx)
    p = e / jnp.sum(e, axis=0, keepdims=True)
    acc = jnp.zeros_like(p[0:1])
    rows = []
    for l in range(lg.shape[0]):
        acc = acc + p[l:l + 1]
        rows.append(acc - p[0:1])
    o_ref[...] = jnp.concatenate(rows, axis=0)


def _lbs(logits):
    return pl.pallas_call(
        _lbs_kernel,
        out_shape=jax.ShapeDtypeStruct(logits.shape, F32),
        name="hgrn2_lower_bounds",
    )(logits)


def _any_tile(j, tiles):
    hit = j == tiles[0]
    for t in tiles[1:]:
        hit = jnp.logical_or(hit, j == t)
    return hit


def _inproj_kernel(x_ref, g_ref, w_ref, ws_ref, cw_ref, cb_ref, o_ref, os_ref, h_ref, tail_ref, hbuf,
                   *, tiles_per_seq):
    i = pl.program_id(0)
    j = pl.program_id(1)
    n_blk = TM_PROJ // EPI_ROWS

    @pl.when(jnp.logical_and(i == 0, j == 0))
    def _():
        tail_ref[...] = jnp.zeros_like(tail_ref)

    @pl.when(j == 0)
    def _():
        x = x_ref[...]
        ms = jnp.mean(x * x, axis=-1, keepdims=True)
        hb = (x * lax.rsqrt(ms + EPS) * g_ref[...]).astype(BF16)
        h_ref[...] = hb
        os_ref[...] = _dot(hb, ws_ref[...])

    def pointwise(fn):
        ys = [_dot(h_ref[r * EPI_ROWS:(r + 1) * EPI_ROWS, :], w_ref[...]) for r in range(n_blk)]
        for r in range(n_blk):
            o_ref[r * EPI_ROWS:(r + 1) * EPI_ROWS, :] = fn(ys[r]).astype(o_ref.dtype)

    @pl.when(_any_tile(j, TILES_SILU))
    def _():
        pointwise(_silu)

    @pl.when(_any_tile(j, TILES_SIGMOID))
    def _():
        pointwise(jax.nn.sigmoid)

    @pl.when(j >= TILES_MIXED[0])
    def _():
        slot = j - TILES_MIXED[0]
        prev = jnp.where(i % tiles_per_seq == 0, 0.0, tail_ref[slot])
        cw = cw_ref[0]
        cb = cb_ref[0]
        ys = [_dot(h_ref[r * EPI_ROWS:(r + 1) * EPI_ROWS, :], w_ref[...]) for r in range(n_blk)]
        for r in range(n_blk):
            yb = ys[r][:, :HALF_TILE]
            out = cb + cw[CONV_W - 1:CONV_W] * yb
            for k in range(CONV_W - 1):
                out = out + cw[k:k + 1] * pltpu.roll(yb, CONV_W - 1 - k, axis=0)
            hbuf[0:CONV_PAD, :] = prev
            hbuf[CONV_PAD:2 * CONV_PAD, :] = yb[:CONV_PAD]
            head = cb
            for k in range(CONV_W):
                off = CONV_PAD - (CONV_W - 1) + k
                head = head + cw[k:k + 1] * hbuf[off:off + CONV_PAD, :]
            prev = yb[EPI_ROWS - CONV_PAD:]
            act = _silu(jnp.concatenate([head, out[CONV_PAD:]], axis=0))
            o_ref[r * EPI_ROWS:(r + 1) * EPI_ROWS, :] = jnp.concatenate(
                [act, ys[r][:, HALF_TILE:]], axis=-1).astype(o_ref.dtype)
        tail_ref[slot] = prev


def _inproj(x2, g, w_big, w_small, cw_all, cb_all, layer, seq):
    n = x2.shape[0]
    nb = w_big.shape[-1]
    n_tiles = nb // TN_PROJ
    assert n_tiles == len(TILES_SILU + TILES_SIGMOID + TILES_MIXED)
    return pl.pallas_call(
        functools.partial(_inproj_kernel, tiles_per_seq=seq // TM_PROJ),
        grid=(n // TM_PROJ, n_tiles),
        in_specs=[
            pl.BlockSpec((TM_PROJ, D_MODEL), lambda i, j: (i, 0)),
            pl.BlockSpec((1, D_MODEL), lambda i, j: (0, 0)),
            pl.BlockSpec((None, D_MODEL, TN_PROJ), lambda i, j: (layer, 0, j)),
            pl.BlockSpec((None, D_MODEL, SMALL_W), lambda i, j: (layer, 0, 0)),
            pl.BlockSpec((None, 1, CONV_W, HALF_TILE), lambda i, j: (layer, j, 0, 0)),
            pl.BlockSpec((None, 1, 1, HALF_TILE), lambda i, j: (layer, j, 0, 0)),
        ],
        out_specs=[
            pl.BlockSpec((TM_PROJ, TN_PROJ), lambda i, j: (i, j)),
            pl.BlockSpec((TM_PROJ, SMALL_W), lambda i, j: (i, 0)),
        ],
        out_shape=[
            jax.ShapeDtypeStruct((n, nb), BF16),
            jax.ShapeDtypeStruct((n, SMALL_W), F32),
        ],
        scratch_shapes=[
            pltpu.VMEM((TM_PROJ, D_MODEL), BF16),
            pltpu.VMEM((len(TILES_MIXED), CONV_PAD, HALF_TILE), F32),
            pltpu.VMEM((2 * CONV_PAD, HALF_TILE), F32),
        ],
        compiler_params=_cparams(("arbitrary", "arbitrary")),
        name="in_projection",
    )(x2, g, w_big, w_small, cw_all, cb_all)


def _hgrn2_stages(q_ref, f_ref, i_ref, g_ref, lb_ref, ng_ref, y_ref, state_ref, c_s):
    @pl.when(pl.program_id(1) == 0)
    def _():
        state_ref[...] = jnp.zeros_like(state_ref)

    lb = lb_ref[...]
    tri = _mask_bf16(_tri_lower(CHUNK))
    n_sub = CHUNK // SUB
    ones_rhs = jnp.ones((HG_D, HG_D), BF16)
    lane = lax.broadcasted_iota(jnp.int32, (SUBLANES, HG_D), 1)
    row = lax.broadcasted_iota(jnp.int32, (SUBLANES, HG_D), 0)

    causal = (lax.broadcasted_iota(jnp.int32, (CHUNK, HG_D), 1)
              <= lax.broadcasted_iota(jnp.int32, (CHUNK, HG_D), 0))

    heads = range(HG_HEADS)
    hsl = [slice(h * HG_D, (h + 1) * HG_D) for h in heads]
    zero_tail = jnp.zeros((HG_D - CHUNK, HG_D), BF16)

    def chunk(c):
        rows = pl.ds(pl.multiple_of(c * CHUNK, CHUNK), CHUNK)
        sig = jax.nn.sigmoid(f_ref[rows, :].astype(F32))
        fgate = lb + (1.0 - lb) * sig
        b = _sel_left(tri, jnp.log(fgate) * LOG2E)
        cc = b - jnp.log(jnp.maximum(1.0 - fgate, 0.0)) * LOG2E
        for h in heads:
            c_s[h] = cc[:, hsl[h]]
        q = q_ref[rows, :].astype(F32)
        v16 = i_ref[rows, :]
        yield

        st = [state_ref[h] for h in heads]
        q_dec = (q * jnp.exp2(b)).astype(BF16)
        b_last = b[CHUNK - 1:CHUNK, :]
        k_dec = jnp.exp2(b_last - cc).astype(BF16)
        st_decay = jnp.exp2(b_last)
        o_inter = [_dot_nt(q_dec[:, hsl[h]], st[h].astype(BF16)) for h in heads]
        for h in heads:
            state_ref[h] = st_decay[:, hsl[h]] * st[h] + _dot_tn(v16[:, hsl[h]], k_dec[:, hsl[h]])
        yield

        def intra_factored():
            k_all = jnp.exp2(-cc).astype(BF16)
            res = []
            for h in heads:
                k_pad = jnp.concatenate([k_all[:, hsl[h]], zero_tail], axis=0)
                res.append(jnp.where(causal, _dot_nt(q_dec[:, hsl[h]], k_pad), 0.0).astype(BF16))
            return tuple(res)

        def intra_blocked():
            zs = []
            for i in range(n_sub):
                lo = i * SUB
                for s in range(SUB):
                    r0 = lo + (s // SUBLANES) * SUBLANES
                    c_row = jnp.concatenate([_bcast_row(c_s, h, lo + s) for h in heads], axis=-1)
                    c_row = jnp.concatenate([c_row] * ((lo + SUB - r0) // SUBLANES), axis=0)
                    zs.append(q[r0:lo + SUB] * jnp.exp2(b[r0:lo + SUB] - c_row))
            z_rows = sum(z.shape[0] for z in zs)
            z_all = jnp.concatenate([z[:, hsl[h]] for h in heads for z in zs], axis=0)
            r = _dot(z_all.astype(BF16), ones_rhs)

            a_off = []
            for i in range(1, n_sub):
                lo = i * SUB
                bref = b[lo - 1:lo, :]
                q_i = (q[lo:lo + SUB] * jnp.exp2(b[lo:lo + SUB] - bref)).astype(BF16)
                k_i = jnp.exp2(bref - cc[:lo]).astype(BF16)
                zero_rows = jnp.zeros((HG_D - lo, HG_D), BF16)
                a_off.append([_dot_nt(q_i[:, hsl[h]], jnp.concatenate([k_i[:, hsl[h]], zero_rows], axis=0))
                              for h in heads])

            res = []
            for h in heads:
                a_rows = []
                off = h * z_rows
                for i in range(n_sub):
                    lo = i * SUB
                    a_i = a_off[i - 1][h] if i > 0 else jnp.zeros((SUB, HG_D), F32)
                    tiles = [a_i[j * SUBLANES:(j + 1) * SUBLANES] for j in range(SUB // SUBLANES)]
                    for s in range(SUB):
                        for j in range(s // SUBLANES, SUB // SUBLANES):
                            tiles[j] = jnp.where(lane == lo + s, r[off:off + SUBLANES], tiles[j])
                            off += SUBLANES
                    for j in range(SUB // SUBLANES):
                        a_rows.append(jnp.where(lane - lo <= row + j * SUBLANES, tiles[j], 0.0))
                res.append(jnp.concatenate(a_rows, axis=0).astype(BF16))
            return tuple(res)

        a_heads = lax.cond(jnp.min(b_last) >= -FACTORED_DECAY_LIMIT, intra_factored, intra_blocked)
        yield

        outs = []
        for h in heads:
            v_pad = jnp.concatenate([v16[:, hsl[h]], zero_tail], axis=0)
            outs.append(o_inter[h] + _dot(a_heads[h], v_pad))
            if h % 2 == 1:
                yield
        parts = []
        for o in outs:
            parts.append(o * lax.rsqrt(jnp.mean(o * o, axis=-1, keepdims=True) + EPS))
        on = jnp.concatenate(parts, axis=-1) * ng_ref[...]
        y_ref[rows, :] = (on * g_ref[rows, :].astype(F32)).astype(y_ref.dtype)

    return chunk


def _mlstm_stages(qk_ref, v_ref, og_ref, sm_ref, smt_ref, gbr_ref, gbc_ref, ng_ref,
                  y_ref, caug_ref, m_ref):
    half = ML_HEADS * ML_DQK
    k_scale = ML_DQK ** -0.5

    @pl.when(pl.program_id(1) == 0)
    def _():
        caug_ref[...] = jnp.zeros_like(caug_ref)
        m_ref[...] = jnp.zeros_like(m_ref)

    tri_b = _tri_lower(CHUNK)
    tri_l = _mask_bf16(tri_b)
    tri_u = _mask_bf16(lax.broadcasted_iota(jnp.int32, (CHUNK, CHUNK), 0)
                       <= lax.broadcasted_iota(jnp.int32, (CHUNK, CHUNK), 1))
    ones_col = _mask_bf16(lax.broadcasted_iota(jnp.int32, (CHUNK, LANES), 1) == 0)

    def chunk(c):
        rows = pl.ds(pl.multiple_of(c * CHUNK, CHUNK), CHUNK)
        pre_c = sm_ref[rows, :] + gbr_ref[...]
        pre_r = smt_ref[c] + gbc_ref[...]
        cum_c = _sel_left(tri_l, _log_sigmoid(pre_c))
        cum_r = _sel_right(_log_sigmoid(pre_r), tri_u)
        yield
        heads = range(ML_HEADS)
        m_all = m_ref[...]
        m_old = [m_all[h:h + 1, 0:1] for h in heads]
        b_col = [cum_c[:, ML_HEADS + h:ML_HEADS + h + 1] for h in heads]
        i_col = [pre_c[:, h:h + 1] for h in heads]
        log_d = [jnp.where(tri_b, b_col[h] - cum_r[ML_HEADS + h:ML_HEADS + h + 1, :] + pre_r[h:h + 1, :],
                           NEG_BIG) for h in heads]
        log_inter = [b_col[h] + m_old[h] for h in heads]
        m_t = [jnp.maximum(jnp.max(log_d[h], axis=-1, keepdims=True), log_inter[h]) for h in heads]
        b_last = [b_col[h][CHUNK - 1:CHUNK, :] for h in heads]
        log_w = [b_last[h] - b_col[h] + i_col[h] for h in heads]
        m_new = [jnp.maximum(b_last[h] + m_old[h], jnp.max(log_w[h], axis=0, keepdims=True)) for h in heads]
        m_ref[...] = jnp.concatenate(
            [jnp.broadcast_to(m_new[h], (1, m_ref.shape[1])) for h in heads] + [m_all[ML_HEADS:]], axis=0)
        yield

        qb = [qk_ref[rows, h * ML_DQK:(h + 1) * ML_DQK] for h in heads]
        kb = [qk_ref[rows, half + h * ML_DQK:half + (h + 1) * ML_DQK] for h in heads]
        v_aug = [jnp.concatenate([v_ref[rows, h * ML_DV:(h + 1) * ML_DV], ones_col], axis=-1)
                 for h in heads]
        qk = [_dot_nt(qb[h], kb[h]) for h in heads]
        c_aug = [caug_ref[h] for h in heads]
        inter = [_dot(qb[h], c_aug[h].astype(BF16)) * jnp.exp(log_inter[h] - m_t[h]) for h in heads]
        yield
        s = [(qk[h] * (jnp.exp(log_d[h] - m_t[h]) * k_scale)).astype(BF16) for h in heads]
        num = [_dot(s[h], v_aug[h]) + inter[h] for h in heads]
        yield
        kw = [(kb[h].astype(F32) * (jnp.exp(log_w[h] - m_new[h]) * k_scale)).astype(BF16) for h in heads]
        for h in heads:
            caug_ref[h] = jnp.exp(b_last[h] + m_old[h] - m_new[h]) * c_aug[h] + _dot_tn(kw[h], v_aug[h])
        yield

        parts = []
        for h in heads:
            denom = jnp.maximum(jnp.abs(num[h][:, ML_DV:ML_DV + 1]), jnp.exp(-m_t[h]))
            o = num[h][:, :ML_DV] / denom
            parts.append(o * lax.rsqrt(jnp.mean(o * o, axis=-1, keepdims=True) + EPS))
        on = jnp.concatenate(parts, axis=-1) * ng_ref[...]
        y_ref[rows, :] = (on * og_ref[rows, :].astype(F32)).astype(y_ref.dtype)

    return chunk


def _ssd_stages(z_ref, xbc_ref, sm_ref, dtt_ref, dtb_c_ref, alog_c_ref,
                dtb_r_ref, alog_r_ref, d_ref, ng_ref, y_ref, state_ref):
    n_pair = MB_HEADS // 2
    pair_w = 2 * MB_P
    gw = MB_GROUPS * MB_N

    @pl.when(pl.program_id(1) == 0)
    def _():
        state_ref[...] = jnp.zeros_like(state_ref)

    tri_l = _mask_bf16(_tri_lower(CHUNK))
    sel_x = _mask_bf16(lax.broadcasted_iota(jnp.int32, (SMALL_W, MB_W), 0) - DT_COL
                       == jnp.right_shift(lax.broadcasted_iota(jnp.int32, (SMALL_W, MB_W), 1), MB_P_LOG2))
    ur = lax.broadcasted_iota(jnp.int32, (pair_w, pair_w), 0)
    uc = lax.broadcasted_iota(jnp.int32, (pair_w, pair_w), 1)
    same_half = jnp.right_shift(ur, CHUNK_LOG2) == jnp.right_shift(uc, CHUNK_LOG2)
    tri_u2 = _mask_bf16(same_half & (ur <= uc))
    causal = (lax.broadcasted_iota(jnp.int32, (CHUNK, MB_W), 0)
              >= jnp.bitwise_and(lax.broadcasted_iota(jnp.int32, (CHUNK, MB_W), 1), CHUNK - 1))
    first_head = lax.broadcasted_iota(jnp.int32, (CHUNK, pair_w), 1) < MB_P
    a_c = -jnp.exp(alog_c_ref[...])
    a_r = -jnp.exp(alog_r_ref[...])

    def chunk(c):
        rows = pl.ds(pl.multiple_of(c * CHUNK, CHUNK), CHUNK)
        dt_c = _softplus(sm_ref[rows, :] + dtb_c_ref[...])
        cum_c = _sel_left(tri_l, dt_c * a_c)
        dt_x = _sel_right(dt_c, sel_x)
        cum_x = _sel_right(cum_c, sel_x)
        dt_r = _softplus(dtt_ref[c] + dtb_r_ref[...])
        cum_r = _sel_right(dt_r * a_r, tri_u2)
        yield
        pairs = range(n_pair)
        grp = [(2 * p) // (MB_HEADS // MB_GROUPS) for p in pairs]
        lanes = [slice(p * pair_w, (p + 1) * pair_w) for p in pairs]
        xs = xbc_ref[rows, :MB_W].astype(F32)
        xdt = xs * dt_x
        cum_row = jnp.concatenate([cum_r[p:p + 1, :] for p in pairs], axis=-1)
        decay = jnp.exp(jnp.where(causal, cum_x - cum_row, NEG_BIG))
        cum_last = cum_x[CHUNK - 1:CHUNK, :]
        x_dec = (xdt * jnp.exp(cum_last - cum_x)).astype(BF16)
        st_decay = jnp.exp(cum_last)
        carry_w = jnp.exp(cum_x)
        yield
        bm = [xbc_ref[rows, MB_W + g * MB_N:MB_W + (g + 1) * MB_N] for g in range(MB_GROUPS)]
        cm = [xbc_ref[rows, MB_W + gw + g * MB_N:MB_W + gw + (g + 1) * MB_N] for g in range(MB_GROUPS)]
        cb2 = [_dot_nt(cm[g], jnp.concatenate([bm[g], bm[g]], axis=0)) for g in range(MB_GROUPS)]
        x2 = [jnp.concatenate([jnp.where(first_head, xdt[:, lanes[p]], 0.0),
                               jnp.where(first_head, 0.0, xdt[:, lanes[p]])], axis=0).astype(BF16)
              for p in pairs]
        yield
        st = [state_ref[p] for p in pairs]
        y_inter = [_dot(cm[grp[p]], st[p].astype(BF16)) for p in pairs]
        y_intra = [_dot((cb2[grp[p]] * decay[:, lanes[p]]).astype(BF16), x2[p]) for p in pairs]
        yield
        for p in pairs:
            state_ref[p] = st_decay[:, lanes[p]] * st[p] + _dot_tn(bm[grp[p]], x_dec[:, lanes[p]])
        yield
        y = (jnp.concatenate(y_intra, axis=-1) + jnp.concatenate(y_inter, axis=-1) * carry_w
             + d_ref[...] * xs)

        yz = y * z_ref[rows, :].astype(F32)
        gsz = MB_W // MB_GROUPS
        parts = []
        for g in range(MB_GROUPS):
            o = yz[:, g * gsz:(g + 1) * gsz]
            parts.append(o * lax.rsqrt(jnp.mean(o * o, axis=-1, keepdims=True) + EPS))
        y_ref[rows, :] = (jnp.concatenate(parts, axis=-1) * ng_ref[...]).astype(y_ref.dtype)

    return chunk


_DONE = object()


class _ColumnView:
    def __init__(self, pieces):
        self.pieces = pieces
        self.width = sum(w for _, _, w in pieces)

    def __getitem__(self, idx):
        rows, cols = idx
        lo, hi, _ = cols.indices(self.width)
        out, base = [], 0
        for ref, start, w in self.pieces:
            a, b = max(lo, base), min(hi, base + w)
            if a < b:
                out.append(ref[rows, start + a - base:start + b - base])
            base += w
        return out[0] if len(out) == 1 else jnp.concatenate(out, axis=-1)


def _mixers_kernel(q_ref, g_ref, z_ref, og_ref, m0, m1, m2, m3, m4, m5, sm_ref, smt_ref, dtt_ref,
                   lb_ref, hg_ng_ref, gbr_ref, gbc_ref, ml_ng_ref,
                   dtb_c_ref, alog_c_ref, dtb_r_ref, alog_r_ref, d_ref, mb_ng_ref,
                   y_hg, y_ml, y_mb, hg_state, hg_c, ml_caug, ml_m, mb_state):
    ts = y_hg.shape[0]
    mixed = (m0, m1, m2, m3, m4, m5)
    conv = [(m, 0, HALF_TILE) for m in mixed]
    raw = [(m, HALF_TILE, HALF_TILE) for m in mixed]
    qk_view = _ColumnView(conv[0:2])
    xbc_view = _ColumnView(conv[2:6])
    f_view = _ColumnView(raw[0:2])
    i_view = _ColumnView(raw[2:4])
    v_view = _ColumnView(raw[4:6])
    mixers = [
        _hgrn2_stages(q_ref, f_view, i_view, g_ref, lb_ref, hg_ng_ref, y_hg, hg_state, hg_c),
        _mlstm_stages(qk_view, v_view, og_ref, sm_ref, smt_ref, gbr_ref, gbc_ref, ml_ng_ref,
                      y_ml, ml_caug, ml_m),
        _ssd_stages(z_ref, xbc_view, sm_ref, dtt_ref, dtb_c_ref, alog_c_ref, dtb_r_ref, alog_r_ref,
                    d_ref, mb_ng_ref, y_mb, mb_state),
    ]

    def chunk_body(c, carry):
        live = [m(c) for m in mixers]
        while live:
            for g in list(live):
                if next(g, _DONE) is _DONE:
                    live.remove(g)
        return carry

    lax.fori_loop(0, ts // CHUNK, chunk_body, 0, unroll=CHUNK_UNROLL)


def _mixers(proj, small, small_t, dt_t, lb, hg_ng, gb_row, gb_col, ml_ng,
            dtb_c, alog_c, dtb_r, alog_r, d_x, mb_ng, batch, seq):
    n = proj.shape[0]
    nt = seq // TS_MIX
    n_pair = MB_HEADS // 2
    w = D_MODEL

    def col(cb):
        return pl.BlockSpec((TS_MIX, w), lambda b, j: (b * nt + j, cb))

    def const(shape):
        return pl.BlockSpec(shape, lambda b, j: tuple(0 for _ in shape))

    tile_specs = [col(T_HG_Q), col(T_HG_G), col(T_MB_Z), col(T_ML_O)] + [col(t) for t in TILES_MIXED]
    gate_specs = [
        pl.BlockSpec((TS_MIX, SMALL_W), lambda b, j: (b * nt + j, 0)),
        pl.BlockSpec((TS_MIX // CHUNK, 2 * ML_HEADS, CHUNK), lambda b, j: (b * nt + j, 0, 0)),
        pl.BlockSpec((TS_MIX // CHUNK, n_pair, 2 * CHUNK), lambda b, j: (b * nt + j, 0, 0)),
    ]
    param_specs = [const((1, w)), const((1, w)),
                   const((1, SMALL_W)), const((2 * ML_HEADS, 1)), const((1, w)),
                   const((1, SMALL_W)), const((1, SMALL_W)),
                   const((n_pair, 2 * CHUNK)), const((n_pair, 2 * CHUNK)),
                   const((1, w)), const((1, w))]
    out = pl.BlockSpec((TS_MIX, w), lambda b, j: (b * nt + j, 0))
    return pl.pallas_call(
        _mixers_kernel,
        grid=(batch, nt),
        in_specs=tile_specs + gate_specs + param_specs,
        out_specs=[out, out, out],
        out_shape=[jax.ShapeDtypeStruct((n, w), BF16)] * 3,
        scratch_shapes=[
            pltpu.VMEM((HG_HEADS, HG_D, HG_D), F32),
            pltpu.VMEM((HG_HEADS, CHUNK, HG_D), F32),
            pltpu.VMEM((ML_HEADS, ML_DQK, ML_DV + LANES), F32),
            pltpu.VMEM((SUBLANES, LANES), F32),
            pltpu.VMEM((n_pair, MB_N, 2 * MB_P), F32),
        ],
        compiler_params=_cparams(("parallel", "arbitrary")),
        name="token_mixers",
    )(*([proj] * (4 + len(TILES_MIXED))), small, small_t, dt_t,
      lb, hg_ng, gb_row, gb_col, ml_ng, dtb_c, alog_c, dtb_r, alog_r, d_x, mb_ng)


def _merge_kernel(x_ref, yh_ref, ym_ref, yb_ref, g0_ref, g1_ref, g2_ref,
                  wh_ref, wm_ref, wb_ref, wo_ref, o_ref):
    mixed = g0_ref[...].astype(F32) * _dot(yh_ref[...], wh_ref[...])
    mixed = mixed + g1_ref[...].astype(F32) * _dot(ym_ref[...], wm_ref[...])
    mixed = mixed + g2_ref[...].astype(F32) * _dot(yb_ref[...], wb_ref[...])
    o_ref[...] = x_ref[...] + _dot(mixed.astype(BF16), wo_ref[...])


def _merge(x2, y_hg, y_ml, y_mb, proj, w_hg, w_ml, w_mb, w_out, layer):
    n = x2.shape[0]
    tile = lambda cb: pl.BlockSpec((TM_MERGE, D_MODEL), lambda i: (i, cb))
    wspec = pl.BlockSpec((None, D_MODEL, D_MODEL), lambda i: (layer, 0, 0))
    return pl.pallas_call(
        _merge_kernel,
        grid=(n // TM_MERGE,),
        in_specs=[tile(0), tile(0), tile(0), tile(0), tile(T_GATE0), tile(T_GATE0 + 1), tile(T_GATE0 + 2),
                  wspec, wspec, wspec, wspec],
        out_specs=tile(0),
        out_shape=jax.ShapeDtypeStruct((n, D_MODEL), F32),
        compiler_params=_cparams(("parallel",)),
        name="branch_merge",
    )(x2, y_hg, y_ml, y_mb, proj, proj, proj, w_hg, w_ml, w_mb, w_out)


def _ffn_kernel(x_ref, g_ref, wu_ref, cw_ref, cb_ref, wd_ref, fg_ref,
                o_ref, tail, hbuf_g, hbuf_v, *, final):
    tm = x_ref.shape[0]
    n_ck = D_FF // FF_CHUNK

    @pl.when(pl.program_id(1) == 0)
    def _():
        tail[...] = jnp.zeros_like(tail)

    x = x_ref[...]
    hb = (x * lax.rsqrt(jnp.mean(x * x, axis=-1, keepdims=True) + EPS) * g_ref[...]).astype(BF16)

    def cols(c, half):
        return slice(half * D_FF + c * FF_CHUNK, half * D_FF + (c + 1) * FF_CHUNK)

    def up(c):
        return _dot(hb, wu_ref[:, cols(c, 0)]), _dot(hb, wu_ref[:, cols(c, 1)])

    def conv(u, hbuf, cs):
        out = cb_ref[:, cs] + cw_ref[FFN_CONV - 1:FFN_CONV, cs] * u
        for k in range(FFN_CONV - 1):
            out = out + cw_ref[k:k + 1, cs] * pltpu.roll(u, FFN_CONV - 1 - k, axis=0)
        hbuf[0:CONV_PAD, :] = tail[:, cs]
        hbuf[CONV_PAD:2 * CONV_PAD, :] = u[:CONV_PAD]
        tail[:, cs] = u[tm - CONV_PAD:]
        head = cb_ref[:, cs]
        for k in range(FFN_CONV):
            off = CONV_PAD - (FFN_CONV - 1) + k
            head = head + cw_ref[k:k + 1, cs] * hbuf[off:off + CONV_PAD, :]
        return jnp.concatenate([head, out[CONV_PAD:]], axis=0)

    acc = x
    u_next = up(0)
    for c in range(n_ck):
        u_g, u_v = u_next
        if c + 1 < n_ck:
            u_next = up(c + 1)
        a_g = conv(u_g, hbuf_g, cols(c, 0))
        a_v = conv(u_v, hbuf_v, cols(c, 1))
        acc = acc + _dot((_silu(a_g) * a_v).astype(BF16), wd_ref[cols(c, 0), :])
    if final:
        acc = acc * lax.rsqrt(jnp.mean(acc * acc, axis=-1, keepdims=True) + EPS) * fg_ref[...]
    o_ref[...] = acc


def _ffn(x2, g, w_up, cw, cb, w_down, final_g, layer, batch, seq, final):
    n = x2.shape[0]
    nt = seq // TM_FFN

    def const(shape):
        return pl.BlockSpec(shape, lambda b, j: tuple(0 for _ in shape),
                            pipeline_mode=pl.Buffered(1))

    def layer_weight(shape):
        return pl.BlockSpec((None,) + shape, lambda b, j: (layer,) + tuple(0 for _ in shape),
                            pipeline_mode=pl.Buffered(1))

    tile = pl.BlockSpec((TM_FFN, D_MODEL), lambda b, j: (b * nt + j, 0))
    return pl.pallas_call(
        functools.partial(_ffn_kernel, final=final),
        grid=(batch, nt),
        in_specs=[
            tile, const((1, D_MODEL)),
            layer_weight((D_MODEL, 2 * D_FF)), const((FFN_CONV, 2 * D_FF)), const((1, 2 * D_FF)),
            layer_weight((D_FF, D_MODEL)), const((1, D_MODEL)),
        ],
        out_specs=tile,
        out_shape=jax.ShapeDtypeStruct((n, D_MODEL), F32),
        scratch_shapes=[
            pltpu.VMEM((CONV_PAD, 2 * D_FF), F32),
            pltpu.VMEM((2 * CONV_PAD, FF_CHUNK), F32),
            pltpu.VMEM((2 * CONV_PAD, FF_CHUNK), F32),
        ],
        compiler_params=_cparams(("parallel", "arbitrary")),
        name="conv_gated_mlp",
    )(x2, g, w_up, cw, cb, w_down, final_g)


def kernel(x, norm1_g, w_in, hg_lb_logits, hg_norm_g, ml_conv_w, ml_conv_b, ml_gate_b, ml_norm_g,
           mb_conv_w, mb_conv_b, mb_dt_bias, mb_a_log, mb_d, mb_norm_g, w_br_hg, w_br_ml, w_br_mb,
           w_out, norm2_g, w_up, ffn_conv_w, ffn_conv_b, w_down, final_g):
    batch, seq, _ = x.shape
    n = batch * seq
    assert seq % TS_MIX == 0 and seq % TM_FFN == 0 and seq % TM_PROJ == 0 and n % TM_MERGE == 0
    depth = w_in.shape[0]

    w = D_MODEL
    o_hg_q, o_hg_f, o_hg_i, o_hg_g = 0, w, 2 * w, 3 * w
    o_ml_qk = 4 * w
    o_ml_v = o_ml_qk + 2 * ML_HEADS * ML_DQK
    o_if = o_ml_v + ML_HEADS * ML_DV
    o_ml_o = o_if + 2 * ML_HEADS
    o_mb_z = o_ml_o + ML_HEADS * ML_DV
    o_mb_xbc = o_mb_z + MB_W
    o_dt = o_mb_xbc + MB_CONV_DIM
    o_gate = o_dt + MB_HEADS
    hw = HALF_TILE
    conv_halves = [o_ml_qk + k * hw for k in range(2)] + [o_mb_xbc + k * hw for k in range(4)]
    raw_halves = [o_hg_f, o_hg_f + hw, o_hg_i, o_hg_i + hw, o_ml_v, o_ml_v + hw]
    cols = [(o_hg_q, w), (o_hg_g, w), (o_mb_z, w), (o_ml_o, w), (o_gate, N_GATES * w)]
    for c0, r0 in zip(conv_halves, raw_halves):
        cols += [(c0, hw), (r0, hw)]
    w_in16 = w_in.astype(BF16)
    w_big = jnp.concatenate([w_in16[:, :, a:a + n_] for a, n_ in cols], axis=-1)
    pad = SMALL_W - 2 * ML_HEADS - MB_HEADS
    w_small = jnp.concatenate(
        [w_in16[:, :, o_if:o_ml_o], w_in16[:, :, o_dt:o_gate],
         jnp.zeros((depth, D_MODEL, pad), BF16)], axis=-1)

    lbs = _lbs(hg_lb_logits.astype(F32))
    hg_ng = jnp.tile(hg_norm_g, (1, HG_HEADS))
    ml_ng = jnp.tile(ml_norm_g, (1, ML_HEADS))
    gb_row = jnp.pad(ml_gate_b, ((0, 0), (0, SMALL_W - 2 * ML_HEADS)))
    dtb_c = jnp.pad(mb_dt_bias, ((0, 0), (DT_COL, SMALL_W - DT_COL - MB_HEADS)))
    alog_c = jnp.pad(mb_a_log, ((0, 0), (DT_COL, SMALL_W - DT_COL - MB_HEADS)))
    n_pair = MB_HEADS // 2
    dtb_r = jnp.repeat(mb_dt_bias, CHUNK, axis=-1).reshape(depth, n_pair, 2 * CHUNK)
    alog_r = jnp.repeat(mb_a_log, CHUNK, axis=-1).reshape(depth, n_pair, 2 * CHUNK)
    d_x = jnp.repeat(mb_d, MB_P, axis=-1)

    n_tiles = w_big.shape[-1] // TN_PROJ
    n_mixed = len(TILES_MIXED)
    conv_w = jnp.concatenate([ml_conv_w, mb_conv_w], axis=-1).reshape(depth, CONV_W, n_mixed, hw)
    conv_b = jnp.concatenate([ml_conv_b, mb_conv_b], axis=-1).reshape(depth, 1, n_mixed, hw)
    front = n_tiles - n_mixed
    cw_all = jnp.pad(jnp.swapaxes(conv_w, 1, 2), ((0, 0), (front, 0), (0, 0), (0, 0)))
    cb_all = jnp.pad(jnp.swapaxes(conv_b, 1, 2), ((0, 0), (front, 0), (0, 0), (0, 0)))

    w_hg = w_br_hg.astype(BF16)
    w_ml = w_br_ml.astype(BF16)
    w_mb = w_br_mb.astype(BF16)
    w_o = w_out.astype(BF16)
    w_u = w_up.astype(BF16)
    w_d = w_down.astype(BF16)

    x2 = x.reshape(n, D_MODEL)
    row = lambda a: a.reshape(1, -1)
    for l in range(depth):
        proj, small = _inproj(x2, row(norm1_g[l]), w_big, w_small, cw_all, cb_all, l, seq)
        chunks = small.reshape(n // CHUNK, CHUNK, SMALL_W)
        small_t = jnp.swapaxes(chunks[:, :, :2 * ML_HEADS], 1, 2)
        dt_t = jnp.swapaxes(chunks[:, :, DT_COL:DT_COL + MB_HEADS], 1, 2).reshape(
            n // CHUNK, n_pair, 2 * CHUNK)

        y_hg, y_ml, y_mb = _mixers(
            proj, small, small_t, dt_t, row(lbs[l]), row(hg_ng[l]),
            row(gb_row[l]), ml_gate_b[l].reshape(2 * ML_HEADS, 1), row(ml_ng[l]),
            row(dtb_c[l]), row(alog_c[l]), dtb_r[l], alog_r[l], row(d_x[l]), row(mb_norm_g[l]), batch, seq)
        x2 = _merge(x2, y_hg, y_ml, y_mb, proj, w_hg, w_ml, w_mb, w_o, l)
        x2 = _ffn(x2, row(norm2_g[l]), w_u, ffn_conv_w[l], row(ffn_conv_b[l]), w_d, row(final_g),
                  l, batch, seq, final=(l == depth - 1))
    return x2.reshape(batch, seq, D_MODEL)
```

```python
import functools
import math

import jax
import jax.numpy as jnp
from jax import lax
from jax.experimental import pallas as pl
from jax.experimental.pallas import tpu as pltpu

F32 = jnp.float32
BF16 = jnp.bfloat16

D_MODEL = 1024
CHUNK = 64
CHUNK_LOG2 = 6
SUB = 8
FACTORED_DECAY_LIMIT = 112.0
SUBLANES = 8
LANES = 128
LOG2E = math.log2(math.e)
EPS = 1e-6
NEG_BIG = -1e30
HG_HEADS = 8
HG_D = 128
ML_HEADS = 4
ML_DQK = 128
ML_DV = 256
MB_HEADS = 16
MB_P = 64
MB_P_LOG2 = 6
MB_GROUPS = 4
MB_N = 128
MB_W = MB_HEADS * MB_P
MB_CONV_DIM = MB_W + 2 * MB_GROUPS * MB_N
D_FF = 2816
FFN_CONV = 3
FF_CHUNK = 1408
SMALL_W = 128
DT_COL = 8
CONV_PAD = 8

VMEM_LIMIT = 56 * 1024 * 1024

TM_PROJ = 2048
TN_PROJ = 1024
EPI_ROWS = 256
TILES_SILU = (0, 1, 2)
TILES_SIGMOID = (3, 4, 5, 6)
TILES_MIXED = (7, 8, 9, 10, 11, 12)
T_HG_Q, T_HG_G, T_MB_Z, T_ML_O, T_GATE0 = 0, 1, 2, 3, 4
N_GATES = 3
HALF_TILE = TN_PROJ // 2
CONV_W = 4
TS_MIX = 512
CHUNK_UNROLL = 4
TM_MERGE = 512
TM_FFN = 512


def _silu(x):
    return x * jax.nn.sigmoid(x)


def _softplus(x):
    return jnp.maximum(x, 0.0) + jnp.log1p(jnp.exp(-jnp.abs(x)))


def _log_sigmoid(x):
    return jnp.minimum(x, 0.0) - jnp.log1p(jnp.exp(-jnp.abs(x)))


def _dot(a, b):
    return jnp.dot(a, b, preferred_element_type=F32)


def _dot_nt(a, b):
    return lax.dot_general(a, b, (((1,), (1,)), ((), ())), preferred_element_type=F32)


def _dot_tn(a, b):
    return lax.dot_general(a, b, (((0,), (0,)), ((), ())), preferred_element_type=F32)


def _split3(a):
    hi = a.astype(BF16)
    r = a - hi.astype(F32)
    mid = r.astype(BF16)
    lo = (r - mid.astype(F32)).astype(BF16)
    return hi, mid, lo


def _sel_left(sel, a):
    hi, mid, lo = _split3(a)
    return _dot(sel, hi) + (_dot(sel, mid) + _dot(sel, lo))


def _sel_right(a, sel):
    hi, mid, lo = _split3(a)
    return _dot(hi, sel) + (_dot(mid, sel) + _dot(lo, sel))


def _tri_lower(n):
    r = lax.broadcasted_iota(jnp.int32, (n, n), 0)
    c = lax.broadcasted_iota(jnp.int32, (n, n), 1)
    return r >= c


def _bcast_row(ref, h, r):
    return ref[h, pl.ds(r, SUBLANES, stride=0), :]


def _mask_bf16(m):
    return jnp.where(m, 1.0, 0.0).astype(BF16)


def _cparams(sem):
    return pltpu.CompilerParams(dimension_semantics=sem, vmem_limit_bytes=VMEM_LIMIT)


def _lbs_kernel(lg_ref, o_ref):
    lg = lg_ref[...]
    mx = jnp.max(lg, axis=0, keepdims=True)
    e = jnp.exp(lg - mx)
    p = e / jnp.sum(e, axis=0, keepdims=True)
    acc = jnp.zeros_like(p[0:1])
    rows = []
    for l in range(lg.shape[0]):
        acc = acc + p[l:l + 1]
        rows.append(acc - p[0:1])
    o_ref[...] = jnp.concatenate(rows, axis=0)


def _lbs(logits):
    return pl.pallas_call(
        _lbs_kernel,
        out_shape=jax.ShapeDtypeStruct(logits.shape, F32),
        name="hgrn2_lower_bounds",
    )(logits)


def _any_tile(j, tiles):
    hit = j == tiles[0]
    for t in tiles[1:]:
        hit = jnp.logical_or(hit, j == t)
    return hit


def _inproj_kernel(x_ref, g_ref, w_ref, ws_ref, cw_ref, cb_ref, o_ref, os_ref, h_ref, tail_ref, hbuf,
                   *, tiles_per_seq):
    i = pl.program_id(0)
    j = pl.program_id(1)
    n_blk = TM_PROJ // EPI_ROWS

    @pl.when(jnp.logical_and(i == 0, j == 0))
    def _():
        tail_ref[...] = jnp.zeros_like(tail_ref)

    @pl.when(j == 0)
    def _():
        x = x_ref[...]
        ms = jnp.mean(x * x, axis=-1, keepdims=True)
        hb = (x * lax.rsqrt(ms + EPS) * g_ref[...]).astype(BF16)
        h_ref[...] = hb
        os_ref[...] = _dot(hb, ws_ref[...])

    def pointwise(fn):
        ys = [_dot(h_ref[r * EPI_ROWS:(r + 1) * EPI_ROWS, :], w_ref[...]) for r in range(n_blk)]
        for r in range(n_blk):
            o_ref[r * EPI_ROWS:(r + 1) * EPI_ROWS, :] = fn(ys[r]).astype(o_ref.dtype)

    @pl.when(_any_tile(j, TILES_SILU))
    def _():
        pointwise(_silu)

    @pl.when(_any_tile(j, TILES_SIGMOID))
    def _():
        pointwise(jax.nn.sigmoid)

    @pl.when(j >= TILES_MIXED[0])
    def _():
        slot = j - TILES_MIXED[0]
        prev = jnp.where(i % tiles_per_seq == 0, 0.0, tail_ref[slot])
        cw = cw_ref[0]
        cb = cb_ref[0]
        ys = [_dot(h_ref[r * EPI_ROWS:(r + 1) * EPI_ROWS, :], w_ref[...]) for r in range(n_blk)]
        for r in range(n_blk):
            yb = ys[r][:, :HALF_TILE]
            out = cb + cw[CONV_W - 1:CONV_W] * yb
            for k in range(CONV_W - 1):
                out = out + cw[k:k + 1] * pltpu.roll(yb, CONV_W - 1 - k, axis=0)
            hbuf[0:CONV_PAD, :] = prev
            hbuf[CONV_PAD:2 * CONV_PAD, :] = yb[:CONV_PAD]
            head = cb
            for k in range(CONV_W):
                off = CONV_PAD - (CONV_W - 1) + k
                head = head + cw[k:k + 1] * hbuf[off:off + CONV_PAD, :]
            prev = yb[EPI_ROWS - CONV_PAD:]
            act = _silu(jnp.concatenate([head, out[CONV_PAD:]], axis=0))
            o_ref[r * EPI_ROWS:(r + 1) * EPI_ROWS, :] = jnp.concatenate(
                [act, ys[r][:, HALF_TILE:]], axis=-1).astype(o_ref.dtype)
        tail_ref[slot] = prev


def _inproj(x2, g, w_big, w_small, cw_all, cb_all, layer, seq):
    n = x2.shape[0]
    nb = w_big.shape[-1]
    n_tiles = nb // TN_PROJ
    assert n_tiles == len(TILES_SILU + TILES_SIGMOID + TILES_MIXED)
    return pl.pallas_call(
        functools.partial(_inproj_kernel, tiles_per_seq=seq // TM_PROJ),
        grid=(n // TM_PROJ, n_tiles),
        in_specs=[
            pl.BlockSpec((TM_PROJ, D_MODEL), lambda i, j: (i, 0)),
            pl.BlockSpec((1, D_MODEL), lambda i, j: (0, 0)),
            pl.BlockSpec((None, D_MODEL, TN_PROJ), lambda i, j: (layer, 0, j)),
            pl.BlockSpec((None, D_MODEL, SMALL_W), lambda i, j: (layer, 0, 0)),
            pl.BlockSpec((None, 1, CONV_W, HALF_TILE), lambda i, j: (layer, j, 0, 0)),
            pl.BlockSpec((None, 1, 1, HALF_TILE), lambda i, j: (layer, j, 0, 0)),
        ],
        out_specs=[
            pl.BlockSpec((TM_PROJ, TN_PROJ), lambda i, j: (i, j)),
            pl.BlockSpec((TM_PROJ, SMALL_W), lambda i, j: (i, 0)),
        ],
        out_shape=[
            jax.ShapeDtypeStruct((n, nb), BF16),
            jax.ShapeDtypeStruct((n, SMALL_W), F32),
        ],
        scratch_shapes=[
            pltpu.VMEM((TM_PROJ, D_MODEL), BF16),
            pltpu.VMEM((len(TILES_MIXED), CONV_PAD, HALF_TILE), F32),
            pltpu.VMEM((2 * CONV_PAD, HALF_TILE), F32),
        ],
        compiler_params=_cparams(("arbitrary", "arbitrary")),
        name="in_projection",
    )(x2, g, w_big, w_small, cw_all, cb_all)


def _hgrn2_stages(q_ref, f_ref, i_ref, g_ref, lb_ref, ng_ref, y_ref, state_ref, c_s):
    @pl.when(pl.program_id(1) == 0)
    def _():
        state_ref[...] = jnp.zeros_like(state_ref)

    lb = lb_ref[...]
    tri = _mask_bf16(_tri_lower(CHUNK))
    n_sub = CHUNK // SUB
    ones_rhs = jnp.ones((HG_D, HG_D), BF16)
    lane = lax.broadcasted_iota(jnp.int32, (SUBLANES, HG_D), 1)
    row = lax.broadcasted_iota(jnp.int32, (SUBLANES, HG_D), 0)

    causal = (lax.broadcasted_iota(jnp.int32, (CHUNK, HG_D), 1)
              <= lax.broadcasted_iota(jnp.int32, (CHUNK, HG_D), 0))

    heads = range(HG_HEADS)
    hsl = [slice(h * HG_D, (h + 1) * HG_D) for h in heads]
    zero_tail = jnp.zeros((HG_D - CHUNK, HG_D), BF16)

    def chunk(c):
        rows = pl.ds(pl.multiple_of(c * CHUNK, CHUNK), CHUNK)
        sig = jax.nn.sigmoid(f_ref[rows, :].astype(F32))
        fgate = lb + (1.0 - lb) * sig
        b = _sel_left(tri, jnp.log(fgate) * LOG2E)
        cc = b - jnp.log(jnp.maximum(1.0 - fgate, 0.0)) * LOG2E
        for h in heads:
            c_s[h] = cc[:, hsl[h]]
        q = q_ref[rows, :].astype(F32)
        v16 = i_ref[rows, :]
        yield

        st = [state_ref[h] for h in heads]
        q_dec = (q * jnp.exp2(b)).astype(BF16)
        b_last = b[CHUNK - 1:CHUNK, :]
        k_dec = jnp.exp2(b_last - cc).astype(BF16)
        st_decay = jnp.exp2(b_last)
        o_inter = [_dot_nt(q_dec[:, hsl[h]], st[h].astype(BF16)) for h in heads]
        for h in heads:
            state_ref[h] = st_decay[:, hsl[h]] * st[h] + _dot_tn(v16[:, hsl[h]], k_dec[:, hsl[h]])
        yield

        def intra_factored():
            k_all = jnp.exp2(-cc).astype(BF16)
            res = []
            for h in heads:
                k_pad = jnp.concatenate([k_all[:, hsl[h]], zero_tail], axis=0)
                res.append(jnp.where(causal, _dot_nt(q_dec[:, hsl[h]], k_pad), 0.0).astype(BF16))
            return tuple(res)

        def intra_blocked():
            zs = []
            for i in range(n_sub):
                lo = i * SUB
                for s in range(SUB):
                    r0 = lo + (s // SUBLANES) * SUBLANES
                    c_row = jnp.concatenate([_bcast_row(c_s, h, lo + s) for h in heads], axis=-1)
                    c_row = jnp.concatenate([c_row] * ((lo + SUB - r0) // SUBLANES), axis=0)
                    zs.append(q[r0:lo + SUB] * jnp.exp2(b[r0:lo + SUB] - c_row))
            z_rows = sum(z.shape[0] for z in zs)
            z_all = jnp.concatenate([z[:, hsl[h]] for h in heads for z in zs], axis=0)
            r = _dot(z_all.astype(BF16), ones_rhs)

            a_off = []
            for i in range(1, n_sub):
                lo = i * SUB
                bref = b[lo - 1:lo, :]
                q_i = (q[lo:lo + SUB] * jnp.exp2(b[lo:lo + SUB] - bref)).astype(BF16)
                k_i = jnp.exp2(bref - cc[:lo]).astype(BF16)
                zero_rows = jnp.zeros((HG_D - lo, HG_D), BF16)
                a_off.append([_dot_nt(q_i[:, hsl[h]], jnp.concatenate([k_i[:, hsl[h]], zero_rows], axis=0))
                              for h in heads])

            res = []
            for h in heads:
                a_rows = []
                off = h * z_rows
                for i in range(n_sub):
                    lo = i * SUB
                    a_i = a_off[i - 1][h] if i > 0 else jnp.zeros((SUB, HG_D), F32)
                    tiles = [a_i[j * SUBLANES:(j + 1) * SUBLANES] for j in range(SUB // SUBLANES)]
                    for s in range(SUB):
                        for j in range(s // SUBLANES, SUB // SUBLANES):
                            tiles[j] = jnp.where(lane == lo + s, r[off:off + SUBLANES], tiles[j])
                            off += SUBLANES
                    for j in range(SUB // SUBLANES):
                        a_rows.append(jnp.where(lane - lo <= row + j * SUBLANES, tiles[j], 0.0))
                res.append(jnp.concatenate(a_rows, axis=0).astype(BF16))
            return tuple(res)

        a_heads = lax.cond(jnp.min(b_last) >= -FACTORED_DECAY_LIMIT, intra_factored, intra_blocked)
        yield

        outs = []
        for h in heads:
            v_pad = jnp.concatenate([v16[:, hsl[h]], zero_tail], axis=0)
            outs.append(o_inter[h] + _dot(a_heads[h], v_pad))
            if h % 2 == 1:
                yield
        parts = []
        for o in outs:
            parts.append(o * lax.rsqrt(jnp.mean(o * o, axis=-1, keepdims=True) + EPS))
        on = jnp.concatenate(parts, axis=-1) * ng_ref[...]
        y_ref[rows, :] = (on * g_ref[rows, :].astype(F32)).astype(y_ref.dtype)

    return chunk


def _mlstm_stages(qk_ref, v_ref, og_ref, sm_ref, smt_ref, gbr_ref, gbc_ref, ng_ref,
                  y_ref, caug_ref, m_ref):
    half = ML_HEADS * ML_DQK
    k_scale = ML_DQK ** -0.5

    @pl.when(pl.program_id(1) == 0)
    def _():
        caug_ref[...] = jnp.zeros_like(caug_ref)
        m_ref[...] = jnp.zeros_like(m_ref)

    tri_b = _tri_lower(CHUNK)
    tri_l = _mask_bf16(tri_b)
    tri_u = _mask_bf16(lax.broadcasted_iota(jnp.int32, (CHUNK, CHUNK), 0)
                       <= lax.broadcasted_iota(jnp.int32, (CHUNK, CHUNK), 1))
    ones_col = _mask_bf16(lax.broadcasted_iota(jnp.int32, (CHUNK, LANES), 1) == 0)

    def chunk(c):
        rows = pl.ds(pl.multiple_of(c * CHUNK, CHUNK), CHUNK)
        pre_c = sm_ref[rows, :] + gbr_ref[...]
        pre_r = smt_ref[c] + gbc_ref[...]
        cum_c = _sel_left(tri_l, _log_sigmoid(pre_c))
        cum_r = _sel_right(_log_sigmoid(pre_r), tri_u)
        yield
        heads = range(ML_HEADS)
        m_all = m_ref[...]
        m_old = [m_all[h:h + 1, 0:1] for h in heads]
        b_col = [cum_c[:, ML_HEADS + h:ML_HEADS + h + 1] for h in heads]
        i_col = [pre_c[:, h:h + 1] for h in heads]
        log_d = [jnp.where(tri_b, b_col[h] - cum_r[ML_HEADS + h:ML_HEADS + h + 1, :] + pre_r[h:h + 1, :],
                           NEG_BIG) for h in heads]
        log_inter = [b_col[h] + m_old[h] for h in heads]
        m_t = [jnp.maximum(jnp.max(log_d[h], axis=-1, keepdims=True), log_inter[h]) for h in heads]
        b_last = [b_col[h][CHUNK - 1:CHUNK, :] for h in heads]
        log_w = [b_last[h] - b_col[h] + i_col[h] for h in heads]
        m_new = [jnp.maximum(b_last[h] + m_old[h], jnp.max(log_w[h], axis=0, keepdims=True)) for h in heads]
        m_ref[...] = jnp.concatenate(
            [jnp.broadcast_to(m_new[h], (1, m_ref.shape[1])) for h in heads] + [m_all[ML_HEADS:]], axis=0)
        yield

        qb = [qk_ref[rows, h * ML_DQK:(h + 1) * ML_DQK] for h in heads]
        kb = [qk_ref[rows, half + h * ML_DQK:half + (h + 1) * ML_DQK] for h in heads]
        v_aug = [jnp.concatenate([v_ref[rows, h * ML_DV:(h + 1) * ML_DV], ones_col], axis=-1)
                 for h in heads]
        qk = [_dot_nt(qb[h], kb[h]) for h in heads]
        c_aug = [caug_ref[h] for h in heads]
        inter = [_dot(qb[h], c_aug[h].astype(BF16)) * jnp.exp(log_inter[h] - m_t[h]) for h in heads]
        yield
        s = [(qk[h] * (jnp.exp(log_d[h] - m_t[h]) * k_scale)).astype(BF16) for h in heads]
        num = [_dot(s[h], v_aug[h]) + inter[h] for h in heads]
        yield
        kw = [(kb[h].astype(F32) * (jnp.exp(log_w[h] - m_new[h]) * k_scale)).astype(BF16) for h in heads]
        for h in heads:
            caug_ref[h] = jnp.exp(b_last[h] + m_old[h] - m_new[h]) * c_aug[h] + _dot_tn(kw[h], v_aug[h])
        yield

        parts = []
        for h in heads:
            denom = jnp.maximum(jnp.abs(num[h][:, ML_DV:ML_DV + 1]), jnp.exp(-m_t[h]))
            o = num[h][:, :ML_DV] / denom
            parts.append(o * lax.rsqrt(jnp.mean(o * o, axis=-1, keepdims=True) + EPS))
        on = jnp.concatenate(parts, axis=-1) * ng_ref[...]
        y_ref[rows, :] = (on * og_ref[rows, :].astype(F32)).astype(y_ref.dtype)

    return chunk


def _ssd_stages(z_ref, xbc_ref, sm_ref, dtt_ref, dtb_c_ref, alog_c_ref,
                dtb_r_ref, alog_r_ref, d_ref, ng_ref, y_ref, state_ref):
    n_pair = MB_HEADS // 2
    pair_w = 2 * MB_P
    gw = MB_GROUPS * MB_N

    @pl.when(pl.program_id(1) == 0)
    def _():
        state_ref[...] = jnp.zeros_like(state_ref)

    tri_l = _mask_bf16(_tri_lower(CHUNK))
    sel_x = _mask_bf16(lax.broadcasted_iota(jnp.int32, (SMALL_W, MB_W), 0) - DT_COL
                       == jnp.right_shift(lax.broadcasted_iota(jnp.int32, (SMALL_W, MB_W), 1), MB_P_LOG2))
    ur = lax.broadcasted_iota(jnp.int32, (pair_w, pair_w), 0)
    uc = lax.broadcasted_iota(jnp.int32, (pair_w, pair_w), 1)
    same_half = jnp.right_shift(ur, CHUNK_LOG2) == jnp.right_shift(uc, CHUNK_LOG2)
    tri_u2 = _mask_bf16(same_half & (ur <= uc))
    causal = (lax.broadcasted_iota(jnp.int32, (CHUNK, MB_W), 0)
              >= jnp.bitwise_and(lax.broadcasted_iota(jnp.int32, (CHUNK, MB_W), 1), CHUNK - 1))
    first_head = lax.broadcasted_iota(jnp.int32, (CHUNK, pair_w), 1) < MB_P
    a_c = -jnp.exp(alog_c_ref[...])
    a_r = -jnp.exp(alog_r_ref[...])

    def chunk(c):
        rows = pl.ds(pl.multiple_of(c * CHUNK, CHUNK), CHUNK)
        dt_c = _softplus(sm_ref[rows, :] + dtb_c_ref[...])
        cum_c = _sel_left(tri_l, dt_c * a_c)
        dt_x = _sel_right(dt_c, sel_x)
        cum_x = _sel_right(cum_c, sel_x)
        dt_r = _softplus(dtt_ref[c] + dtb_r_ref[...])
        cum_r = _sel_right(dt_r * a_r, tri_u2)
        yield
        pairs = range(n_pair)
        grp = [(2 * p) // (MB_HEADS // MB_GROUPS) for p in pairs]
        lanes = [slice(p * pair_w, (p + 1) * pair_w) for p in pairs]
        xs = xbc_ref[rows, :MB_W].astype(F32)
        xdt = xs * dt_x
        cum_row = jnp.concatenate([cum_r[p:p + 1, :] for p in pairs], axis=-1)
        decay = jnp.exp(jnp.where(causal, cum_x - cum_row, NEG_BIG))
        cum_last = cum_x[CHUNK - 1:CHUNK, :]
        x_dec = (xdt * jnp.exp(cum_last - cum_x)).astype(BF16)
        st_decay = jnp.exp(cum_last)
        carry_w = jnp.exp(cum_x)
        yield
        bm = [xbc_ref[rows, MB_W + g * MB_N:MB_W + (g + 1) * MB_N] for g in range(MB_GROUPS)]
        cm = [xbc_ref[rows, MB_W + gw + g * MB_N:MB_W + gw + (g + 1) * MB_N] for g in range(MB_GROUPS)]
        cb2 = [_dot_nt(cm[g], jnp.concatenate([bm[g], bm[g]], axis=0)) for g in range(MB_GROUPS)]
        x2 = [jnp.concatenate([jnp.where(first_head, xdt[:, lanes[p]], 0.0),
                               jnp.where(first_head, 0.0, xdt[:, lanes[p]])], axis=0).astype(BF16)
              for p in pairs]
        yield
        st = [state_ref[p] for p in pairs]
        y_inter = [_dot(cm[grp[p]], st[p].astype(BF16)) for p in pairs]
        y_intra = [_dot((cb2[grp[p]] * decay[:, lanes[p]]).astype(BF16), x2[p]) for p in pairs]
        yield
        for p in pairs:
            state_ref[p] = st_decay[:, lanes[p]] * st[p] + _dot_tn(bm[grp[p]], x_dec[:, lanes[p]])
        yield
        y = (jnp.concatenate(y_intra, axis=-1) + jnp.concatenate(y_inter, axis=-1) * carry_w
             + d_ref[...] * xs)

        yz = y * z_ref[rows, :].astype(F32)
        gsz = MB_W // MB_GROUPS
        parts = []
        for g in range(MB_GROUPS):
            o = yz[:, g * gsz:(g + 1) * gsz]
            parts.append(o * lax.rsqrt(jnp.mean(o * o, axis=-1, keepdims=True) + EPS))
        y_ref[rows, :] = (jnp.concatenate(parts, axis=-1) * ng_ref[...]).astype(y_ref.dtype)

    return chunk


_DONE = object()


class _ColumnView:
    def __init__(self, pieces):
        self.pieces = pieces
        self.width = sum(w for _, _, w in pieces)

    def __getitem__(self, idx):
        rows, cols = idx
        lo, hi, _ = cols.indices(self.width)
        out, base = [], 0
        for ref, start, w in self.pieces:
            a, b = max(lo, base), min(hi, base + w)
            if a < b:
                out.append(ref[rows, start + a - base:start + b - base])
            base += w
        return out[0] if len(out) == 1 else jnp.concatenate(out, axis=-1)


def _mixers_kernel(q_ref, g_ref, z_ref, og_ref, m0, m1, m2, m3, m4, m5, sm_ref, smt_ref, dtt_ref,
                   lb_ref, hg_ng_ref, gbr_ref, gbc_ref, ml_ng_ref,
                   dtb_c_ref, alog_c_ref, dtb_r_ref, alog_r_ref, d_ref, mb_ng_ref,
                   y_hg, y_ml, y_mb, hg_state, hg_c, ml_caug, ml_m, mb_state):
    ts = y_hg.shape[0]
    mixed = (m0, m1, m2, m3, m4, m5)
    conv = [(m, 0, HALF_TILE) for m in mixed]
    raw = [(m, HALF_TILE, HALF_TILE) for m in mixed]
    qk_view = _ColumnView(conv[0:2])
    xbc_view = _ColumnView(conv[2:6])
    f_view = _ColumnView(raw[0:2])
    i_view = _ColumnView(raw[2:4])
    v_view = _ColumnView(raw[4:6])
    mixers = [
        _hgrn2_stages(q_ref, f_view, i_view, g_ref, lb_ref, hg_ng_ref, y_hg, hg_state, hg_c),
        _mlstm_stages(qk_view, v_view, og_ref, sm_ref, smt_ref, gbr_ref, gbc_ref, ml_ng_ref,
                      y_ml, ml_caug, ml_m),
        _ssd_stages(z_ref, xbc_view, sm_ref, dtt_ref, dtb_c_ref, alog_c_ref, dtb_r_ref, alog_r_ref,
                    d_ref, mb_ng_ref, y_mb, mb_state),
    ]

    def chunk_body(c, carry):
        live = [m(c) for m in mixers]
        while live:
            for g in list(live):
                if next(g, _DONE) is _DONE:
                    live.remove(g)
        return carry

    lax.fori_loop(0, ts // CHUNK, chunk_body, 0, unroll=CHUNK_UNROLL)


def _mixers(proj, small, small_t, dt_t, lb, hg_ng, gb_row, gb_col, ml_ng,
            dtb_c, alog_c, dtb_r, alog_r, d_x, mb_ng, batch, seq):
    n = proj.shape[0]
    nt = seq // TS_MIX
    n_pair = MB_HEADS // 2
    w = D_MODEL

    def col(cb):
        return pl.BlockSpec((TS_MIX, w), lambda b, j: (b * nt + j, cb))

    def const(shape):
        return pl.BlockSpec(shape, lambda b, j: tuple(0 for _ in shape))

    tile_specs = [col(T_HG_Q), col(T_HG_G), col(T_MB_Z), col(T_ML_O)] + [col(t) for t in TILES_MIXED]
    gate_specs = [
        pl.BlockSpec((TS_MIX, SMALL_W), lambda b, j: (b * nt + j, 0)),
        pl.BlockSpec((TS_MIX // CHUNK, 2 * ML_HEADS, CHUNK), lambda b, j: (b * nt + j, 0, 0)),
        pl.BlockSpec((TS_MIX // CHUNK, n_pair, 2 * CHUNK), lambda b, j: (b * nt + j, 0, 0)),
    ]
    param_specs = [const((1, w)), const((1, w)),
                   const((1, SMALL_W)), const((2 * ML_HEADS, 1)), const((1, w)),
                   const((1, SMALL_W)), const((1, SMALL_W)),
                   const((n_pair, 2 * CHUNK)), const((n_pair, 2 * CHUNK)),
                   const((1, w)), const((1, w))]
    out = pl.BlockSpec((TS_MIX, w), lambda b, j: (b * nt + j, 0))
    return pl.pallas_call(
        _mixers_kernel,
        grid=(batch, nt),
        in_specs=tile_specs + gate_specs + param_specs,
        out_specs=[out, out, out],
        out_shape=[jax.ShapeDtypeStruct((n, w), BF16)] * 3,
        scratch_shapes=[
            pltpu.VMEM((HG_HEADS, HG_D, HG_D), F32),
            pltpu.VMEM((HG_HEADS, CHUNK, HG_D), F32),
            pltpu.VMEM((ML_HEADS, ML_DQK, ML_DV + LANES), F32),
            pltpu.VMEM((SUBLANES, LANES), F32),
            pltpu.VMEM((n_pair, MB_N, 2 * MB_P), F32),
        ],
        compiler_params=_cparams(("parallel", "arbitrary")),
        name="token_mixers",
    )(*([proj] * (4 + len(TILES_MIXED))), small, small_t, dt_t,
      lb, hg_ng, gb_row, gb_col, ml_ng, dtb_c, alog_c, dtb_r, alog_r, d_x, mb_ng)


def _merge_kernel(x_ref, yh_ref, ym_ref, yb_ref, g0_ref, g1_ref, g2_ref,
                  wh_ref, wm_ref, wb_ref, wo_ref, o_ref):
    mixed = g0_ref[...].astype(F32) * _dot(yh_ref[...], wh_ref[...])
    mixed = mixed + g1_ref[...].astype(F32) * _dot(ym_ref[...], wm_ref[...])
    mixed = mixed + g2_ref[...].astype(F32) * _dot(yb_ref[...], wb_ref[...])
    o_ref[...] = x_ref[...] + _dot(mixed.astype(BF16), wo_ref[...])


def _merge(x2, y_hg, y_ml, y_mb, proj, w_hg, w_ml, w_mb, w_out, layer):
    n = x2.shape[0]
    tile = lambda cb: pl.BlockSpec((TM_MERGE, D_MODEL), lambda i: (i, cb))
    wspec = pl.BlockSpec((None, D_MODEL, D_MODEL), lambda i: (layer, 0, 0))
    return pl.pallas_call(
        _merge_kernel,
        grid=(n // TM_MERGE,),
        in_specs=[tile(0), tile(0), tile(0), tile(0), tile(T_GATE0), tile(T_GATE0 + 1), tile(T_GATE0 + 2),
                  wspec, wspec, wspec, wspec],
        out_specs=tile(0),
        out_shape=jax.ShapeDtypeStruct((n, D_MODEL), F32),
        compiler_params=_cparams(("parallel",)),
        name="branch_merge",
    )(x2, y_hg, y_ml, y_mb, proj, proj, proj, w_hg, w_ml, w_mb, w_out)


def _ffn_kernel(x_ref, g_ref, wu_ref, cw_ref, cb_ref, wd_ref, fg_ref,
                o_ref, tail, hbuf_g, hbuf_v, *, final):
    tm = x_ref.shape[0]
    n_ck = D_FF // FF_CHUNK

    @pl.when(pl.program_id(1) == 0)
    def _():
        tail[...] = jnp.zeros_like(tail)

    x = x_ref[...]
    hb = (x * lax.rsqrt(jnp.mean(x * x, axis=-1, keepdims=True) + EPS) * g_ref[...]).astype(BF16)

    def cols(c, half):
        return slice(half * D_FF + c * FF_CHUNK, half * D_FF + (c + 1) * FF_CHUNK)

    def up(c):
        return _dot(hb, wu_ref[:, cols(c, 0)]), _dot(hb, wu_ref[:, cols(c, 1)])

    def conv(u, hbuf, cs):
        out = cb_ref[:, cs] + cw_ref[FFN_CONV - 1:FFN_CONV, cs] * u
        for k in range(FFN_CONV - 1):
            out = out + cw_ref[k:k + 1, cs] * pltpu.roll(u, FFN_CONV - 1 - k, axis=0)
        hbuf[0:CONV_PAD, :] = tail[:, cs]
        hbuf[CONV_PAD:2 * CONV_PAD, :] = u[:CONV_PAD]
        tail[:, cs] = u[tm - CONV_PAD:]
        head = cb_ref[:, cs]
        for k in range(FFN_CONV):
            off = CONV_PAD - (FFN_CONV - 1) + k
            head = head + cw_ref[k:k + 1, cs] * hbuf[off:off + CONV_PAD, :]
        return jnp.concatenate([head, out[CONV_PAD:]], axis=0)

    acc = x
    u_next = up(0)
    for c in range(n_ck):
        u_g, u_v = u_next
        if c + 1 < n_ck:
            u_next = up(c + 1)
        a_g = conv(u_g, hbuf_g, cols(c, 0))
        a_v = conv(u_v, hbuf_v, cols(c, 1))
        acc = acc + _dot((_silu(a_g) * a_v).astype(BF16), wd_ref[cols(c, 0), :])
    if final:
        acc = acc * lax.rsqrt(jnp.mean(acc * acc, axis=-1, keepdims=True) + EPS) * fg_ref[...]
    o_ref[...] = acc


def _ffn(x2, g, w_up, cw, cb, w_down, final_g, layer, batch, seq, final):
    n = x2.shape[0]
    nt = seq // TM_FFN

    def const(shape):
        return pl.BlockSpec(shape, lambda b, j: tuple(0 for _ in shape),
                            pipeline_mode=pl.Buffered(1))

    def layer_weight(shape):
        return pl.BlockSpec((None,) + shape, lambda b, j: (layer,) + tuple(0 for _ in shape),
                            pipeline_mode=pl.Buffered(1))

    tile = pl.BlockSpec((TM_FFN, D_MODEL), lambda b, j: (b * nt + j, 0))
    return pl.pallas_call(
        functools.partial(_ffn_kernel, final=final),
        grid=(batch, nt),
        in_specs=[
            tile, const((1, D_MODEL)),
            layer_weight((D_MODEL, 2 * D_FF)), const((FFN_CONV, 2 * D_FF)), const((1, 2 * D_FF)),
            layer_weight((D_FF, D_MODEL)), const((1, D_MODEL)),
        ],
        out_specs=tile,
        out_shape=jax.ShapeDtypeStruct((n, D_MODEL), F32),
        scratch_shapes=[
            pltpu.VMEM((CONV_PAD, 2 * D_FF), F32),
            pltpu.VMEM((2 * CONV_PAD, FF_CHUNK), F32),
            pltpu.VMEM((2 * CONV_PAD, FF_CHUNK), F32),
        ],
        compiler_params=_cparams(("parallel", "arbitrary")),
        name="conv_gated_mlp",
    )(x2, g, w_up, cw, cb, w_down, final_g)


def kernel(x, norm1_g, w_in, hg_lb_logits, hg_norm_g, ml_conv_w, ml_conv_b, ml_gate_b, ml_norm_g,
           mb_conv_w, mb_conv_b, mb_dt_bias, mb_a_log, mb_d, mb_norm_g, w_br_hg, w_br_ml, w_br_mb,
           w_out, norm2_g, w_up, ffn_conv_w, ffn_conv_b, w_down, final_g):
    batch, seq, _ = x.shape
    n = batch * seq
    assert seq % TS_MIX == 0 and seq % TM_FFN == 0 and seq % TM_PROJ == 0 and n % TM_MERGE == 0
    depth = w_in.shape[0]

    w = D_MODEL
    o_hg_q, o_hg_f, o_hg_i, o_hg_g = 0, w, 2 * w, 3 * w
    o_ml_qk = 4 * w
    o_ml_v = o_ml_qk + 2 * ML_HEADS * ML_DQK
    o_if = o_ml_v + ML_HEADS * ML_DV
    o_ml_o = o_if + 2 * ML_HEADS
    o_mb_z = o_ml_o + ML_HEADS * ML_DV
    o_mb_xbc = o_mb_z + MB_W
    o_dt = o_mb_xbc + MB_CONV_DIM
    o_gate = o_dt + MB_HEADS
    hw = HALF_TILE
    conv_halves = [o_ml_qk + k * hw for k in range(2)] + [o_mb_xbc + k * hw for k in range(4)]
    raw_halves = [o_hg_f, o_hg_f + hw, o_hg_i, o_hg_i + hw, o_ml_v, o_ml_v + hw]
    cols = [(o_hg_q, w), (o_hg_g, w), (o_mb_z, w), (o_ml_o, w), (o_gate, N_GATES * w)]
    for c0, r0 in zip(conv_halves, raw_halves):
        cols += [(c0, hw), (r0, hw)]
    w_in16 = w_in.astype(BF16)
    w_big = jnp.concatenate([w_in16[:, :, a:a + n_] for a, n_ in cols], axis=-1)
    pad = SMALL_W - 2 * ML_HEADS - MB_HEADS
    w_small = jnp.concatenate(
        [w_in16[:, :, o_if:o_ml_o], w_in16[:, :, o_dt:o_gate],
         jnp.zeros((depth, D_MODEL, pad), BF16)], axis=-1)

    lbs = _lbs(hg_lb_logits.astype(F32))
    hg_ng = jnp.tile(hg_norm_g, (1, HG_HEADS))
    ml_ng = jnp.tile(ml_norm_g, (1, ML_HEADS))
    gb_row = jnp.pad(ml_gate_b, ((0, 0), (0, SMALL_W - 2 * ML_HEADS)))
    dtb_c = jnp.pad(mb_dt_bias, ((0, 0), (DT_COL, SMALL_W - DT_COL - MB_HEADS)))
    alog_c = jnp.pad(mb_a_log, ((0, 0), (DT_COL, SMALL_W - DT_COL - MB_HEADS)))
    n_pair = MB_HEADS // 2
    dtb_r = jnp.repeat(mb_dt_bias, CHUNK, axis=-1).reshape(depth, n_pair, 2 * CHUNK)
    alog_r = jnp.repeat(mb_a_log, CHUNK, axis=-1).reshape(depth, n_pair, 2 * CHUNK)
    d_x = jnp.repeat(mb_d, MB_P, axis=-1)

    n_tiles = w_big.shape[-1] // TN_PROJ
    n_mixed = len(TILES_MIXED)
    conv_w = jnp.concatenate([ml_conv_w, mb_conv_w], axis=-1).reshape(depth, CONV_W, n_mixed, hw)
    conv_b = jnp.concatenate([ml_conv_b, mb_conv_b], axis=-1).reshape(depth, 1, n_mixed, hw)
    front = n_tiles - n_mixed
    cw_all = jnp.pad(jnp.swapaxes(conv_w, 1, 2), ((0, 0), (front, 0), (0, 0), (0, 0)))
    cb_all = jnp.pad(jnp.swapaxes(conv_b, 1, 2), ((0, 0), (front, 0), (0, 0), (0, 0)))

    w_hg = w_br_hg.astype(BF16)
    w_ml = w_br_ml.astype(BF16)
    w_mb = w_br_mb.astype(BF16)
    w_o = w_out.astype(BF16)
    w_u = w_up.astype(BF16)
    w_d = w_down.astype(BF16)

    x2 = x.reshape(n, D_MODEL)
    row = lambda a: a.reshape(1, -1)
    for l in range(depth):
        proj, small = _inproj(x2, row(norm1_g[l]), w_big, w_small, cw_all, cb_all, l, seq)
        chunks = small.reshape(n // CHUNK, CHUNK, SMALL_W)
        small_t = jnp.swapaxes(chunks[:, :, :2 * ML_HEADS], 1, 2)
        dt_t = jnp.swapaxes(chunks[:, :, DT_COL:DT_COL + MB_HEADS], 1, 2).reshape(
            n // CHUNK, n_pair, 2 * CHUNK)

        y_hg, y_ml, y_mb = _mixers(
            proj, small, small_t, dt_t, row(lbs[l]), row(hg_ng[l]),
            row(gb_row[l]), ml_gate_b[l].reshape(2 * ML_HEADS, 1), row(ml_ng[l]),
            row(dtb_c[l]), row(alog_c[l]), dtb_r[l], alog_r[l], row(d_x[l]), row(mb_norm_g[l]), batch, seq)
        x2 = _merge(x2, y_hg, y_ml, y_mb, proj, w_hg, w_ml, w_mb, w_o, l)
        x2 = _ffn(x2, row(norm2_g[l]), w_u, ffn_conv_w[l], row(ffn_conv_b[l]), w_d, row(final_g),
                  l, batch, seq, final=(l == depth - 1))
    return x2.reshape(batch, seq, D_MODEL)
```

```python
import functools
import itertools
import math

import jax
import jax.numpy as jnp
from jax import lax
from jax.experimental import pallas as pl
from jax.experimental.pallas import tpu as pltpu

F32 = jnp.float32
BF16 = jnp.bfloat16

D_MODEL = 1024
CHUNK = 64
CHUNK_LOG2 = 6
SUB = 8
FACTORED_DECAY_LIMIT = 112.0
SUBLANES = 8
LANES = 128
LOG2E = math.log2(math.e)
EPS = 1e-6
NEG_BIG = -1e30
HG_HEADS = 8
HG_D = 128
ML_HEADS = 4
ML_DQK = 128
ML_DV = 256
MB_HEADS = 16
MB_P = 64
MB_P_LOG2 = 6
MB_GROUPS = 4
MB_N = 128
MB_W = MB_HEADS * MB_P
MB_CONV_DIM = MB_W + 2 * MB_GROUPS * MB_N
D_FF = 2816
FFN_CONV = 3
FF_CHUNK = 1408
SMALL_W = 128
DT_COL = 8
CONV_PAD = 8

VMEM_LIMIT = 56 * 1024 * 1024

TM_PROJ = 2048
TN_PROJ = 1024
EPI_ROWS = 256
TILES_SILU = (0, 1, 2)
TILES_SIGMOID = (3, 4, 5, 6)
TILES_MIXED = (7, 8, 9, 10, 11, 12)
T_HG_Q, T_HG_G, T_MB_Z, T_ML_O, T_GATE0 = 0, 1, 2, 3, 4
N_GATES = 3
HALF_TILE = TN_PROJ // 2
CONV_W = 4
TS_MIX = 512
CHUNK_UNROLL = 4
ML_BLOCK = 2 * CHUNK
TM_MERGE = 512
TM_FFN = 512


def _silu(x):
    return x * jax.nn.sigmoid(x)


def _softplus(x):
    return jnp.maximum(x, 0.0) + jnp.log1p(jnp.exp(-jnp.abs(x)))


def _log_sigmoid(x):
    return jnp.minimum(x, 0.0) - jnp.log1p(jnp.exp(-jnp.abs(x)))


def _dot(a, b):
    return jnp.dot(a, b, preferred_element_type=F32)


def _dot_nt(a, b):
    return lax.dot_general(a, b, (((1,), (1,)), ((), ())), preferred_element_type=F32)


def _dot_tn(a, b):
    return lax.dot_general(a, b, (((0,), (0,)), ((), ())), preferred_element_type=F32)


def _split3(a):
    hi = a.astype(BF16)
    r = a - hi.astype(F32)
    mid = r.astype(BF16)
    lo = (r - mid.astype(F32)).astype(BF16)
    return hi, mid, lo


def _sel_left(sel, a):
    hi, mid, lo = _split3(a)
    return _dot(sel, hi) + (_dot(sel, mid) + _dot(sel, lo))


def _sel_right(a, sel):
    hi, mid, lo = _split3(a)
    return _dot(hi, sel) + (_dot(mid, sel) + _dot(lo, sel))


def _tri_lower(n):
    r = lax.broadcasted_iota(jnp.int32, (n, n), 0)
    c = lax.broadcasted_iota(jnp.int32, (n, n), 1)
    return r >= c


def _bcast_row(ref, h, r):
    return ref[h, pl.ds(r, SUBLANES, stride=0), :]


def _mask_bf16(m):
    return jnp.where(m, 1.0, 0.0).astype(BF16)


def _cparams(sem):
    return pltpu.CompilerParams(dimension_semantics=sem, vmem_limit_bytes=VMEM_LIMIT)


def _lbs_kernel(lg_ref, o_ref):
    lg = lg_ref[...]
    mx = jnp.max(lg, axis=0, keepdims=True)
    e = jnp.exp(lg - mx)
    p = e / jnp.sum(e, axis=0, keepdims=True)
    acc = jnp.zeros_like(p[0:1])
    rows = []
    for l in range(lg.shape[0]):
        acc = acc + p[l:l + 1]
        rows.append(acc - p[0:1])
    o_ref[...] = jnp.concatenate(rows, axis=0)


def _lbs(logits):
    return pl.pallas_call(
        _lbs_kernel,
        out_shape=jax.ShapeDtypeStruct(logits.shape, F32),
        name="hgrn2_lower_bounds",
    )(logits)


def _any_tile(j, tiles):
    hit = j == tiles[0]
    for t in tiles[1:]:
        hit = jnp.logical_or(hit, j == t)
    return hit


def _inproj_kernel(x_ref, g_ref, w_ref, ws_ref, cw_ref, cb_ref, o_ref, os_ref, h_ref, tail_ref, hbuf,
                   *, tiles_per_seq):
    i = pl.program_id(0)
    j = pl.program_id(1)
    n_blk = TM_PROJ // EPI_ROWS

    @pl.when(jnp.logical_and(i == 0, j == 0))
    def _():
        tail_ref[...] = jnp.zeros_like(tail_ref)

    @pl.when(j == 0)
    def _():
        x = x_ref[...]
        ms = jnp.mean(x * x, axis=-1, keepdims=True)
        hb = (x * lax.rsqrt(ms + EPS) * g_ref[...]).astype(BF16)
        h_ref[...] = hb
        os_ref[...] = _dot(hb, ws_ref[...])

    def pointwise(fn):
        ys = [_dot(h_ref[r * EPI_ROWS:(r + 1) * EPI_ROWS, :], w_ref[...]) for r in range(n_blk)]
        for r in range(n_blk):
            o_ref[r * EPI_ROWS:(r + 1) * EPI_ROWS, :] = fn(ys[r]).astype(o_ref.dtype)

    @pl.when(_any_tile(j, TILES_SILU))
    def _():
        pointwise(_silu)

    @pl.when(_any_tile(j, TILES_SIGMOID))
    def _():
        pointwise(jax.nn.sigmoid)

    @pl.when(j >= TILES_MIXED[0])
    def _():
        slot = j - TILES_MIXED[0]
        prev = jnp.where(i % tiles_per_seq == 0, 0.0, tail_ref[slot])
        cw = cw_ref[0]
        cb = cb_ref[0]
        ys = [_dot(h_ref[r * EPI_ROWS:(r + 1) * EPI_ROWS, :], w_ref[...]) for r in range(n_blk)]
        for r in range(n_blk):
            yb = ys[r][:, :HALF_TILE]
            out = cb + cw[CONV_W - 1:CONV_W] * yb
            for k in range(CONV_W - 1):
                out = out + cw[k:k + 1] * pltpu.roll(yb, CONV_W - 1 - k, axis=0)
            hbuf[0:CONV_PAD, :] = prev
            hbuf[CONV_PAD:2 * CONV_PAD, :] = yb[:CONV_PAD]
            head = cb
            for k in range(CONV_W):
                off = CONV_PAD - (CONV_W - 1) + k
                head = head + cw[k:k + 1] * hbuf[off:off + CONV_PAD, :]
            prev = yb[EPI_ROWS - CONV_PAD:]
            act = _silu(jnp.concatenate([head, out[CONV_PAD:]], axis=0))
            o_ref[r * EPI_ROWS:(r + 1) * EPI_ROWS, :] = jnp.concatenate(
                [act, ys[r][:, HALF_TILE:]], axis=-1).astype(o_ref.dtype)
        tail_ref[slot] = prev


def _inproj(x2, g, w_big, w_small, cw_all, cb_all, layer, seq):
    n = x2.shape[0]
    nb = w_big.shape[-1]
    n_tiles = nb // TN_PROJ
    assert n_tiles == len(TILES_SILU + TILES_SIGMOID + TILES_MIXED)
    return pl.pallas_call(
        functools.partial(_inproj_kernel, tiles_per_seq=seq // TM_PROJ),
        grid=(n // TM_PROJ, n_tiles),
        in_specs=[
            pl.BlockSpec((TM_PROJ, D_MODEL), lambda i, j: (i, 0)),
            pl.BlockSpec((1, D_MODEL), lambda i, j: (0, 0)),
            pl.BlockSpec((None, D_MODEL, TN_PROJ), lambda i, j: (layer, 0, j)),
            pl.BlockSpec((None, D_MODEL, SMALL_W), lambda i, j: (layer, 0, 0)),
            pl.BlockSpec((None, 1, CONV_W, HALF_TILE), lambda i, j: (layer, j, 0, 0)),
            pl.BlockSpec((None, 1, 1, HALF_TILE), lambda i, j: (layer, j, 0, 0)),
        ],
        out_specs=[
            pl.BlockSpec((TM_PROJ, TN_PROJ), lambda i, j: (i, j)),
            pl.BlockSpec((TM_PROJ, SMALL_W), lambda i, j: (i, 0)),
        ],
        out_shape=[
            jax.ShapeDtypeStruct((n, nb), BF16),
            jax.ShapeDtypeStruct((n, SMALL_W), F32),
        ],
        scratch_shapes=[
            pltpu.VMEM((TM_PROJ, D_MODEL), BF16),
            pltpu.VMEM((len(TILES_MIXED), CONV_PAD, HALF_TILE), F32),
            pltpu.VMEM((2 * CONV_PAD, HALF_TILE), F32),
        ],
        compiler_params=_cparams(("arbitrary", "arbitrary")),
        name="in_projection",
    )(x2, g, w_big, w_small, cw_all, cb_all)


def _hgrn2_stages(q_ref, f_ref, i_ref, g_ref, lb_ref, ng_ref, y_ref, state_ref, c_s):
    @pl.when(pl.program_id(1) == 0)
    def _():
        state_ref[...] = jnp.zeros_like(state_ref)

    lb = lb_ref[...]
    tri = _mask_bf16(_tri_lower(CHUNK))
    n_sub = CHUNK // SUB
    ones_rhs = jnp.ones((HG_D, HG_D), BF16)
    lane = lax.broadcasted_iota(jnp.int32, (SUBLANES, HG_D), 1)
    row = lax.broadcasted_iota(jnp.int32, (SUBLANES, HG_D), 0)

    causal = (lax.broadcasted_iota(jnp.int32, (CHUNK, HG_D), 1)
              <= lax.broadcasted_iota(jnp.int32, (CHUNK, HG_D), 0))

    heads = range(HG_HEADS)
    hsl = [slice(h * HG_D, (h + 1) * HG_D) for h in heads]
    zero_tail = jnp.zeros((HG_D - CHUNK, HG_D), BF16)

    def chunk(c):
        rows = pl.ds(pl.multiple_of(c * CHUNK, CHUNK), CHUNK)
        sig = jax.nn.sigmoid(f_ref[rows, :].astype(F32))
        fgate = lb + (1.0 - lb) * sig
        b = _sel_left(tri, jnp.log(fgate) * LOG2E)
        cc = b - jnp.log(jnp.maximum(1.0 - fgate, 0.0)) * LOG2E
        for h in heads:
            c_s[h] = cc[:, hsl[h]]
        q = q_ref[rows, :].astype(F32)
        v16 = i_ref[rows, :]
        yield

        st = [state_ref[h] for h in heads]
        q_dec = (q * jnp.exp2(b)).astype(BF16)
        b_last = b[CHUNK - 1:CHUNK, :]
        k_dec = jnp.exp2(b_last - cc).astype(BF16)
        st_decay = jnp.exp2(b_last)
        o_inter = [_dot_nt(q_dec[:, hsl[h]], st[h].astype(BF16)) for h in heads]
        for h in heads:
            state_ref[h] = st_decay[:, hsl[h]] * st[h] + _dot_tn(v16[:, hsl[h]], k_dec[:, hsl[h]])
        yield

        def intra_factored():
            k_all = jnp.exp2(-cc).astype(BF16)
            res = []
            for h in heads:
                k_pad = jnp.concatenate([k_all[:, hsl[h]], zero_tail], axis=0)
                res.append(jnp.where(causal, _dot_nt(q_dec[:, hsl[h]], k_pad), 0.0).astype(BF16))
            return tuple(res)

        def intra_blocked():
            zs = []
            for i in range(n_sub):
                lo = i * SUB
                for s in range(SUB):
                    r0 = lo + (s // SUBLANES) * SUBLANES
                    c_row = jnp.concatenate([_bcast_row(c_s, h, lo + s) for h in heads], axis=-1)
                    c_row = jnp.concatenate([c_row] * ((lo + SUB - r0) // SUBLANES), axis=0)
                    zs.append(q[r0:lo + SUB] * jnp.exp2(b[r0:lo + SUB] - c_row))
            z_rows = sum(z.shape[0] for z in zs)
            z_all = jnp.concatenate([z[:, hsl[h]] for h in heads for z in zs], axis=0)
            r = _dot(z_all.astype(BF16), ones_rhs)

            a_off = []
            for i in range(1, n_sub):
                lo = i * SUB
                bref = b[lo - 1:lo, :]
                q_i = (q[lo:lo + SUB] * jnp.exp2(b[lo:lo + SUB] - bref)).astype(BF16)
                k_i = jnp.exp2(bref - cc[:lo]).astype(BF16)
                zero_rows = jnp.zeros((HG_D - lo, HG_D), BF16)
                a_off.append([_dot_nt(q_i[:, hsl[h]], jnp.concatenate([k_i[:, hsl[h]], zero_rows], axis=0))
                              for h in heads])

            res = []
            for h in heads:
                a_rows = []
                off = h * z_rows
                for i in range(n_sub):
                    lo = i * SUB
                    a_i = a_off[i - 1][h] if i > 0 else jnp.zeros((SUB, HG_D), F32)
                    tiles = [a_i[j * SUBLANES:(j + 1) * SUBLANES] for j in range(SUB // SUBLANES)]
                    for s in range(SUB):
                        for j in range(s // SUBLANES, SUB // SUBLANES):
                            tiles[j] = jnp.where(lane == lo + s, r[off:off + SUBLANES], tiles[j])
                            off += SUBLANES
                    for j in range(SUB // SUBLANES):
                        a_rows.append(jnp.where(lane - lo <= row + j * SUBLANES, tiles[j], 0.0))
                res.append(jnp.concatenate(a_rows, axis=0).astype(BF16))
            return tuple(res)

        a_heads = lax.cond(jnp.min(b_last) >= -FACTORED_DECAY_LIMIT, intra_factored, intra_blocked)
        yield

        outs = []
        for h in heads:
            v_pad = jnp.concatenate([v16[:, hsl[h]], zero_tail], axis=0)
            outs.append(o_inter[h] + _dot(a_heads[h], v_pad))
            if h % 2 == 1:
                yield
        parts = []
        for o in outs:
            parts.append(o * lax.rsqrt(jnp.mean(o * o, axis=-1, keepdims=True) + EPS))
        on = jnp.concatenate(parts, axis=-1) * ng_ref[...]
        y_ref[rows, :] = (on * g_ref[rows, :].astype(F32)).astype(y_ref.dtype)

    return chunk


def _mlstm_stages(qk_ref, v_ref, og_ref, sm_ref, smt_ref, gbr_ref, gbc_ref, ng_ref,
                  y_ref, caug_ref, m_ref):
    half = ML_HEADS * ML_DQK
    k_scale = ML_DQK ** -0.5

    @pl.when(pl.program_id(1) == 0)
    def _():
        caug_ref[...] = jnp.zeros_like(caug_ref)
        m_ref[...] = jnp.zeros_like(m_ref)

    tri_b = _tri_lower(ML_BLOCK)
    tri_l = _mask_bf16(tri_b)
    tri_u = _mask_bf16(lax.broadcasted_iota(jnp.int32, (ML_BLOCK, ML_BLOCK), 0)
                       <= lax.broadcasted_iota(jnp.int32, (ML_BLOCK, ML_BLOCK), 1))
    ones_col = _mask_bf16(lax.broadcasted_iota(jnp.int32, (ML_BLOCK, LANES), 1) == 0)

    def chunk(c):
        rows = pl.ds(pl.multiple_of(c * ML_BLOCK, ML_BLOCK), ML_BLOCK)
        pre_c = sm_ref[rows, :] + gbr_ref[...]
        pre_r = smt_ref[c] + gbc_ref[...]
        cum_c = _sel_left(tri_l, _log_sigmoid(pre_c))
        cum_r = _sel_right(_log_sigmoid(pre_r), tri_u)
        yield
        heads = range(ML_HEADS)
        m_all = m_ref[...]
        m_old = [m_all[h:h + 1, 0:1] for h in heads]
        b_col = [cum_c[:, ML_HEADS + h:ML_HEADS + h + 1] for h in heads]
        i_col = [pre_c[:, h:h + 1] for h in heads]
        log_d = [jnp.where(tri_b, b_col[h] - cum_r[ML_HEADS + h:ML_HEADS + h + 1, :] + pre_r[h:h + 1, :],
                           NEG_BIG) for h in heads]
        log_inter = [b_col[h] + m_old[h] for h in heads]
        m_t = [jnp.maximum(jnp.max(log_d[h], axis=-1, keepdims=True), log_inter[h]) for h in heads]
        b_last = [b_col[h][ML_BLOCK - 1:ML_BLOCK, :] for h in heads]
        log_w = [b_last[h] - b_col[h] + i_col[h] for h in heads]
        m_new = [jnp.maximum(b_last[h] + m_old[h], jnp.max(log_w[h], axis=0, keepdims=True)) for h in heads]
        m_ref[...] = jnp.concatenate(
            [jnp.broadcast_to(m_new[h], (1, m_ref.shape[1])) for h in heads] + [m_all[ML_HEADS:]], axis=0)
        yield

        qb = [qk_ref[rows, h * ML_DQK:(h + 1) * ML_DQK] for h in heads]
        kb = [qk_ref[rows, half + h * ML_DQK:half + (h + 1) * ML_DQK] for h in heads]
        v_aug = [jnp.concatenate([v_ref[rows, h * ML_DV:(h + 1) * ML_DV], ones_col], axis=-1)
                 for h in heads]
        qk = [_dot_nt(qb[h], kb[h]) for h in heads]
        c_aug = [caug_ref[h] for h in heads]
        inter = [_dot(qb[h], c_aug[h].astype(BF16)) * jnp.exp(log_inter[h] - m_t[h]) for h in heads]
        yield
        s = [(qk[h] * (jnp.exp(log_d[h] - m_t[h]) * k_scale)).astype(BF16) for h in heads]
        num = [_dot(s[h], v_aug[h]) + inter[h] for h in heads]
        yield
        kw = [(kb[h].astype(F32) * (jnp.exp(log_w[h] - m_new[h]) * k_scale)).astype(BF16) for h in heads]
        for h in heads:
            caug_ref[h] = jnp.exp(b_last[h] + m_old[h] - m_new[h]) * c_aug[h] + _dot_tn(kw[h], v_aug[h])
        yield

        parts = []
        for h in heads:
            denom = jnp.maximum(jnp.abs(num[h][:, ML_DV:ML_DV + 1]), jnp.exp(-m_t[h]))
            o = num[h][:, :ML_DV] / denom
            parts.append(o * lax.rsqrt(jnp.mean(o * o, axis=-1, keepdims=True) + EPS))
        on = jnp.concatenate(parts, axis=-1) * ng_ref[...]
        y_ref[rows, :] = (on * og_ref[rows, :].astype(F32)).astype(y_ref.dtype)

    return chunk


def _ssd_stages(z_ref, xbc_ref, sm_ref, dtt_ref, dtb_c_ref, alog_c_ref,
                dtb_r_ref, alog_r_ref, d_ref, ng_ref, y_ref, state_ref):
    n_pair = MB_HEADS // 2
    pair_w = 2 * MB_P
    gw = MB_GROUPS * MB_N

    @pl.when(pl.program_id(1) == 0)
    def _():
        state_ref[...] = jnp.zeros_like(state_ref)

    tri_l = _mask_bf16(_tri_lower(CHUNK))
    sel_x = _mask_bf16(lax.broadcasted_iota(jnp.int32, (SMALL_W, MB_W), 0) - DT_COL
                       == jnp.right_shift(lax.broadcasted_iota(jnp.int32, (SMALL_W, MB_W), 1), MB_P_LOG2))
    ur = lax.broadcasted_iota(jnp.int32, (pair_w, pair_w), 0)
    uc = lax.broadcasted_iota(jnp.int32, (pair_w, pair_w), 1)
    same_half = jnp.right_shift(ur, CHUNK_LOG2) == jnp.right_shift(uc, CHUNK_LOG2)
    tri_u2 = _mask_bf16(same_half & (ur <= uc))
    causal = (lax.broadcasted_iota(jnp.int32, (CHUNK, MB_W), 0)
              >= jnp.bitwise_and(lax.broadcasted_iota(jnp.int32, (CHUNK, MB_W), 1), CHUNK - 1))
    first_head = lax.broadcasted_iota(jnp.int32, (CHUNK, pair_w), 1) < MB_P
    a_c = -jnp.exp(alog_c_ref[...])
    a_r = -jnp.exp(alog_r_ref[...])

    def chunk(c):
        rows = pl.ds(pl.multiple_of(c * CHUNK, CHUNK), CHUNK)
        dt_c = _softplus(sm_ref[rows, :] + dtb_c_ref[...])
        cum_c = _sel_left(tri_l, dt_c * a_c)
        dt_x = _sel_right(dt_c, sel_x)
        cum_x = _sel_right(cum_c, sel_x)
        dt_r = _softplus(dtt_ref[c] + dtb_r_ref[...])
        cum_r = _sel_right(dt_r * a_r, tri_u2)
        yield
        pairs = range(n_pair)
        grp = [(2 * p) // (MB_HEADS // MB_GROUPS) for p in pairs]
        lanes = [slice(p * pair_w, (p + 1) * pair_w) for p in pairs]
        xs = xbc_ref[rows, :MB_W].astype(F32)
        xdt = xs * dt_x
        cum_row = jnp.concatenate([cum_r[p:p + 1, :] for p in pairs], axis=-1)
        decay = jnp.exp(jnp.where(causal, cum_x - cum_row, NEG_BIG))
        cum_last = cum_x[CHUNK - 1:CHUNK, :]
        x_dec = (xdt * jnp.exp(cum_last - cum_x)).astype(BF16)
        st_decay = jnp.exp(cum_last)
        carry_w = jnp.exp(cum_x)
        yield
        bm = [xbc_ref[rows, MB_W + g * MB_N:MB_W + (g + 1) * MB_N] for g in range(MB_GROUPS)]
        cm = [xbc_ref[rows, MB_W + gw + g * MB_N:MB_W + gw + (g + 1) * MB_N] for g in range(MB_GROUPS)]
        cb2 = [_dot_nt(cm[g], jnp.concatenate([bm[g], bm[g]], axis=0)) for g in range(MB_GROUPS)]
        x2 = [jnp.concatenate([jnp.where(first_head, xdt[:, lanes[p]], 0.0),
                               jnp.where(first_head, 0.0, xdt[:, lanes[p]])], axis=0).astype(BF16)
              for p in pairs]
        yield
        st = [state_ref[p] for p in pairs]
        y_inter = [_dot(cm[grp[p]], st[p].astype(BF16)) for p in pairs]
        y_intra = [_dot((cb2[grp[p]] * decay[:, lanes[p]]).astype(BF16), x2[p]) for p in pairs]
        yield
        for p in pairs:
            state_ref[p] = st_decay[:, lanes[p]] * st[p] + _dot_tn(bm[grp[p]], x_dec[:, lanes[p]])
        yield
        y = (jnp.concatenate(y_intra, axis=-1) + jnp.concatenate(y_inter, axis=-1) * carry_w
             + d_ref[...] * xs)

        yz = y * z_ref[rows, :].astype(F32)
        gsz = MB_W // MB_GROUPS
        parts = []
        for g in range(MB_GROUPS):
            o = yz[:, g * gsz:(g + 1) * gsz]
            parts.append(o * lax.rsqrt(jnp.mean(o * o, axis=-1, keepdims=True) + EPS))
        y_ref[rows, :] = (jnp.concatenate(parts, axis=-1) * ng_ref[...]).astype(y_ref.dtype)

    return chunk


_DONE = object()


class _ColumnView:
    def __init__(self, pieces):
        self.pieces = pieces
        self.width = sum(w for _, _, w in pieces)

    def __getitem__(self, idx):
        rows, cols = idx
        lo, hi, _ = cols.indices(self.width)
        out, base = [], 0
        for ref, start, w in self.pieces:
            a, b = max(lo, base), min(hi, base + w)
            if a < b:
                out.append(ref[rows, start + a - base:start + b - base])
            base += w
        return out[0] if len(out) == 1 else jnp.concatenate(out, axis=-1)


def _mixers_kernel(q_ref, g_ref, z_ref, og_ref, m0, m1, m2, m3, m4, m5, sm_ref, smt_ref, dtt_ref,
                   lb_ref, hg_ng_ref, gbr_ref, gbc_ref, ml_ng_ref,
                   dtb_c_ref, alog_c_ref, dtb_r_ref, alog_r_ref, d_ref, mb_ng_ref,
                   y_hg, y_ml, y_mb, hg_state, hg_c, ml_caug, ml_m, mb_state):
    ts = y_hg.shape[0]
    mixed = (m0, m1, m2, m3, m4, m5)
    conv = [(m, 0, HALF_TILE) for m in mixed]
    raw = [(m, HALF_TILE, HALF_TILE) for m in mixed]
    qk_view = _ColumnView(conv[0:2])
    xbc_view = _ColumnView(conv[2:6])
    f_view = _ColumnView(raw[0:2])
    i_view = _ColumnView(raw[2:4])
    v_view = _ColumnView(raw[4:6])
    mixers = [
        _hgrn2_stages(q_ref, f_view, i_view, g_ref, lb_ref, hg_ng_ref, y_hg, hg_state, hg_c),
        _mlstm_stages(qk_view, v_view, og_ref, sm_ref, smt_ref, gbr_ref, gbc_ref, ml_ng_ref,
                      y_ml, ml_caug, ml_m),
        _ssd_stages(z_ref, xbc_view, sm_ref, dtt_ref, dtb_c_ref, alog_c_ref, dtb_r_ref, alog_r_ref,
                    d_ref, mb_ng_ref, y_mb, mb_state),
    ]

    per_block = ML_BLOCK // CHUNK

    def block_body(cb, carry):
        hg, ml, mb = mixers
        live = [itertools.chain(*[hg(cb * per_block + k) for k in range(per_block)]), ml(cb),
                itertools.chain(*[mb(cb * per_block + k) for k in range(per_block)])]
        while live:
            for g in list(live):
                if next(g, _DONE) is _DONE:
                    live.remove(g)
        return carry

    lax.fori_loop(0, ts // ML_BLOCK, block_body, 0, unroll=CHUNK_UNROLL // per_block)


def _mixers(proj, small, small_t, dt_t, lb, hg_ng, gb_row, gb_col, ml_ng,
            dtb_c, alog_c, dtb_r, alog_r, d_x, mb_ng, batch, seq):
    n = proj.shape[0]
    nt = seq // TS_MIX
    n_pair = MB_HEADS // 2
    w = D_MODEL

    def col(cb):
        return pl.BlockSpec((TS_MIX, w), lambda b, j: (b * nt + j, cb))

    def const(shape):
        return pl.BlockSpec(shape, lambda b, j: tuple(0 for _ in shape))

    tile_specs = [col(T_HG_Q), col(T_HG_G), col(T_MB_Z), col(T_ML_O)] + [col(t) for t in TILES_MIXED]
    gate_specs = [
        pl.BlockSpec((TS_MIX, SMALL_W), lambda b, j: (b * nt + j, 0)),
        pl.BlockSpec((TS_MIX // ML_BLOCK, 2 * ML_HEADS, ML_BLOCK), lambda b, j: (b * nt + j, 0, 0)),
        pl.BlockSpec((TS_MIX // CHUNK, n_pair, 2 * CHUNK), lambda b, j: (b * nt + j, 0, 0)),
    ]
    param_specs = [const((1, w)), const((1, w)),
                   const((1, SMALL_W)), const((2 * ML_HEADS, 1)), const((1, w)),
                   const((1, SMALL_W)), const((1, SMALL_W)),
                   const((n_pair, 2 * CHUNK)), const((n_pair, 2 * CHUNK)),
                   const((1, w)), const((1, w))]
    out = pl.BlockSpec((TS_MIX, w), lambda b, j: (b * nt + j, 0))
    return pl.pallas_call(
        _mixers_kernel,
        grid=(batch, nt),
        in_specs=tile_specs + gate_specs + param_specs,
        out_specs=[out, out, out],
        out_shape=[jax.ShapeDtypeStruct((n, w), BF16)] * 3,
        scratch_shapes=[
            pltpu.VMEM((HG_HEADS, HG_D, HG_D), F32),
            pltpu.VMEM((HG_HEADS, CHUNK, HG_D), F32),
            pltpu.VMEM((ML_HEADS, ML_DQK, ML_DV + LANES), F32),
            pltpu.VMEM((SUBLANES, LANES), F32),
            pltpu.VMEM((n_pair, MB_N, 2 * MB_P), F32),
        ],
        compiler_params=_cparams(("parallel", "arbitrary")),
        name="token_mixers",
    )(*([proj] * (4 + len(TILES_MIXED))), small, small_t, dt_t,
      lb, hg_ng, gb_row, gb_col, ml_ng, dtb_c, alog_c, dtb_r, alog_r, d_x, mb_ng)


def _merge_kernel(x_ref, yh_ref, ym_ref, yb_ref, g0_ref, g1_ref, g2_ref,
                  wh_ref, wm_ref, wb_ref, wo_ref, o_ref):
    mixed = g0_ref[...].astype(F32) * _dot(yh_ref[...], wh_ref[...])
    mixed = mixed + g1_ref[...].astype(F32) * _dot(ym_ref[...], wm_ref[...])
    mixed = mixed + g2_ref[...].astype(F32) * _dot(yb_ref[...], wb_ref[...])
    o_ref[...] = x_ref[...] + _dot(mixed.astype(BF16), wo_ref[...])


def _merge(x2, y_hg, y_ml, y_mb, proj, w_hg, w_ml, w_mb, w_out, layer):
    n = x2.shape[0]
    tile = lambda cb: pl.BlockSpec((TM_MERGE, D_MODEL), lambda i: (i, cb))
    wspec = pl.BlockSpec((None, D_MODEL, D_MODEL), lambda i: (layer, 0, 0))
    return pl.pallas_call(
        _merge_kernel,
        grid=(n // TM_MERGE,),
        in_specs=[tile(0), tile(0), tile(0), tile(0), tile(T_GATE0), tile(T_GATE0 + 1), tile(T_GATE0 + 2),
                  wspec, wspec, wspec, wspec],
        out_specs=tile(0),
        out_shape=jax.ShapeDtypeStruct((n, D_MODEL), F32),
        compiler_params=_cparams(("parallel",)),
        name="branch_merge",
    )(x2, y_hg, y_ml, y_mb, proj, proj, proj, w_hg, w_ml, w_mb, w_out)


def _ffn_kernel(x_ref, g_ref, wu_ref, cw_ref, cb_ref, wd_ref, fg_ref,
                o_ref, tail, hbuf_g, hbuf_v, *, final):
    tm = x_ref.shape[0]
    n_ck = D_FF // FF_CHUNK

    @pl.when(pl.program_id(1) == 0)
    def _():
        tail[...] = jnp.zeros_like(tail)

    x = x_ref[...]
    hb = (x * lax.rsqrt(jnp.mean(x * x, axis=-1, keepdims=True) + EPS) * g_ref[...]).astype(BF16)

    def cols(c, half):
        return slice(half * D_FF + c * FF_CHUNK, half * D_FF + (c + 1) * FF_CHUNK)

    def up(c):
        return _dot(hb, wu_ref[:, cols(c, 0)]), _dot(hb, wu_ref[:, cols(c, 1)])

    def conv(u, hbuf, cs):
        out = cb_ref[:, cs] + cw_ref[FFN_CONV - 1:FFN_CONV, cs] * u
        for k in range(FFN_CONV - 1):
            out = out + cw_ref[k:k + 1, cs] * pltpu.roll(u, FFN_CONV - 1 - k, axis=0)
        hbuf[0:CONV_PAD, :] = tail[:, cs]
        hbuf[CONV_PAD:2 * CONV_PAD, :] = u[:CONV_PAD]
        tail[:, cs] = u[tm - CONV_PAD:]
        head = cb_ref[:, cs]
        for k in range(FFN_CONV):
            off = CONV_PAD - (FFN_CONV - 1) + k
            head = head + cw_ref[k:k + 1, cs] * hbuf[off:off + CONV_PAD, :]
        return jnp.concatenate([head, out[CONV_PAD:]], axis=0)

    acc = x
    u_next = up(0)
    for c in range(n_ck):
        u_g, u_v = u_next
        if c + 1 < n_ck:
            u_next = up(c + 1)
        a_g = conv(u_g, hbuf_g, cols(c, 0))
        a_v = conv(u_v, hbuf_v, cols(c, 1))
        acc = acc + _dot((_silu(a_g) * a_v).astype(BF16), wd_ref[cols(c, 0), :])
    if final:
        acc = acc * lax.rsqrt(jnp.mean(acc * acc, axis=-1, keepdims=True) + EPS) * fg_ref[...]
    o_ref[...] = acc


def _ffn(x2, g, w_up, cw, cb, w_down, final_g, layer, batch, seq, final):
    n = x2.shape[0]
    nt = seq // TM_FFN

    def const(shape):
        return pl.BlockSpec(shape, lambda b, j: tuple(0 for _ in shape),
                            pipeline_mode=pl.Buffered(1))

    def layer_weight(shape):
        return pl.BlockSpec((None,) + shape, lambda b, j: (layer,) + tuple(0 for _ in shape),
                            pipeline_mode=pl.Buffered(1))

    tile = pl.BlockSpec((TM_FFN, D_MODEL), lambda b, j: (b * nt + j, 0))
    return pl.pallas_call(
        functools.partial(_ffn_kernel, final=final),
        grid=(batch, nt),
        in_specs=[
            tile, const((1, D_MODEL)),
            layer_weight((D_MODEL, 2 * D_FF)), const((FFN_CONV, 2 * D_FF)), const((1, 2 * D_FF)),
            layer_weight((D_FF, D_MODEL)), const((1, D_MODEL)),
        ],
        out_specs=tile,
        out_shape=jax.ShapeDtypeStruct((n, D_MODEL), F32),
        scratch_shapes=[
            pltpu.VMEM((CONV_PAD, 2 * D_FF), F32),
            pltpu.VMEM((2 * CONV_PAD, FF_CHUNK), F32),
            pltpu.VMEM((2 * CONV_PAD, FF_CHUNK), F32),
        ],
        compiler_params=_cparams(("parallel", "arbitrary")),
        name="conv_gated_mlp",
    )(x2, g, w_up, cw, cb, w_down, final_g)


def kernel(x, norm1_g, w_in, hg_lb_logits, hg_norm_g, ml_conv_w, ml_conv_b, ml_gate_b, ml_norm_g,
           mb_conv_w, mb_conv_b, mb_dt_bias, mb_a_log, mb_d, mb_norm_g, w_br_hg, w_br_ml, w_br_mb,
           w_out, norm2_g, w_up, ffn_conv_w, ffn_conv_b, w_down, final_g):
    batch, seq, _ = x.shape
    n = batch * seq
    assert seq % TS_MIX == 0 and seq % TM_FFN == 0 and seq % TM_PROJ == 0 and n % TM_MERGE == 0
    depth = w_in.shape[0]

    w = D_MODEL
    o_hg_q, o_hg_f, o_hg_i, o_hg_g = 0, w, 2 * w, 3 * w
    o_ml_qk = 4 * w
    o_ml_v = o_ml_qk + 2 * ML_HEADS * ML_DQK
    o_if = o_ml_v + ML_HEADS * ML_DV
    o_ml_o = o_if + 2 * ML_HEADS
    o_mb_z = o_ml_o + ML_HEADS * ML_DV
    o_mb_xbc = o_mb_z + MB_W
    o_dt = o_mb_xbc + MB_CONV_DIM
    o_gate = o_dt + MB_HEADS
    hw = HALF_TILE
    conv_halves = [o_ml_qk + k * hw for k in range(2)] + [o_mb_xbc + k * hw for k in range(4)]
    raw_halves = [o_hg_f, o_hg_f + hw, o_hg_i, o_hg_i + hw, o_ml_v, o_ml_v + hw]
    cols = [(o_hg_q, w), (o_hg_g, w), (o_mb_z, w), (o_ml_o, w), (o_gate, N_GATES * w)]
    for c0, r0 in zip(conv_halves, raw_halves):
        cols += [(c0, hw), (r0, hw)]
    w_in16 = w_in.astype(BF16)
    w_big = jnp.concatenate([w_in16[:, :, a:a + n_] for a, n_ in cols], axis=-1)
    pad = SMALL_W - 2 * ML_HEADS - MB_HEADS
    w_small = jnp.concatenate(
        [w_in16[:, :, o_if:o_ml_o], w_in16[:, :, o_dt:o_gate],
         jnp.zeros((depth, D_MODEL, pad), BF16)], axis=-1)

    lbs = _lbs(hg_lb_logits.astype(F32))
    hg_ng = jnp.tile(hg_norm_g, (1, HG_HEADS))
    ml_ng = jnp.tile(ml_norm_g, (1, ML_HEADS))
    gb_row = jnp.pad(ml_gate_b, ((0, 0), (0, SMALL_W - 2 * ML_HEADS)))
    dtb_c = jnp.pad(mb_dt_bias, ((0, 0), (DT_COL, SMALL_W - DT_COL - MB_HEADS)))
    alog_c = jnp.pad(mb_a_log, ((0, 0), (DT_COL, SMALL_W - DT_COL - MB_HEADS)))
    n_pair = MB_HEADS // 2
    dtb_r = jnp.repeat(mb_dt_bias, CHUNK, axis=-1).reshape(depth, n_pair, 2 * CHUNK)
    alog_r = jnp.repeat(mb_a_log, CHUNK, axis=-1).reshape(depth, n_pair, 2 * CHUNK)
    d_x = jnp.repeat(mb_d, MB_P, axis=-1)

    n_tiles = w_big.shape[-1] // TN_PROJ
    n_mixed = len(TILES_MIXED)
    conv_w = jnp.concatenate([ml_conv_w, mb_conv_w], axis=-1).reshape(depth, CONV_W, n_mixed, hw)
    conv_b = jnp.concatenate([ml_conv_b, mb_conv_b], axis=-1).reshape(depth, 1, n_mixed, hw)
    front = n_tiles - n_mixed
    cw_all = jnp.pad(jnp.swapaxes(conv_w, 1, 2), ((0, 0), (front, 0), (0, 0), (0, 0)))
    cb_all = jnp.pad(jnp.swapaxes(conv_b, 1, 2), ((0, 0), (front, 0), (0, 0), (0, 0)))

    w_hg = w_br_hg.astype(BF16)
    w_ml = w_br_ml.astype(BF16)
    w_mb = w_br_mb.astype(BF16)
    w_o = w_out.astype(BF16)
    w_u = w_up.astype(BF16)
    w_d = w_down.astype(BF16)

    x2 = x.reshape(n, D_MODEL)
    row = lambda a: a.reshape(1, -1)
    for l in range(depth):
        proj, small = _inproj(x2, row(norm1_g[l]), w_big, w_small, cw_all, cb_all, l, seq)
        chunks = small.reshape(n // CHUNK, CHUNK, SMALL_W)
        small_t = jnp.swapaxes(small.reshape(n // ML_BLOCK, ML_BLOCK, SMALL_W)[:, :, :2 * ML_HEADS], 1, 2)
        dt_t = jnp.swapaxes(chunks[:, :, DT_COL:DT_COL + MB_HEADS], 1, 2).reshape(
            n // CHUNK, n_pair, 2 * CHUNK)

        y_hg, y_ml, y_mb = _mixers(
            proj, small, small_t, dt_t, row(lbs[l]), row(hg_ng[l]),
            row(gb_row[l]), ml_gate_b[l].reshape(2 * ML_HEADS, 1), row(ml_ng[l]),
            row(dtb_c[l]), row(alog_c[l]), dtb_r[l], alog_r[l], row(d_x[l]), row(mb_norm_g[l]), batch, seq)
        x2 = _merge(x2, y_hg, y_ml, y_mb, proj, w_hg, w_ml, w_mb, w_o, l)
        x2 = _ffn(x2, row(norm2_g[l]), w_u, ffn_conv_w[l], row(ffn_conv_b[l]), w_d, row(final_g),
                  l, batch, seq, final=(l == depth - 1))
    return x2.reshape(batch, seq, D_MODEL)
```

```python
import functools
import itertools
import math

import jax
import jax.numpy as jnp
from jax import lax
from jax.experimental import pallas as pl
from jax.experimental.pallas import tpu as pltpu

F32 = jnp.float32
BF16 = jnp.bfloat16

D_MODEL = 1024
CHUNK = 64
CHUNK_LOG2 = 6
SUB = 8
FACTORED_DECAY_LIMIT = 112.0
SUBLANES = 8
LANES = 128
LOG2E = math.log2(math.e)
EPS = 1e-6
NEG_BIG = -1e30
HG_HEADS = 8
HG_D = 128
ML_HEADS = 4
ML_DQK = 128
ML_DV = 256
MB_HEADS = 16
MB_P = 64
MB_P_LOG2 = 6
MB_GROUPS = 4
MB_N = 128
MB_W = MB_HEADS * MB_P
MB_CONV_DIM = MB_W + 2 * MB_GROUPS * MB_N
D_FF = 2816
FFN_CONV = 3
FF_CHUNK = 1408
SMALL_W = 128
DT_COL = 8
CONV_PAD = 8

VMEM_LIMIT = 56 * 1024 * 1024

TM_PROJ = 2048
TN_PROJ = 1024
EPI_ROWS = 256
TILES_SILU = (0, 1, 2)
TILES_SIGMOID = (3, 4, 5, 6)
TILES_MIXED = (7, 8, 9, 10, 11, 12)
T_HG_Q, T_HG_G, T_MB_Z, T_ML_O, T_GATE0 = 0, 1, 2, 3, 4
N_GATES = 3
HALF_TILE = TN_PROJ // 2
CONV_W = 4
TS_MIX = 512
CHUNK_UNROLL = 4
ML_BLOCK = 2 * CHUNK
TM_MERGE = 512
TM_FFN = 512


def _silu(x):
    return x * jax.nn.sigmoid(x)


def _softplus(x):
    return jnp.maximum(x, 0.0) + jnp.log1p(jnp.exp(-jnp.abs(x)))


def _log_sigmoid(x):
    return jnp.minimum(x, 0.0) - jnp.log1p(jnp.exp(-jnp.abs(x)))


def _dot(a, b):
    return jnp.dot(a, b, preferred_element_type=F32)


def _dot_nt(a, b):
    return lax.dot_general(a, b, (((1,), (1,)), ((), ())), preferred_element_type=F32)


def _dot_tn(a, b):
    return lax.dot_general(a, b, (((0,), (0,)), ((), ())), preferred_element_type=F32)


def _split3(a):
    hi = a.astype(BF16)
    r = a - hi.astype(F32)
    mid = r.astype(BF16)
    lo = (r - mid.astype(F32)).astype(BF16)
    return hi, mid, lo


def _sel_left(sel, a):
    hi, mid, lo = _split3(a)
    return _dot(sel, hi) + (_dot(sel, mid) + _dot(sel, lo))


def _sel_right(a, sel):
    hi, mid, lo = _split3(a)
    return _dot(hi, sel) + (_dot(mid, sel) + _dot(lo, sel))


def _tri_lower(n):
    r = lax.broadcasted_iota(jnp.int32, (n, n), 0)
    c = lax.broadcasted_iota(jnp.int32, (n, n), 1)
    return r >= c


def _bcast_row(ref, h, r):
    return ref[h, pl.ds(r, SUBLANES, stride=0), :]


def _mask_bf16(m):
    return jnp.where(m, 1.0, 0.0).astype(BF16)


def _cparams(sem):
    return pltpu.CompilerParams(dimension_semantics=sem, vmem_limit_bytes=VMEM_LIMIT)


def _lbs_kernel(lg_ref, o_ref):
    lg = lg_ref[...]
    mx = jnp.max(lg, axis=0, keepdims=True)
    e = jnp.exp(lg - mx)
    p = e / jnp.sum(e, axis=0, keepdims=True)
    acc = jnp.zeros_like(p[0:1])
    rows = []
    for l in range(lg.shape[0]):
        acc = acc + p[l:l + 1]
        rows.append(acc - p[0:1])
    o_ref[...] = jnp.concatenate(rows, axis=0)


def _lbs(logits):
    return pl.pallas_call(
        _lbs_kernel,
        out_shape=jax.ShapeDtypeStruct(logits.shape, F32),
        name="hgrn2_lower_bounds",
    )(logits)


def _any_tile(j, tiles):
    hit = j == tiles[0]
    for t in tiles[1:]:
        hit = jnp.logical_or(hit, j == t)
    return hit


def _inproj_kernel(x_ref, g_ref, w_ref, ws_ref, cw_ref, cb_ref, o_ref, os_ref, h_ref, tail_ref, hbuf,
                   *, tiles_per_seq):
    i = pl.program_id(0)
    j = pl.program_id(1)
    n_blk = TM_PROJ // EPI_ROWS

    @pl.when(jnp.logical_and(i == 0, j == 0))
    def _():
        tail_ref[...] = jnp.zeros_like(tail_ref)

    @pl.when(j == 0)
    def _():
        x = x_ref[...]
        ms = jnp.mean(x * x, axis=-1, keepdims=True)
        hb = (x * lax.rsqrt(ms + EPS) * g_ref[...]).astype(BF16)
        h_ref[...] = hb
        os_ref[...] = _dot(hb, ws_ref[...])

    def pointwise(fn):
        ys = [_dot(h_ref[r * EPI_ROWS:(r + 1) * EPI_ROWS, :], w_ref[...]) for r in range(n_blk)]
        for r in range(n_blk):
            o_ref[r * EPI_ROWS:(r + 1) * EPI_ROWS, :] = fn(ys[r]).astype(o_ref.dtype)

    @pl.when(_any_tile(j, TILES_SILU))
    def _():
        pointwise(_silu)

    @pl.when(_any_tile(j, TILES_SIGMOID))
    def _():
        pointwise(jax.nn.sigmoid)

    @pl.when(j >= TILES_MIXED[0])
    def _():
        slot = j - TILES_MIXED[0]
        prev = jnp.where(i % tiles_per_seq == 0, 0.0, tail_ref[slot])
        cw = cw_ref[0]
        cb = cb_ref[0]
        ys = [_dot(h_ref[r * EPI_ROWS:(r + 1) * EPI_ROWS, :], w_ref[...]) for r in range(n_blk)]
        for r in range(n_blk):
            yb = ys[r][:, :HALF_TILE]
            out = cb + cw[CONV_W - 1:CONV_W] * yb
            for k in range(CONV_W - 1):
                out = out + cw[k:k + 1] * pltpu.roll(yb, CONV_W - 1 - k, axis=0)
            hbuf[0:CONV_PAD, :] = prev
            hbuf[CONV_PAD:2 * CONV_PAD, :] = yb[:CONV_PAD]
            head = cb
            for k in range(CONV_W):
                off = CONV_PAD - (CONV_W - 1) + k
                head = head + cw[k:k + 1] * hbuf[off:off + CONV_PAD, :]
            prev = yb[EPI_ROWS - CONV_PAD:]
            act = _silu(jnp.concatenate([head, out[CONV_PAD:]], axis=0))
            o_ref[r * EPI_ROWS:(r + 1) * EPI_ROWS, :] = jnp.concatenate(
                [act, ys[r][:, HALF_TILE:]], axis=-1).astype(o_ref.dtype)
        tail_ref[slot] = prev


def _inproj(x2, g, w_big, w_small, cw_all, cb_all, layer, seq):
    n = x2.shape[0]
    nb = w_big.shape[-1]
    n_tiles = nb // TN_PROJ
    assert n_tiles == len(TILES_SILU + TILES_SIGMOID + TILES_MIXED)
    return pl.pallas_call(
        functools.partial(_inproj_kernel, tiles_per_seq=seq // TM_PROJ),
        grid=(n // TM_PROJ, n_tiles),
        in_specs=[
            pl.BlockSpec((TM_PROJ, D_MODEL), lambda i, j: (i, 0)),
            pl.BlockSpec((1, D_MODEL), lambda i, j: (0, 0)),
            pl.BlockSpec((None, D_MODEL, TN_PROJ), lambda i, j: (layer, 0, j)),
            pl.BlockSpec((None, D_MODEL, SMALL_W), lambda i, j: (layer, 0, 0)),
            pl.BlockSpec((None, 1, CONV_W, HALF_TILE), lambda i, j: (layer, j, 0, 0)),
            pl.BlockSpec((None, 1, 1, HALF_TILE), lambda i, j: (layer, j, 0, 0)),
        ],
        out_specs=[
            pl.BlockSpec((TM_PROJ, TN_PROJ), lambda i, j: (i, j)),
            pl.BlockSpec((TM_PROJ, SMALL_W), lambda i, j: (i, 0)),
        ],
        out_shape=[
            jax.ShapeDtypeStruct((n, nb), BF16),
            jax.ShapeDtypeStruct((n, SMALL_W), F32),
        ],
        scratch_shapes=[
            pltpu.VMEM((TM_PROJ, D_MODEL), BF16),
            pltpu.VMEM((len(TILES_MIXED), CONV_PAD, HALF_TILE), F32),
            pltpu.VMEM((2 * CONV_PAD, HALF_TILE), F32),
        ],
        compiler_params=_cparams(("arbitrary", "arbitrary")),
        name="in_projection",
    )(x2, g, w_big, w_small, cw_all, cb_all)


def _hgrn2_stages(q_ref, f_ref, i_ref, g_ref, lb_ref, ng_ref, y_ref, state_ref, c_s, o_s):
    @pl.when(pl.program_id(1) == 0)
    def _():
        state_ref[...] = jnp.zeros_like(state_ref)

    lb = lb_ref[...]
    tri = _mask_bf16(_tri_lower(CHUNK))
    n_sub = CHUNK // SUB
    lane = lax.broadcasted_iota(jnp.int32, (SUBLANES, HG_D), 1)
    row = lax.broadcasted_iota(jnp.int32, (SUBLANES, HG_D), 0)
    causal = (lax.broadcasted_iota(jnp.int32, (CHUNK, HG_D), 1)
              <= lax.broadcasted_iota(jnp.int32, (CHUNK, HG_D), 0))

    heads = range(HG_HEADS)
    hsl = [slice(h * HG_D, (h + 1) * HG_D) for h in heads]
    zero_tail = jnp.zeros((HG_D - CHUNK, HG_D), BF16)
    unsafe = {}

    def decays(rows):
        sig = jax.nn.sigmoid(f_ref[rows, :].astype(F32))
        fgate = lb + (1.0 - lb) * sig
        b = _sel_left(tri, jnp.log(fgate) * LOG2E)
        return b, b - jnp.log(jnp.maximum(1.0 - fgate, 0.0)) * LOG2E

    def finish(rows, v16, o_inter, a_heads):
        parts = []
        for h in heads:
            v_pad = jnp.concatenate([v16[:, hsl[h]], zero_tail], axis=0)
            o = o_inter[h] + _dot(a_heads[h], v_pad)
            parts.append(o * lax.rsqrt(jnp.mean(o * o, axis=-1, keepdims=True) + EPS))
        on = jnp.concatenate(parts, axis=-1) * ng_ref[...]
        y_ref[rows, :] = (on * g_ref[rows, :].astype(F32)).astype(y_ref.dtype)

    def chunk(c, slot):
        rows = pl.ds(pl.multiple_of(c * CHUNK, CHUNK), CHUNK)
        b, cc = decays(rows)
        q = q_ref[rows, :].astype(F32)
        v16 = i_ref[rows, :]
        yield

        st = [state_ref[h] for h in heads]
        q_dec = (q * jnp.exp2(b)).astype(BF16)
        b_last = b[CHUNK - 1:CHUNK, :]
        k_dec = jnp.exp2(b_last - cc).astype(BF16)
        st_decay = jnp.exp2(b_last)
        o_inter = [_dot_nt(q_dec[:, hsl[h]], st[h].astype(BF16)) for h in heads]
        for h in heads:
            state_ref[h] = st_decay[:, hsl[h]] * st[h] + _dot_tn(v16[:, hsl[h]], k_dec[:, hsl[h]])
        o_s[slot] = jnp.concatenate(o_inter, axis=-1)
        unsafe[slot] = jnp.logical_not(jnp.min(b_last) >= -FACTORED_DECAY_LIMIT)
        yield

        k_all = jnp.exp2(-cc).astype(BF16)
        a_heads = []
        for h in heads:
            k_pad = jnp.concatenate([k_all[:, hsl[h]], zero_tail], axis=0)
            a_heads.append(jnp.where(causal, _dot_nt(q_dec[:, hsl[h]], k_pad), 0.0).astype(BF16))
        yield
        finish(rows, v16, o_inter, a_heads)

    def redo(c, slot):
        rows = pl.ds(pl.multiple_of(c * CHUNK, CHUNK), CHUNK)
        b, cc = decays(rows)
        for h in heads:
            c_s[h] = cc[:, hsl[h]]
        q = q_ref[rows, :].astype(F32)
        ones_rhs = jnp.ones((HG_D, HG_D), BF16)
        zs = []
        for i in range(n_sub):
            lo = i * SUB
            for s in range(SUB):
                r0 = lo + (s // SUBLANES) * SUBLANES
                c_row = jnp.concatenate([_bcast_row(c_s, h, lo + s) for h in heads], axis=-1)
                c_row = jnp.concatenate([c_row] * ((lo + SUB - r0) // SUBLANES), axis=0)
                zs.append(q[r0:lo + SUB] * jnp.exp2(b[r0:lo + SUB] - c_row))
        z_rows = sum(z.shape[0] for z in zs)
        z_all = jnp.concatenate([z[:, hsl[h]] for h in heads for z in zs], axis=0)
        r = _dot(z_all.astype(BF16), ones_rhs)

        a_off = []
        for i in range(1, n_sub):
            lo = i * SUB
            bref = b[lo - 1:lo, :]
            q_i = (q[lo:lo + SUB] * jnp.exp2(b[lo:lo + SUB] - bref)).astype(BF16)
            k_i = jnp.exp2(bref - cc[:lo]).astype(BF16)
            zero_rows = jnp.zeros((HG_D - lo, HG_D), BF16)
            a_off.append([_dot_nt(q_i[:, hsl[h]], jnp.concatenate([k_i[:, hsl[h]], zero_rows], axis=0))
                          for h in heads])

        a_heads = []
        for h in heads:
            a_rows = []
            off = h * z_rows
            for i in range(n_sub):
                lo = i * SUB
                a_i = a_off[i - 1][h] if i > 0 else jnp.zeros((SUB, HG_D), F32)
                tiles = [a_i[j * SUBLANES:(j + 1) * SUBLANES] for j in range(SUB // SUBLANES)]
                for s in range(SUB):
                    for j in range(s // SUBLANES, SUB // SUBLANES):
                        tiles[j] = jnp.where(lane == lo + s, r[off:off + SUBLANES], tiles[j])
                        off += SUBLANES
                for j in range(SUB // SUBLANES):
                    a_rows.append(jnp.where(lane - lo <= row + j * SUBLANES, tiles[j], 0.0))
            a_heads.append(jnp.concatenate(a_rows, axis=0).astype(BF16))
        o_all = o_s[slot]
        finish(rows, i_ref[rows, :], [o_all[:, hsl[h]] for h in heads], a_heads)

    return chunk, redo, unsafe


def _mlstm_stages(qk_ref, v_ref, og_ref, sm_ref, smt_ref, gbr_ref, gbc_ref, ng_ref,
                  y_ref, caug_ref, m_ref):
    half = ML_HEADS * ML_DQK
    k_scale = ML_DQK ** -0.5

    @pl.when(pl.program_id(1) == 0)
    def _():
        caug_ref[...] = jnp.zeros_like(caug_ref)
        m_ref[...] = jnp.zeros_like(m_ref)

    tri_b = _tri_lower(ML_BLOCK)
    tri_l = _mask_bf16(tri_b)
    tri_u = _mask_bf16(lax.broadcasted_iota(jnp.int32, (ML_BLOCK, ML_BLOCK), 0)
                       <= lax.broadcasted_iota(jnp.int32, (ML_BLOCK, ML_BLOCK), 1))
    ones_col = _mask_bf16(lax.broadcasted_iota(jnp.int32, (ML_BLOCK, LANES), 1) == 0)

    def chunk(c):
        rows = pl.ds(pl.multiple_of(c * ML_BLOCK, ML_BLOCK), ML_BLOCK)
        pre_c = sm_ref[rows, :] + gbr_ref[...]
        pre_r = smt_ref[c] + gbc_ref[...]
        cum_c = _sel_left(tri_l, _log_sigmoid(pre_c))
        cum_r = _sel_right(_log_sigmoid(pre_r), tri_u)
        yield
        heads = range(ML_HEADS)
        m_all = m_ref[...]
        m_old = [m_all[h:h + 1, 0:1] for h in heads]
        b_col = [cum_c[:, ML_HEADS + h:ML_HEADS + h + 1] for h in heads]
        i_col = [pre_c[:, h:h + 1] for h in heads]
        log_d = [jnp.where(tri_b, b_col[h] - cum_r[ML_HEADS + h:ML_HEADS + h + 1, :] + pre_r[h:h + 1, :],
                           NEG_BIG) for h in heads]
        log_inter = [b_col[h] + m_old[h] for h in heads]
        m_t = [jnp.maximum(jnp.max(log_d[h], axis=-1, keepdims=True), log_inter[h]) for h in heads]
        b_last = [b_col[h][ML_BLOCK - 1:ML_BLOCK, :] for h in heads]
        log_w = [b_last[h] - b_col[h] + i_col[h] for h in heads]
        m_new = [jnp.maximum(b_last[h] + m_old[h], jnp.max(log_w[h], axis=0, keepdims=True)) for h in heads]
        m_ref[...] = jnp.concatenate(
            [jnp.broadcast_to(m_new[h], (1, m_ref.shape[1])) for h in heads] + [m_all[ML_HEADS:]], axis=0)
        yield

        qb = [qk_ref[rows, h * ML_DQK:(h + 1) * ML_DQK] for h in heads]
        kb = [qk_ref[rows, half + h * ML_DQK:half + (h + 1) * ML_DQK] for h in heads]
        v_aug = [jnp.concatenate([v_ref[rows, h * ML_DV:(h + 1) * ML_DV], ones_col], axis=-1)
                 for h in heads]
        qk = [_dot_nt(qb[h], kb[h]) for h in heads]
        c_aug = [caug_ref[h] for h in heads]
        inter = [_dot(qb[h], c_aug[h].astype(BF16)) * jnp.exp(log_inter[h] - m_t[h]) for h in heads]
        yield
        s = [(qk[h] * (jnp.exp(log_d[h] - m_t[h]) * k_scale)).astype(BF16) for h in heads]
        num = [_dot(s[h], v_aug[h]) + inter[h] for h in heads]
        yield
        kw = [(kb[h].astype(F32) * (jnp.exp(log_w[h] - m_new[h]) * k_scale)).astype(BF16) for h in heads]
        for h in heads:
            caug_ref[h] = jnp.exp(b_last[h] + m_old[h] - m_new[h]) * c_aug[h] + _dot_tn(kw[h], v_aug[h])
        yield

        parts = []
        for h in heads:
            denom = jnp.maximum(jnp.abs(num[h][:, ML_DV:ML_DV + 1]), jnp.exp(-m_t[h]))
            o = num[h][:, :ML_DV] / denom
            parts.append(o * lax.rsqrt(jnp.mean(o * o, axis=-1, keepdims=True) + EPS))
        on = jnp.concatenate(parts, axis=-1) * ng_ref[...]
        y_ref[rows, :] = (on * og_ref[rows, :].astype(F32)).astype(y_ref.dtype)

    return chunk


def _ssd_stages(z_ref, xbc_ref, sm_ref, dtt_ref, dtb_c_ref, alog_c_ref,
                dtb_r_ref, alog_r_ref, d_ref, ng_ref, y_ref, state_ref):
    n_pair = MB_HEADS // 2
    pair_w = 2 * MB_P
    gw = MB_GROUPS * MB_N

    @pl.when(pl.program_id(1) == 0)
    def _():
        state_ref[...] = jnp.zeros_like(state_ref)

    tri_l = _mask_bf16(_tri_lower(CHUNK))
    sel_x = _mask_bf16(lax.broadcasted_iota(jnp.int32, (SMALL_W, MB_W), 0) - DT_COL
                       == jnp.right_shift(lax.broadcasted_iota(jnp.int32, (SMALL_W, MB_W), 1), MB_P_LOG2))
    ur = lax.broadcasted_iota(jnp.int32, (pair_w, pair_w), 0)
    uc = lax.broadcasted_iota(jnp.int32, (pair_w, pair_w), 1)
    same_half = jnp.right_shift(ur, CHUNK_LOG2) == jnp.right_shift(uc, CHUNK_LOG2)
    tri_u2 = _mask_bf16(same_half & (ur <= uc))
    causal = (lax.broadcasted_iota(jnp.int32, (CHUNK, MB_W), 0)
              >= jnp.bitwise_and(lax.broadcasted_iota(jnp.int32, (CHUNK, MB_W), 1), CHUNK - 1))
    first_head = lax.broadcasted_iota(jnp.int32, (CHUNK, pair_w), 1) < MB_P
    a_c = -jnp.exp(alog_c_ref[...])
    a_r = -jnp.exp(alog_r_ref[...])

    def chunk(c):
        rows = pl.ds(pl.multiple_of(c * CHUNK, CHUNK), CHUNK)
        dt_c = _softplus(sm_ref[rows, :] + dtb_c_ref[...])
        cum_c = _sel_left(tri_l, dt_c * a_c)
        dt_x = _sel_right(dt_c, sel_x)
        cum_x = _sel_right(cum_c, sel_x)
        dt_r = _softplus(dtt_ref[c] + dtb_r_ref[...])
        cum_r = _sel_right(dt_r * a_r, tri_u2)
        yield
        pairs = range(n_pair)
        grp = [(2 * p) // (MB_HEADS // MB_GROUPS) for p in pairs]
        lanes = [slice(p * pair_w, (p + 1) * pair_w) for p in pairs]
        xs = xbc_ref[rows, :MB_W].astype(F32)
        xdt = xs * dt_x
        cum_row = jnp.concatenate([cum_r[p:p + 1, :] for p in pairs], axis=-1)
        decay = jnp.exp(jnp.where(causal, cum_x - cum_row, NEG_BIG))
        cum_last = cum_x[CHUNK - 1:CHUNK, :]
        x_dec = (xdt * jnp.exp(cum_last - cum_x)).astype(BF16)
        st_decay = jnp.exp(cum_last)
        carry_w = jnp.exp(cum_x)
        yield
        bm = [xbc_ref[rows, MB_W + g * MB_N:MB_W + (g + 1) * MB_N] for g in range(MB_GROUPS)]
        cm = [xbc_ref[rows, MB_W + gw + g * MB_N:MB_W + gw + (g + 1) * MB_N] for g in range(MB_GROUPS)]
        cb2 = [_dot_nt(cm[g], jnp.concatenate([bm[g], bm[g]], axis=0)) for g in range(MB_GROUPS)]
        x2 = [jnp.concatenate([jnp.where(first_head, xdt[:, lanes[p]], 0.0),
                               jnp.where(first_head, 0.0, xdt[:, lanes[p]])], axis=0).astype(BF16)
              for p in pairs]
        yield
        st = [state_ref[p] for p in pairs]
        y_inter = [_dot(cm[grp[p]], st[p].astype(BF16)) for p in pairs]
        y_intra = [_dot((cb2[grp[p]] * decay[:, lanes[p]]).astype(BF16), x2[p]) for p in pairs]
        yield
        for p in pairs:
            state_ref[p] = st_decay[:, lanes[p]] * st[p] + _dot_tn(bm[grp[p]], x_dec[:, lanes[p]])
        yield
        y = (jnp.concatenate(y_intra, axis=-1) + jnp.concatenate(y_inter, axis=-1) * carry_w
             + d_ref[...] * xs)

        yz = y * z_ref[rows, :].astype(F32)
        gsz = MB_W // MB_GROUPS
        parts = []
        for g in range(MB_GROUPS):
            o = yz[:, g * gsz:(g + 1) * gsz]
            parts.append(o * lax.rsqrt(jnp.mean(o * o, axis=-1, keepdims=True) + EPS))
        y_ref[rows, :] = (jnp.concatenate(parts, axis=-1) * ng_ref[...]).astype(y_ref.dtype)

    return chunk


_DONE = object()


class _ColumnView:
    def __init__(self, pieces):
        self.pieces = pieces
        self.width = sum(w for _, _, w in pieces)

    def __getitem__(self, idx):
        rows, cols = idx
        lo, hi, _ = cols.indices(self.width)
        out, base = [], 0
        for ref, start, w in self.pieces:
            a, b = max(lo, base), min(hi, base + w)
            if a < b:
                out.append(ref[rows, start + a - base:start + b - base])
            base += w
        return out[0] if len(out) == 1 else jnp.concatenate(out, axis=-1)


def _mixers_kernel(q_ref, g_ref, z_ref, og_ref, m0, m1, m2, m3, m4, m5, sm_ref, smt_ref, dtt_ref,
                   lb_ref, hg_ng_ref, gbr_ref, gbc_ref, ml_ng_ref,
                   dtb_c_ref, alog_c_ref, dtb_r_ref, alog_r_ref, d_ref, mb_ng_ref,
                   y_hg, y_ml, y_mb, hg_state, hg_c, hg_o, ml_caug, ml_m, mb_state):
    ts = y_hg.shape[0]
    mixed = (m0, m1, m2, m3, m4, m5)
    conv = [(m, 0, HALF_TILE) for m in mixed]
    raw = [(m, HALF_TILE, HALF_TILE) for m in mixed]
    qk_view = _ColumnView(conv[0:2])
    xbc_view = _ColumnView(conv[2:6])
    f_view = _ColumnView(raw[0:2])
    i_view = _ColumnView(raw[2:4])
    v_view = _ColumnView(raw[4:6])
    hg, hg_redo, hg_unsafe = _hgrn2_stages(q_ref, f_view, i_view, g_ref, lb_ref, hg_ng_ref, y_hg,
                                           hg_state, hg_c, hg_o)
    ml = _mlstm_stages(qk_view, v_view, og_ref, sm_ref, smt_ref, gbr_ref, gbc_ref, ml_ng_ref,
                       y_ml, ml_caug, ml_m)
    mb = _ssd_stages(z_ref, xbc_view, sm_ref, dtt_ref, dtb_c_ref, alog_c_ref, dtb_r_ref, alog_r_ref,
                     d_ref, mb_ng_ref, y_mb, mb_state)
    per_block = ML_BLOCK // CHUNK
    blocks_per_trip = CHUNK_UNROLL // per_block
    chunks_per_trip = blocks_per_trip * per_block

    def trip_body(t, carry):
        for k in range(blocks_per_trip):
            cb = t * blocks_per_trip + k
            live = [itertools.chain(*[hg(cb * per_block + j, k * per_block + j) for j in range(per_block)]),
                    ml(cb),
                    itertools.chain(*[mb(cb * per_block + j) for j in range(per_block)])]
            while live:
                for g in list(live):
                    if next(g, _DONE) is _DONE:
                        live.remove(g)
        for slot in range(chunks_per_trip):
            pl.when(hg_unsafe[slot])(functools.partial(hg_redo, t * chunks_per_trip + slot, slot))
        return carry

    lax.fori_loop(0, ts // (chunks_per_trip * CHUNK), trip_body, 0)


def _mixers(proj, small, small_t, dt_t, lb, hg_ng, gb_row, gb_col, ml_ng,
            dtb_c, alog_c, dtb_r, alog_r, d_x, mb_ng, batch, seq):
    n = proj.shape[0]
    nt = seq // TS_MIX
    n_pair = MB_HEADS // 2
    w = D_MODEL

    def col(cb):
        return pl.BlockSpec((TS_MIX, w), lambda b, j: (b * nt + j, cb))

    def const(shape):
        return pl.BlockSpec(shape, lambda b, j: tuple(0 for _ in shape))

    tile_specs = [col(T_HG_Q), col(T_HG_G), col(T_MB_Z), col(T_ML_O)] + [col(t) for t in TILES_MIXED]
    gate_specs = [
        pl.BlockSpec((TS_MIX, SMALL_W), lambda b, j: (b * nt + j, 0)),
        pl.BlockSpec((TS_MIX // ML_BLOCK, 2 * ML_HEADS, ML_BLOCK), lambda b, j: (b * nt + j, 0, 0)),
        pl.BlockSpec((TS_MIX // CHUNK, n_pair, 2 * CHUNK), lambda b, j: (b * nt + j, 0, 0)),
    ]
    param_specs = [const((1, w)), const((1, w)),
                   const((1, SMALL_W)), const((2 * ML_HEADS, 1)), const((1, w)),
                   const((1, SMALL_W)), const((1, SMALL_W)),
                   const((n_pair, 2 * CHUNK)), const((n_pair, 2 * CHUNK)),
                   const((1, w)), const((1, w))]
    out = pl.BlockSpec((TS_MIX, w), lambda b, j: (b * nt + j, 0))
    return pl.pallas_call(
        _mixers_kernel,
        grid=(batch, nt),
        in_specs=tile_specs + gate_specs + param_specs,
        out_specs=[out, out, out],
        out_shape=[jax.ShapeDtypeStruct((n, w), BF16)] * 3,
        scratch_shapes=[
            pltpu.VMEM((HG_HEADS, HG_D, HG_D), F32),
            pltpu.VMEM((HG_HEADS, CHUNK, HG_D), F32),
            pltpu.VMEM((CHUNK_UNROLL, CHUNK, w), F32),
            pltpu.VMEM((ML_HEADS, ML_DQK, ML_DV + LANES), F32),
            pltpu.VMEM((SUBLANES, LANES), F32),
            pltpu.VMEM((n_pair, MB_N, 2 * MB_P), F32),
        ],
        compiler_params=_cparams(("parallel", "arbitrary")),
        name="token_mixers",
    )(*([proj] * (4 + len(TILES_MIXED))), small, small_t, dt_t,
      lb, hg_ng, gb_row, gb_col, ml_ng, dtb_c, alog_c, dtb_r, alog_r, d_x, mb_ng)


def _merge_kernel(x_ref, yh_ref, ym_ref, yb_ref, g0_ref, g1_ref, g2_ref,
                  wh_ref, wm_ref, wb_ref, wo_ref, o_ref):
    mixed = g0_ref[...].astype(F32) * _dot(yh_ref[...], wh_ref[...])
    mixed = mixed + g1_ref[...].astype(F32) * _dot(ym_ref[...], wm_ref[...])
    mixed = mixed + g2_ref[...].astype(F32) * _dot(yb_ref[...], wb_ref[...])
    o_ref[...] = x_ref[...] + _dot(mixed.astype(BF16), wo_ref[...])


def _merge(x2, y_hg, y_ml, y_mb, proj, w_hg, w_ml, w_mb, w_out, layer):
    n = x2.shape[0]
    tile = lambda cb: pl.BlockSpec((TM_MERGE, D_MODEL), lambda i: (i, cb))
    wspec = pl.BlockSpec((None, D_MODEL, D_MODEL), lambda i: (layer, 0, 0))
    return pl.pallas_call(
        _merge_kernel,
        grid=(n // TM_MERGE,),
        in_specs=[tile(0), tile(0), tile(0), tile(0), tile(T_GATE0), tile(T_GATE0 + 1), tile(T_GATE0 + 2),
                  wspec, wspec, wspec, wspec],
        out_specs=tile(0),
        out_shape=jax.ShapeDtypeStruct((n, D_MODEL), F32),
        compiler_params=_cparams(("parallel",)),
        name="branch_merge",
    )(x2, y_hg, y_ml, y_mb, proj, proj, proj, w_hg, w_ml, w_mb, w_out)


def _ffn_kernel(x_ref, g_ref, wu_ref, cw_ref, cb_ref, wd_ref, fg_ref,
                o_ref, tail, hbuf_g, hbuf_v, *, final):
    tm = x_ref.shape[0]
    n_ck = D_FF // FF_CHUNK

    @pl.when(pl.program_id(1) == 0)
    def _():
        tail[...] = jnp.zeros_like(tail)

    x = x_ref[...]
    hb = (x * lax.rsqrt(jnp.mean(x * x, axis=-1, keepdims=True) + EPS) * g_ref[...]).astype(BF16)

    def cols(c, half):
        return slice(half * D_FF + c * FF_CHUNK, half * D_FF + (c + 1) * FF_CHUNK)

    def up(c):
        return _dot(hb, wu_ref[:, cols(c, 0)]), _dot(hb, wu_ref[:, cols(c, 1)])

    def conv(u, hbuf, cs):
        out = cb_ref[:, cs] + cw_ref[FFN_CONV - 1:FFN_CONV, cs] * u
        for k in range(FFN_CONV - 1):
            out = out + cw_ref[k:k + 1, cs] * pltpu.roll(u, FFN_CONV - 1 - k, axis=0)
        hbuf[0:CONV_PAD, :] = tail[:, cs]
        hbuf[CONV_PAD:2 * CONV_PAD, :] = u[:CONV_PAD]
        tail[:, cs] = u[tm - CONV_PAD:]
        head = cb_ref[:, cs]
        for k in range(FFN_CONV):
            off = CONV_PAD - (FFN_CONV - 1) + k
            head = head + cw_ref[k:k + 1, cs] * hbuf[off:off + CONV_PAD, :]
        return jnp.concatenate([head, out[CONV_PAD:]], axis=0)

    acc = x
    u_next = up(0)
    for c in range(n_ck):
        u_g, u_v = u_next
        if c + 1 < n_ck:
            u_next = up(c + 1)
        a_g = conv(u_g, hbuf_g, cols(c, 0))
        a_v = conv(u_v, hbuf_v, cols(c, 1))
        acc = acc + _dot((_silu(a_g) * a_v).astype(BF16), wd_ref[cols(c, 0), :])
    if final:
        acc = acc * lax.rsqrt(jnp.mean(acc * acc, axis=-1, keepdims=True) + EPS) * fg_ref[...]
    o_ref[...] = acc


def _ffn(x2, g, w_up, cw, cb, w_down, final_g, layer, batch, seq, final):
    n = x2.shape[0]
    nt = seq // TM_FFN

    def const(shape):
        return pl.BlockSpec(shape, lambda b, j: tuple(0 for _ in shape),
                            pipeline_mode=pl.Buffered(1))

    def layer_weight(shape):
        return pl.BlockSpec((None,) + shape, lambda b, j: (layer,) + tuple(0 for _ in shape),
                            pipeline_mode=pl.Buffered(1))

    tile = pl.BlockSpec((TM_FFN, D_MODEL), lambda b, j: (b * nt + j, 0))
    return pl.pallas_call(
        functools.partial(_ffn_kernel, final=final),
        grid=(batch, nt),
        in_specs=[
            tile, const((1, D_MODEL)),
            layer_weight((D_MODEL, 2 * D_FF)), const((FFN_CONV, 2 * D_FF)), const((1, 2 * D_FF)),
            layer_weight((D_FF, D_MODEL)), const((1, D_MODEL)),
        ],
        out_specs=tile,
        out_shape=jax.ShapeDtypeStruct((n, D_MODEL), F32),
        scratch_shapes=[
            pltpu.VMEM((CONV_PAD, 2 * D_FF), F32),
            pltpu.VMEM((2 * CONV_PAD, FF_CHUNK), F32),
            pltpu.VMEM((2 * CONV_PAD, FF_CHUNK), F32),
        ],
        compiler_params=_cparams(("parallel", "arbitrary")),
        name="conv_gated_mlp",
    )(x2, g, w_up, cw, cb, w_down, final_g)


def kernel(x, norm1_g, w_in, hg_lb_logits, hg_norm_g, ml_conv_w, ml_conv_b, ml_gate_b, ml_norm_g,
           mb_conv_w, mb_conv_b, mb_dt_bias, mb_a_log, mb_d, mb_norm_g, w_br_hg, w_br_ml, w_br_mb,
           w_out, norm2_g, w_up, ffn_conv_w, ffn_conv_b, w_down, final_g):
    batch, seq, _ = x.shape
    n = batch * seq
    assert seq % TS_MIX == 0 and seq % TM_FFN == 0 and seq % TM_PROJ == 0 and n % TM_MERGE == 0
    depth = w_in.shape[0]

    w = D_MODEL
    o_hg_q, o_hg_f, o_hg_i, o_hg_g = 0, w, 2 * w, 3 * w
    o_ml_qk = 4 * w
    o_ml_v = o_ml_qk + 2 * ML_HEADS * ML_DQK
    o_if = o_ml_v + ML_HEADS * ML_DV
    o_ml_o = o_if + 2 * ML_HEADS
    o_mb_z = o_ml_o + ML_HEADS * ML_DV
    o_mb_xbc = o_mb_z + MB_W
    o_dt = o_mb_xbc + MB_CONV_DIM
    o_gate = o_dt + MB_HEADS
    hw = HALF_TILE
    conv_halves = [o_ml_qk + k * hw for k in range(2)] + [o_mb_xbc + k * hw for k in range(4)]
    raw_halves = [o_hg_f, o_hg_f + hw, o_hg_i, o_hg_i + hw, o_ml_v, o_ml_v + hw]
    cols = [(o_hg_q, w), (o_hg_g, w), (o_mb_z, w), (o_ml_o, w), (o_gate, N_GATES * w)]
    for c0, r0 in zip(conv_halves, raw_halves):
        cols += [(c0, hw), (r0, hw)]
    w_in16 = w_in.astype(BF16)
    w_big = jnp.concatenate([w_in16[:, :, a:a + n_] for a, n_ in cols], axis=-1)
    pad = SMALL_W - 2 * ML_HEADS - MB_HEADS
    w_small = jnp.concatenate(
        [w_in16[:, :, o_if:o_ml_o], w_in16[:, :, o_dt:o_gate],
         jnp.zeros((depth, D_MODEL, pad), BF16)], axis=-1)

    lbs = _lbs(hg_lb_logits.astype(F32))
    hg_ng = jnp.tile(hg_norm_g, (1, HG_HEADS))
    ml_ng = jnp.tile(ml_norm_g, (1, ML_HEADS))
    gb_row = jnp.pad(ml_gate_b, ((0, 0), (0, SMALL_W - 2 * ML_HEADS)))
    dtb_c = jnp.pad(mb_dt_bias, ((0, 0), (DT_COL, SMALL_W - DT_COL - MB_HEADS)))
    alog_c = jnp.pad(mb_a_log, ((0, 0), (DT_COL, SMALL_W - DT_COL - MB_HEADS)))
    n_pair = MB_HEADS // 2
    dtb_r = jnp.repeat(mb_dt_bias, CHUNK, axis=-1).reshape(depth, n_pair, 2 * CHUNK)
    alog_r = jnp.repeat(mb_a_log, CHUNK, axis=-1).reshape(depth, n_pair, 2 * CHUNK)
    d_x = jnp.repeat(mb_d, MB_P, axis=-1)

    n_tiles = w_big.shape[-1] // TN_PROJ
    n_mixed = len(TILES_MIXED)
    conv_w = jnp.concatenate([ml_conv_w, mb_conv_w], axis=-1).reshape(depth, CONV_W, n_mixed, hw)
    conv_b = jnp.concatenate([ml_conv_b, mb_conv_b], axis=-1).reshape(depth, 1, n_mixed, hw)
    front = n_tiles - n_mixed
    cw_all = jnp.pad(jnp.swapaxes(conv_w, 1, 2), ((0, 0), (front, 0), (0, 0), (0, 0)))
    cb_all = jnp.pad(jnp.swapaxes(conv_b, 1, 2), ((0, 0), (front, 0), (0, 0), (0, 0)))

    w_hg = w_br_hg.astype(BF16)
    w_ml = w_br_ml.astype(BF16)
    w_mb = w_br_mb.astype(BF16)
    w_o = w_out.astype(BF16)
    w_u = w_up.astype(BF16)
    w_d = w_down.astype(BF16)

    x2 = x.reshape(n, D_MODEL)
    row = lambda a: a.reshape(1, -1)
    for l in range(depth):
        proj, small = _inproj(x2, row(norm1_g[l]), w_big, w_small, cw_all, cb_all, l, seq)
        chunks = small.reshape(n // CHUNK, CHUNK, SMALL_W)
        small_t = jnp.swapaxes(small.reshape(n // ML_BLOCK, ML_BLOCK, SMALL_W)[:, :, :2 * ML_HEADS], 1, 2)
        dt_t = jnp.swapaxes(chunks[:, :, DT_COL:DT_COL + MB_HEADS], 1, 2).reshape(
            n // CHUNK, n_pair, 2 * CHUNK)

        y_hg, y_ml, y_mb = _mixers(
            proj, small, small_t, dt_t, row(lbs[l]), row(hg_ng[l]),
            row(gb_row[l]), ml_gate_b[l].reshape(2 * ML_HEADS, 1), row(ml_ng[l]),
            row(dtb_c[l]), row(alog_c[l]), dtb_r[l], alog_r[l], row(d_x[l]), row(mb_norm_g[l]), batch, seq)
        x2 = _merge(x2, y_hg, y_ml, y_mb, proj, w_hg, w_ml, w_mb, w_o, l)
        x2 = _ffn(x2, row(norm2_g[l]), w_u, ffn_conv_w[l], row(ffn_conv_b[l]), w_d, row(final_g),
                  l, batch, seq, final=(l == depth - 1))
    return x2.reshape(batch, seq, D_MODEL)
```

```python
import functools
import itertools
import math

import jax
import jax.numpy as jnp
from jax import lax
from jax.experimental import pallas as pl
from jax.experimental.pallas import tpu as pltpu

F32 = jnp.float32
BF16 = jnp.bfloat16

D_MODEL = 1024
CHUNK = 64
CHUNK_LOG2 = 6
SUB = 8
FACTORED_DECAY_LIMIT = 112.0
SUBLANES = 8
LANES = 128
LOG2E = math.log2(math.e)
EPS = 1e-6
NEG_BIG = -1e30
HG_HEADS = 8
HG_D = 128
ML_HEADS = 4
ML_DQK = 128
ML_DV = 256
MB_HEADS = 16
MB_P = 64
MB_P_LOG2 = 6
MB_GROUPS = 4
MB_N = 128
MB_W = MB_HEADS * MB_P
MB_CONV_DIM = MB_W + 2 * MB_GROUPS * MB_N
D_FF = 2816
FFN_CONV = 3
FF_CHUNK = 1408
SMALL_W = 128
DT_COL = 8
CONV_PAD = 8

VMEM_LIMIT = 56 * 1024 * 1024

TM_PROJ = 2048
TN_PROJ = 1024
EPI_ROWS = 256
TILES_SILU = (0, 1, 2)
TILES_SIGMOID = (3, 4, 5, 6)
TILES_MIXED = (7, 8, 9, 10, 11, 12)
T_HG_Q, T_HG_G, T_MB_Z, T_ML_O, T_GATE0 = 0, 1, 2, 3, 4
N_GATES = 3
HALF_TILE = TN_PROJ // 2
CONV_W = 4
TS_MIX = 512
CHUNK_UNROLL = 4
ML_BLOCK = 2 * CHUNK
TM_MERGE = 512
TM_FFN = 512


def _silu(x):
    return x * jax.nn.sigmoid(x)


def _softplus(x):
    return jnp.maximum(x, 0.0) + jnp.log(1.0 + jnp.exp(-jnp.abs(x)))


def _log_sigmoid(x):
    return jnp.minimum(x, 0.0) - jnp.log(1.0 + jnp.exp(-jnp.abs(x)))


def _dot(a, b):
    return jnp.dot(a, b, preferred_element_type=F32)


def _dot_nt(a, b):
    return lax.dot_general(a, b, (((1,), (1,)), ((), ())), preferred_element_type=F32)


def _dot_tn(a, b):
    return lax.dot_general(a, b, (((0,), (0,)), ((), ())), preferred_element_type=F32)


def _split3(a):
    hi = a.astype(BF16)
    r = a - hi.astype(F32)
    mid = r.astype(BF16)
    lo = (r - mid.astype(F32)).astype(BF16)
    return hi, mid, lo


def _sel_left(sel, a):
    return _dot(jnp.concatenate([sel] * 3, axis=1), jnp.concatenate(_split3(a), axis=0))


def _sel_right(a, sel):
    return _dot(jnp.concatenate(_split3(a), axis=1), jnp.concatenate([sel] * 3, axis=0))


def _tri_lower(n):
    r = lax.broadcasted_iota(jnp.int32, (n, n), 0)
    c = lax.broadcasted_iota(jnp.int32, (n, n), 1)
    return r >= c


def _bcast_row(ref, h, r):
    return ref[h, pl.ds(r, SUBLANES, stride=0), :]


def _mask_bf16(m):
    return jnp.where(m, 1.0, 0.0).astype(BF16)


def _cparams(sem):
    return pltpu.CompilerParams(dimension_semantics=sem, vmem_limit_bytes=VMEM_LIMIT)


def _lbs_kernel(lg_ref, o_ref):
    lg = lg_ref[...]
    mx = jnp.max(lg, axis=0, keepdims=True)
    e = jnp.exp(lg - mx)
    p = e / jnp.sum(e, axis=0, keepdims=True)
    acc = jnp.zeros_like(p[0:1])
    rows = []
    for l in range(lg.shape[0]):
        acc = acc + p[l:l + 1]
        rows.append(acc - p[0:1])
    o_ref[...] = jnp.concatenate(rows, axis=0)


def _lbs(logits):
    return pl.pallas_call(
        _lbs_kernel,
        out_shape=jax.ShapeDtypeStruct(logits.shape, F32),
        name="hgrn2_lower_bounds",
    )(logits)


def _any_tile(j, tiles):
    hit = j == tiles[0]
    for t in tiles[1:]:
        hit = jnp.logical_or(hit, j == t)
    return hit


def _inproj_kernel(x_ref, g_ref, w_ref, ws_ref, cw_ref, cb_ref, o_ref, os_ref, h_ref, tail_ref, hbuf,
                   *, tiles_per_seq):
    i = pl.program_id(0)
    j = pl.program_id(1)
    n_blk = TM_PROJ // EPI_ROWS

    @pl.when(jnp.logical_and(i == 0, j == 0))
    def _():
        tail_ref[...] = jnp.zeros_like(tail_ref)

    @pl.when(j == 0)
    def _():
        x = x_ref[...]
        ms = jnp.mean(x * x, axis=-1, keepdims=True)
        hb = (x * lax.rsqrt(ms + EPS) * g_ref[...]).astype(BF16)
        h_ref[...] = hb
        os_ref[...] = _dot(hb, ws_ref[...])

    def pointwise(fn):
        ys = [_dot(h_ref[r * EPI_ROWS:(r + 1) * EPI_ROWS, :], w_ref[...]) for r in range(n_blk)]
        for r in range(n_blk):
            o_ref[r * EPI_ROWS:(r + 1) * EPI_ROWS, :] = fn(ys[r]).astype(o_ref.dtype)

    @pl.when(_any_tile(j, TILES_SILU))
    def _():
        pointwise(_silu)

    @pl.when(_any_tile(j, TILES_SIGMOID))
    def _():
        pointwise(jax.nn.sigmoid)

    @pl.when(j >= TILES_MIXED[0])
    def _():
        slot = j - TILES_MIXED[0]
        prev = jnp.where(i % tiles_per_seq == 0, 0.0, tail_ref[slot])
        cw = cw_ref[0]
        cb = cb_ref[0]
        ys = [_dot(h_ref[r * EPI_ROWS:(r + 1) * EPI_ROWS, :], w_ref[...]) for r in range(n_blk)]
        for r in range(n_blk):
            yb = ys[r][:, :HALF_TILE]
            out = cb + cw[CONV_W - 1:CONV_W] * yb
            for k in range(CONV_W - 1):
                out = out + cw[k:k + 1] * pltpu.roll(yb, CONV_W - 1 - k, axis=0)
            hbuf[0:CONV_PAD, :] = prev
            hbuf[CONV_PAD:2 * CONV_PAD, :] = yb[:CONV_PAD]
            head = cb
            for k in range(CONV_W):
                off = CONV_PAD - (CONV_W - 1) + k
                head = head + cw[k:k + 1] * hbuf[off:off + CONV_PAD, :]
            prev = yb[EPI_ROWS - CONV_PAD:]
            act = _silu(jnp.concatenate([head, out[CONV_PAD:]], axis=0))
            o_ref[r * EPI_ROWS:(r + 1) * EPI_ROWS, :] = jnp.concatenate(
                [act, ys[r][:, HALF_TILE:]], axis=-1).astype(o_ref.dtype)
        tail_ref[slot] = prev


def _inproj(x2, g, w_big, w_small, cw_all, cb_all, layer, seq):
    n = x2.shape[0]
    nb = w_big.shape[-1]
    n_tiles = nb // TN_PROJ
    assert n_tiles == len(TILES_SILU + TILES_SIGMOID + TILES_MIXED)
    return pl.pallas_call(
        functools.partial(_inproj_kernel, tiles_per_seq=seq // TM_PROJ),
        grid=(n // TM_PROJ, n_tiles),
        in_specs=[
            pl.BlockSpec((TM_PROJ, D_MODEL), lambda i, j: (i, 0)),
            pl.BlockSpec((1, D_MODEL), lambda i, j: (0, 0)),
            pl.BlockSpec((None, D_MODEL, TN_PROJ), lambda i, j: (layer, 0, j)),
            pl.BlockSpec((None, D_MODEL, SMALL_W), lambda i, j: (layer, 0, 0)),
            pl.BlockSpec((None, 1, CONV_W, HALF_TILE), lambda i, j: (layer, j, 0, 0)),
            pl.BlockSpec((None, 1, 1, HALF_TILE), lambda i, j: (layer, j, 0, 0)),
        ],
        out_specs=[
            pl.BlockSpec((TM_PROJ, TN_PROJ), lambda i, j: (i, j)),
            pl.BlockSpec((TM_PROJ, SMALL_W), lambda i, j: (i, 0)),
        ],
        out_shape=[
            jax.ShapeDtypeStruct((n, nb), BF16),
            jax.ShapeDtypeStruct((n, SMALL_W), F32),
        ],
        scratch_shapes=[
            pltpu.VMEM((TM_PROJ, D_MODEL), BF16),
            pltpu.VMEM((len(TILES_MIXED), CONV_PAD, HALF_TILE), F32),
            pltpu.VMEM((2 * CONV_PAD, HALF_TILE), F32),
        ],
        compiler_params=_cparams(("arbitrary", "arbitrary")),
        name="in_projection",
    )(x2, g, w_big, w_small, cw_all, cb_all)


def _hgrn2_stages(q_ref, f_ref, i_ref, g_ref, lb_ref, ng_ref, y_ref, state_ref, c_s, o_s):
    @pl.when(pl.program_id(1) == 0)
    def _():
        state_ref[...] = jnp.zeros_like(state_ref)

    lb = lb_ref[...]
    tri = _mask_bf16(_tri_lower(CHUNK))
    n_sub = CHUNK // SUB
    lane = lax.broadcasted_iota(jnp.int32, (SUBLANES, HG_D), 1)
    row = lax.broadcasted_iota(jnp.int32, (SUBLANES, HG_D), 0)
    causal = (lax.broadcasted_iota(jnp.int32, (CHUNK, HG_D), 1)
              <= lax.broadcasted_iota(jnp.int32, (CHUNK, HG_D), 0))

    heads = range(HG_HEADS)
    hsl = [slice(h * HG_D, (h + 1) * HG_D) for h in heads]
    zero_tail = jnp.zeros((HG_D - CHUNK, HG_D), BF16)
    unsafe = {}

    def decays(rows):
        sig = jax.nn.sigmoid(f_ref[rows, :].astype(F32))
        fgate = lb + (1.0 - lb) * sig
        b = _sel_left(tri, jnp.log(fgate) * LOG2E)
        return b, b - jnp.log(jnp.maximum(1.0 - fgate, 0.0)) * LOG2E

    def finish(rows, v16, o_inter, a_heads):
        parts = []
        for h in heads:
            v_pad = jnp.concatenate([v16[:, hsl[h]], zero_tail], axis=0)
            o = o_inter[h] + _dot(a_heads[h], v_pad)
            parts.append(o * lax.rsqrt(jnp.mean(o * o, axis=-1, keepdims=True) + EPS))
        on = jnp.concatenate(parts, axis=-1) * ng_ref[...]
        y_ref[rows, :] = (on * g_ref[rows, :].astype(F32)).astype(y_ref.dtype)

    def chunk(c, slot):
        rows = pl.ds(pl.multiple_of(c * CHUNK, CHUNK), CHUNK)
        b, cc = decays(rows)
        q = q_ref[rows, :].astype(F32)
        v16 = i_ref[rows, :]
        yield

        st = [state_ref[h] for h in heads]
        q_dec = (q * jnp.exp2(b)).astype(BF16)
        b_last = b[CHUNK - 1:CHUNK, :]
        k_dec = jnp.exp2(b_last - cc).astype(BF16)
        st_decay = jnp.exp2(b_last)
        o_inter = [_dot_nt(q_dec[:, hsl[h]], st[h].astype(BF16)) for h in heads]
        for h in heads:
            state_ref[h] = st_decay[:, hsl[h]] * st[h] + _dot_tn(v16[:, hsl[h]], k_dec[:, hsl[h]])
        o_s[slot] = jnp.concatenate(o_inter, axis=-1)
        unsafe[slot] = jnp.logical_not(jnp.min(b_last) >= -FACTORED_DECAY_LIMIT)
        yield

        k_all = jnp.exp2(-cc).astype(BF16)
        a_heads = []
        for h in heads:
            k_pad = jnp.concatenate([k_all[:, hsl[h]], zero_tail], axis=0)
            a_heads.append(jnp.where(causal, _dot_nt(q_dec[:, hsl[h]], k_pad), 0.0).astype(BF16))
        yield
        finish(rows, v16, o_inter, a_heads)

    def redo(c, slot):
        rows = pl.ds(pl.multiple_of(c * CHUNK, CHUNK), CHUNK)
        b, cc = decays(rows)
        for h in heads:
            c_s[h] = cc[:, hsl[h]]
        q = q_ref[rows, :].astype(F32)
        ones_rhs = jnp.ones((HG_D, HG_D), BF16)
        zs = []
        for i in range(n_sub):
            lo = i * SUB
            for s in range(SUB):
                r0 = lo + (s // SUBLANES) * SUBLANES
                c_row = jnp.concatenate([_bcast_row(c_s, h, lo + s) for h in heads], axis=-1)
                c_row = jnp.concatenate([c_row] * ((lo + SUB - r0) // SUBLANES), axis=0)
                zs.append(q[r0:lo + SUB] * jnp.exp2(b[r0:lo + SUB] - c_row))
        z_rows = sum(z.shape[0] for z in zs)
        z_all = jnp.concatenate([z[:, hsl[h]] for h in heads for z in zs], axis=0)
        r = _dot(z_all.astype(BF16), ones_rhs)

        a_off = []
        for i in range(1, n_sub):
            lo = i * SUB
            bref = b[lo - 1:lo, :]
            q_i = (q[lo:lo + SUB] * jnp.exp2(b[lo:lo + SUB] - bref)).astype(BF16)
            k_i = jnp.exp2(bref - cc[:lo]).astype(BF16)
            zero_rows = jnp.zeros((HG_D - lo, HG_D), BF16)
            a_off.append([_dot_nt(q_i[:, hsl[h]], jnp.concatenate([k_i[:, hsl[h]], zero_rows], axis=0))
                          for h in heads])

        a_heads = []
        for h in heads:
            a_rows = []
            off = h * z_rows
            for i in range(n_sub):
                lo = i * SUB
                a_i = a_off[i - 1][h] if i > 0 else jnp.zeros((SUB, HG_D), F32)
                tiles = [a_i[j * SUBLANES:(j + 1) * SUBLANES] for j in range(SUB // SUBLANES)]
                for s in range(SUB):
                    for j in range(s // SUBLANES, SUB // SUBLANES):
                        tiles[j] = jnp.where(lane == lo + s, r[off:off + SUBLANES], tiles[j])
                        off += SUBLANES
                for j in range(SUB // SUBLANES):
                    a_rows.append(jnp.where(lane - lo <= row + j * SUBLANES, tiles[j], 0.0))
            a_heads.append(jnp.concatenate(a_rows, axis=0).astype(BF16))
        o_all = o_s[slot]
        finish(rows, i_ref[rows, :], [o_all[:, hsl[h]] for h in heads], a_heads)

    return chunk, redo, unsafe


def _mlstm_stages(qk_ref, v_ref, og_ref, sm_ref, smt_ref, gbr_ref, gbc_ref, ng_ref,
                  y_ref, caug_ref, m_ref):
    half = ML_HEADS * ML_DQK
    k_scale = ML_DQK ** -0.5

    @pl.when(pl.program_id(1) == 0)
    def _():
        caug_ref[...] = jnp.zeros_like(caug_ref)
        m_ref[...] = jnp.zeros_like(m_ref)

    tri_b = _tri_lower(ML_BLOCK)
    tri_l = _mask_bf16(tri_b)
    tri_u = _mask_bf16(lax.broadcasted_iota(jnp.int32, (ML_BLOCK, ML_BLOCK), 0)
                       <= lax.broadcasted_iota(jnp.int32, (ML_BLOCK, ML_BLOCK), 1))
    ones_col = _mask_bf16(lax.broadcasted_iota(jnp.int32, (ML_BLOCK, LANES), 1) == 0)

    def chunk(c):
        rows = pl.ds(pl.multiple_of(c * ML_BLOCK, ML_BLOCK), ML_BLOCK)
        pre_c = sm_ref[rows, :] + gbr_ref[...]
        pre_r = smt_ref[c] + gbc_ref[...]
        cum_c = _sel_left(tri_l, _log_sigmoid(pre_c))
        cum_r = _sel_right(_log_sigmoid(pre_r), tri_u)
        yield
        heads = range(ML_HEADS)
        m_all = m_ref[...]
        m_old = [m_all[h:h + 1, 0:1] for h in heads]
        b_col = [cum_c[:, ML_HEADS + h:ML_HEADS + h + 1] for h in heads]
        i_col = [pre_c[:, h:h + 1] for h in heads]
        log_d = [jnp.where(tri_b, b_col[h] - cum_r[ML_HEADS + h:ML_HEADS + h + 1, :] + pre_r[h:h + 1, :],
                           NEG_BIG) for h in heads]
        log_inter = [b_col[h] + m_old[h] for h in heads]
        m_t = [jnp.maximum(jnp.max(log_d[h], axis=-1, keepdims=True), log_inter[h]) for h in heads]
        b_last = [b_col[h][ML_BLOCK - 1:ML_BLOCK, :] for h in heads]
        log_w = [b_last[h] - b_col[h] + i_col[h] for h in heads]
        m_new = [jnp.maximum(b_last[h] + m_old[h], jnp.max(log_w[h], axis=0, keepdims=True)) for h in heads]
        m_ref[...] = jnp.concatenate(
            [jnp.broadcast_to(m_new[h], (1, m_ref.shape[1])) for h in heads] + [m_all[ML_HEADS:]], axis=0)
        yield

        qb = [qk_ref[rows, h * ML_DQK:(h + 1) * ML_DQK] for h in heads]
        kb = [qk_ref[rows, half + h * ML_DQK:half + (h + 1) * ML_DQK] for h in heads]
        v_aug = [jnp.concatenate([v_ref[rows, h * ML_DV:(h + 1) * ML_DV], ones_col], axis=-1)
                 for h in heads]
        qk = [_dot_nt(qb[h], kb[h]) for h in heads]
        c_aug = [caug_ref[h] for h in heads]
        inter = [_dot(qb[h], c_aug[h].astype(BF16)) * jnp.exp(log_inter[h] - m_t[h]) for h in heads]
        yield
        s = [(qk[h] * (jnp.exp(log_d[h] - m_t[h]) * k_scale)).astype(BF16) for h in heads]
        num = [_dot(s[h], v_aug[h]) + inter[h] for h in heads]
        yield
        kw = [(kb[h].astype(F32) * (jnp.exp(log_w[h] - m_new[h]) * k_scale)).astype(BF16) for h in heads]
        for h in heads:
            caug_ref[h] = jnp.exp(b_last[h] + m_old[h] - m_new[h]) * c_aug[h] + _dot_tn(kw[h], v_aug[h])
        yield

        parts = []
        for h in heads:
            denom = jnp.maximum(jnp.abs(num[h][:, ML_DV:ML_DV + 1]), jnp.exp(-m_t[h]))
            o = num[h][:, :ML_DV] / denom
            parts.append(o * lax.rsqrt(jnp.mean(o * o, axis=-1, keepdims=True) + EPS))
        on = jnp.concatenate(parts, axis=-1) * ng_ref[...]
        y_ref[rows, :] = (on * og_ref[rows, :].astype(F32)).astype(y_ref.dtype)

    return chunk


def _ssd_stages(z_ref, xbc_ref, sm_ref, dtt_ref, dtb_c_ref, alog_c_ref,
                dtb_r_ref, alog_r_ref, d_ref, ng_ref, y_ref, state_ref):
    n_pair = MB_HEADS // 2
    pair_w = 2 * MB_P
    gw = MB_GROUPS * MB_N

    @pl.when(pl.program_id(1) == 0)
    def _():
        state_ref[...] = jnp.zeros_like(state_ref)

    tri_l = _mask_bf16(_tri_lower(CHUNK))
    sel_x = _mask_bf16(lax.broadcasted_iota(jnp.int32, (SMALL_W, MB_W), 0) - DT_COL
                       == jnp.right_shift(lax.broadcasted_iota(jnp.int32, (SMALL_W, MB_W), 1), MB_P_LOG2))
    ur = lax.broadcasted_iota(jnp.int32, (pair_w, pair_w), 0)
    uc = lax.broadcasted_iota(jnp.int32, (pair_w, pair_w), 1)
    same_half = jnp.right_shift(ur, CHUNK_LOG2) == jnp.right_shift(uc, CHUNK_LOG2)
    tri_u2 = _mask_bf16(same_half & (ur <= uc))
    causal = (lax.broadcasted_iota(jnp.int32, (CHUNK, MB_W), 0)
              >= jnp.bitwise_and(lax.broadcasted_iota(jnp.int32, (CHUNK, MB_W), 1), CHUNK - 1))
    first_head = lax.broadcasted_iota(jnp.int32, (CHUNK, pair_w), 1) < MB_P
    a_c = -jnp.exp(alog_c_ref[...])
    a_r = -jnp.exp(alog_r_ref[...])

    def chunk(c):
        rows = pl.ds(pl.multiple_of(c * CHUNK, CHUNK), CHUNK)
        dt_c = _softplus(sm_ref[rows, :] + dtb_c_ref[...])
        cum_c = _sel_left(tri_l, dt_c * a_c)
        dt_x = _sel_right(dt_c, sel_x)
        cum_x = _sel_right(cum_c, sel_x)
        dt_r = _softplus(dtt_ref[c] + dtb_r_ref[...])
        cum_r = _sel_right(dt_r * a_r, tri_u2)
        yield
        pairs = range(n_pair)
        grp = [(2 * p) // (MB_HEADS // MB_GROUPS) for p in pairs]
        lanes = [slice(p * pair_w, (p + 1) * pair_w) for p in pairs]
        xs = xbc_ref[rows, :MB_W].astype(F32)
        xdt = xs * dt_x
        cum_row = jnp.concatenate([cum_r[p:p + 1, :] for p in pairs], axis=-1)
        decay = jnp.exp(jnp.where(causal, cum_x - cum_row, NEG_BIG))
        cum_last = cum_x[CHUNK - 1:CHUNK, :]
        x_dec = (xdt * jnp.exp(cum_last - cum_x)).astype(BF16)
        st_decay = jnp.exp(cum_last)
        carry_w = jnp.exp(cum_x)
        yield
        bm = [xbc_ref[rows, MB_W + g * MB_N:MB_W + (g + 1) * MB_N] for g in range(MB_GROUPS)]
        cm = [xbc_ref[rows, MB_W + gw + g * MB_N:MB_W + gw + (g + 1) * MB_N] for g in range(MB_GROUPS)]
        cb2 = [_dot_nt(cm[g], jnp.concatenate([bm[g], bm[g]], axis=0)) for g in range(MB_GROUPS)]
        x2 = [jnp.concatenate([jnp.where(first_head, xdt[:, lanes[p]], 0.0),
                               jnp.where(first_head, 0.0, xdt[:, lanes[p]])], axis=0).astype(BF16)
              for p in pairs]
        yield
        st = [state_ref[p] for p in pairs]
        y_inter = [_dot(cm[grp[p]], st[p].astype(BF16)) for p in pairs]
        y_intra = [_dot((cb2[grp[p]] * decay[:, lanes[p]]).astype(BF16), x2[p]) for p in pairs]
        yield
        for p in pairs:
            state_ref[p] = st_decay[:, lanes[p]] * st[p] + _dot_tn(bm[grp[p]], x_dec[:, lanes[p]])
        yield
        y = (jnp.concatenate(y_intra, axis=-1) + jnp.concatenate(y_inter, axis=-1) * carry_w
             + d_ref[...] * xs)

        yz = y * z_ref[rows, :].astype(F32)
        gsz = MB_W // MB_GROUPS
        parts = []
        for g in range(MB_GROUPS):
            o = yz[:, g * gsz:(g + 1) * gsz]
            parts.append(o * lax.rsqrt(jnp.mean(o * o, axis=-1, keepdims=True) + EPS))
        y_ref[rows, :] = (jnp.concatenate(parts, axis=-1) * ng_ref[...]).astype(y_ref.dtype)

    return chunk


_DONE = object()


class _ColumnView:
    def __init__(self, pieces):
        self.pieces = pieces
        self.width = sum(w for _, _, w in pieces)

    def __getitem__(self, idx):
        rows, cols = idx
        lo, hi, _ = cols.indices(self.width)
        out, base = [], 0
        for ref, start, w in self.pieces:
            a, b = max(lo, base), min(hi, base + w)
            if a < b:
                out.append(ref[rows, start + a - base:start + b - base])
            base += w
        return out[0] if len(out) == 1 else jnp.concatenate(out, axis=-1)


def _mixers_kernel(q_ref, g_ref, z_ref, og_ref, m0, m1, m2, m3, m4, m5, sm_ref, smt_ref, dtt_ref,
                   lb_ref, hg_ng_ref, gbr_ref, gbc_ref, ml_ng_ref,
                   dtb_c_ref, alog_c_ref, dtb_r_ref, alog_r_ref, d_ref, mb_ng_ref,
                   y_hg, y_ml, y_mb, hg_state, hg_c, hg_o, ml_caug, ml_m, mb_state):
    ts = y_hg.shape[0]
    mixed = (m0, m1, m2, m3, m4, m5)
    conv = [(m, 0, HALF_TILE) for m in mixed]
    raw = [(m, HALF_TILE, HALF_TILE) for m in mixed]
    qk_view = _ColumnView(conv[0:2])
    xbc_view = _ColumnView(conv[2:6])
    f_view = _ColumnView(raw[0:2])
    i_view = _ColumnView(raw[2:4])
    v_view = _ColumnView(raw[4:6])
    hg, hg_redo, hg_unsafe = _hgrn2_stages(q_ref, f_view, i_view, g_ref, lb_ref, hg_ng_ref, y_hg,
                                           hg_state, hg_c, hg_o)
    ml = _mlstm_stages(qk_view, v_view, og_ref, sm_ref, smt_ref, gbr_ref, gbc_ref, ml_ng_ref,
                       y_ml, ml_caug, ml_m)
    mb = _ssd_stages(z_ref, xbc_view, sm_ref, dtt_ref, dtb_c_ref, alog_c_ref, dtb_r_ref, alog_r_ref,
                     d_ref, mb_ng_ref, y_mb, mb_state)
    per_block = ML_BLOCK // CHUNK
    blocks_per_trip = CHUNK_UNROLL // per_block
    chunks_per_trip = blocks_per_trip * per_block

    def trip_body(t, carry):
        for k in range(blocks_per_trip):
            cb = t * blocks_per_trip + k
            live = [itertools.chain(*[hg(cb * per_block + j, k * per_block + j) for j in range(per_block)]),
                    ml(cb),
                    itertools.chain(*[mb(cb * per_block + j) for j in range(per_block)])]
            while live:
                for g in list(live):
                    if next(g, _DONE) is _DONE:
                        live.remove(g)
        for slot in range(chunks_per_trip):
            pl.when(hg_unsafe[slot])(functools.partial(hg_redo, t * chunks_per_trip + slot, slot))
        return carry

    lax.fori_loop(0, ts // (chunks_per_trip * CHUNK), trip_body, 0)


def _mixers(proj, small, small_t, dt_t, lb, hg_ng, gb_row, gb_col, ml_ng,
            dtb_c, alog_c, dtb_r, alog_r, d_x, mb_ng, batch, seq):
    n = proj.shape[0]
    nt = seq // TS_MIX
    n_pair = MB_HEADS // 2
    w = D_MODEL

    def col(cb):
        return pl.BlockSpec((TS_MIX, w), lambda b, j: (b * nt + j, cb))

    def const(shape):
        return pl.BlockSpec(shape, lambda b, j: tuple(0 for _ in shape))

    tile_specs = [col(T_HG_Q), col(T_HG_G), col(T_MB_Z), col(T_ML_O)] + [col(t) for t in TILES_MIXED]
    gate_specs = [
        pl.BlockSpec((TS_MIX, SMALL_W), lambda b, j: (b * nt + j, 0)),
        pl.BlockSpec((TS_MIX // ML_BLOCK, 2 * ML_HEADS, ML_BLOCK), lambda b, j: (b * nt + j, 0, 0)),
        pl.BlockSpec((TS_MIX // CHUNK, n_pair, 2 * CHUNK), lambda b, j: (b * nt + j, 0, 0)),
    ]
    param_specs = [const((1, w)), const((1, w)),
                   const((1, SMALL_W)), const((2 * ML_HEADS, 1)), const((1, w)),
                   const((1, SMALL_W)), const((1, SMALL_W)),
                   const((n_pair, 2 * CHUNK)), const((n_pair, 2 * CHUNK)),
                   const((1, w)), const((1, w))]
    out = pl.BlockSpec((TS_MIX, w), lambda b, j: (b * nt + j, 0))
    return pl.pallas_call(
        _mixers_kernel,
        grid=(batch, nt),
        in_specs=tile_specs + gate_specs + param_specs,
        out_specs=[out, out, out],
        out_shape=[jax.ShapeDtypeStruct((n, w), BF16)] * 3,
        scratch_shapes=[
            pltpu.VMEM((HG_HEADS, HG_D, HG_D), F32),
            pltpu.VMEM((HG_HEADS, CHUNK, HG_D), F32),
            pltpu.VMEM((CHUNK_UNROLL, CHUNK, w), F32),
            pltpu.VMEM((ML_HEADS, ML_DQK, ML_DV + LANES), F32),
            pltpu.VMEM((SUBLANES, LANES), F32),
            pltpu.VMEM((n_pair, MB_N, 2 * MB_P), F32),
        ],
        compiler_params=_cparams(("parallel", "arbitrary")),
        name="token_mixers",
    )(*([proj] * (4 + len(TILES_MIXED))), small, small_t, dt_t,
      lb, hg_ng, gb_row, gb_col, ml_ng, dtb_c, alog_c, dtb_r, alog_r, d_x, mb_ng)


def _merge_kernel(x_ref, yh_ref, ym_ref, yb_ref, g0_ref, g1_ref, g2_ref,
                  wh_ref, wm_ref, wb_ref, wo_ref, o_ref):
    mixed = g0_ref[...].astype(F32) * _dot(yh_ref[...], wh_ref[...])
    mixed = mixed + g1_ref[...].astype(F32) * _dot(ym_ref[...], wm_ref[...])
    mixed = mixed + g2_ref[...].astype(F32) * _dot(yb_ref[...], wb_ref[...])
    o_ref[...] = x_ref[...] + _dot(mixed.astype(BF16), wo_ref[...])


def _merge(x2, y_hg, y_ml, y_mb, proj, w_hg, w_ml, w_mb, w_out, layer):
    n = x2.shape[0]
    tile = lambda cb: pl.BlockSpec((TM_MERGE, D_MODEL), lambda i: (i, cb))
    wspec = pl.BlockSpec((None, D_MODEL, D_MODEL), lambda i: (layer, 0, 0))
    return pl.pallas_call(
        _merge_kernel,
        grid=(n // TM_MERGE,),
        in_specs=[tile(0), tile(0), tile(0), tile(0), tile(T_GATE0), tile(T_GATE0 + 1), tile(T_GATE0 + 2),
                  wspec, wspec, wspec, wspec],
        out_specs=tile(0),
        out_shape=jax.ShapeDtypeStruct((n, D_MODEL), F32),
        compiler_params=_cparams(("parallel",)),
        name="branch_merge",
    )(x2, y_hg, y_ml, y_mb, proj, proj, proj, w_hg, w_ml, w_mb, w_out)


def _ffn_kernel(x_ref, g_ref, wu_ref, cw_ref, cb_ref, wd_ref, fg_ref,
                o_ref, tail, hbuf_g, hbuf_v, *, final):
    tm = x_ref.shape[0]
    n_ck = D_FF // FF_CHUNK

    @pl.when(pl.program_id(1) == 0)
    def _():
        tail[...] = jnp.zeros_like(tail)

    x = x_ref[...]
    hb = (x * lax.rsqrt(jnp.mean(x * x, axis=-1, keepdims=True) + EPS) * g_ref[...]).astype(BF16)

    def cols(c, half):
        return slice(half * D_FF + c * FF_CHUNK, half * D_FF + (c + 1) * FF_CHUNK)

    def up(c):
        return _dot(hb, wu_ref[:, cols(c, 0)]), _dot(hb, wu_ref[:, cols(c, 1)])

    def conv(u, hbuf, cs):
        out = cb_ref[:, cs] + cw_ref[FFN_CONV - 1:FFN_CONV, cs] * u
        for k in range(FFN_CONV - 1):
            out = out + cw_ref[k:k + 1, cs] * pltpu.roll(u, FFN_CONV - 1 - k, axis=0)
        hbuf[0:CONV_PAD, :] = tail[:, cs]
        hbuf[CONV_PAD:2 * CONV_PAD, :] = u[:CONV_PAD]
        tail[:, cs] = u[tm - CONV_PAD:]
        head = cb_ref[:, cs]
        for k in range(FFN_CONV):
            off = CONV_PAD - (FFN_CONV - 1) + k
            head = head + cw_ref[k:k + 1, cs] * hbuf[off:off + CONV_PAD, :]
        return jnp.concatenate([head, out[CONV_PAD:]], axis=0)

    acc = x
    u_next = up(0)
    for c in range(n_ck):
        u_g, u_v = u_next
        if c + 1 < n_ck:
            u_next = up(c + 1)
        a_g = conv(u_g, hbuf_g, cols(c, 0))
        a_v = conv(u_v, hbuf_v, cols(c, 1))
        acc = acc + _dot((_silu(a_g) * a_v).astype(BF16), wd_ref[cols(c, 0), :])
    if final:
        acc = acc * lax.rsqrt(jnp.mean(acc * acc, axis=-1, keepdims=True) + EPS) * fg_ref[...]
    o_ref[...] = acc


def _ffn(x2, g, w_up, cw, cb, w_down, final_g, layer, batch, seq, final):
    n = x2.shape[0]
    nt = seq // TM_FFN

    def const(shape):
        return pl.BlockSpec(shape, lambda b, j: tuple(0 for _ in shape),
                            pipeline_mode=pl.Buffered(1))

    def layer_weight(shape):
        return pl.BlockSpec((None,) + shape, lambda b, j: (layer,) + tuple(0 for _ in shape),
                            pipeline_mode=pl.Buffered(1))

    tile = pl.BlockSpec((TM_FFN, D_MODEL), lambda b, j: (b * nt + j, 0))
    return pl.pallas_call(
        functools.partial(_ffn_kernel, final=final),
        grid=(batch, nt),
        in_specs=[
            tile, const((1, D_MODEL)),
            layer_weight((D_MODEL, 2 * D_FF)), const((FFN_CONV, 2 * D_FF)), const((1, 2 * D_FF)),
            layer_weight((D_FF, D_MODEL)), const((1, D_MODEL)),
        ],
        out_specs=tile,
        out_shape=jax.ShapeDtypeStruct((n, D_MODEL), F32),
        scratch_shapes=[
            pltpu.VMEM((CONV_PAD, 2 * D_FF), F32),
            pltpu.VMEM((2 * CONV_PAD, FF_CHUNK), F32),
            pltpu.VMEM((2 * CONV_PAD, FF_CHUNK), F32),
        ],
        compiler_params=_cparams(("parallel", "arbitrary")),
        name="conv_gated_mlp",
    )(x2, g, w_up, cw, cb, w_down, final_g)


def kernel(x, norm1_g, w_in, hg_lb_logits, hg_norm_g, ml_conv_w, ml_conv_b, ml_gate_b, ml_norm_g,
           mb_conv_w, mb_conv_b, mb_dt_bias, mb_a_log, mb_d, mb_norm_g, w_br_hg, w_br_ml, w_br_mb,
           w_out, norm2_g, w_up, ffn_conv_w, ffn_conv_b, w_down, final_g):
    batch, seq, _ = x.shape
    n = batch * seq
    assert seq % TS_MIX == 0 and seq % TM_FFN == 0 and seq % TM_PROJ == 0 and n % TM_MERGE == 0
    depth = w_in.shape[0]

    w = D_MODEL
    o_hg_q, o_hg_f, o_hg_i, o_hg_g = 0, w, 2 * w, 3 * w
    o_ml_qk = 4 * w
    o_ml_v = o_ml_qk + 2 * ML_HEADS * ML_DQK
    o_if = o_ml_v + ML_HEADS * ML_DV
    o_ml_o = o_if + 2 * ML_HEADS
    o_mb_z = o_ml_o + ML_HEADS * ML_DV
    o_mb_xbc = o_mb_z + MB_W
    o_dt = o_mb_xbc + MB_CONV_DIM
    o_gate = o_dt + MB_HEADS
    hw = HALF_TILE
    conv_halves = [o_ml_qk + k * hw for k in range(2)] + [o_mb_xbc + k * hw for k in range(4)]
    raw_halves = [o_hg_f, o_hg_f + hw, o_hg_i, o_hg_i + hw, o_ml_v, o_ml_v + hw]
    cols = [(o_hg_q, w), (o_hg_g, w), (o_mb_z, w), (o_ml_o, w), (o_gate, N_GATES * w)]
    for c0, r0 in zip(conv_halves, raw_halves):
        cols += [(c0, hw), (r0, hw)]
    w_in16 = w_in.astype(BF16)
    w_big = jnp.concatenate([w_in16[:, :, a:a + n_] for a, n_ in cols], axis=-1)
    pad = SMALL_W - 2 * ML_HEADS - MB_HEADS
    w_small = jnp.concatenate(
        [w_in16[:, :, o_if:o_ml_o], w_in16[:, :, o_dt:o_gate],
         jnp.zeros((depth, D_MODEL, pad), BF16)], axis=-1)

    lbs = _lbs(hg_lb_logits.astype(F32))
    hg_ng = jnp.tile(hg_norm_g, (1, HG_HEADS))
    ml_ng = jnp.tile(ml_norm_g, (1, ML_HEADS))
    gb_row = jnp.pad(ml_gate_b, ((0, 0), (0, SMALL_W - 2 * ML_HEADS)))
    dtb_c = jnp.pad(mb_dt_bias, ((0, 0), (DT_COL, SMALL_W - DT_COL - MB_HEADS)))
    alog_c = jnp.pad(mb_a_log, ((0, 0), (DT_COL, SMALL_W - DT_COL - MB_HEADS)))
    n_pair = MB_HEADS // 2
    dtb_r = jnp.repeat(mb_dt_bias, CHUNK, axis=-1).reshape(depth, n_pair, 2 * CHUNK)
    alog_r = jnp.repeat(mb_a_log, CHUNK, axis=-1).reshape(depth, n_pair, 2 * CHUNK)
    d_x = jnp.repeat(mb_d, MB_P, axis=-1)

    n_tiles = w_big.shape[-1] // TN_PROJ
    n_mixed = len(TILES_MIXED)
    conv_w = jnp.concatenate([ml_conv_w, mb_conv_w], axis=-1).reshape(depth, CONV_W, n_mixed, hw)
    conv_b = jnp.concatenate([ml_conv_b, mb_conv_b], axis=-1).reshape(depth, 1, n_mixed, hw)
    front = n_tiles - n_mixed
    cw_all = jnp.pad(jnp.swapaxes(conv_w, 1, 2), ((0, 0), (front, 0), (0, 0), (0, 0)))
    cb_all = jnp.pad(jnp.swapaxes(conv_b, 1, 2), ((0, 0), (front, 0), (0, 0), (0, 0)))

    w_hg = w_br_hg.astype(BF16)
    w_ml = w_br_ml.astype(BF16)
    w_mb = w_br_mb.astype(BF16)
    w_o = w_out.astype(BF16)
    w_u = w_up.astype(BF16)
    w_d = w_down.astype(BF16)

    x2 = x.reshape(n, D_MODEL)
    row = lambda a: a.reshape(1, -1)
    for l in range(depth):
        proj, small = _inproj(x2, row(norm1_g[l]), w_big, w_small, cw_all, cb_all, l, seq)
        chunks = small.reshape(n // CHUNK, CHUNK, SMALL_W)
        small_t = jnp.swapaxes(small.reshape(n // ML_BLOCK, ML_BLOCK, SMALL_W)[:, :, :2 * ML_HEADS], 1, 2)
        dt_t = jnp.swapaxes(chunks[:, :, DT_COL:DT_COL + MB_HEADS], 1, 2).reshape(
            n // CHUNK, n_pair, 2 * CHUNK)

        y_hg, y_ml, y_mb = _mixers(
            proj, small, small_t, dt_t, row(lbs[l]), row(hg_ng[l]),
            row(gb_row[l]), ml_gate_b[l].reshape(2 * ML_HEADS, 1), row(ml_ng[l]),
            row(dtb_c[l]), row(alog_c[l]), dtb_r[l], alog_r[l], row(d_x[l]), row(mb_norm_g[l]), batch, seq)
        x2 = _merge(x2, y_hg, y_ml, y_mb, proj, w_hg, w_ml, w_mb, w_o, l)
        x2 = _ffn(x2, row(norm2_g[l]), w_u, ffn_conv_w[l], row(ffn_conv_b[l]), w_d, row(final_g),
                  l, batch, seq, final=(l == depth - 1))
    return x2.reshape(batch, seq, D_MODEL)
```

```python
import functools
import itertools
import math

import jax
import jax.numpy as jnp
from jax import lax
from jax.experimental import pallas as pl
from jax.experimental.pallas import tpu as pltpu

F32 = jnp.float32
BF16 = jnp.bfloat16

D_MODEL = 1024
CHUNK = 64
CHUNK_LOG2 = 6
SUB = 8
FACTORED_DECAY_LIMIT = 112.0
SUBLANES = 8
LANES = 128
LOG2E = math.log2(math.e)
EPS = 1e-6
NEG_BIG = -1e30
HG_HEADS = 8
HG_D = 128
ML_HEADS = 4
ML_DQK = 128
ML_DV = 256
MB_HEADS = 16
MB_P = 64
MB_P_LOG2 = 6
MB_GROUPS = 4
MB_N = 128
MB_W = MB_HEADS * MB_P
MB_CONV_DIM = MB_W + 2 * MB_GROUPS * MB_N
D_FF = 2816
FFN_CONV = 3
FF_CHUNK = 1408
SMALL_W = 128
DT_COL = 8
CONV_PAD = 8

VMEM_LIMIT = 56 * 1024 * 1024

TM_PROJ = 2048
TN_PROJ = 1024
EPI_ROWS = 256
TILES_SILU = (0, 1, 2)
TILES_SIGMOID = (3, 4, 5, 6)
TILES_MIXED = (7, 8, 9, 10, 11, 12)
T_HG_Q, T_HG_G, T_MB_Z, T_ML_O, T_GATE0 = 0, 1, 2, 3, 4
N_GATES = 3
HALF_TILE = TN_PROJ // 2
CONV_W = 4
TS_MIX = 512
CHUNK_UNROLL = 8
ML_BLOCK = 2 * CHUNK
TM_MERGE = 512
TM_FFN = 512


def _silu(x):
    return x * jax.nn.sigmoid(x)


def _softplus(x):
    return jnp.maximum(x, 0.0) + jnp.log(1.0 + jnp.exp(-jnp.abs(x)))


def _log_sigmoid(x):
    return jnp.minimum(x, 0.0) - jnp.log(1.0 + jnp.exp(-jnp.abs(x)))


def _dot(a, b):
    return jnp.dot(a, b, preferred_element_type=F32)


def _dot_nt(a, b):
    return lax.dot_general(a, b, (((1,), (1,)), ((), ())), preferred_element_type=F32)


def _dot_tn(a, b):
    return lax.dot_general(a, b, (((0,), (0,)), ((), ())), preferred_element_type=F32)


def _split3(a):
    hi = a.astype(BF16)
    r = a - hi.astype(F32)
    mid = r.astype(BF16)
    lo = (r - mid.astype(F32)).astype(BF16)
    return hi, mid, lo


def _sel_left(sel, a):
    return _dot(jnp.concatenate([sel] * 3, axis=1), jnp.concatenate(_split3(a), axis=0))


def _sel_right(a, sel):
    return _dot(jnp.concatenate(_split3(a), axis=1), jnp.concatenate([sel] * 3, axis=0))


def _tri_lower(n):
    r = lax.broadcasted_iota(jnp.int32, (n, n), 0)
    c = lax.broadcasted_iota(jnp.int32, (n, n), 1)
    return r >= c


def _bcast_row(ref, h, r):
    return ref[h, pl.ds(r, SUBLANES, stride=0), :]


def _mask_bf16(m):
    return jnp.where(m, 1.0, 0.0).astype(BF16)


def _cparams(sem):
    return pltpu.CompilerParams(dimension_semantics=sem, vmem_limit_bytes=VMEM_LIMIT)


def _lbs_kernel(lg_ref, o_ref):
    lg = lg_ref[...]
    mx = jnp.max(lg, axis=0, keepdims=True)
    e = jnp.exp(lg - mx)
    p = e / jnp.sum(e, axis=0, keepdims=True)
    acc = jnp.zeros_like(p[0:1])
    rows = []
    for l in range(lg.shape[0]):
        acc = acc + p[l:l + 1]
        rows.append(acc - p[0:1])
    o_ref[...] = jnp.concatenate(rows, axis=0)


def _lbs(logits):
    return pl.pallas_call(
        _lbs_kernel,
        out_shape=jax.ShapeDtypeStruct(logits.shape, F32),
        name="hgrn2_lower_bounds",
    )(logits)


def _any_tile(j, tiles):
    hit = j == tiles[0]
    for t in tiles[1:]:
        hit = jnp.logical_or(hit, j == t)
    return hit


def _inproj_kernel(x_ref, g_ref, w_ref, ws_ref, cw_ref, cb_ref, o_ref, os_ref, h_ref, tail_ref, hbuf,
                   *, tiles_per_seq):
    i = pl.program_id(0)
    j = pl.program_id(1)
    n_blk = TM_PROJ // EPI_ROWS

    @pl.when(jnp.logical_and(i == 0, j == 0))
    def _():
        tail_ref[...] = jnp.zeros_like(tail_ref)

    @pl.when(j == 0)
    def _():
        x = x_ref[...]
        ms = jnp.mean(x * x, axis=-1, keepdims=True)
        hb = (x * lax.rsqrt(ms + EPS) * g_ref[...]).astype(BF16)
        h_ref[...] = hb
        os_ref[...] = _dot(hb, ws_ref[...])

    def pointwise(fn):
        ys = [_dot(h_ref[r * EPI_ROWS:(r + 1) * EPI_ROWS, :], w_ref[...]) for r in range(n_blk)]
        for r in range(n_blk):
            o_ref[r * EPI_ROWS:(r + 1) * EPI_ROWS, :] = fn(ys[r]).astype(o_ref.dtype)

    @pl.when(_any_tile(j, TILES_SILU))
    def _():
        pointwise(_silu)

    @pl.when(_any_tile(j, TILES_SIGMOID))
    def _():
        pointwise(jax.nn.sigmoid)

    @pl.when(j >= TILES_MIXED[0])
    def _():
        slot = j - TILES_MIXED[0]
        prev = jnp.where(i % tiles_per_seq == 0, 0.0, tail_ref[slot])
        cw = cw_ref[0]
        cb = cb_ref[0]
        ys = [_dot(h_ref[r * EPI_ROWS:(r + 1) * EPI_ROWS, :], w_ref[...]) for r in range(n_blk)]
        for r in range(n_blk):
            yb = ys[r][:, :HALF_TILE]
            out = cb + cw[CONV_W - 1:CONV_W] * yb
            for k in range(CONV_W - 1):
                out = out + cw[k:k + 1] * pltpu.roll(yb, CONV_W - 1 - k, axis=0)
            hbuf[0:CONV_PAD, :] = prev
            hbuf[CONV_PAD:2 * CONV_PAD, :] = yb[:CONV_PAD]
            head = cb
            for k in range(CONV_W):
                off = CONV_PAD - (CONV_W - 1) + k
                head = head + cw[k:k + 1] * hbuf[off:off + CONV_PAD, :]
            prev = yb[EPI_ROWS - CONV_PAD:]
            act = _silu(jnp.concatenate([head, out[CONV_PAD:]], axis=0))
            o_ref[r * EPI_ROWS:(r + 1) * EPI_ROWS, :] = jnp.concatenate(
                [act, ys[r][:, HALF_TILE:]], axis=-1).astype(o_ref.dtype)
        tail_ref[slot] = prev


def _inproj(x2, g, w_big, w_small, cw_all, cb_all, layer, seq):
    n = x2.shape[0]
    nb = w_big.shape[-1]
    n_tiles = nb // TN_PROJ
    assert n_tiles == len(TILES_SILU + TILES_SIGMOID + TILES_MIXED)
    return pl.pallas_call(
        functools.partial(_inproj_kernel, tiles_per_seq=seq // TM_PROJ),
        grid=(n // TM_PROJ, n_tiles),
        in_specs=[
            pl.BlockSpec((TM_PROJ, D_MODEL), lambda i, j: (i, 0)),
            pl.BlockSpec((1, D_MODEL), lambda i, j: (0, 0)),
            pl.BlockSpec((None, D_MODEL, TN_PROJ), lambda i, j: (layer, 0, j)),
            pl.BlockSpec((None, D_MODEL, SMALL_W), lambda i, j: (layer, 0, 0)),
            pl.BlockSpec((None, 1, CONV_W, HALF_TILE), lambda i, j: (layer, j, 0, 0)),
            pl.BlockSpec((None, 1, 1, HALF_TILE), lambda i, j: (layer, j, 0, 0)),
        ],
        out_specs=[
            pl.BlockSpec((TM_PROJ, TN_PROJ), lambda i, j: (i, j)),
            pl.BlockSpec((TM_PROJ, SMALL_W), lambda i, j: (i, 0)),
        ],
        out_shape=[
            jax.ShapeDtypeStruct((n, nb), BF16),
            jax.ShapeDtypeStruct((n, SMALL_W), F32),
        ],
        scratch_shapes=[
            pltpu.VMEM((TM_PROJ, D_MODEL), BF16),
            pltpu.VMEM((len(TILES_MIXED), CONV_PAD, HALF_TILE), F32),
            pltpu.VMEM((2 * CONV_PAD, HALF_TILE), F32),
        ],
        compiler_params=_cparams(("arbitrary", "arbitrary")),
        name="in_projection",
    )(x2, g, w_big, w_small, cw_all, cb_all)


def _hgrn2_stages(q_ref, f_ref, i_ref, g_ref, lb_ref, ng_ref, y_ref, state_ref, c_s, o_s):
    @pl.when(pl.program_id(1) == 0)
    def _():
        state_ref[...] = jnp.zeros_like(state_ref)

    lb = lb_ref[...]
    tri = _mask_bf16(_tri_lower(CHUNK))
    n_sub = CHUNK // SUB
    lane = lax.broadcasted_iota(jnp.int32, (SUBLANES, HG_D), 1)
    row = lax.broadcasted_iota(jnp.int32, (SUBLANES, HG_D), 0)
    causal = (lax.broadcasted_iota(jnp.int32, (CHUNK, HG_D), 1)
              <= lax.broadcasted_iota(jnp.int32, (CHUNK, HG_D), 0))

    heads = range(HG_HEADS)
    hsl = [slice(h * HG_D, (h + 1) * HG_D) for h in heads]
    zero_tail = jnp.zeros((HG_D - CHUNK, HG_D), BF16)
    unsafe = {}

    def decays(rows):
        sig = jax.nn.sigmoid(f_ref[rows, :].astype(F32))
        fgate = lb + (1.0 - lb) * sig
        b = _sel_left(tri, jnp.log(fgate) * LOG2E)
        return b, b - jnp.log(jnp.maximum(1.0 - fgate, 0.0)) * LOG2E

    def finish(rows, v16, o_inter, a_heads):
        parts = []
        for h in heads:
            v_pad = jnp.concatenate([v16[:, hsl[h]], zero_tail], axis=0)
            o = o_inter[h] + _dot(a_heads[h], v_pad)
            parts.append(o * lax.rsqrt(jnp.mean(o * o, axis=-1, keepdims=True) + EPS))
        on = jnp.concatenate(parts, axis=-1) * ng_ref[...]
        y_ref[rows, :] = (on * g_ref[rows, :].astype(F32)).astype(y_ref.dtype)

    def chunk(c, slot):
        rows = pl.ds(pl.multiple_of(c * CHUNK, CHUNK), CHUNK)
        b, cc = decays(rows)
        q = q_ref[rows, :].astype(F32)
        v16 = i_ref[rows, :]
        yield

        st = [state_ref[h] for h in heads]
        q_dec = (q * jnp.exp2(b)).astype(BF16)
        b_last = b[CHUNK - 1:CHUNK, :]
        k_dec = jnp.exp2(b_last - cc).astype(BF16)
        st_decay = jnp.exp2(b_last)
        o_inter = [_dot_nt(q_dec[:, hsl[h]], st[h].astype(BF16)) for h in heads]
        for h in heads:
            state_ref[h] = st_decay[:, hsl[h]] * st[h] + _dot_tn(v16[:, hsl[h]], k_dec[:, hsl[h]])
        o_s[slot] = jnp.concatenate(o_inter, axis=-1)
        unsafe[slot] = jnp.logical_not(jnp.min(b_last) >= -FACTORED_DECAY_LIMIT)
        yield

        k_all = jnp.exp2(-cc).astype(BF16)
        a_heads = []
        for h in heads:
            k_pad = jnp.concatenate([k_all[:, hsl[h]], zero_tail], axis=0)
            a_heads.append(jnp.where(causal, _dot_nt(q_dec[:, hsl[h]], k_pad), 0.0).astype(BF16))
        yield
        finish(rows, v16, o_inter, a_heads)

    def redo(c, slot):
        rows = pl.ds(pl.multiple_of(c * CHUNK, CHUNK), CHUNK)
        b, cc = decays(rows)
        for h in heads:
            c_s[h] = cc[:, hsl[h]]
        q = q_ref[rows, :].astype(F32)
        ones_rhs = jnp.ones((HG_D, HG_D), BF16)
        zs = []
        for i in range(n_sub):
            lo = i * SUB
            for s in range(SUB):
                r0 = lo + (s // SUBLANES) * SUBLANES
                c_row = jnp.concatenate([_bcast_row(c_s, h, lo + s) for h in heads], axis=-1)
                c_row = jnp.concatenate([c_row] * ((lo + SUB - r0) // SUBLANES), axis=0)
                zs.append(q[r0:lo + SUB] * jnp.exp2(b[r0:lo + SUB] - c_row))
        z_rows = sum(z.shape[0] for z in zs)
        z_all = jnp.concatenate([z[:, hsl[h]] for h in heads for z in zs], axis=0)
        r = _dot(z_all.astype(BF16), ones_rhs)

        a_off = []
        for i in range(1, n_sub):
            lo = i * SUB
            bref = b[lo - 1:lo, :]
            q_i = (q[lo:lo + SUB] * jnp.exp2(b[lo:lo + SUB] - bref)).astype(BF16)
            k_i = jnp.exp2(bref - cc[:lo]).astype(BF16)
            zero_rows = jnp.zeros((HG_D - lo, HG_D), BF16)
            a_off.append([_dot_nt(q_i[:, hsl[h]], jnp.concatenate([k_i[:, hsl[h]], zero_rows], axis=0))
                          for h in heads])

        a_heads = []
        for h in heads:
            a_rows = []
            off = h * z_rows
            for i in range(n_sub):
                lo = i * SUB
                a_i = a_off[i - 1][h] if i > 0 else jnp.zeros((SUB, HG_D), F32)
                tiles = [a_i[j * SUBLANES:(j + 1) * SUBLANES] for j in range(SUB // SUBLANES)]
                for s in range(SUB):
                    for j in range(s // SUBLANES, SUB // SUBLANES):
                        tiles[j] = jnp.where(lane == lo + s, r[off:off + SUBLANES], tiles[j])
                        off += SUBLANES
                for j in range(SUB // SUBLANES):
                    a_rows.append(jnp.where(lane - lo <= row + j * SUBLANES, tiles[j], 0.0))
            a_heads.append(jnp.concatenate(a_rows, axis=0).astype(BF16))
        o_all = o_s[slot]
        finish(rows, i_ref[rows, :], [o_all[:, hsl[h]] for h in heads], a_heads)

    return chunk, redo, unsafe


def _mlstm_stages(qk_ref, v_ref, og_ref, sm_ref, smt_ref, gbr_ref, gbc_ref, ng_ref,
                  y_ref, caug_ref, m_ref):
    half = ML_HEADS * ML_DQK
    k_scale = ML_DQK ** -0.5

    @pl.when(pl.program_id(1) == 0)
    def _():
        caug_ref[...] = jnp.zeros_like(caug_ref)
        m_ref[...] = jnp.zeros_like(m_ref)

    tri_b = _tri_lower(ML_BLOCK)
    tri_l = _mask_bf16(tri_b)
    tri_u = _mask_bf16(lax.broadcasted_iota(jnp.int32, (ML_BLOCK, ML_BLOCK), 0)
                       <= lax.broadcasted_iota(jnp.int32, (ML_BLOCK, ML_BLOCK), 1))
    ones_col = _mask_bf16(lax.broadcasted_iota(jnp.int32, (ML_BLOCK, LANES), 1) == 0)

    def chunk(c):
        rows = pl.ds(pl.multiple_of(c * ML_BLOCK, ML_BLOCK), ML_BLOCK)
        pre_c = sm_ref[rows, :] + gbr_ref[...]
        pre_r = smt_ref[c] + gbc_ref[...]
        cum_c = _sel_left(tri_l, _log_sigmoid(pre_c))
        cum_r = _sel_right(_log_sigmoid(pre_r), tri_u)
        yield
        heads = range(ML_HEADS)
        m_all = m_ref[...]
        m_old = [m_all[h:h + 1, 0:1] for h in heads]
        b_col = [cum_c[:, ML_HEADS + h:ML_HEADS + h + 1] for h in heads]
        i_col = [pre_c[:, h:h + 1] for h in heads]
        log_d = [jnp.where(tri_b, b_col[h] - cum_r[ML_HEADS + h:ML_HEADS + h + 1, :] + pre_r[h:h + 1, :],
                           NEG_BIG) for h in heads]
        log_inter = [b_col[h] + m_old[h] for h in heads]
        m_t = [jnp.maximum(jnp.max(log_d[h], axis=-1, keepdims=True), log_inter[h]) for h in heads]
        b_last = [b_col[h][ML_BLOCK - 1:ML_BLOCK, :] for h in heads]
        log_w = [b_last[h] - b_col[h] + i_col[h] for h in heads]
        m_new = [jnp.maximum(b_last[h] + m_old[h], jnp.max(log_w[h], axis=0, keepdims=True)) for h in heads]
        m_ref[...] = jnp.concatenate(
            [jnp.broadcast_to(m_new[h], (1, m_ref.shape[1])) for h in heads] + [m_all[ML_HEADS:]], axis=0)
        yield

        qb = [qk_ref[rows, h * ML_DQK:(h + 1) * ML_DQK] for h in heads]
        kb = [qk_ref[rows, half + h * ML_DQK:half + (h + 1) * ML_DQK] for h in heads]
        v_aug = [jnp.concatenate([v_ref[rows, h * ML_DV:(h + 1) * ML_DV], ones_col], axis=-1)
                 for h in heads]
        qk = [_dot_nt(qb[h], kb[h]) for h in heads]
        c_aug = [caug_ref[h] for h in heads]
        inter = [_dot(qb[h], c_aug[h].astype(BF16)) * jnp.exp(log_inter[h] - m_t[h]) for h in heads]
        yield
        s = [(qk[h] * (jnp.exp(log_d[h] - m_t[h]) * k_scale)).astype(BF16) for h in heads]
        num = [_dot(s[h], v_aug[h]) + inter[h] for h in heads]
        yield
        kw = [(kb[h].astype(F32) * (jnp.exp(log_w[h] - m_new[h]) * k_scale)).astype(BF16) for h in heads]
        for h in heads:
            caug_ref[h] = jnp.exp(b_last[h] + m_old[h] - m_new[h]) * c_aug[h] + _dot_tn(kw[h], v_aug[h])
        yield

        parts = []
        for h in heads:
            denom = jnp.maximum(jnp.abs(num[h][:, ML_DV:ML_DV + 1]), jnp.exp(-m_t[h]))
            o = num[h][:, :ML_DV] / denom
            parts.append(o * lax.rsqrt(jnp.mean(o * o, axis=-1, keepdims=True) + EPS))
        on = jnp.concatenate(parts, axis=-1) * ng_ref[...]
        y_ref[rows, :] = (on * og_ref[rows, :].astype(F32)).astype(y_ref.dtype)

    return chunk


def _ssd_stages(z_ref, xbc_ref, sm_ref, dtt_ref, dtb_c_ref, alog_c_ref,
                dtb_r_ref, alog_r_ref, d_ref, ng_ref, y_ref, state_ref):
    n_pair = MB_HEADS // 2
    pair_w = 2 * MB_P
    gw = MB_GROUPS * MB_N

    @pl.when(pl.program_id(1) == 0)
    def _():
        state_ref[...] = jnp.zeros_like(state_ref)

    tri_l = _mask_bf16(_tri_lower(CHUNK))
    sel_x = _mask_bf16(lax.broadcasted_iota(jnp.int32, (SMALL_W, MB_W), 0) - DT_COL
                       == jnp.right_shift(lax.broadcasted_iota(jnp.int32, (SMALL_W, MB_W), 1), MB_P_LOG2))
    ur = lax.broadcasted_iota(jnp.int32, (pair_w, pair_w), 0)
    uc = lax.broadcasted_iota(jnp.int32, (pair_w, pair_w), 1)
    same_half = jnp.right_shift(ur, CHUNK_LOG2) == jnp.right_shift(uc, CHUNK_LOG2)
    tri_u2 = _mask_bf16(same_half & (ur <= uc))
    causal = (lax.broadcasted_iota(jnp.int32, (CHUNK, MB_W), 0)
              >= jnp.bitwise_and(lax.broadcasted_iota(jnp.int32, (CHUNK, MB_W), 1), CHUNK - 1))
    first_head = lax.broadcasted_iota(jnp.int32, (CHUNK, pair_w), 1) < MB_P
    a_c = -jnp.exp(alog_c_ref[...])
    a_r = -jnp.exp(alog_r_ref[...])

    def chunk(c):
        rows = pl.ds(pl.multiple_of(c * CHUNK, CHUNK), CHUNK)
        dt_c = _softplus(sm_ref[rows, :] + dtb_c_ref[...])
        cum_c = _sel_left(tri_l, dt_c * a_c)
        dt_x = _sel_right(dt_c, sel_x)
        cum_x = _sel_right(cum_c, sel_x)
        dt_r = _softplus(dtt_ref[c] + dtb_r_ref[...])
        cum_r = _sel_right(dt_r * a_r, tri_u2)
        yield
        pairs = range(n_pair)
        grp = [(2 * p) // (MB_HEADS // MB_GROUPS) for p in pairs]
        lanes = [slice(p * pair_w, (p + 1) * pair_w) for p in pairs]
        xs = xbc_ref[rows, :MB_W].astype(F32)
        xdt = xs * dt_x
        cum_row = jnp.concatenate([cum_r[p:p + 1, :] for p in pairs], axis=-1)
        decay = jnp.exp(jnp.where(causal, cum_x - cum_row, NEG_BIG))
        cum_last = cum_x[CHUNK - 1:CHUNK, :]
        x_dec = (xdt * jnp.exp(cum_last - cum_x)).astype(BF16)
        st_decay = jnp.exp(cum_last)
        carry_w = jnp.exp(cum_x)
        yield
        bm = [xbc_ref[rows, MB_W + g * MB_N:MB_W + (g + 1) * MB_N] for g in range(MB_GROUPS)]
        cm = [xbc_ref[rows, MB_W + gw + g * MB_N:MB_W + gw + (g + 1) * MB_N] for g in range(MB_GROUPS)]
        cb2 = [_dot_nt(cm[g], jnp.concatenate([bm[g], bm[g]], axis=0)) for g in range(MB_GROUPS)]
        x2 = [jnp.concatenate([jnp.where(first_head, xdt[:, lanes[p]], 0.0),
                               jnp.where(first_head, 0.0, xdt[:, lanes[p]])], axis=0).astype(BF16)
              for p in pairs]
        yield
        st = [state_ref[p] for p in pairs]
        y_inter = [_dot(cm[grp[p]], st[p].astype(BF16)) for p in pairs]
        y_intra = [_dot((cb2[grp[p]] * decay[:, lanes[p]]).astype(BF16), x2[p]) for p in pairs]
        yield
        for p in pairs:
            state_ref[p] = st_decay[:, lanes[p]] * st[p] + _dot_tn(bm[grp[p]], x_dec[:, lanes[p]])
        yield
        y = (jnp.concatenate(y_intra, axis=-1) + jnp.concatenate(y_inter, axis=-1) * carry_w
             + d_ref[...] * xs)

        yz = y * z_ref[rows, :].astype(F32)
        gsz = MB_W // MB_GROUPS
        parts = []
        for g in range(MB_GROUPS):
            o = yz[:, g * gsz:(g + 1) * gsz]
            parts.append(o * lax.rsqrt(jnp.mean(o * o, axis=-1, keepdims=True) + EPS))
        y_ref[rows, :] = (jnp.concatenate(parts, axis=-1) * ng_ref[...]).astype(y_ref.dtype)

    return chunk


_DONE = object()


class _ColumnView:
    def __init__(self, pieces):
        self.pieces = pieces
        self.width = sum(w for _, _, w in pieces)

    def __getitem__(self, idx):
        rows, cols = idx
        lo, hi, _ = cols.indices(self.width)
        out, base = [], 0
        for ref, start, w in self.pieces:
            a, b = max(lo, base), min(hi, base + w)
            if a < b:
                out.append(ref[rows, start + a - base:start + b - base])
            base += w
        return out[0] if len(out) == 1 else jnp.concatenate(out, axis=-1)


def _mixers_kernel(q_ref, g_ref, z_ref, og_ref, m0, m1, m2, m3, m4, m5, sm_ref, smt_ref, dtt_ref,
                   lb_ref, hg_ng_ref, gbr_ref, gbc_ref, ml_ng_ref,
                   dtb_c_ref, alog_c_ref, dtb_r_ref, alog_r_ref, d_ref, mb_ng_ref,
                   y_hg, y_ml, y_mb, hg_state, hg_c, hg_o, ml_caug, ml_m, mb_state):
    ts = y_hg.shape[0]
    mixed = (m0, m1, m2, m3, m4, m5)
    conv = [(m, 0, HALF_TILE) for m in mixed]
    raw = [(m, HALF_TILE, HALF_TILE) for m in mixed]
    qk_view = _ColumnView(conv[0:2])
    xbc_view = _ColumnView(conv[2:6])
    f_view = _ColumnView(raw[0:2])
    i_view = _ColumnView(raw[2:4])
    v_view = _ColumnView(raw[4:6])
    hg, hg_redo, hg_unsafe = _hgrn2_stages(q_ref, f_view, i_view, g_ref, lb_ref, hg_ng_ref, y_hg,
                                           hg_state, hg_c, hg_o)
    ml = _mlstm_stages(qk_view, v_view, og_ref, sm_ref, smt_ref, gbr_ref, gbc_ref, ml_ng_ref,
                       y_ml, ml_caug, ml_m)
    mb = _ssd_stages(z_ref, xbc_view, sm_ref, dtt_ref, dtb_c_ref, alog_c_ref, dtb_r_ref, alog_r_ref,
                     d_ref, mb_ng_ref, y_mb, mb_state)
    per_block = ML_BLOCK // CHUNK
    blocks_per_trip = CHUNK_UNROLL // per_block
    chunks_per_trip = blocks_per_trip * per_block

    def trip_body(t, carry):
        for k in range(blocks_per_trip):
            cb = t * blocks_per_trip + k
            live = [itertools.chain(*[hg(cb * per_block + j, k * per_block + j) for j in range(per_block)]),
                    ml(cb),
                    itertools.chain(*[mb(cb * per_block + j) for j in range(per_block)])]
            while live:
                for g in list(live):
                    if next(g, _DONE) is _DONE:
                        live.remove(g)
        for slot in range(chunks_per_trip):
            pl.when(hg_unsafe[slot])(functools.partial(hg_redo, t * chunks_per_trip + slot, slot))
        return carry

    lax.fori_loop(0, ts // (chunks_per_trip * CHUNK), trip_body, 0)


def _mixers(proj, small, small_t, dt_t, lb, hg_ng, gb_row, gb_col, ml_ng,
            dtb_c, alog_c, dtb_r, alog_r, d_x, mb_ng, batch, seq):
    n = proj.shape[0]
    nt = seq // TS_MIX
    n_pair = MB_HEADS // 2
    w = D_MODEL

    def col(cb):
        return pl.BlockSpec((TS_MIX, w), lambda b, j: (b * nt + j, cb))

    def const(shape):
        return pl.BlockSpec(shape, lambda b, j: tuple(0 for _ in shape))

    tile_specs = [col(T_HG_Q), col(T_HG_G), col(T_MB_Z), col(T_ML_O)] + [col(t) for t in TILES_MIXED]
    gate_specs = [
        pl.BlockSpec((TS_MIX, SMALL_W), lambda b, j: (b * nt + j, 0)),
        pl.BlockSpec((TS_MIX // ML_BLOCK, 2 * ML_HEADS, ML_BLOCK), lambda b, j: (b * nt + j, 0, 0)),
        pl.BlockSpec((TS_MIX // CHUNK, n_pair, 2 * CHUNK), lambda b, j: (b * nt + j, 0, 0)),
    ]
    param_specs = [const((1, w)), const((1, w)),
                   const((1, SMALL_W)), const((2 * ML_HEADS, 1)), const((1, w)),
                   const((1, SMALL_W)), const((1, SMALL_W)),
                   const((n_pair, 2 * CHUNK)), const((n_pair, 2 * CHUNK)),
                   const((1, w)), const((1, w))]
    out = pl.BlockSpec((TS_MIX, w), lambda b, j: (b * nt + j, 0))
    return pl.pallas_call(
        _mixers_kernel,
        grid=(batch, nt),
        in_specs=tile_specs + gate_specs + param_specs,
        out_specs=[out, out, out],
        out_shape=[jax.ShapeDtypeStruct((n, w), BF16)] * 3,
        scratch_shapes=[
            pltpu.VMEM((HG_HEADS, HG_D, HG_D), F32),
            pltpu.VMEM((HG_HEADS, CHUNK, HG_D), F32),
            pltpu.VMEM((CHUNK_UNROLL, CHUNK, w), F32),
            pltpu.VMEM((ML_HEADS, ML_DQK, ML_DV + LANES), F32),
            pltpu.VMEM((SUBLANES, LANES), F32),
            pltpu.VMEM((n_pair, MB_N, 2 * MB_P), F32),
        ],
        compiler_params=_cparams(("parallel", "arbitrary")),
        name="token_mixers",
    )(*([proj] * (4 + len(TILES_MIXED))), small, small_t, dt_t,
      lb, hg_ng, gb_row, gb_col, ml_ng, dtb_c, alog_c, dtb_r, alog_r, d_x, mb_ng)


def _merge_kernel(x_ref, yh_ref, ym_ref, yb_ref, g0_ref, g1_ref, g2_ref,
                  wh_ref, wm_ref, wb_ref, wo_ref, o_ref):
    mixed = g0_ref[...].astype(F32) * _dot(yh_ref[...], wh_ref[...])
    mixed = mixed + g1_ref[...].astype(F32) * _dot(ym_ref[...], wm_ref[...])
    mixed = mixed + g2_ref[...].astype(F32) * _dot(yb_ref[...], wb_ref[...])
    o_ref[...] = x_ref[...] + _dot(mixed.astype(BF16), wo_ref[...])


def _merge(x2, y_hg, y_ml, y_mb, proj, w_hg, w_ml, w_mb, w_out, layer):
    n = x2.shape[0]
    tile = lambda cb: pl.BlockSpec((TM_MERGE, D_MODEL), lambda i: (i, cb))
    wspec = pl.BlockSpec((None, D_MODEL, D_MODEL), lambda i: (layer, 0, 0))
    return pl.pallas_call(
        _merge_kernel,
        grid=(n // TM_MERGE,),
        in_specs=[tile(0), tile(0), tile(0), tile(0), tile(T_GATE0), tile(T_GATE0 + 1), tile(T_GATE0 + 2),
                  wspec, wspec, wspec, wspec],
        out_specs=tile(0),
        out_shape=jax.ShapeDtypeStruct((n, D_MODEL), F32),
        compiler_params=_cparams(("parallel",)),
        name="branch_merge",
    )(x2, y_hg, y_ml, y_mb, proj, proj, proj, w_hg, w_ml, w_mb, w_out)


def _ffn_kernel(x_ref, g_ref, wu_ref, cw_ref, cb_ref, wd_ref, fg_ref,
                o_ref, tail, hbuf_g, hbuf_v, *, final):
    tm = x_ref.shape[0]
    n_ck = D_FF // FF_CHUNK

    @pl.when(pl.program_id(1) == 0)
    def _():
        tail[...] = jnp.zeros_like(tail)

    x = x_ref[...]
    hb = (x * lax.rsqrt(jnp.mean(x * x, axis=-1, keepdims=True) + EPS) * g_ref[...]).astype(BF16)

    def cols(c, half):
        return slice(half * D_FF + c * FF_CHUNK, half * D_FF + (c + 1) * FF_CHUNK)

    def up(c):
        return _dot(hb, wu_ref[:, cols(c, 0)]), _dot(hb, wu_ref[:, cols(c, 1)])

    def conv(u, hbuf, cs):
        out = cb_ref[:, cs] + cw_ref[FFN_CONV - 1:FFN_CONV, cs] * u
        for k in range(FFN_CONV - 1):
            out = out + cw_ref[k:k + 1, cs] * pltpu.roll(u, FFN_CONV - 1 - k, axis=0)
        hbuf[0:CONV_PAD, :] = tail[:, cs]
        hbuf[CONV_PAD:2 * CONV_PAD, :] = u[:CONV_PAD]
        tail[:, cs] = u[tm - CONV_PAD:]
        head = cb_ref[:, cs]
        for k in range(FFN_CONV):
            off = CONV_PAD - (FFN_CONV - 1) + k
            head = head + cw_ref[k:k + 1, cs] * hbuf[off:off + CONV_PAD, :]
        return jnp.concatenate([head, out[CONV_PAD:]], axis=0)

    acc = x
    u_next = up(0)
    for c in range(n_ck):
        u_g, u_v = u_next
        if c + 1 < n_ck:
            u_next = up(c + 1)
        a_g = conv(u_g, hbuf_g, cols(c, 0))
        a_v = conv(u_v, hbuf_v, cols(c, 1))
        acc = acc + _dot((_silu(a_g) * a_v).astype(BF16), wd_ref[cols(c, 0), :])
    if final:
        acc = acc * lax.rsqrt(jnp.mean(acc * acc, axis=-1, keepdims=True) + EPS) * fg_ref[...]
    o_ref[...] = acc


def _ffn(x2, g, w_up, cw, cb, w_down, final_g, layer, batch, seq, final):
    n = x2.shape[0]
    nt = seq // TM_FFN

    def const(shape):
        return pl.BlockSpec(shape, lambda b, j: tuple(0 for _ in shape),
                            pipeline_mode=pl.Buffered(1))

    def layer_weight(shape):
        return pl.BlockSpec((None,) + shape, lambda b, j: (layer,) + tuple(0 for _ in shape),
                            pipeline_mode=pl.Buffered(1))

    tile = pl.BlockSpec((TM_FFN, D_MODEL), lambda b, j: (b * nt + j, 0))
    return pl.pallas_call(
        functools.partial(_ffn_kernel, final=final),
        grid=(batch, nt),
        in_specs=[
            tile, const((1, D_MODEL)),
            layer_weight((D_MODEL, 2 * D_FF)), const((FFN_CONV, 2 * D_FF)), const((1, 2 * D_FF)),
            layer_weight((D_FF, D_MODEL)), const((1, D_MODEL)),
        ],
        out_specs=tile,
        out_shape=jax.ShapeDtypeStruct((n, D_MODEL), F32),
        scratch_shapes=[
            pltpu.VMEM((CONV_PAD, 2 * D_FF), F32),
            pltpu.VMEM((2 * CONV_PAD, FF_CHUNK), F32),
            pltpu.VMEM((2 * CONV_PAD, FF_CHUNK), F32),
        ],
        compiler_params=_cparams(("parallel", "arbitrary")),
        name="conv_gated_mlp",
    )(x2, g, w_up, cw, cb, w_down, final_g)


def kernel(x, norm1_g, w_in, hg_lb_logits, hg_norm_g, ml_conv_w, ml_conv_b, ml_gate_b, ml_norm_g,
           mb_conv_w, mb_conv_b, mb_dt_bias, mb_a_log, mb_d, mb_norm_g, w_br_hg, w_br_ml, w_br_mb,
           w_out, norm2_g, w_up, ffn_conv_w, ffn_conv_b, w_down, final_g):
    batch, seq, _ = x.shape
    n = batch * seq
    assert seq % TS_MIX == 0 and seq % TM_FFN == 0 and seq % TM_PROJ == 0 and n % TM_MERGE == 0
    depth = w_in.shape[0]

    w = D_MODEL
    o_hg_q, o_hg_f, o_hg_i, o_hg_g = 0, w, 2 * w, 3 * w
    o_ml_qk = 4 * w
    o_ml_v = o_ml_qk + 2 * ML_HEADS * ML_DQK
    o_if = o_ml_v + ML_HEADS * ML_DV
    o_ml_o = o_if + 2 * ML_HEADS
    o_mb_z = o_ml_o + ML_HEADS * ML_DV
    o_mb_xbc = o_mb_z + MB_W
    o_dt = o_mb_xbc + MB_CONV_DIM
    o_gate = o_dt + MB_HEADS
    hw = HALF_TILE
    conv_halves = [o_ml_qk + k * hw for k in range(2)] + [o_mb_xbc + k * hw for k in range(4)]
    raw_halves = [o_hg_f, o_hg_f + hw, o_hg_i, o_hg_i + hw, o_ml_v, o_ml_v + hw]
    cols = [(o_hg_q, w), (o_hg_g, w), (o_mb_z, w), (o_ml_o, w), (o_gate, N_GATES * w)]
    for c0, r0 in zip(conv_halves, raw_halves):
        cols += [(c0, hw), (r0, hw)]
    w_in16 = w_in.astype(BF16)
    w_big = jnp.concatenate([w_in16[:, :, a:a + n_] for a, n_ in cols], axis=-1)
    pad = SMALL_W - 2 * ML_HEADS - MB_HEADS
    w_small = jnp.concatenate(
        [w_in16[:, :, o_if:o_ml_o], w_in16[:, :, o_dt:o_gate],
         jnp.zeros((depth, D_MODEL, pad), BF16)], axis=-1)

    lbs = _lbs(hg_lb_logits.astype(F32))
    hg_ng = jnp.tile(hg_norm_g, (1, HG_HEADS))
    ml_ng = jnp.tile(ml_norm_g, (1, ML_HEADS))
    gb_row = jnp.pad(ml_gate_b, ((0, 0), (0, SMALL_W - 2 * ML_HEADS)))
    dtb_c = jnp.pad(mb_dt_bias, ((0, 0), (DT_COL, SMALL_W - DT_COL - MB_HEADS)))
    alog_c = jnp.pad(mb_a_log, ((0, 0), (DT_COL, SMALL_W - DT_COL - MB_HEADS)))
    n_pair = MB_HEADS // 2
    dtb_r = jnp.repeat(mb_dt_bias, CHUNK, axis=-1).reshape(depth, n_pair, 2 * CHUNK)
    alog_r = jnp.repeat(mb_a_log, CHUNK, axis=-1).reshape(depth, n_pair, 2 * CHUNK)
    d_x = jnp.repeat(mb_d, MB_P, axis=-1)

    n_tiles = w_big.shape[-1] // TN_PROJ
    n_mixed = len(TILES_MIXED)
    conv_w = jnp.concatenate([ml_conv_w, mb_conv_w], axis=-1).reshape(depth, CONV_W, n_mixed, hw)
    conv_b = jnp.concatenate([ml_conv_b, mb_conv_b], axis=-1).reshape(depth, 1, n_mixed, hw)
    front = n_tiles - n_mixed
    cw_all = jnp.pad(jnp.swapaxes(conv_w, 1, 2), ((0, 0), (front, 0), (0, 0), (0, 0)))
    cb_all = jnp.pad(jnp.swapaxes(conv_b, 1, 2), ((0, 0), (front, 0), (0, 0), (0, 0)))

    w_hg = w_br_hg.astype(BF16)
    w_ml = w_br_ml.astype(BF16)
    w_mb = w_br_mb.astype(BF16)
    w_o = w_out.astype(BF16)
    w_u = w_up.astype(BF16)
    w_d = w_down.astype(BF16)

    x2 = x.reshape(n, D_MODEL)
    row = lambda a: a.reshape(1, -1)
    for l in range(depth):
        proj, small = _inproj(x2, row(norm1_g[l]), w_big, w_small, cw_all, cb_all, l, seq)
        chunks = small.reshape(n // CHUNK, CHUNK, SMALL_W)
        small_t = jnp.swapaxes(small.reshape(n // ML_BLOCK, ML_BLOCK, SMALL_W)[:, :, :2 * ML_HEADS], 1, 2)
        dt_t = jnp.swapaxes(chunks[:, :, DT_COL:DT_COL + MB_HEADS], 1, 2).reshape(
            n // CHUNK, n_pair, 2 * CHUNK)

        y_hg, y_ml, y_mb = _mixers(
            proj, small, small_t, dt_t, row(lbs[l]), row(hg_ng[l]),
            row(gb_row[l]), ml_gate_b[l].reshape(2 * ML_HEADS, 1), row(ml_ng[l]),
            row(dtb_c[l]), row(alog_c[l]), dtb_r[l], alog_r[l], row(d_x[l]), row(mb_norm_g[l]), batch, seq)
        x2 = _merge(x2, y_hg, y_ml, y_mb, proj, w_hg, w_ml, w_mb, w_o, l)
        x2 = _ffn(x2, row(norm2_g[l]), w_u, ffn_conv_w[l], row(ffn_conv_b[l]), w_d, row(final_g),
                  l, batch, seq, final=(l == depth - 1))
    return x2.reshape(batch, seq, D_MODEL)
```

```python
import functools
import itertools
import math

import jax
import jax.numpy as jnp
from jax import lax
from jax.experimental import pallas as pl
from jax.experimental.pallas import tpu as pltpu

F32 = jnp.float32
BF16 = jnp.bfloat16

D_MODEL = 1024
CHUNK = 64
CHUNK_LOG2 = 6
SUB = 8
FACTORED_DECAY_LIMIT = 112.0
SUBLANES = 8
LANES = 128
LOG2E = math.log2(math.e)
EPS = 1e-6
NEG_BIG = -1e30
HG_HEADS = 8
HG_D = 128
ML_HEADS = 4
ML_DQK = 128
ML_DV = 256
MB_HEADS = 16
MB_P = 64
MB_P_LOG2 = 6
MB_GROUPS = 4
MB_N = 128
MB_W = MB_HEADS * MB_P
MB_CONV_DIM = MB_W + 2 * MB_GROUPS * MB_N
D_FF = 2816
FFN_CONV = 3
FF_CHUNK = 1408
SMALL_W = 128
DT_COL = 8
CONV_PAD = 8

VMEM_LIMIT = 56 * 1024 * 1024

TM_PROJ = 2048
TN_PROJ = 1024
EPI_ROWS = 256
TILES_SILU = (0, 1, 2)
TILES_SIGMOID = (3, 4, 5, 6)
TILES_MIXED = (7, 8, 9, 10, 11, 12)
T_HG_Q, T_HG_G, T_MB_Z, T_ML_O, T_GATE0 = 0, 1, 2, 3, 4
N_GATES = 3
HALF_TILE = TN_PROJ // 2
CONV_W = 4
TS_MIX = 512
CHUNK_UNROLL = 8
ML_BLOCK = 2 * CHUNK
TM_MERGE = 1024
TM_FFN = 512


def _silu(x):
    return x * jax.nn.sigmoid(x)


def _softplus(x):
    return jnp.maximum(x, 0.0) + jnp.log(1.0 + jnp.exp(-jnp.abs(x)))


def _log_sigmoid(x):
    return jnp.minimum(x, 0.0) - jnp.log(1.0 + jnp.exp(-jnp.abs(x)))


def _dot(a, b):
    return jnp.dot(a, b, preferred_element_type=F32)


def _dot_nt(a, b):
    return lax.dot_general(a, b, (((1,), (1,)), ((), ())), preferred_element_type=F32)


def _dot_tn(a, b):
    return lax.dot_general(a, b, (((0,), (0,)), ((), ())), preferred_element_type=F32)


def _split3(a):
    hi = a.astype(BF16)
    r = a - hi.astype(F32)
    mid = r.astype(BF16)
    lo = (r - mid.astype(F32)).astype(BF16)
    return hi, mid, lo


def _sel_left(sel, a):
    return _dot(jnp.concatenate([sel] * 3, axis=1), jnp.concatenate(_split3(a), axis=0))


def _sel_right(a, sel):
    return _dot(jnp.concatenate(_split3(a), axis=1), jnp.concatenate([sel] * 3, axis=0))


def _tri_lower(n):
    r = lax.broadcasted_iota(jnp.int32, (n, n), 0)
    c = lax.broadcasted_iota(jnp.int32, (n, n), 1)
    return r >= c


def _bcast_row(ref, h, r):
    return ref[h, pl.ds(r, SUBLANES, stride=0), :]


def _mask_bf16(m):
    return jnp.where(m, 1.0, 0.0).astype(BF16)


def _cparams(sem):
    return pltpu.CompilerParams(dimension_semantics=sem, vmem_limit_bytes=VMEM_LIMIT)


def _lbs_kernel(lg_ref, o_ref):
    lg = lg_ref[...]
    mx = jnp.max(lg, axis=0, keepdims=True)
    e = jnp.exp(lg - mx)
    p = e / jnp.sum(e, axis=0, keepdims=True)
    acc = jnp.zeros_like(p[0:1])
    rows = []
    for l in range(lg.shape[0]):
        acc = acc + p[l:l + 1]
        rows.append(acc - p[0:1])
    o_ref[...] = jnp.concatenate(rows, axis=0)


def _lbs(logits):
    return pl.pallas_call(
        _lbs_kernel,
        out_shape=jax.ShapeDtypeStruct(logits.shape, F32),
        name="hgrn2_lower_bounds",
    )(logits)


def _any_tile(j, tiles):
    hit = j == tiles[0]
    for t in tiles[1:]:
        hit = jnp.logical_or(hit, j == t)
    return hit


def _inproj_kernel(x_ref, g_ref, w_ref, ws_ref, cw_ref, cb_ref, o_ref, os_ref, h_ref, tail_ref, hbuf,
                   *, tiles_per_seq):
    i = pl.program_id(0)
    j = pl.program_id(1)
    n_blk = TM_PROJ // EPI_ROWS

    @pl.when(jnp.logical_and(i == 0, j == 0))
    def _():
        tail_ref[...] = jnp.zeros_like(tail_ref)

    @pl.when(j == 0)
    def _():
        x = x_ref[...]
        ms = jnp.mean(x * x, axis=-1, keepdims=True)
        hb = (x * lax.rsqrt(ms + EPS) * g_ref[...]).astype(BF16)
        h_ref[...] = hb
        os_ref[...] = _dot(hb, ws_ref[...])

    def pointwise(fn):
        ys = [_dot(h_ref[r * EPI_ROWS:(r + 1) * EPI_ROWS, :], w_ref[...]) for r in range(n_blk)]
        for r in range(n_blk):
            o_ref[r * EPI_ROWS:(r + 1) * EPI_ROWS, :] = fn(ys[r]).astype(o_ref.dtype)

    @pl.when(_any_tile(j, TILES_SILU))
    def _():
        pointwise(_silu)

    @pl.when(_any_tile(j, TILES_SIGMOID))
    def _():
        pointwise(jax.nn.sigmoid)

    @pl.when(j >= TILES_MIXED[0])
    def _():
        slot = j - TILES_MIXED[0]
        prev = jnp.where(i % tiles_per_seq == 0, 0.0, tail_ref[slot])
        cw = cw_ref[0]
        cb = cb_ref[0]
        ys = [_dot(h_ref[r * EPI_ROWS:(r + 1) * EPI_ROWS, :], w_ref[...]) for r in range(n_blk)]
        for r in range(n_blk):
            yb = ys[r][:, :HALF_TILE]
            out = cb + cw[CONV_W - 1:CONV_W] * yb
            for k in range(CONV_W - 1):
                out = out + cw[k:k + 1] * pltpu.roll(yb, CONV_W - 1 - k, axis=0)
            hbuf[0:CONV_PAD, :] = prev
            hbuf[CONV_PAD:2 * CONV_PAD, :] = yb[:CONV_PAD]
            head = cb
            for k in range(CONV_W):
                off = CONV_PAD - (CONV_W - 1) + k
                head = head + cw[k:k + 1] * hbuf[off:off + CONV_PAD, :]
            prev = yb[EPI_ROWS - CONV_PAD:]
            act = _silu(jnp.concatenate([head, out[CONV_PAD:]], axis=0))
            o_ref[r * EPI_ROWS:(r + 1) * EPI_ROWS, :] = jnp.concatenate(
                [act, ys[r][:, HALF_TILE:]], axis=-1).astype(o_ref.dtype)
        tail_ref[slot] = prev


def _inproj(x2, g, w_big, w_small, cw_all, cb_all, layer, seq):
    n = x2.shape[0]
    nb = w_big.shape[-1]
    n_tiles = nb // TN_PROJ
    assert n_tiles == len(TILES_SILU + TILES_SIGMOID + TILES_MIXED)
    return pl.pallas_call(
        functools.partial(_inproj_kernel, tiles_per_seq=seq // TM_PROJ),
        grid=(n // TM_PROJ, n_tiles),
        in_specs=[
            pl.BlockSpec((TM_PROJ, D_MODEL), lambda i, j: (i, 0)),
            pl.BlockSpec((1, D_MODEL), lambda i, j: (0, 0)),
            pl.BlockSpec((None, D_MODEL, TN_PROJ), lambda i, j: (layer, 0, j)),
            pl.BlockSpec((None, D_MODEL, SMALL_W), lambda i, j: (layer, 0, 0)),
            pl.BlockSpec((None, 1, CONV_W, HALF_TILE), lambda i, j: (layer, j, 0, 0)),
            pl.BlockSpec((None, 1, 1, HALF_TILE), lambda i, j: (layer, j, 0, 0)),
        ],
        out_specs=[
            pl.BlockSpec((TM_PROJ, TN_PROJ), lambda i, j: (i, j)),
            pl.BlockSpec((TM_PROJ, SMALL_W), lambda i, j: (i, 0)),
        ],
        out_shape=[
            jax.ShapeDtypeStruct((n, nb), BF16),
            jax.ShapeDtypeStruct((n, SMALL_W), F32),
        ],
        scratch_shapes=[
            pltpu.VMEM((TM_PROJ, D_MODEL), BF16),
            pltpu.VMEM((len(TILES_MIXED), CONV_PAD, HALF_TILE), F32),
            pltpu.VMEM((2 * CONV_PAD, HALF_TILE), F32),
        ],
        compiler_params=_cparams(("arbitrary", "arbitrary")),
        name="in_projection",
    )(x2, g, w_big, w_small, cw_all, cb_all)


def _hgrn2_stages(q_ref, f_ref, i_ref, g_ref, lb_ref, ng_ref, y_ref, state_ref, c_s, o_s):
    @pl.when(pl.program_id(1) == 0)
    def _():
        state_ref[...] = jnp.zeros_like(state_ref)

    lb = lb_ref[...]
    tri = _mask_bf16(_tri_lower(CHUNK))
    n_sub = CHUNK // SUB
    lane = lax.broadcasted_iota(jnp.int32, (SUBLANES, HG_D), 1)
    row = lax.broadcasted_iota(jnp.int32, (SUBLANES, HG_D), 0)
    causal = (lax.broadcasted_iota(jnp.int32, (CHUNK, HG_D), 1)
              <= lax.broadcasted_iota(jnp.int32, (CHUNK, HG_D), 0))

    heads = range(HG_HEADS)
    hsl = [slice(h * HG_D, (h + 1) * HG_D) for h in heads]
    zero_tail = jnp.zeros((HG_D - CHUNK, HG_D), BF16)
    unsafe = {}

    def decays(rows):
        sig = jax.nn.sigmoid(f_ref[rows, :].astype(F32))
        fgate = lb + (1.0 - lb) * sig
        b = _sel_left(tri, jnp.log(fgate) * LOG2E)
        return b, b - jnp.log(jnp.maximum(1.0 - fgate, 0.0)) * LOG2E

    def finish(rows, v16, o_inter, a_heads):
        parts = []
        for h in heads:
            v_pad = jnp.concatenate([v16[:, hsl[h]], zero_tail], axis=0)
            o = o_inter[h] + _dot(a_heads[h], v_pad)
            parts.append(o * lax.rsqrt(jnp.mean(o * o, axis=-1, keepdims=True) + EPS))
        on = jnp.concatenate(parts, axis=-1) * ng_ref[...]
        y_ref[rows, :] = (on * g_ref[rows, :].astype(F32)).astype(y_ref.dtype)

    def chunk(c, slot):
        rows = pl.ds(pl.multiple_of(c * CHUNK, CHUNK), CHUNK)
        b, cc = decays(rows)
        q = q_ref[rows, :].astype(F32)
        v16 = i_ref[rows, :]
        yield

        st = [state_ref[h] for h in heads]
        q_dec = (q * jnp.exp2(b)).astype(BF16)
        b_last = b[CHUNK - 1:CHUNK, :]
        k_dec = jnp.exp2(b_last - cc).astype(BF16)
        st_decay = jnp.exp2(b_last)
        o_inter = [_dot_nt(q_dec[:, hsl[h]], st[h].astype(BF16)) for h in heads]
        for h in heads:
            state_ref[h] = st_decay[:, hsl[h]] * st[h] + _dot_tn(v16[:, hsl[h]], k_dec[:, hsl[h]])
        o_s[slot] = jnp.concatenate(o_inter, axis=-1)
        unsafe[slot] = jnp.logical_not(jnp.min(b_last) >= -FACTORED_DECAY_LIMIT)
        yield

        k_all = jnp.exp2(-cc).astype(BF16)
        a_heads = []
        for h in heads:
            k_pad = jnp.concatenate([k_all[:, hsl[h]], zero_tail], axis=0)
            a_heads.append(jnp.where(causal, _dot_nt(q_dec[:, hsl[h]], k_pad), 0.0).astype(BF16))
        yield
        finish(rows, v16, o_inter, a_heads)

    def redo(c, slot):
        rows = pl.ds(pl.multiple_of(c * CHUNK, CHUNK), CHUNK)
        b, cc = decays(rows)
        for h in heads:
            c_s[h] = cc[:, hsl[h]]
        q = q_ref[rows, :].astype(F32)
        ones_rhs = jnp.ones((HG_D, HG_D), BF16)
        zs = []
        for i in range(n_sub):
            lo = i * SUB
            for s in range(SUB):
                r0 = lo + (s // SUBLANES) * SUBLANES
                c_row = jnp.concatenate([_bcast_row(c_s, h, lo + s) for h in heads], axis=-1)
                c_row = jnp.concatenate([c_row] * ((lo + SUB - r0) // SUBLANES), axis=0)
                zs.append(q[r0:lo + SUB] * jnp.exp2(b[r0:lo + SUB] - c_row))
        z_rows = sum(z.shape[0] for z in zs)
        z_all = jnp.concatenate([z[:, hsl[h]] for h in heads for z in zs], axis=0)
        r = _dot(z_all.astype(BF16), ones_rhs)

        a_off = []
        for i in range(1, n_sub):
            lo = i * SUB
            bref = b[lo - 1:lo, :]
            q_i = (q[lo:lo + SUB] * jnp.exp2(b[lo:lo + SUB] - bref)).astype(BF16)
            k_i = jnp.exp2(bref - cc[:lo]).astype(BF16)
            zero_rows = jnp.zeros((HG_D - lo, HG_D), BF16)
            a_off.append([_dot_nt(q_i[:, hsl[h]], jnp.concatenate([k_i[:, hsl[h]], zero_rows], axis=0))
                          for h in heads])

        a_heads = []
        for h in heads:
            a_rows = []
            off = h * z_rows
            for i in range(n_sub):
                lo = i * SUB
                a_i = a_off[i - 1][h] if i > 0 else jnp.zeros((SUB, HG_D), F32)
                tiles = [a_i[j * SUBLANES:(j + 1) * SUBLANES] for j in range(SUB // SUBLANES)]
                for s in range(SUB):
                    for j in range(s // SUBLANES, SUB // SUBLANES):
                        tiles[j] = jnp.where(lane == lo + s, r[off:off + SUBLANES], tiles[j])
                        off += SUBLANES
                for j in range(SUB // SUBLANES):
                    a_rows.append(jnp.where(lane - lo <= row + j * SUBLANES, tiles[j], 0.0))
            a_heads.append(jnp.concatenate(a_rows, axis=0).astype(BF16))
        o_all = o_s[slot]
        finish(rows, i_ref[rows, :], [o_all[:, hsl[h]] for h in heads], a_heads)

    return chunk, redo, unsafe


def _mlstm_stages(qk_ref, v_ref, og_ref, sm_ref, smt_ref, gbr_ref, gbc_ref, ng_ref,
                  y_ref, caug_ref, m_ref):
    half = ML_HEADS * ML_DQK
    k_scale = ML_DQK ** -0.5

    @pl.when(pl.program_id(1) == 0)
    def _():
        caug_ref[...] = jnp.zeros_like(caug_ref)
        m_ref[...] = jnp.zeros_like(m_ref)

    tri_b = _tri_lower(ML_BLOCK)
    tri_l = _mask_bf16(tri_b)
    tri_u = _mask_bf16(lax.broadcasted_iota(jnp.int32, (ML_BLOCK, ML_BLOCK), 0)
                       <= lax.broadcasted_iota(jnp.int32, (ML_BLOCK, ML_BLOCK), 1))
    ones_col = _mask_bf16(lax.broadcasted_iota(jnp.int32, (ML_BLOCK, LANES), 1) == 0)

    def chunk(c):
        rows = pl.ds(pl.multiple_of(c * ML_BLOCK, ML_BLOCK), ML_BLOCK)
        pre_c = sm_ref[rows, :] + gbr_ref[...]
        pre_r = smt_ref[c] + gbc_ref[...]
        cum_c = _sel_left(tri_l, _log_sigmoid(pre_c))
        cum_r = _sel_right(_log_sigmoid(pre_r), tri_u)
        yield
        heads = range(ML_HEADS)
        m_all = m_ref[...]
        m_old = [m_all[h:h + 1, 0:1] for h in heads]
        b_col = [cum_c[:, ML_HEADS + h:ML_HEADS + h + 1] for h in heads]
        i_col = [pre_c[:, h:h + 1] for h in heads]
        log_d = [jnp.where(tri_b, b_col[h] - cum_r[ML_HEADS + h:ML_HEADS + h + 1, :] + pre_r[h:h + 1, :],
                           NEG_BIG) for h in heads]
        log_inter = [b_col[h] + m_old[h] for h in heads]
        m_t = [jnp.maximum(jnp.max(log_d[h], axis=-1, keepdims=True), log_inter[h]) for h in heads]
        b_last = [b_col[h][ML_BLOCK - 1:ML_BLOCK, :] for h in heads]
        log_w = [b_last[h] - b_col[h] + i_col[h] for h in heads]
        m_new = [jnp.maximum(b_last[h] + m_old[h], jnp.max(log_w[h], axis=0, keepdims=True)) for h in heads]
        m_ref[...] = jnp.concatenate(
            [jnp.broadcast_to(m_new[h], (1, m_ref.shape[1])) for h in heads] + [m_all[ML_HEADS:]], axis=0)
        yield

        qb = [qk_ref[rows, h * ML_DQK:(h + 1) * ML_DQK] for h in heads]
        kb = [qk_ref[rows, half + h * ML_DQK:half + (h + 1) * ML_DQK] for h in heads]
        v_aug = [jnp.concatenate([v_ref[rows, h * ML_DV:(h + 1) * ML_DV], ones_col], axis=-1)
                 for h in heads]
        qk = [_dot_nt(qb[h], kb[h]) for h in heads]
        c_aug = [caug_ref[h] for h in heads]
        inter = [_dot(qb[h], c_aug[h].astype(BF16)) * jnp.exp(log_inter[h] - m_t[h]) for h in heads]
        yield
        s = [(qk[h] * (jnp.exp(log_d[h] - m_t[h]) * k_scale)).astype(BF16) for h in heads]
        num = [_dot(s[h], v_aug[h]) + inter[h] for h in heads]
        yield
        kw = [(kb[h].astype(F32) * (jnp.exp(log_w[h] - m_new[h]) * k_scale)).astype(BF16) for h in heads]
        for h in heads:
            caug_ref[h] = jnp.exp(b_last[h] + m_old[h] - m_new[h]) * c_aug[h] + _dot_tn(kw[h], v_aug[h])
        yield

        parts = []
        for h in heads:
            denom = jnp.maximum(jnp.abs(num[h][:, ML_DV:ML_DV + 1]), jnp.exp(-m_t[h]))
            o = num[h][:, :ML_DV] / denom
            parts.append(o * lax.rsqrt(jnp.mean(o * o, axis=-1, keepdims=True) + EPS))
        on = jnp.concatenate(parts, axis=-1) * ng_ref[...]
        y_ref[rows, :] = (on * og_ref[rows, :].astype(F32)).astype(y_ref.dtype)

    return chunk


def _ssd_stages(z_ref, xbc_ref, sm_ref, dtt_ref, dtb_c_ref, alog_c_ref,
                dtb_r_ref, alog_r_ref, d_ref, ng_ref, y_ref, state_ref):
    n_pair = MB_HEADS // 2
    pair_w = 2 * MB_P
    gw = MB_GROUPS * MB_N

    @pl.when(pl.program_id(1) == 0)
    def _():
        state_ref[...] = jnp.zeros_like(state_ref)

    tri_l = _mask_bf16(_tri_lower(CHUNK))
    sel_x = _mask_bf16(lax.broadcasted_iota(jnp.int32, (SMALL_W, MB_W), 0) - DT_COL
                       == jnp.right_shift(lax.broadcasted_iota(jnp.int32, (SMALL_W, MB_W), 1), MB_P_LOG2))
    ur = lax.broadcasted_iota(jnp.int32, (pair_w, pair_w), 0)
    uc = lax.broadcasted_iota(jnp.int32, (pair_w, pair_w), 1)
    same_half = jnp.right_shift(ur, CHUNK_LOG2) == jnp.right_shift(uc, CHUNK_LOG2)
    tri_u2 = _mask_bf16(same_half & (ur <= uc))
    causal = (lax.broadcasted_iota(jnp.int32, (CHUNK, MB_W), 0)
              >= jnp.bitwise_and(lax.broadcasted_iota(jnp.int32, (CHUNK, MB_W), 1), CHUNK - 1))
    first_head = lax.broadcasted_iota(jnp.int32, (CHUNK, pair_w), 1) < MB_P
    a_c = -jnp.exp(alog_c_ref[...])
    a_r = -jnp.exp(alog_r_ref[...])

    def chunk(c):
        rows = pl.ds(pl.multiple_of(c * CHUNK, CHUNK), CHUNK)
        dt_c = _softplus(sm_ref[rows, :] + dtb_c_ref[...])
        cum_c = _sel_left(tri_l, dt_c * a_c)
        dt_x = _sel_right(dt_c, sel_x)
        cum_x = _sel_right(cum_c, sel_x)
        dt_r = _softplus(dtt_ref[c] + dtb_r_ref[...])
        cum_r = _sel_right(dt_r * a_r, tri_u2)
        yield
        pairs = range(n_pair)
        grp = [(2 * p) // (MB_HEADS // MB_GROUPS) for p in pairs]
        lanes = [slice(p * pair_w, (p + 1) * pair_w) for p in pairs]
        xs = xbc_ref[rows, :MB_W].astype(F32)
        xdt = xs * dt_x
        cum_row = jnp.concatenate([cum_r[p:p + 1, :] for p in pairs], axis=-1)
        decay = jnp.exp(jnp.where(causal, cum_x - cum_row, NEG_BIG))
        cum_last = cum_x[CHUNK - 1:CHUNK, :]
        x_dec = (xdt * jnp.exp(cum_last - cum_x)).astype(BF16)
        st_decay = jnp.exp(cum_last)
        carry_w = jnp.exp(cum_x)
        yield
        bm = [xbc_ref[rows, MB_W + g * MB_N:MB_W + (g + 1) * MB_N] for g in range(MB_GROUPS)]
        cm = [xbc_ref[rows, MB_W + gw + g * MB_N:MB_W + gw + (g + 1) * MB_N] for g in range(MB_GROUPS)]
        cb2 = [_dot_nt(cm[g], jnp.concatenate([bm[g], bm[g]], axis=0)) for g in range(MB_GROUPS)]
        x2 = [jnp.concatenate([jnp.where(first_head, xdt[:, lanes[p]], 0.0),
                               jnp.where(first_head, 0.0, xdt[:, lanes[p]])], axis=0).astype(BF16)
              for p in pairs]
        yield
        st = [state_ref[p] for p in pairs]
        y_inter = [_dot(cm[grp[p]], st[p].astype(BF16)) for p in pairs]
        y_intra = [_dot((cb2[grp[p]] * decay[:, lanes[p]]).astype(BF16), x2[p]) for p in pairs]
        yield
        for p in pairs:
            state_ref[p] = st_decay[:, lanes[p]] * st[p] + _dot_tn(bm[grp[p]], x_dec[:, lanes[p]])
        yield
        y = (jnp.concatenate(y_intra, axis=-1) + jnp.concatenate(y_inter, axis=-1) * carry_w
             + d_ref[...] * xs)

        yz = y * z_ref[rows, :].astype(F32)
        gsz = MB_W // MB_GROUPS
        parts = []
        for g in range(MB_GROUPS):
            o = yz[:, g * gsz:(g + 1) * gsz]
            parts.append(o * lax.rsqrt(jnp.mean(o * o, axis=-1, keepdims=True) + EPS))
        y_ref[rows, :] = (jnp.concatenate(parts, axis=-1) * ng_ref[...]).astype(y_ref.dtype)

    return chunk


_DONE = object()


class _ColumnView:
    def __init__(self, pieces):
        self.pieces = pieces
        self.width = sum(w for _, _, w in pieces)

    def __getitem__(self, idx):
        rows, cols = idx
        lo, hi, _ = cols.indices(self.width)
        out, base = [], 0
        for ref, start, w in self.pieces:
            a, b = max(lo, base), min(hi, base + w)
            if a < b:
                out.append(ref[rows, start + a - base:start + b - base])
            base += w
        return out[0] if len(out) == 1 else jnp.concatenate(out, axis=-1)


def _mixers_kernel(q_ref, g_ref, z_ref, og_ref, m0, m1, m2, m3, m4, m5, sm_ref, smt_ref, dtt_ref,
                   lb_ref, hg_ng_ref, gbr_ref, gbc_ref, ml_ng_ref,
                   dtb_c_ref, alog_c_ref, dtb_r_ref, alog_r_ref, d_ref, mb_ng_ref,
                   y_hg, y_ml, y_mb, hg_state, hg_c, hg_o, ml_caug, ml_m, mb_state):
    ts = y_hg.shape[0]
    mixed = (m0, m1, m2, m3, m4, m5)
    conv = [(m, 0, HALF_TILE) for m in mixed]
    raw = [(m, HALF_TILE, HALF_TILE) for m in mixed]
    qk_view = _ColumnView(conv[0:2])
    xbc_view = _ColumnView(conv[2:6])
    f_view = _ColumnView(raw[0:2])
    i_view = _ColumnView(raw[2:4])
    v_view = _ColumnView(raw[4:6])
    hg, hg_redo, hg_unsafe = _hgrn2_stages(q_ref, f_view, i_view, g_ref, lb_ref, hg_ng_ref, y_hg,
                                           hg_state, hg_c, hg_o)
    ml = _mlstm_stages(qk_view, v_view, og_ref, sm_ref, smt_ref, gbr_ref, gbc_ref, ml_ng_ref,
                       y_ml, ml_caug, ml_m)
    mb = _ssd_stages(z_ref, xbc_view, sm_ref, dtt_ref, dtb_c_ref, alog_c_ref, dtb_r_ref, alog_r_ref,
                     d_ref, mb_ng_ref, y_mb, mb_state)
    per_block = ML_BLOCK // CHUNK
    blocks_per_trip = CHUNK_UNROLL // per_block
    chunks_per_trip = blocks_per_trip * per_block

    def trip_body(t, carry):
        for k in range(blocks_per_trip):
            cb = t * blocks_per_trip + k
            live = [itertools.chain(*[hg(cb * per_block + j, k * per_block + j) for j in range(per_block)]),
                    ml(cb),
                    itertools.chain(*[mb(cb * per_block + j) for j in range(per_block)])]
            while live:
                for g in list(live):
                    if next(g, _DONE) is _DONE:
                        live.remove(g)
        for slot in range(chunks_per_trip):
            pl.when(hg_unsafe[slot])(functools.partial(hg_redo, t * chunks_per_trip + slot, slot))
        return carry

    lax.fori_loop(0, ts // (chunks_per_trip * CHUNK), trip_body, 0)


def _mixers(proj, small, small_t, dt_t, lb, hg_ng, gb_row, gb_col, ml_ng,
            dtb_c, alog_c, dtb_r, alog_r, d_x, mb_ng, batch, seq):
    n = proj.shape[0]
    nt = seq // TS_MIX
    n_pair = MB_HEADS // 2
    w = D_MODEL

    def col(cb):
        return pl.BlockSpec((TS_MIX, w), lambda b, j: (b * nt + j, cb))

    def const(shape):
        return pl.BlockSpec(shape, lambda b, j: tuple(0 for _ in shape))

    tile_specs = [col(T_HG_Q), col(T_HG_G), col(T_MB_Z), col(T_ML_O)] + [col(t) for t in TILES_MIXED]
    gate_specs = [
        pl.BlockSpec((TS_MIX, SMALL_W), lambda b, j: (b * nt + j, 0)),
        pl.BlockSpec((TS_MIX // ML_BLOCK, 2 * ML_HEADS, ML_BLOCK), lambda b, j: (b * nt + j, 0, 0)),
        pl.BlockSpec((TS_MIX // CHUNK, n_pair, 2 * CHUNK), lambda b, j: (b * nt + j, 0, 0)),
    ]
    param_specs = [const((1, w)), const((1, w)),
                   const((1, SMALL_W)), const((2 * ML_HEADS, 1)), const((1, w)),
                   const((1, SMALL_W)), const((1, SMALL_W)),
                   const((n_pair, 2 * CHUNK)), const((n_pair, 2 * CHUNK)),
                   const((1, w)), const((1, w))]
    out = pl.BlockSpec((TS_MIX, w), lambda b, j: (b * nt + j, 0))
    return pl.pallas_call(
        _mixers_kernel,
        grid=(batch, nt),
        in_specs=tile_specs + gate_specs + param_specs,
        out_specs=[out, out, out],
        out_shape=[jax.ShapeDtypeStruct((n, w), BF16)] * 3,
        scratch_shapes=[
            pltpu.VMEM((HG_HEADS, HG_D, HG_D), F32),
            pltpu.VMEM((HG_HEADS, CHUNK, HG_D), F32),
            pltpu.VMEM((CHUNK_UNROLL, CHUNK, w), F32),
            pltpu.VMEM((ML_HEADS, ML_DQK, ML_DV + LANES), F32),
            pltpu.VMEM((SUBLANES, LANES), F32),
            pltpu.VMEM((n_pair, MB_N, 2 * MB_P), F32),
        ],
        compiler_params=_cparams(("parallel", "arbitrary")),
        name="token_mixers",
    )(*([proj] * (4 + len(TILES_MIXED))), small, small_t, dt_t,
      lb, hg_ng, gb_row, gb_col, ml_ng, dtb_c, alog_c, dtb_r, alog_r, d_x, mb_ng)


def _merge_kernel(x_ref, yh_ref, ym_ref, yb_ref, g0_ref, g1_ref, g2_ref,
                  wh_ref, wm_ref, wb_ref, wo_ref, o_ref):
    mixed = g0_ref[...].astype(F32) * _dot(yh_ref[...], wh_ref[...])
    mixed = mixed + g1_ref[...].astype(F32) * _dot(ym_ref[...], wm_ref[...])
    mixed = mixed + g2_ref[...].astype(F32) * _dot(yb_ref[...], wb_ref[...])
    o_ref[...] = x_ref[...] + _dot(mixed.astype(BF16), wo_ref[...])


def _merge(x2, y_hg, y_ml, y_mb, proj, w_hg, w_ml, w_mb, w_out, layer):
    n = x2.shape[0]
    tile = lambda cb: pl.BlockSpec((TM_MERGE, D_MODEL), lambda i: (i, cb))
    wspec = pl.BlockSpec((None, D_MODEL, D_MODEL), lambda i: (layer, 0, 0))
    return pl.pallas_call(
        _merge_kernel,
        grid=(n // TM_MERGE,),
        in_specs=[tile(0), tile(0), tile(0), tile(0), tile(T_GATE0), tile(T_GATE0 + 1), tile(T_GATE0 + 2),
                  wspec, wspec, wspec, wspec],
        out_specs=tile(0),
        out_shape=jax.ShapeDtypeStruct((n, D_MODEL), F32),
        compiler_params=_cparams(("parallel",)),
        name="branch_merge",
    )(x2, y_hg, y_ml, y_mb, proj, proj, proj, w_hg, w_ml, w_mb, w_out)


def _ffn_kernel(x_ref, g_ref, wu_ref, cw_ref, cb_ref, wd_ref, fg_ref,
                o_ref, tail, hbuf_g, hbuf_v, *, final):
    tm = x_ref.shape[0]
    n_ck = D_FF // FF_CHUNK

    @pl.when(pl.program_id(1) == 0)
    def _():
        tail[...] = jnp.zeros_like(tail)

    x = x_ref[...]
    hb = (x * lax.rsqrt(jnp.mean(x * x, axis=-1, keepdims=True) + EPS) * g_ref[...]).astype(BF16)

    def cols(c, half):
        return slice(half * D_FF + c * FF_CHUNK, half * D_FF + (c + 1) * FF_CHUNK)

    def up(c):
        return _dot(hb, wu_ref[:, cols(c, 0)]), _dot(hb, wu_ref[:, cols(c, 1)])

    def conv(u, hbuf, cs):
        out = cb_ref[:, cs] + cw_ref[FFN_CONV - 1:FFN_CONV, cs] * u
        for k in range(FFN_CONV - 1):
            out = out + cw_ref[k:k + 1, cs] * pltpu.roll(u, FFN_CONV - 1 - k, axis=0)
        hbuf[0:CONV_PAD, :] = tail[:, cs]
        hbuf[CONV_PAD:2 * CONV_PAD, :] = u[:CONV_PAD]
        tail[:, cs] = u[tm - CONV_PAD:]
        head = cb_ref[:, cs]
        for k in range(FFN_CONV):
            off = CONV_PAD - (FFN_CONV - 1) + k
            head = head + cw_ref[k:k + 1, cs] * hbuf[off:off + CONV_PAD, :]
        return jnp.concatenate([head, out[CONV_PAD:]], axis=0)

    acc = x
    u_next = up(0)
    for c in range(n_ck):
        u_g, u_v = u_next
        if c + 1 < n_ck:
            u_next = up(c + 1)
        a_g = conv(u_g, hbuf_g, cols(c, 0))
        a_v = conv(u_v, hbuf_v, cols(c, 1))
        acc = acc + _dot((_silu(a_g) * a_v).astype(BF16), wd_ref[cols(c, 0), :])
    if final:
        acc = acc * lax.rsqrt(jnp.mean(acc * acc, axis=-1, keepdims=True) + EPS) * fg_ref[...]
    o_ref[...] = acc


def _ffn(x2, g, w_up, cw, cb, w_down, final_g, layer, batch, seq, final):
    n = x2.shape[0]
    nt = seq // TM_FFN

    def const(shape):
        return pl.BlockSpec(shape, lambda b, j: tuple(0 for _ in shape),
                            pipeline_mode=pl.Buffered(1))

    def layer_weight(shape):
        return pl.BlockSpec((None,) + shape, lambda b, j: (layer,) + tuple(0 for _ in shape),
                            pipeline_mode=pl.Buffered(1))

    tile = pl.BlockSpec((TM_FFN, D_MODEL), lambda b, j: (b * nt + j, 0))
    return pl.pallas_call(
        functools.partial(_ffn_kernel, final=final),
        grid=(batch, nt),
        in_specs=[
            tile, const((1, D_MODEL)),
            layer_weight((D_MODEL, 2 * D_FF)), const((FFN_CONV, 2 * D_FF)), const((1, 2 * D_FF)),
            layer_weight((D_FF, D_MODEL)), const((1, D_MODEL)),
        ],
        out_specs=tile,
        out_shape=jax.ShapeDtypeStruct((n, D_MODEL), F32),
        scratch_shapes=[
            pltpu.VMEM((CONV_PAD, 2 * D_FF), F32),
            pltpu.VMEM((2 * CONV_PAD, FF_CHUNK), F32),
            pltpu.VMEM((2 * CONV_PAD, FF_CHUNK), F32),
        ],
        compiler_params=_cparams(("parallel", "arbitrary")),
        name="conv_gated_mlp",
    )(x2, g, w_up, cw, cb, w_down, final_g)


def kernel(x, norm1_g, w_in, hg_lb_logits, hg_norm_g, ml_conv_w, ml_conv_b, ml_gate_b, ml_norm_g,
           mb_conv_w, mb_conv_b, mb_dt_bias, mb_a_log, mb_d, mb_norm_g, w_br_hg, w_br_ml, w_br_mb,
           w_out, norm2_g, w_up, ffn_conv_w, ffn_conv_b, w_down, final_g):
    batch, seq, _ = x.shape
    n = batch * seq
    assert seq % TS_MIX == 0 and seq % TM_FFN == 0 and seq % TM_PROJ == 0 and n % TM_MERGE == 0
    depth = w_in.shape[0]

    w = D_MODEL
    o_hg_q, o_hg_f, o_hg_i, o_hg_g = 0, w, 2 * w, 3 * w
    o_ml_qk = 4 * w
    o_ml_v = o_ml_qk + 2 * ML_HEADS * ML_DQK
    o_if = o_ml_v + ML_HEADS * ML_DV
    o_ml_o = o_if + 2 * ML_HEADS
    o_mb_z = o_ml_o + ML_HEADS * ML_DV
    o_mb_xbc = o_mb_z + MB_W
    o_dt = o_mb_xbc + MB_CONV_DIM
    o_gate = o_dt + MB_HEADS
    hw = HALF_TILE
    conv_halves = [o_ml_qk + k * hw for k in range(2)] + [o_mb_xbc + k * hw for k in range(4)]
    raw_halves = [o_hg_f, o_hg_f + hw, o_hg_i, o_hg_i + hw, o_ml_v, o_ml_v + hw]
    cols = [(o_hg_q, w), (o_hg_g, w), (o_mb_z, w), (o_ml_o, w), (o_gate, N_GATES * w)]
    for c0, r0 in zip(conv_halves, raw_halves):
        cols += [(c0, hw), (r0, hw)]
    w_in16 = w_in.astype(BF16)
    w_big = jnp.concatenate([w_in16[:, :, a:a + n_] for a, n_ in cols], axis=-1)
    pad = SMALL_W - 2 * ML_HEADS - MB_HEADS
    w_small = jnp.concatenate(
        [w_in16[:, :, o_if:o_ml_o], w_in16[:, :, o_dt:o_gate],
         jnp.zeros((depth, D_MODEL, pad), BF16)], axis=-1)

    lbs = _lbs(hg_lb_logits.astype(F32))
    hg_ng = jnp.tile(hg_norm_g, (1, HG_HEADS))
    ml_ng = jnp.tile(ml_norm_g, (1, ML_HEADS))
    gb_row = jnp.pad(ml_gate_b, ((0, 0), (0, SMALL_W - 2 * ML_HEADS)))
    dtb_c = jnp.pad(mb_dt_bias, ((0, 0), (DT_COL, SMALL_W - DT_COL - MB_HEADS)))
    alog_c = jnp.pad(mb_a_log, ((0, 0), (DT_COL, SMALL_W - DT_COL - MB_HEADS)))
    n_pair = MB_HEADS // 2
    dtb_r = jnp.repeat(mb_dt_bias, CHUNK, axis=-1).reshape(depth, n_pair, 2 * CHUNK)
    alog_r = jnp.repeat(mb_a_log, CHUNK, axis=-1).reshape(depth, n_pair, 2 * CHUNK)
    d_x = jnp.repeat(mb_d, MB_P, axis=-1)

    n_tiles = w_big.shape[-1] // TN_PROJ
    n_mixed = len(TILES_MIXED)
    conv_w = jnp.concatenate([ml_conv_w, mb_conv_w], axis=-1).reshape(depth, CONV_W, n_mixed, hw)
    conv_b = jnp.concatenate([ml_conv_b, mb_conv_b], axis=-1).reshape(depth, 1, n_mixed, hw)
    front = n_tiles - n_mixed
    cw_all = jnp.pad(jnp.swapaxes(conv_w, 1, 2), ((0, 0), (front, 0), (0, 0), (0, 0)))
    cb_all = jnp.pad(jnp.swapaxes(conv_b, 1, 2), ((0, 0), (front, 0), (0, 0), (0, 0)))

    w_hg = w_br_hg.astype(BF16)
    w_ml = w_br_ml.astype(BF16)
    w_mb = w_br_mb.astype(BF16)
    w_o = w_out.astype(BF16)
    w_u = w_up.astype(BF16)
    w_d = w_down.astype(BF16)

    x2 = x.reshape(n, D_MODEL)
    row = lambda a: a.reshape(1, -1)
    for l in range(depth):
        proj, small = _inproj(x2, row(norm1_g[l]), w_big, w_small, cw_all, cb_all, l, seq)
        chunks = small.reshape(n // CHUNK, CHUNK, SMALL_W)
        small_t = jnp.swapaxes(small.reshape(n // ML_BLOCK, ML_BLOCK, SMALL_W)[:, :, :2 * ML_HEADS], 1, 2)
        dt_t = jnp.swapaxes(chunks[:, :, DT_COL:DT_COL + MB_HEADS], 1, 2).reshape(
            n // CHUNK, n_pair, 2 * CHUNK)

        y_hg, y_ml, y_mb = _mixers(
            proj, small, small_t, dt_t, row(lbs[l]), row(hg_ng[l]),
            row(gb_row[l]), ml_gate_b[l].reshape(2 * ML_HEADS, 1), row(ml_ng[l]),
            row(dtb_c[l]), row(alog_c[l]), dtb_r[l], alog_r[l], row(d_x[l]), row(mb_norm_g[l]), batch, seq)
        x2 = _merge(x2, y_hg, y_ml, y_mb, proj, w_hg, w_ml, w_mb, w_o, l)
        x2 = _ffn(x2, row(norm2_g[l]), w_u, ffn_conv_w[l], row(ffn_conv_b[l]), w_d, row(final_g),
                  l, batch, seq, final=(l == depth - 1))
    return x2.reshape(batch, seq, D_MODEL)
```
